```python
import jax, jax.numpy as jnp
from jax import lax
import numpy as np

D_MODEL = 2048
BATCH = 8
SEQ = 4096
DEPTH = 4

N_MIXERS = 4
N_SB = len(range(0, DEPTH, N_MIXERS))
N_GDN = len(range(1, DEPTH, N_MIXERS))
N_DSW = len(range(2, DEPTH, N_MIXERS))
N_LRU = len(range(3, DEPTH, N_MIXERS))

HEAD_DIM = 128
NORM_EPS = 1e-6
Q_BLOCK = 128

SB_HEADS = D_MODEL // HEAD_DIM

GDN_K_HEADS = D_MODEL // HEAD_DIM
GDN_V_HEADS = 2 * GDN_K_HEADS
GDN_KEY_DIM = GDN_K_HEADS * HEAD_DIM
GDN_VAL_DIM = GDN_V_HEADS * HEAD_DIM
GDN_CONV = 4
GDN_CHUNK = 64
GDN_IN = 2 * GDN_KEY_DIM + 2 * GDN_VAL_DIM + 2 * GDN_V_HEADS

DSW_GROUPS = ((128, 1), (512, 4), (2048, 16))
DSW_HEADS_PER_GROUP = 6
DSW_HEADS = len(DSW_GROUPS) * DSW_HEADS_PER_GROUP
DSW_BLOCK = 128
ROPE_DIM = HEAD_DIM // 4
ROPE_THETA = 500000.0

LRU_WIDTH = D_MODEL
LRU_BLOCK_DIM = 256
LRU_BLOCKS = LRU_WIDTH // LRU_BLOCK_DIM
LRU_CONV = 4
LRU_C = 8.0

FFN_HIDDEN = ((8 * D_MODEL + 3 * 256 - 1) // (3 * 256)) * 256

kernel_name = 'hybrid_sb_gdn_dilated_rglru_trunk'


def rms_norm(x, g):
    xf = x.astype(jnp.float32)
    y = xf * lax.rsqrt(jnp.mean(xf * xf, axis=-1, keepdims=True) + NORM_EPS)
    return (y * g.astype(jnp.float32)).astype(x.dtype)


def l2_norm(x):
    xf = x.astype(jnp.float32)
    return xf * lax.rsqrt(jnp.sum(xf * xf, axis=-1, keepdims=True) + NORM_EPS)


def causal_dwconv(x, w):
    K, C = w.shape
    return lax.conv_general_dilated(
        x, w[:, None, :], window_strides=(1,), padding=((K - 1, 0),),
        dimension_numbers=('NWC', 'WIO', 'NWC'), feature_group_count=C)


def partial_rope(x, positions):
    half = ROPE_DIM // 2
    inv_freq = ROPE_THETA ** (-jnp.arange(half, dtype=jnp.float32) / half)
    ang = positions.astype(jnp.float32)[..., None] * inv_freq
    cos = jnp.cos(ang)[:, :, None, :]
    sin = jnp.sin(ang)[:, :, None, :]
    xr = x[..., :ROPE_DIM].astype(jnp.float32)
    x1, x2 = xr[..., :half], xr[..., half:]
    rot = jnp.concatenate([x1 * cos - x2 * sin, x2 * cos + x1 * sin], axis=-1)
    return jnp.concatenate([rot.astype(x.dtype), x[..., ROPE_DIM:]], axis=-1)


def swiglu(h, w_gu, w_down):
    gate, up = jnp.split(h @ w_gu, 2, axis=-1)
    return (jax.nn.silu(gate) * up) @ w_down


def stick_breaking_mixer(h, w_in, q_norm, k_norm, w_out):
    B, T, _ = h.shape
    H, dh = SB_HEADS, HEAD_DIM
    q, k, v = jnp.split(h @ w_in, 3, axis=-1)
    q = rms_norm(q.reshape(B, T, H, dh), q_norm)
    k = rms_norm(k.reshape(B, T, H, dh), k_norm)
    v = v.reshape(B, T, H, dh)
    nb = T // Q_BLOCK
    qb = q.reshape(B, nb, Q_BLOCK, H, dh).transpose(1, 0, 3, 2, 4)
    key_pos = jnp.arange(T)
    scale = HEAD_DIM ** -0.5

    def block(args):
        q_blk, n = args
        z = jnp.einsum('bhqd,bshd->bhqs', q_blk, k, preferred_element_type=jnp.float32) * scale
        q_pos = n * Q_BLOCK + jnp.arange(Q_BLOCK)
        past = key_pos[None, :] < q_pos[:, None]
        neg_log_1m_beta = jnp.where(past, jax.nn.softplus(z), 0.0)
        between = lax.cumsum(neg_log_1m_beta, axis=3, reverse=True) - neg_log_1m_beta
        log_w = jax.nn.log_sigmoid(z) - between
        w = jnp.where(past, jnp.exp(log_w), 0.0)
        return jnp.einsum('bhqs,bshd->bqhd', w.astype(v.dtype), v)

    o = lax.map(block, (qb, jnp.arange(nb)))
    o = o.transpose(1, 0, 2, 3, 4).reshape(B, T, H * dh)
    return o @ w_out


def chunk_gated_delta_rule(q, k, v, g, beta):
    B, T, H, dk = k.shape
    dv = v.shape[-1]
    C = GDN_CHUNK
    N = T // C
    f32 = jnp.float32

    def chunks(x):
        return x.astype(f32).reshape(B, N, C, H, -1).transpose(1, 0, 3, 2, 4)

    q, k, v = chunks(q), chunks(k), chunks(v)
    beta = chunks(beta[..., None])[..., 0]
    g = jnp.cumsum(chunks(g[..., None])[..., 0], axis=-1)
    causal = jnp.tril(jnp.ones((C, C), bool))
    strict = jnp.tril(jnp.ones((C, C), bool), -1)
    diff = g[..., :, None] - g[..., None, :]
    decay = jnp.where(causal, jnp.exp(jnp.where(causal, diff, 0.0)), 0.0)
    kb = k * beta[..., None]
    vb = v * beta[..., None]
    a_mat = jnp.where(strict, jnp.einsum('nbhcd,nbhsd->nbhcs', kb, k) * decay, 0.0)
    eye = jnp.eye(C, dtype=f32)
    t_mat = lax.linalg.triangular_solve(a_mat + eye, jnp.broadcast_to(eye, a_mat.shape),
                                        left_side=True, lower=True, unit_diagonal=True)
    u = t_mat @ vb
    w = t_mat @ (kb * jnp.exp(g)[..., None])

    def step(state, inp):
        q_c, k_c, u_c, w_c, g_c, decay_c = inp
        v_new = u_c - w_c @ state
        attn = jnp.einsum('bhcd,bhsd->bhcs', q_c, k_c) * decay_c
        out = (q_c * jnp.exp(g_c)[..., None]) @ state + attn @ v_new
        g_last = g_c[..., -1:]
        k_dec = k_c * jnp.exp(g_last - g_c)[..., None]
        state = state * jnp.exp(g_last)[..., None] + jnp.einsum('bhcd,bhce->bhde', k_dec, v_new)
        return state, out

    state0 = jnp.zeros((B, H, dk, dv), f32)
    _, out = lax.scan(step, state0, (q, k, u, w, g, decay))
    return out.transpose(1, 0, 3, 2, 4).reshape(B, T, H, dv)


def gated_deltanet_mixer(h, w_in, conv_w, a_log, dt_bias, o_norm, w_out):
    B, T, _ = h.shape
    Kd, Vd, Hk, Hv, dh = GDN_KEY_DIM, GDN_VAL_DIM, GDN_K_HEADS, GDN_V_HEADS, HEAD_DIM
    proj = h @ w_in
    qkv, z, b, a = jnp.split(proj, [2 * Kd + Vd, 2 * Kd + 2 * Vd, 2 * Kd + 2 * Vd + Hv], axis=-1)
    qkv = jax.nn.silu(causal_dwconv(qkv, conv_w))
    q, k, v = jnp.split(qkv, [Kd, 2 * Kd], axis=-1)
    rep = Hv // Hk
    q = jnp.repeat(q.reshape(B, T, Hk, dh), rep, axis=2)
    k = jnp.repeat(k.reshape(B, T, Hk, dh), rep, axis=2)
    v = v.reshape(B, T, Hv, dh)
    q = l2_norm(q) * dh ** -0.5
    k = l2_norm(k)
    beta = jax.nn.sigmoid(b.astype(jnp.float32))
    g = -jnp.exp(a_log.astype(jnp.float32)) * jax.nn.softplus(
        a.astype(jnp.float32) + dt_bias.astype(jnp.float32))
    o = chunk_gated_delta_rule(q, k, v, g, beta)
    o = rms_norm(o, o_norm) * jax.nn.silu(z.reshape(B, T, Hv, dh).astype(jnp.float32))
    return o.astype(h.dtype).reshape(B, T, Vd) @ w_out


def dilated_band_attention(q, k, v, window, dilation):
    B, T, H, dh = q.shape
    blk = DSW_BLOCK
    span = window // dilation
    unit = dilation * blk
    t_pad = -(-T // unit) * unit
    sub_len = t_pad // dilation
    nb = sub_len // blk

    def gather(x):
        x = jnp.pad(x, ((0, 0), (0, t_pad - T), (0, 0), (0, 0)))
        x = x.reshape(B, sub_len, dilation, H, dh).transpose(0, 2, 1, 3, 4)
        return x.reshape(B, dilation, nb, blk, H, dh)

    def with_previous_block(x):
        prev = jnp.pad(x[:, :, :-1], ((0, 0), (0, 0), (1, 0), (0, 0), (0, 0), (0, 0)))
        return jnp.concatenate([prev, x], axis=3)

    qg = gather(q)
    kw = with_previous_block(gather(k))
    vw = with_previous_block(gather(v))
    s = jnp.einsum('brnqhd,brnkhd->brnhqk', qg, kw, preferred_element_type=jnp.float32) * dh ** -0.5
    qi = jnp.arange(blk)[:, None]
    kj = jnp.arange(2 * blk)[None, :]
    steps = blk + qi - kj
    band = (steps >= 0) & (steps <= span)
    key_sub = jnp.arange(nb)[:, None, None] * blk + (kj - blk)[None]
    mask = band[None] & (key_sub >= 0)
    s = jnp.where(mask[:, None], s, -jnp.inf)
    m = jnp.max(s, axis=-1, keepdims=True)
    p = jnp.exp(s - m)
    den = jnp.sum(p, axis=-1, keepdims=True)
    o = jnp.einsum('brnhqk,brnkhd->brnqhd', (p / den).astype(v.dtype), vw)
    lse = (m + jnp.log(den))[..., 0]
    o = o.reshape(B, dilation, sub_len, H, dh).transpose(0, 2, 1, 3, 4).reshape(B, t_pad, H, dh)[:, :T]
    lse = lse.transpose(0, 1, 2, 4, 3).reshape(B, dilation, sub_len, H)
    lse = lse.transpose(0, 2, 1, 3).reshape(B, t_pad, H)[:, :T]
    return o, lse


def dilated_window_mixer(h, positions, w_in, q_norm, k_norm, w_out):
    B, T, _ = h.shape
    G, Hg, dh = len(DSW_GROUPS), DSW_HEADS_PER_GROUP, HEAD_DIM
    q, k, v = jnp.split(h @ w_in, 3, axis=-1)
    q = partial_rope(rms_norm(q.reshape(B, T, G * Hg, dh), q_norm), positions)
    k = partial_rope(rms_norm(k.reshape(B, T, G * Hg, dh), k_norm), positions)
    v = v.reshape(B, T, G * Hg, dh)
    outs, lses = [], []
    for gi, (window, dilation) in enumerate(DSW_GROUPS):
        sl = slice(gi * Hg, (gi + 1) * Hg)
        o_g, lse_g = dilated_band_attention(q[:, :, sl], k[:, :, sl], v[:, :, sl], window, dilation)
        outs.append(o_g)
        lses.append(lse_g)
    o = jnp.stack(outs, axis=2)
    alpha = jax.nn.softmax(jnp.stack(lses, axis=2), axis=2)
    o = (o * alpha[..., None].astype(o.dtype)).reshape(B, T, G * Hg * dh)
    return o @ w_out


def rglru_mixer(h, w_in, conv_w, conv_b, w_a, b_a, w_x, b_x, lam, w_out):
    B, T, _ = h.shape
    gate, xr = jnp.split(h @ w_in, 2, axis=-1)
    gate = jax.nn.gelu(gate, approximate=True)
    xr = causal_dwconv(xr, conv_w) + conv_b
    xb = xr.reshape(B, T, LRU_BLOCKS, LRU_BLOCK_DIM)
    r = jax.nn.sigmoid(jnp.einsum('btni,nij->btnj', xb, w_a) + b_a).reshape(B, T, LRU_WIDTH)
    i = jax.nn.sigmoid(jnp.einsum('btni,nij->btnj', xb, w_x) + b_x).reshape(B, T, LRU_WIDTH)
    log_a = -LRU_C * r.astype(jnp.float32) * jax.nn.softplus(-lam.astype(jnp.float32))
    a = jnp.exp(log_a)
    u = jnp.sqrt(-jnp.expm1(2.0 * log_a)) * (i * xr).astype(jnp.float32)

    def combine(left, right):
        a_l, b_l = left
        a_r, b_r = right
        return a_l * a_r, a_r * b_l + b_r

    _, hs = lax.associative_scan(combine, (a, u), axis=1)
    return (hs.astype(h.dtype) * gate) @ w_out


def _fwd_setup_inputs(seed: int = 0) -> dict:
    key = jax.random.key(seed)
    ks = iter(jax.random.split(key, 48))
    f32 = jnp.float32

    def dense(shape, fan_in):
        return jax.random.normal(next(ks), shape, f32) * fan_in ** -0.5

    def gain(shape):
        return 1.0 + 0.02 * jax.random.normal(next(ks), shape, f32)

    def small(shape):
        return 0.01 * jax.random.normal(next(ks), shape, f32)

    x = jax.random.normal(next(ks), (BATCH, SEQ, D_MODEL), f32)
    offsets = jax.random.randint(next(ks), (BATCH, 1), 0, SEQ, jnp.int32)
    positions = offsets + jnp.arange(SEQ, dtype=jnp.int32)[None, :]

    dt = jnp.exp(jax.random.uniform(next(ks), (N_GDN, GDN_V_HEADS), f32, np.log(1e-3), np.log(1e-1)))
    gdn_dt_bias = dt + jnp.log(-jnp.expm1(-dt))
    gdn_a_log = jnp.log(jax.random.uniform(next(ks), (N_GDN, GDN_V_HEADS), f32, 1.0, 16.0))
    a_c = jax.random.uniform(next(ks), (N_LRU, LRU_WIDTH), f32, 0.9, 0.999)
    s_lam = a_c ** (1.0 / LRU_C)
    lru_lambda = jnp.log(s_lam) - jnp.log1p(-s_lam)

    return {
        'x': x,
        'positions': positions,
        'mix_norm': gain((DEPTH, D_MODEL)),
        'ffn_norm': gain((DEPTH, D_MODEL)),
        'ffn_w_gu': dense((DEPTH, D_MODEL, 2 * FFN_HIDDEN), D_MODEL),
        'ffn_w_down': dense((DEPTH, FFN_HIDDEN, D_MODEL), FFN_HIDDEN),
        'sb_w_in': dense((N_SB, D_MODEL, 3 * SB_HEADS * HEAD_DIM), D_MODEL),
        'sb_q_norm': gain((N_SB, HEAD_DIM)),
        'sb_k_norm': gain((N_SB, HEAD_DIM)),
        'sb_w_out': dense((N_SB, SB_HEADS * HEAD_DIM, D_MODEL), SB_HEADS * HEAD_DIM),
        'gdn_w_in': dense((N_GDN, D_MODEL, GDN_IN), D_MODEL),
        'gdn_conv_w': dense((N_GDN, GDN_CONV, 2 * GDN_KEY_DIM + GDN_VAL_DIM), GDN_CONV),
        'gdn_a_log': gdn_a_log,
        'gdn_dt_bias': gdn_dt_bias,
        'gdn_o_norm': gain((N_GDN, HEAD_DIM)),
        'gdn_w_out': dense((N_GDN, GDN_VAL_DIM, D_MODEL), GDN_VAL_DIM),
        'dsw_w_in': dense((N_DSW, D_MODEL, 3 * DSW_HEADS * HEAD_DIM), D_MODEL),
        'dsw_q_norm': gain((N_DSW, HEAD_DIM)),
        'dsw_k_norm': gain((N_DSW, HEAD_DIM)),
        'dsw_w_out': dense((N_DSW, DSW_HEADS * HEAD_DIM, D_MODEL), DSW_HEADS * HEAD_DIM),
        'lru_w_in': dense((N_LRU, D_MODEL, 2 * LRU_WIDTH), D_MODEL),
        'lru_conv_w': dense((N_LRU, LRU_CONV, LRU_WIDTH), LRU_CONV),
        'lru_conv_b': small((N_LRU, LRU_WIDTH)),
        'lru_w_a': dense((N_LRU, LRU_BLOCKS, LRU_BLOCK_DIM, LRU_BLOCK_DIM), LRU_BLOCK_DIM),
        'lru_b_a': small((N_LRU, LRU_BLOCKS, LRU_BLOCK_DIM)),
        'lru_w_x': dense((N_LRU, LRU_BLOCKS, LRU_BLOCK_DIM, LRU_BLOCK_DIM), LRU_BLOCK_DIM),
        'lru_b_x': small((N_LRU, LRU_BLOCKS, LRU_BLOCK_DIM)),
        'lru_lambda': lru_lambda,
        'lru_w_out': dense((N_LRU, LRU_WIDTH, D_MODEL), LRU_WIDTH),
    }


def _fwd_reference(x, positions, mix_norm, ffn_norm, ffn_w_gu, ffn_w_down,
              sb_w_in, sb_q_norm, sb_k_norm, sb_w_out,
              gdn_w_in, gdn_conv_w, gdn_a_log, gdn_dt_bias, gdn_o_norm, gdn_w_out,
              dsw_w_in, dsw_q_norm, dsw_k_norm, dsw_w_out,
              lru_w_in, lru_conv_w, lru_conv_b, lru_w_a, lru_b_a, lru_w_x, lru_b_x,
              lru_lambda, lru_w_out):
    for i in range(DEPTH):
        kind, j = i % N_MIXERS, i // N_MIXERS
        h = rms_norm(x, mix_norm[i])
        if kind == 0:
            y = stick_breaking_mixer(h, sb_w_in[j], sb_q_norm[j], sb_k_norm[j], sb_w_out[j])
        elif kind == 1:
            y = gated_deltanet_mixer(h, gdn_w_in[j], gdn_conv_w[j], gdn_a_log[j], gdn_dt_bias[j],
                                     gdn_o_norm[j], gdn_w_out[j])
        elif kind == 2:
            y = dilated_window_mixer(h, positions, dsw_w_in[j], dsw_q_norm[j], dsw_k_norm[j], dsw_w_out[j])
        else:
            y = rglru_mixer(h, lru_w_in[j], lru_conv_w[j], lru_conv_b[j], lru_w_a[j], lru_b_a[j],
                            lru_w_x[j], lru_b_x[j], lru_lambda[j], lru_w_out[j])
        x = x + y.astype(x.dtype)
        x = x + swiglu(rms_norm(x, ffn_norm[i]), ffn_w_gu[i], ffn_w_down[i]).astype(x.dtype)
    return x


import jax as _jax
import jax.numpy as _jnp

TWIN_FORMAT = 'train_step'
FWD_PARAMS = ['x', 'positions', 'mix_norm', 'ffn_norm', 'ffn_w_gu', 'ffn_w_down', 'sb_w_in', 'sb_q_norm', 'sb_k_norm', 'sb_w_out', 'gdn_w_in', 'gdn_conv_w', 'gdn_a_log', 'gdn_dt_bias', 'gdn_o_norm', 'gdn_w_out', 'dsw_w_in', 'dsw_q_norm', 'dsw_k_norm', 'dsw_w_out', 'lru_w_in', 'lru_conv_w', 'lru_conv_b', 'lru_w_a', 'lru_b_a', 'lru_w_x', 'lru_b_x', 'lru_lambda', 'lru_w_out']
TWIN_WEIGHTS = ['mix_norm', 'ffn_norm', 'ffn_w_gu', 'ffn_w_down', 'sb_w_in', 'sb_q_norm', 'sb_k_norm', 'sb_w_out', 'gdn_w_in', 'gdn_conv_w', 'gdn_a_log', 'gdn_dt_bias', 'gdn_o_norm', 'gdn_w_out', 'dsw_w_in', 'dsw_q_norm', 'dsw_k_norm', 'dsw_w_out', 'lru_w_in', 'lru_conv_w', 'lru_conv_b', 'lru_w_a', 'lru_b_a', 'lru_w_x', 'lru_b_x', 'lru_lambda', 'lru_w_out']
TWIN_DIFF_INPUT = 'x'
TWIN_INPUTS = ['x', 'positions', 'mix_norm', 'ffn_norm', 'ffn_w_gu', 'ffn_w_down', 'sb_w_in', 'sb_q_norm', 'sb_k_norm', 'sb_w_out', 'gdn_w_in', 'gdn_conv_w', 'gdn_a_log', 'gdn_dt_bias', 'gdn_o_norm', 'gdn_w_out', 'dsw_w_in', 'dsw_q_norm', 'dsw_k_norm', 'dsw_w_out', 'lru_w_in', 'lru_conv_w', 'lru_conv_b', 'lru_w_a', 'lru_b_a', 'lru_w_x', 'lru_b_x', 'lru_lambda', 'lru_w_out', 'loss_target', 'm_mix_norm', 'm_ffn_norm', 'm_ffn_w_gu', 'm_ffn_w_down', 'm_sb_w_in', 'm_sb_q_norm', 'm_sb_k_norm', 'm_sb_w_out', 'm_gdn_w_in', 'm_gdn_conv_w', 'm_gdn_a_log', 'm_gdn_dt_bias', 'm_gdn_o_norm', 'm_gdn_w_out', 'm_dsw_w_in', 'm_dsw_q_norm', 'm_dsw_k_norm', 'm_dsw_w_out', 'm_lru_w_in', 'm_lru_conv_w', 'm_lru_conv_b', 'm_lru_w_a', 'm_lru_b_a', 'm_lru_w_x', 'm_lru_b_x', 'm_lru_lambda', 'm_lru_w_out', 'v_mix_norm', 'v_ffn_norm', 'v_ffn_w_gu', 'v_ffn_w_down', 'v_sb_w_in', 'v_sb_q_norm', 'v_sb_k_norm', 'v_sb_w_out', 'v_gdn_w_in', 'v_gdn_conv_w', 'v_gdn_a_log', 'v_gdn_dt_bias', 'v_gdn_o_norm', 'v_gdn_w_out', 'v_dsw_w_in', 'v_dsw_q_norm', 'v_dsw_k_norm', 'v_dsw_w_out', 'v_lru_w_in', 'v_lru_conv_w', 'v_lru_conv_b', 'v_lru_w_a', 'v_lru_b_a', 'v_lru_w_x', 'v_lru_b_x', 'v_lru_lambda', 'v_lru_w_out']
TWIN_OUTPUTS = ['loss', 'grad_x', 'grad_mix_norm', 'grad_ffn_norm', 'grad_ffn_w_gu', 'grad_ffn_w_down', 'grad_sb_w_in', 'grad_sb_q_norm', 'grad_sb_k_norm', 'grad_sb_w_out', 'grad_gdn_w_in', 'grad_gdn_conv_w', 'grad_gdn_a_log', 'grad_gdn_dt_bias', 'grad_gdn_o_norm', 'grad_gdn_w_out', 'grad_dsw_w_in', 'grad_dsw_q_norm', 'grad_dsw_k_norm', 'grad_dsw_w_out', 'grad_lru_w_in', 'grad_lru_conv_w', 'grad_lru_conv_b', 'grad_lru_w_a', 'grad_lru_b_a', 'grad_lru_w_x', 'grad_lru_b_x', 'grad_lru_lambda', 'grad_lru_w_out', 'delta_mix_norm', 'delta_ffn_norm', 'delta_ffn_w_gu', 'delta_ffn_w_down', 'delta_sb_w_in', 'delta_sb_q_norm', 'delta_sb_k_norm', 'delta_sb_w_out', 'delta_gdn_w_in', 'delta_gdn_conv_w', 'delta_gdn_a_log', 'delta_gdn_dt_bias', 'delta_gdn_o_norm', 'delta_gdn_w_out', 'delta_dsw_w_in', 'delta_dsw_q_norm', 'delta_dsw_k_norm', 'delta_dsw_w_out', 'delta_lru_w_in', 'delta_lru_conv_w', 'delta_lru_conv_b', 'delta_lru_w_a', 'delta_lru_b_a', 'delta_lru_w_x', 'delta_lru_b_x', 'delta_lru_lambda', 'delta_lru_w_out', 'new_m_mix_norm', 'new_m_ffn_norm', 'new_m_ffn_w_gu', 'new_m_ffn_w_down', 'new_m_sb_w_in', 'new_m_sb_q_norm', 'new_m_sb_k_norm', 'new_m_sb_w_out', 'new_m_gdn_w_in', 'new_m_gdn_conv_w', 'new_m_gdn_a_log', 'new_m_gdn_dt_bias', 'new_m_gdn_o_norm', 'new_m_gdn_w_out', 'new_m_dsw_w_in', 'new_m_dsw_q_norm', 'new_m_dsw_k_norm', 'new_m_dsw_w_out', 'new_m_lru_w_in', 'new_m_lru_conv_w', 'new_m_lru_conv_b', 'new_m_lru_w_a', 'new_m_lru_b_a', 'new_m_lru_w_x', 'new_m_lru_b_x', 'new_m_lru_lambda', 'new_m_lru_w_out', 'new_v_mix_norm', 'new_v_ffn_norm', 'new_v_ffn_w_gu', 'new_v_ffn_w_down', 'new_v_sb_w_in', 'new_v_sb_q_norm', 'new_v_sb_k_norm', 'new_v_sb_w_out', 'new_v_gdn_w_in', 'new_v_gdn_conv_w', 'new_v_gdn_a_log', 'new_v_gdn_dt_bias', 'new_v_gdn_o_norm', 'new_v_gdn_w_out', 'new_v_dsw_w_in', 'new_v_dsw_q_norm', 'new_v_dsw_k_norm', 'new_v_dsw_w_out', 'new_v_lru_w_in', 'new_v_lru_conv_w', 'new_v_lru_conv_b', 'new_v_lru_w_a', 'new_v_lru_b_a', 'new_v_lru_w_x', 'new_v_lru_b_x', 'new_v_lru_lambda', 'new_v_lru_w_out']
TWIN_LEAF_KINDS = {'loss': 'loss', 'grad_x': 'grad_x', 'grad_mix_norm': 'grad_w', 'grad_ffn_norm': 'grad_w', 'grad_ffn_w_gu': 'grad_w', 'grad_ffn_w_down': 'grad_w', 'grad_sb_w_in': 'grad_w', 'grad_sb_q_norm': 'grad_w', 'grad_sb_k_norm': 'grad_w', 'grad_sb_w_out': 'grad_w', 'grad_gdn_w_in': 'grad_w', 'grad_gdn_conv_w': 'grad_w', 'grad_gdn_a_log': 'grad_w', 'grad_gdn_dt_bias': 'grad_w', 'grad_gdn_o_norm': 'grad_w', 'grad_gdn_w_out': 'grad_w', 'grad_dsw_w_in': 'grad_w', 'grad_dsw_q_norm': 'grad_w', 'grad_dsw_k_norm': 'grad_w', 'grad_dsw_w_out': 'grad_w', 'grad_lru_w_in': 'grad_w', 'grad_lru_conv_w': 'grad_w', 'grad_lru_conv_b': 'grad_w', 'grad_lru_w_a': 'grad_w', 'grad_lru_b_a': 'grad_w', 'grad_lru_w_x': 'grad_w', 'grad_lru_b_x': 'grad_w', 'grad_lru_lambda': 'grad_w', 'grad_lru_w_out': 'grad_w', 'delta_mix_norm': 'delta_w', 'delta_ffn_norm': 'delta_w', 'delta_ffn_w_gu': 'delta_w', 'delta_ffn_w_down': 'delta_w', 'delta_sb_w_in': 'delta_w', 'delta_sb_q_norm': 'delta_w', 'delta_sb_k_norm': 'delta_w', 'delta_sb_w_out': 'delta_w', 'delta_gdn_w_in': 'delta_w', 'delta_gdn_conv_w': 'delta_w', 'delta_gdn_a_log': 'delta_w', 'delta_gdn_dt_bias': 'delta_w', 'delta_gdn_o_norm': 'delta_w', 'delta_gdn_w_out': 'delta_w', 'delta_dsw_w_in': 'delta_w', 'delta_dsw_q_norm': 'delta_w', 'delta_dsw_k_norm': 'delta_w', 'delta_dsw_w_out': 'delta_w', 'delta_lru_w_in': 'delta_w', 'delta_lru_conv_w': 'delta_w', 'delta_lru_conv_b': 'delta_w', 'delta_lru_w_a': 'delta_w', 'delta_lru_b_a': 'delta_w', 'delta_lru_w_x': 'delta_w', 'delta_lru_b_x': 'delta_w', 'delta_lru_lambda': 'delta_w', 'delta_lru_w_out': 'delta_w', 'new_m_mix_norm': 'new_m', 'new_m_ffn_norm': 'new_m', 'new_m_ffn_w_gu': 'new_m', 'new_m_ffn_w_down': 'new_m', 'new_m_sb_w_in': 'new_m', 'new_m_sb_q_norm': 'new_m', 'new_m_sb_k_norm': 'new_m', 'new_m_sb_w_out': 'new_m', 'new_m_gdn_w_in': 'new_m', 'new_m_gdn_conv_w': 'new_m', 'new_m_gdn_a_log': 'new_m', 'new_m_gdn_dt_bias': 'new_m', 'new_m_gdn_o_norm': 'new_m', 'new_m_gdn_w_out': 'new_m', 'new_m_dsw_w_in': 'new_m', 'new_m_dsw_q_norm': 'new_m', 'new_m_dsw_k_norm': 'new_m', 'new_m_dsw_w_out': 'new_m', 'new_m_lru_w_in': 'new_m', 'new_m_lru_conv_w': 'new_m', 'new_m_lru_conv_b': 'new_m', 'new_m_lru_w_a': 'new_m', 'new_m_lru_b_a': 'new_m', 'new_m_lru_w_x': 'new_m', 'new_m_lru_b_x': 'new_m', 'new_m_lru_lambda': 'new_m', 'new_m_lru_w_out': 'new_m', 'new_v_mix_norm': 'new_v', 'new_v_ffn_norm': 'new_v', 'new_v_ffn_w_gu': 'new_v', 'new_v_ffn_w_down': 'new_v', 'new_v_sb_w_in': 'new_v', 'new_v_sb_q_norm': 'new_v', 'new_v_sb_k_norm': 'new_v', 'new_v_sb_w_out': 'new_v', 'new_v_gdn_w_in': 'new_v', 'new_v_gdn_conv_w': 'new_v', 'new_v_gdn_a_log': 'new_v', 'new_v_gdn_dt_bias': 'new_v', 'new_v_gdn_o_norm': 'new_v', 'new_v_gdn_w_out': 'new_v', 'new_v_dsw_w_in': 'new_v', 'new_v_dsw_q_norm': 'new_v', 'new_v_dsw_k_norm': 'new_v', 'new_v_dsw_w_out': 'new_v', 'new_v_lru_w_in': 'new_v', 'new_v_lru_conv_w': 'new_v', 'new_v_lru_conv_b': 'new_v', 'new_v_lru_w_a': 'new_v', 'new_v_lru_b_a': 'new_v', 'new_v_lru_w_x': 'new_v', 'new_v_lru_b_x': 'new_v', 'new_v_lru_lambda': 'new_v', 'new_v_lru_w_out': 'new_v'}


def _forward(args):
    return _fwd_reference(*[args[k] for k in FWD_PARAMS])


def _output_shape():
    out = _jax.eval_shape(lambda: _forward(_fwd_setup_inputs(0)))
    return out.shape, out.dtype

N_MICROBATCH = 1
ADAM_LR = 0.001
ADAM_B1 = 0.9
ADAM_B2 = 0.999
ADAM_EPS = 1e-08
ADAM_WD = 0.01
ADAM_STEP = 10
PER_EXAMPLE_BATCH_AXIS = {'x': 0, 'positions': 0, 'loss_target': 0}
SHARED_INPUTS = []
_WEIGHT_DTYPES = {'mix_norm': _jnp.float32, 'ffn_norm': _jnp.float32, 'ffn_w_gu': _jnp.float32, 'ffn_w_down': _jnp.float32, 'sb_w_in': _jnp.float32, 'sb_q_norm': _jnp.float32, 'sb_k_norm': _jnp.float32, 'sb_w_out': _jnp.float32, 'gdn_w_in': _jnp.float32, 'gdn_conv_w': _jnp.float32, 'gdn_a_log': _jnp.float32, 'gdn_dt_bias': _jnp.float32, 'gdn_o_norm': _jnp.float32, 'gdn_w_out': _jnp.float32, 'dsw_w_in': _jnp.float32, 'dsw_q_norm': _jnp.float32, 'dsw_k_norm': _jnp.float32, 'dsw_w_out': _jnp.float32, 'lru_w_in': _jnp.float32, 'lru_conv_w': _jnp.float32, 'lru_conv_b': _jnp.float32, 'lru_w_a': _jnp.float32, 'lru_b_a': _jnp.float32, 'lru_w_x': _jnp.float32, 'lru_b_x': _jnp.float32, 'lru_lambda': _jnp.float32, 'lru_w_out': _jnp.float32}
MOMENT_SCALE = {'mix_norm': 6.467729e+00, 'ffn_norm': 1.185894e+01, 'ffn_w_gu': 2.381242e-01, 'ffn_w_down': 4.253496e-01, 'sb_w_in': 4.154186e-01, 'sb_q_norm': 1.504261e+01, 'sb_k_norm': 1.503052e+01, 'sb_w_out': 6.453264e-01, 'gdn_w_in': 2.543121e-01, 'gdn_conv_w': 1.118432e+00, 'gdn_a_log': 1.416632e+01, 'gdn_dt_bias': 1.341082e+01, 'gdn_o_norm': 9.526577e+01, 'gdn_w_out': 2.120608e+00, 'dsw_w_in': 3.637042e-01, 'dsw_q_norm': 3.708286e-01, 'dsw_k_norm': 3.715586e-01, 'dsw_w_out': 6.206106e-01, 'lru_w_in': 1.377480e+00, 'lru_conv_w': 3.152237e+00, 'lru_conv_b': 1.145501e+01, 'lru_w_a': 3.235352e-01, 'lru_b_a': 3.618572e-01, 'lru_w_x': 6.464363e-01, 'lru_b_x': 1.967474e+00, 'lru_lambda': 7.881069e-01, 'lru_w_out': 4.414101e-01}


def _to_microbatches(a, axis):
    t = _jnp.moveaxis(a, axis, 0)
    t = t.reshape((N_MICROBATCH, t.shape[0] // N_MICROBATCH) + t.shape[1:])
    return _jnp.moveaxis(t, 1, axis + 1)


def setup_inputs(seed: int = 0) -> dict:
    inp = _fwd_setup_inputs(seed)
    key = _jax.random.fold_in(_jax.random.key(seed), 7919)
    shape, _ = _output_shape()
    out = dict(inp)
    out["loss_target"] = _jax.random.normal(_jax.random.fold_in(key, 0), shape, _jnp.float32)
    for i, name in enumerate(TWIN_WEIGHTS):
        w = inp[name].astype(_jnp.float32)
        if MOMENT_SCALE is None:
            s = _jnp.sqrt(_jnp.mean(_jnp.square(w)) + 1e-30)
        else:
            s = MOMENT_SCALE[name]
        km, kv = _jax.random.split(_jax.random.fold_in(key, i + 1))
        out[name] = w
        out["m_" + name] = s * _jax.random.normal(km, w.shape, _jnp.float32)
        out["v_" + name] = (s * s) * _jax.random.uniform(kv, w.shape, _jnp.float32, 0.5, 1.5)
    if N_MICROBATCH > 1:
        for name, axis in PER_EXAMPLE_BATCH_AXIS.items():
            out[name] = _to_microbatches(out[name], axis)
    return {'x': out['x'], 'positions': out['positions'], 'mix_norm': out['mix_norm'], 'ffn_norm': out['ffn_norm'], 'ffn_w_gu': out['ffn_w_gu'], 'ffn_w_down': out['ffn_w_down'], 'sb_w_in': out['sb_w_in'], 'sb_q_norm': out['sb_q_norm'], 'sb_k_norm': out['sb_k_norm'], 'sb_w_out': out['sb_w_out'], 'gdn_w_in': out['gdn_w_in'], 'gdn_conv_w': out['gdn_conv_w'], 'gdn_a_log': out['gdn_a_log'], 'gdn_dt_bias': out['gdn_dt_bias'], 'gdn_o_norm': out['gdn_o_norm'], 'gdn_w_out': out['gdn_w_out'], 'dsw_w_in': out['dsw_w_in'], 'dsw_q_norm': out['dsw_q_norm'], 'dsw_k_norm': out['dsw_k_norm'], 'dsw_w_out': out['dsw_w_out'], 'lru_w_in': out['lru_w_in'], 'lru_conv_w': out['lru_conv_w'], 'lru_conv_b': out['lru_conv_b'], 'lru_w_a': out['lru_w_a'], 'lru_b_a': out['lru_b_a'], 'lru_w_x': out['lru_w_x'], 'lru_b_x': out['lru_b_x'], 'lru_lambda': out['lru_lambda'], 'lru_w_out': out['lru_w_out'], 'loss_target': out['loss_target'], 'm_mix_norm': out['m_mix_norm'], 'm_ffn_norm': out['m_ffn_norm'], 'm_ffn_w_gu': out['m_ffn_w_gu'], 'm_ffn_w_down': out['m_ffn_w_down'], 'm_sb_w_in': out['m_sb_w_in'], 'm_sb_q_norm': out['m_sb_q_norm'], 'm_sb_k_norm': out['m_sb_k_norm'], 'm_sb_w_out': out['m_sb_w_out'], 'm_gdn_w_in': out['m_gdn_w_in'], 'm_gdn_conv_w': out['m_gdn_conv_w'], 'm_gdn_a_log': out['m_gdn_a_log'], 'm_gdn_dt_bias': out['m_gdn_dt_bias'], 'm_gdn_o_norm': out['m_gdn_o_norm'], 'm_gdn_w_out': out['m_gdn_w_out'], 'm_dsw_w_in': out['m_dsw_w_in'], 'm_dsw_q_norm': out['m_dsw_q_norm'], 'm_dsw_k_norm': out['m_dsw_k_norm'], 'm_dsw_w_out': out['m_dsw_w_out'], 'm_lru_w_in': out['m_lru_w_in'], 'm_lru_conv_w': out['m_lru_conv_w'], 'm_lru_conv_b': out['m_lru_conv_b'], 'm_lru_w_a': out['m_lru_w_a'], 'm_lru_b_a': out['m_lru_b_a'], 'm_lru_w_x': out['m_lru_w_x'], 'm_lru_b_x': out['m_lru_b_x'], 'm_lru_lambda': out['m_lru_lambda'], 'm_lru_w_out': out['m_lru_w_out'], 'v_mix_norm': out['v_mix_norm'], 'v_ffn_norm': out['v_ffn_norm'], 'v_ffn_w_gu': out['v_ffn_w_gu'], 'v_ffn_w_down': out['v_ffn_w_down'], 'v_sb_w_in': out['v_sb_w_in'], 'v_sb_q_norm': out['v_sb_q_norm'], 'v_sb_k_norm': out['v_sb_k_norm'], 'v_sb_w_out': out['v_sb_w_out'], 'v_gdn_w_in': out['v_gdn_w_in'], 'v_gdn_conv_w': out['v_gdn_conv_w'], 'v_gdn_a_log': out['v_gdn_a_log'], 'v_gdn_dt_bias': out['v_gdn_dt_bias'], 'v_gdn_o_norm': out['v_gdn_o_norm'], 'v_gdn_w_out': out['v_gdn_w_out'], 'v_dsw_w_in': out['v_dsw_w_in'], 'v_dsw_q_norm': out['v_dsw_q_norm'], 'v_dsw_k_norm': out['v_dsw_k_norm'], 'v_dsw_w_out': out['v_dsw_w_out'], 'v_lru_w_in': out['v_lru_w_in'], 'v_lru_conv_w': out['v_lru_conv_w'], 'v_lru_conv_b': out['v_lru_conv_b'], 'v_lru_w_a': out['v_lru_w_a'], 'v_lru_b_a': out['v_lru_b_a'], 'v_lru_w_x': out['v_lru_w_x'], 'v_lru_b_x': out['v_lru_b_x'], 'v_lru_lambda': out['v_lru_lambda'], 'v_lru_w_out': out['v_lru_w_out']}


def _loss(weights, diff, rest, loss_target):
    with _jax.named_scope("forward"):
        args = {**rest, TWIN_DIFF_INPUT: diff, **{k: w.astype(_WEIGHT_DTYPES[k]) for k, w in weights.items()}}
        y = _forward(args)
    with _jax.named_scope("loss_head"):
        err = _jnp.square(y.astype(_jnp.float32) - loss_target)
        return 0.5 * _jnp.sum(_jnp.mean(err, axis=-1)) if err.ndim else 0.5 * err


def _adamw(w, g, m, v):
    m = ADAM_B1 * m + (1.0 - ADAM_B1) * g
    v = ADAM_B2 * v + (1.0 - ADAM_B2) * _jnp.square(g)
    m_hat = m / (1.0 - ADAM_B1 ** ADAM_STEP)
    v_hat = v / (1.0 - ADAM_B2 ** ADAM_STEP)
    delta = -ADAM_LR * (m_hat / (_jnp.sqrt(v_hat) + ADAM_EPS) + ADAM_WD * w)
    return delta, m, v


def reference(x, positions, mix_norm, ffn_norm, ffn_w_gu, ffn_w_down, sb_w_in, sb_q_norm, sb_k_norm, sb_w_out, gdn_w_in, gdn_conv_w, gdn_a_log, gdn_dt_bias, gdn_o_norm, gdn_w_out, dsw_w_in, dsw_q_norm, dsw_k_norm, dsw_w_out, lru_w_in, lru_conv_w, lru_conv_b, lru_w_a, lru_b_a, lru_w_x, lru_b_x, lru_lambda, lru_w_out, loss_target, m_mix_norm, m_ffn_norm, m_ffn_w_gu, m_ffn_w_down, m_sb_w_in, m_sb_q_norm, m_sb_k_norm, m_sb_w_out, m_gdn_w_in, m_gdn_conv_w, m_gdn_a_log, m_gdn_dt_bias, m_gdn_o_norm, m_gdn_w_out, m_dsw_w_in, m_dsw_q_norm, m_dsw_k_norm, m_dsw_w_out, m_lru_w_in, m_lru_conv_w, m_lru_conv_b, m_lru_w_a, m_lru_b_a, m_lru_w_x, m_lru_b_x, m_lru_lambda, m_lru_w_out, v_mix_norm, v_ffn_norm, v_ffn_w_gu, v_ffn_w_down, v_sb_w_in, v_sb_q_norm, v_sb_k_norm, v_sb_w_out, v_gdn_w_in, v_gdn_conv_w, v_gdn_a_log, v_gdn_dt_bias, v_gdn_o_norm, v_gdn_w_out, v_dsw_w_in, v_dsw_q_norm, v_dsw_k_norm, v_dsw_w_out, v_lru_w_in, v_lru_conv_w, v_lru_conv_b, v_lru_w_a, v_lru_b_a, v_lru_w_x, v_lru_b_x, v_lru_lambda, v_lru_w_out):
    given = dict(x=x, positions=positions, mix_norm=mix_norm, ffn_norm=ffn_norm, ffn_w_gu=ffn_w_gu, ffn_w_down=ffn_w_down, sb_w_in=sb_w_in, sb_q_norm=sb_q_norm, sb_k_norm=sb_k_norm, sb_w_out=sb_w_out, gdn_w_in=gdn_w_in, gdn_conv_w=gdn_conv_w, gdn_a_log=gdn_a_log, gdn_dt_bias=gdn_dt_bias, gdn_o_norm=gdn_o_norm, gdn_w_out=gdn_w_out, dsw_w_in=dsw_w_in, dsw_q_norm=dsw_q_norm, dsw_k_norm=dsw_k_norm, dsw_w_out=dsw_w_out, lru_w_in=lru_w_in, lru_conv_w=lru_conv_w, lru_conv_b=lru_conv_b, lru_w_a=lru_w_a, lru_b_a=lru_b_a, lru_w_x=lru_w_x, lru_b_x=lru_b_x, lru_lambda=lru_lambda, lru_w_out=lru_w_out, loss_target=loss_target, m_mix_norm=m_mix_norm, m_ffn_norm=m_ffn_norm, m_ffn_w_gu=m_ffn_w_gu, m_ffn_w_down=m_ffn_w_down, m_sb_w_in=m_sb_w_in, m_sb_q_norm=m_sb_q_norm, m_sb_k_norm=m_sb_k_norm, m_sb_w_out=m_sb_w_out, m_gdn_w_in=m_gdn_w_in, m_gdn_conv_w=m_gdn_conv_w, m_gdn_a_log=m_gdn_a_log, m_gdn_dt_bias=m_gdn_dt_bias, m_gdn_o_norm=m_gdn_o_norm, m_gdn_w_out=m_gdn_w_out, m_dsw_w_in=m_dsw_w_in, m_dsw_q_norm=m_dsw_q_norm, m_dsw_k_norm=m_dsw_k_norm, m_dsw_w_out=m_dsw_w_out, m_lru_w_in=m_lru_w_in, m_lru_conv_w=m_lru_conv_w, m_lru_conv_b=m_lru_conv_b, m_lru_w_a=m_lru_w_a, m_lru_b_a=m_lru_b_a, m_lru_w_x=m_lru_w_x, m_lru_b_x=m_lru_b_x, m_lru_lambda=m_lru_lambda, m_lru_w_out=m_lru_w_out, v_mix_norm=v_mix_norm, v_ffn_norm=v_ffn_norm, v_ffn_w_gu=v_ffn_w_gu, v_ffn_w_down=v_ffn_w_down, v_sb_w_in=v_sb_w_in, v_sb_q_norm=v_sb_q_norm, v_sb_k_norm=v_sb_k_norm, v_sb_w_out=v_sb_w_out, v_gdn_w_in=v_gdn_w_in, v_gdn_conv_w=v_gdn_conv_w, v_gdn_a_log=v_gdn_a_log, v_gdn_dt_bias=v_gdn_dt_bias, v_gdn_o_norm=v_gdn_o_norm, v_gdn_w_out=v_gdn_w_out, v_dsw_w_in=v_dsw_w_in, v_dsw_q_norm=v_dsw_q_norm, v_dsw_k_norm=v_dsw_k_norm, v_dsw_w_out=v_dsw_w_out, v_lru_w_in=v_lru_w_in, v_lru_conv_w=v_lru_conv_w, v_lru_conv_b=v_lru_conv_b, v_lru_w_a=v_lru_w_a, v_lru_b_a=v_lru_b_a, v_lru_w_x=v_lru_w_x, v_lru_b_x=v_lru_b_x, v_lru_lambda=v_lru_lambda, v_lru_w_out=v_lru_w_out)
    weights = {n: given[n] for n in TWIN_WEIGHTS}
    shared = {n: given[n] for n in SHARED_INPUTS}
    per_example = {n: given[n] for n in ['x', 'positions']}
    grad_fn = _jax.value_and_grad(_loss, argnums=(0, 1))

    def one_microbatch(ex, loss_target):
        ex = dict(ex)
        diff = ex.pop(TWIN_DIFF_INPUT)
        return grad_fn(weights, diff, {**shared, **ex}, loss_target)

    if N_MICROBATCH == 1:
        loss, (grad_w, grad_x) = one_microbatch(per_example, given["loss_target"])
    else:
        def body(carry, xs):
            loss_sum, grad_sum = carry
            l_k, (gw_k, gx_k) = one_microbatch(xs[0], xs[1])
            with _jax.named_scope("update"):
                return (loss_sum + l_k, _jax.tree.map(_jnp.add, grad_sum, gw_k)), gx_k

        init = (_jnp.zeros((), _jnp.float32), _jax.tree.map(_jnp.zeros_like, weights))
        (loss, grad_w), grad_x = _jax.lax.scan(body, init, (per_example, given["loss_target"]))
    with _jax.named_scope("update"):
        delta_w, new_m, new_v = {}, {}, {}
        for n in TWIN_WEIGHTS:
            delta_w[n], new_m[n], new_v[n] = _adamw(weights[n], grad_w[n], given["m_" + n], given["v_" + n])
    return (loss, grad_x, *[grad_w[n] for n in TWIN_WEIGHTS], *[delta_w[n] for n in TWIN_WEIGHTS],
            *[new_m[n] for n in TWIN_WEIGHTS], *[new_v[n] for n in TWIN_WEIGHTS])
```

```python
import functools
import math

import jax
import jax.numpy as jnp
from jax import lax
from jax.experimental import pallas as pl
from jax.experimental.pallas import tpu as pltpu

F32 = jnp.float32
BF16 = jnp.bfloat16

D_MODEL = 2048
HEAD_DIM = 128
NORM_EPS = 1e-6
SB_HEADS = 16
SB_BLOCK = 256
GDN_K_HEADS = 16
GDN_V_HEADS = 32
GDN_KEY_DIM = 2048
GDN_VAL_DIM = 4096
GDN_CHUNK = 64
GDN_HEADS_PER_STEP = 4
DSW_GROUPS = ((128, 1), (512, 4), (2048, 16))
DSW_HG = 6
DSW_HEADS = 18
DSW_BLOCK = 128
ROPE_DIM = 32
ROPE_THETA = 500000.0
LRU_WIDTH = 2048
LRU_BLOCKS = 8
LRU_BLOCK_DIM = 256
LRU_C = 8.0
FFN_HIDDEN = 5632
DEPTH = 4
ADAM_LR, ADAM_B1, ADAM_B2, ADAM_EPS, ADAM_WD, ADAM_STEP = 0.001, 0.9, 0.999, 1e-08, 0.01, 10
N_DEV = 8
N_CHIPS = 4

V7X_VMEM_LIMIT = 56 * 1024 * 1024
LANES = 128
MESH = pl.DeviceIdType.MESH


def _params(**kw):
    return pltpu.CompilerParams(vmem_limit_bytes=V7X_VMEM_LIMIT, **kw)


def _pick(n, cands):
    for c in cands:
        if n % c == 0:
            return c
    return n


_DN = {"nn": (((1,), (0,)), ((), ())), "nt": (((1,), (1,)), ((), ())), "tn": (((0,), (0,)), ((), ()))}


def _raw_dot(a, b, mode):
    return lax.dot_general(a.astype(BF16), b.astype(BF16), _DN[mode], preferred_element_type=F32)


@functools.partial(jax.custom_vjp, nondiff_argnums=(2,))
def bdot(a, b, mode):
    return _raw_dot(a, b, mode)


def _bdot_fwd(a, b, mode):
    return _raw_dot(a, b, mode), (a, b)


def _bdot_bwd(mode, res, ct):
    a, b = res
    if mode == "nn":
        return bdot(ct, b, "nt"), bdot(a, ct, "tn")
    if mode == "nt":
        return bdot(ct, b, "nn"), bdot(ct, a, "tn")
    return bdot(b, ct, "nt"), bdot(a, ct, "nn")


bdot.defvjp(_bdot_fwd, _bdot_bwd)


def _split2(x):
    hi = x.astype(BF16)
    lo = (x - hi.astype(F32)).astype(BF16)
    return hi, lo


def _dot01_raw(x, m, dn, left):
    hi, lo = _split2(x)
    if left:
        return (lax.dot_general(m, hi, dn, preferred_element_type=F32)
                + lax.dot_general(m, lo, dn, preferred_element_type=F32))
    return (lax.dot_general(hi, m, dn, preferred_element_type=F32)
            + lax.dot_general(lo, m, dn, preferred_element_type=F32))


@functools.partial(jax.custom_vjp, nondiff_argnums=(2,))
def dot01(x, m, left):
    return _dot01_raw(x, m, _DN["nn"], left)


def _dot01_fwd(x, m, left):
    return _dot01_raw(x, m, _DN["nn"], left), m


def _dot01_bwd(left, m, ct):
    dx = _dot01_raw(ct, m, _DN["tn"] if left else _DN["nt"], left)
    return dx, jnp.zeros_like(m)


dot01.defvjp(_dot01_fwd, _dot01_bwd)


def _mm3(a, b):
    ah, al = _split2(a)
    bh, bl = _split2(b)
    dn = _DN["nn"]
    return (lax.dot_general(ah, bh, dn, preferred_element_type=F32)
            + lax.dot_general(ah, bl, dn, preferred_element_type=F32)
            + lax.dot_general(al, bh, dn, preferred_element_type=F32))


@jax.custom_vjp
def mm3(a, b):
    return _mm3(a, b)


def _mm3_fwd(a, b):
    return _mm3(a, b), (a, b)


def _mm3_bwd(res, ct):
    a, b = res
    return _mm3(ct, b.T), _mm3(a.T, ct)


mm3.defvjp(_mm3_fwd, _mm3_bwd)


def _softplus_parts(z):
    e = jnp.exp(-jnp.abs(z))
    l = jnp.log1p(e)
    sp = jnp.maximum(z, 0.0) + l
    ls = jnp.minimum(z, 0.0) - l
    inv = 1.0 / (1.0 + e)
    sig = jnp.where(z >= 0.0, inv, e * inv)
    return sp, ls, sig


def _rms(x, g):
    return x * lax.rsqrt(jnp.mean(x * x, axis=-1, keepdims=True) + NORM_EPS) * g


def _swap16(x):
    lane = lax.broadcasted_iota(jnp.int32, x.shape, 1)
    return jnp.where(lane < 16, pltpu.roll(x, 112, axis=1), jnp.where(lane < 32, pltpu.roll(x, 16, axis=1), 0.0))


@jax.custom_vjp
def rope(x, cosm, sinm):
    return x * cosm + _swap16(x) * sinm


def _rope_fwd(x, cosm, sinm):
    return rope(x, cosm, sinm), (cosm, sinm)


def _rope_bwd(res, ct):
    cosm, sinm = res
    return ct * cosm + _swap16(ct * sinm), jnp.zeros_like(cosm), jnp.zeros_like(sinm)


rope.defvjp(_rope_fwd, _rope_bwd)


def mm(a, b, mode, out_dtype=F32):
    if mode == "nt":
        (M, K), N = a.shape, b.shape[0]
    elif mode == "nn":
        (M, K), N = a.shape, b.shape[1]
    else:
        (K, M), N = a.shape, b.shape[1]
    tm = _pick(M, (1024, 512, 256, 128))
    tn = _pick(N, (1024, 512, 256, 128))
    tk = _pick(K, (2048, 1024, 512, 256, 128)) if mode != "tn" else _pick(K, (512, 256, 128))
    if mode != "tn" and tk > 1024 and a.dtype == F32 and tn > 512:
        tn = _pick(N, (512, 256, 128))
    nk = K // tk
    if mode == "nt":
        a_spec = pl.BlockSpec((tm, tk), lambda i, j, k: (i, k))
        b_spec = pl.BlockSpec((tn, tk), lambda i, j, k: (j, k))
    elif mode == "nn":
        a_spec = pl.BlockSpec((tm, tk), lambda i, j, k: (i, k))
        b_spec = pl.BlockSpec((tk, tn), lambda i, j, k: (k, j))
    else:
        a_spec = pl.BlockSpec((tk, tm), lambda i, j, k: (k, i))
        b_spec = pl.BlockSpec((tk, tn), lambda i, j, k: (k, j))

    def body(a_ref, b_ref, o_ref, *scr):
        p = _raw_dot(a_ref[...], b_ref[...], mode)
        if nk == 1:
            o_ref[...] = p.astype(o_ref.dtype)
        else:
            acc = scr[0]
            k = pl.program_id(2)

            @pl.when(k == 0)
            def _():
                acc[...] = p

            @pl.when(k > 0)
            def _():
                acc[...] += p

            @pl.when(k == nk - 1)
            def _():
                o_ref[...] = acc[...].astype(o_ref.dtype)

    return pl.pallas_call(
        body, name=f"mm_{mode}_{M}x{N}x{K}", out_shape=jax.ShapeDtypeStruct((M, N), out_dtype),
        grid=(M // tm, N // tn, nk), in_specs=[a_spec, b_spec],
        out_specs=pl.BlockSpec((tm, tn), lambda i, j, k: (i, j)),
        scratch_shapes=[] if nk == 1 else [pltpu.VMEM((tm, tn), F32)],
        compiler_params=_params(dimension_semantics=("arbitrary", "arbitrary", "arbitrary")),
    )(a, b)


@jax.custom_vjp
def lin_t(x, wt):
    return mm(x, wt, "nt")


def _lin_t_fwd(x, wt):
    return mm(x, wt, "nt"), (x, wt)


def _lin_t_bwd(res, dy):
    x, wt = res
    return mm(dy, wt, "nn"), mm(dy, x, "tn", out_dtype=wt.dtype)


lin_t.defvjp(_lin_t_fwd, _lin_t_bwd)


@jax.custom_vjp
def lin_n(x, w):
    return mm(x, w, "nn")


def _lin_n_fwd(x, w):
    return mm(x, w, "nn"), (x, w)


def _lin_n_bwd(res, dy):
    x, w = res
    return mm(dy, w, "nt"), mm(x, dy, "tn", out_dtype=w.dtype)


lin_n.defvjp(_lin_n_fwd, _lin_n_bwd)


def _full_spec(shape):
    nd = len(shape)
    return pl.BlockSpec(tuple(shape), lambda i: (0,) * nd)


def _row_spec(tr, c):
    return pl.BlockSpec((tr, c), lambda i: (i, 0))


def _rowop_tr(total_cols, T):
    budget = 20 * 1024 * 1024
    for tr in (512, 256, 128, 64, 32, 16, 8):
        if T % tr == 0 and total_cols * tr * 4 * 2 <= budget:
            return tr
    return 8


def rowop(fn, name, rows, consts, params, out_cols):
    nr, nc, npar, nout = len(rows), len(consts), len(params), len(out_cols)
    T = rows[0].shape[0]
    in_cols = [r.shape[1] for r in rows] + [c.shape[1] for c in consts]
    tr_f = _rowop_tr(sum(in_cols) + sum(out_cols), T)
    tr_b = _rowop_tr(sum(in_cols) + sum(out_cols) + sum(r.shape[1] for r in rows), T)

    def fwd_call(rows, consts, params):
        def body(*refs):
            ins = [r[...] for r in refs[:nr + nc + npar]]
            outs = fn(*ins)
            for o_ref, o in zip(refs[nr + nc + npar:], outs):
                o_ref[...] = o.astype(F32)

        return pl.pallas_call(
            body, name=name + "_fwd", grid=(T // tr_f,),
            out_shape=[jax.ShapeDtypeStruct((T, c), F32) for c in out_cols],
            in_specs=[_row_spec(tr_f, c) for c in in_cols] + [_full_spec(p.shape) for p in params],
            out_specs=[_row_spec(tr_f, c) for c in out_cols],
            compiler_params=_params(dimension_semantics=("arbitrary",)),
        )(*rows, *consts, *params)

    def bwd_call(rows, consts, params, douts):
        def body(*refs):
            i = pl.program_id(0)
            rv = [r[...] for r in refs[:nr]]
            cv = [r[...] for r in refs[nr:nr + nc]]
            pv = [r[...] for r in refs[nr + nc:nr + nc + npar]]
            dv = [r[...] for r in refs[nr + nc + npar:nr + nc + npar + nout]]
            orefs = refs[nr + nc + npar + nout:]
            _, vjp = jax.vjp(lambda rr, pp: tuple(fn(*rr, *cv, *pp)), rv, pv)
            drows, dpars = vjp(tuple(dv))
            for o_ref, g in zip(orefs[:nr], drows):
                o_ref[...] = g.astype(F32)

            @pl.when(i == 0)
            def _():
                for o_ref in orefs[nr:]:
                    o_ref[...] = jnp.zeros(o_ref.shape, F32)

            for o_ref, g in zip(orefs[nr:], dpars):
                o_ref[...] += g.astype(F32)

        res = pl.pallas_call(
            body, name=name + "_bwd", grid=(T // tr_b,),
            out_shape=[jax.ShapeDtypeStruct(r.shape, F32) for r in rows]
            + [jax.ShapeDtypeStruct(p.shape, F32) for p in params],
            in_specs=[_row_spec(tr_b, c) for c in in_cols] + [_full_spec(p.shape) for p in params]
            + [_row_spec(tr_b, c) for c in out_cols],
            out_specs=[_row_spec(tr_b, r.shape[1]) for r in rows] + [_full_spec(p.shape) for p in params],
            compiler_params=_params(dimension_semantics=("arbitrary",)),
        )(*rows, *consts, *params, *douts)
        return tuple(res[:nr]), tuple(res[nr:])

    @jax.custom_vjp
    def op(rows, consts, params):
        return tuple(fwd_call(rows, consts, params))

    def op_fwd(rows, consts, params):
        return tuple(fwd_call(rows, consts, params)), (rows, consts, params)

    def op_bwd(res, douts):
        rows, consts, params = res
        drows, dpars = bwd_call(rows, consts, params, douts)
        return drows, tuple(jnp.zeros_like(c) for c in consts), dpars

    op.defvjp(op_fwd, op_bwd)
    return op(tuple(rows), tuple(consts), tuple(params))


def _fn_norm(x, g):
    return (_rms(x, g),)


def _fn_add_norm(x, y, g):
    s = x + y
    return s, _rms(s, g)


def _fn_swiglu(gate, up):
    return (jax.nn.silu(gate) * up,)


def _heads(x, n, width=HEAD_DIM):
    return [x[:, h * width:(h + 1) * width] for h in range(n)]


def _fn_sb_pre(qkv, qn, kn):
    hs = _heads(qkv, 3 * SB_HEADS)
    q = jnp.concatenate([_rms(h, qn) for h in hs[:SB_HEADS]], axis=1)
    k = jnp.concatenate([_rms(h, kn) for h in hs[SB_HEADS:2 * SB_HEADS]], axis=1)
    v = jnp.concatenate(hs[2 * SB_HEADS:], axis=1)
    return q, k, v


def _fn_dsw_pre(qkv, cosm, sinm, qn, kn):
    hs = _heads(qkv, 3 * DSW_HEADS)
    q = jnp.concatenate([rope(_rms(h, qn), cosm, sinm) for h in hs[:DSW_HEADS]], axis=1)
    k = jnp.concatenate([rope(_rms(h, kn), cosm, sinm) for h in hs[DSW_HEADS:2 * DSW_HEADS]], axis=1)
    v = jnp.concatenate(hs[2 * DSW_HEADS:], axis=1)
    return q, k, v


def _fn_dsw_combine(o, lse):
    os_, ls_ = _heads(o, DSW_HEADS), _heads(lse, DSW_HEADS)
    out = [None] * DSW_HEADS
    for hg in range(DSW_HG):
        l3 = [ls_[g * DSW_HG + hg] for g in range(3)]
        m = jnp.maximum(jnp.maximum(l3[0], l3[1]), l3[2])
        e3 = [jnp.exp(l - m) for l in l3]
        den = e3[0] + e3[1] + e3[2]
        for g in range(3):
            out[g * DSW_HG + hg] = os_[g * DSW_HG + hg] * (e3[g] / den)
    return (jnp.concatenate(out, axis=1),)


def _l2(x):
    return x * lax.rsqrt(jnp.sum(x * x, axis=-1, keepdims=True) + NORM_EPS)


def _fn_gdn_pre(qkv, ba, a_log, dt_bias):
    x = jax.nn.silu(qkv)
    hs = _heads(x, 2 * GDN_K_HEADS + GDN_V_HEADS)
    rep = GDN_V_HEADS // GDN_K_HEADS
    qh = [_l2(h) * HEAD_DIM ** -0.5 for h in hs[:GDN_K_HEADS]]
    kh = [_l2(h) for h in hs[GDN_K_HEADS:2 * GDN_K_HEADS]]
    q = jnp.concatenate([qh[h // rep] for h in range(GDN_V_HEADS)], axis=1)
    k = jnp.concatenate([kh[h // rep] for h in range(GDN_V_HEADS)], axis=1)
    v = jnp.concatenate(hs[2 * GDN_K_HEADS:], axis=1)
    b = ba[:, :GDN_V_HEADS]
    a = ba[:, GDN_V_HEADS:2 * GDN_V_HEADS]
    beta = jax.nn.sigmoid(b)
    g = -jnp.exp(a_log) * jax.nn.softplus(a + dt_bias)
    rows = b.shape[0]
    beta_b = jnp.concatenate([jnp.broadcast_to(beta[:, h:h + 1], (rows, HEAD_DIM)) for h in range(GDN_V_HEADS)], axis=1)
    g_b = jnp.concatenate([jnp.broadcast_to(g[:, h:h + 1], (rows, HEAD_DIM)) for h in range(GDN_V_HEADS)], axis=1)
    return q, k, v, g_b, beta_b


def _fn_gdn_post(o, z, o_norm):
    os_, zs = _heads(o, GDN_V_HEADS), _heads(z, GDN_V_HEADS)
    return (jnp.concatenate([_rms(oh, o_norm) * jax.nn.silu(zh) for oh, zh in zip(os_, zs)], axis=1),)


def _expm1(x):
    return jnp.tanh(0.5 * x) * (jnp.exp(x) + 1.0)


def _fn_lru_gates(xc, conv_b, w_a, b_a, w_x, b_x, lam):
    xr = xc + conv_b
    xs = _heads(xr, LRU_BLOCKS, LRU_BLOCK_DIM)
    r = jnp.concatenate([bdot(xs[n], w_a[n], "nn") for n in range(LRU_BLOCKS)], axis=1) + b_a
    i = jnp.concatenate([bdot(xs[n], w_x[n], "nn") for n in range(LRU_BLOCKS)], axis=1) + b_x
    r = jax.nn.sigmoid(r)
    i = jax.nn.sigmoid(i)
    log_a = -LRU_C * r * jax.nn.softplus(-lam)
    a = jnp.exp(log_a)
    u = jnp.sqrt(-_expm1(2.0 * log_a)) * (i * xr)
    return a, u


def _fn_lru_out(hs, gate):
    c = math.sqrt(2.0 / math.pi)
    gl = 0.5 * gate * (1.0 + jnp.tanh(c * (gate + 0.044715 * (gate * gate * gate))))
    return (hs * gl,)


def loss_head(x, f, target):
    T, D = x.shape
    tr = _pick(T, (256, 128, 64, 32, 16, 8))

    def body(x_ref, f_ref, t_ref, l_ref, dy_ref):
        i = pl.program_id(0)
        err = (x_ref[...] + f_ref[...]) - t_ref[...]
        dy_ref[...] = err * (1.0 / D)
        part = 0.5 * jnp.sum(jnp.mean(err * err, axis=-1, keepdims=True), axis=0, keepdims=True)

        @pl.when(i == 0)
        def _():
            l_ref[...] = jnp.zeros(l_ref.shape, F32)

        l_ref[...] += jnp.broadcast_to(part, l_ref.shape)

    l, dy = pl.pallas_call(
        body, name="loss_head", grid=(T // tr,),
        out_shape=[jax.ShapeDtypeStruct((8, LANES), F32), jax.ShapeDtypeStruct((T, D), F32)],
        in_specs=[_row_spec(tr, D)] * 3, out_specs=[_full_spec((8, LANES)), _row_spec(tr, D)],
        compiler_params=_params(dimension_semantics=("arbitrary",)),
    )(x, f, target)
    return l[0, 0], dy


def _shift_rows(x, s):
    if s == 0:
        return x
    n = x.shape[0]
    row = lax.broadcasted_iota(jnp.int32, x.shape, 0)
    rolled = pltpu.roll(x, s % n, axis=0)
    keep = (row >= s) if s > 0 else (row < n + s)
    return jnp.where(keep, rolled, 0.0)


def _conv_fwd_call(x, w):
    T, C = x.shape
    K = w.shape[0]
    cb = _pick(C, (256, 128))

    def body(x_ref, w_ref, y_ref):
        xv = x_ref[...]
        acc = xv * w_ref[K - 1:K, :]
        for k in range(K - 1):
            acc = acc + _shift_rows(xv, K - 1 - k) * w_ref[k:k + 1, :]
        y_ref[...] = acc

    return pl.pallas_call(
        body, name=f"conv_fwd_{C}", grid=(C // cb,), out_shape=jax.ShapeDtypeStruct((T, C), F32),
        in_specs=[pl.BlockSpec((T, cb), lambda j: (0, j)), pl.BlockSpec((K, cb), lambda j: (0, j))],
        out_specs=pl.BlockSpec((T, cb), lambda j: (0, j)),
        compiler_params=_params(dimension_semantics=("arbitrary",)),
    )(x, w)


def _conv_bwd_call(x, w, dy):
    T, C = x.shape
    K = w.shape[0]
    cb = _pick(C, (256, 128))

    def body(x_ref, w_ref, dy_ref, dx_ref, dw_ref):
        xv, dv = x_ref[...], dy_ref[...]
        acc = dv * w_ref[K - 1:K, :]
        rows = [None] * K
        rows[K - 1] = jnp.sum(dv * xv, axis=0, keepdims=True)
        for k in range(K - 1):
            s = K - 1 - k
            acc = acc + _shift_rows(dv, -s) * w_ref[k:k + 1, :]
            rows[k] = jnp.sum(dv * _shift_rows(xv, s), axis=0, keepdims=True)
        dx_ref[...] = acc
        dw_ref[...] = jnp.concatenate(rows + [jnp.zeros((8 - K, cb), F32)], axis=0)

    dx, dw = pl.pallas_call(
        body, name=f"conv_bwd_{C}", grid=(C // cb,),
        out_shape=[jax.ShapeDtypeStruct((T, C), F32), jax.ShapeDtypeStruct((8, C), F32)],
        in_specs=[pl.BlockSpec((T, cb), lambda j: (0, j)), pl.BlockSpec((K, cb), lambda j: (0, j)),
                  pl.BlockSpec((T, cb), lambda j: (0, j))],
        out_specs=[pl.BlockSpec((T, cb), lambda j: (0, j)), pl.BlockSpec((8, cb), lambda j: (0, j))],
        compiler_params=_params(dimension_semantics=("arbitrary",)),
    )(x, w, dy)
    return dx, dw[:K]


@jax.custom_vjp
def dwconv(x, w):
    return _conv_fwd_call(x, w)


def _dwconv_fwd(x, w):
    return _conv_fwd_call(x, w), (x, w)


def _dwconv_bwd(res, dy):
    return _conv_bwd_call(*res, dy)


dwconv.defvjp(_dwconv_fwd, _dwconv_bwd)


def _scan_fwd_call(a, u):
    T, C = a.shape
    cb = _pick(C, (256, 128))

    def body(a_ref, u_ref, h_ref):
        def step(i, h):
            r = pl.multiple_of(i * 8, 8)
            at, ut = a_ref[pl.ds(r, 8), :], u_ref[pl.ds(r, 8), :]
            rows = []
            for j in range(8):
                h = at[j:j + 1, :] * h + ut[j:j + 1, :]
                rows.append(h)
            h_ref[pl.ds(r, 8), :] = jnp.concatenate(rows, axis=0)
            return h

        lax.fori_loop(0, T // 8, step, jnp.zeros((1, cb), F32))

    return pl.pallas_call(
        body, name="lru_scan_fwd", grid=(C // cb,), out_shape=jax.ShapeDtypeStruct((T, C), F32),
        in_specs=[pl.BlockSpec((T, cb), lambda j: (0, j))] * 2, out_specs=pl.BlockSpec((T, cb), lambda j: (0, j)),
        compiler_params=_params(dimension_semantics=("arbitrary",)),
    )(a, u)


def _scan_bwd_call(a, hs, dh):
    T, C = a.shape
    cb = _pick(C, (256, 128))
    nt = T // 8

    def body(a_ref, h_ref, dh_ref, da_ref, du_ref):
        def step(s, carry):
            i = nt - 1 - s
            r = pl.multiple_of(i * 8, 8)
            rp = pl.multiple_of(jnp.maximum(i - 1, 0) * 8, 8)
            at, ht, dt = a_ref[pl.ds(r, 8), :], h_ref[pl.ds(r, 8), :], dh_ref[pl.ds(r, 8), :]
            hprev_tile = h_ref[pl.ds(rp, 8), :]
            h_before = jnp.where(i > 0, hprev_tile[7:8, :], 0.0)
            da_rows, du_rows = [None] * 8, [None] * 8
            for j in range(7, -1, -1):
                lam = dt[j:j + 1, :] + carry
                du_rows[j] = lam
                hp = ht[j - 1:j, :] if j > 0 else h_before
                da_rows[j] = lam * hp
                carry = at[j:j + 1, :] * lam
            da_ref[pl.ds(r, 8), :] = jnp.concatenate(da_rows, axis=0)
            du_ref[pl.ds(r, 8), :] = jnp.concatenate(du_rows, axis=0)
            return carry

        lax.fori_loop(0, nt, step, jnp.zeros((1, cb), F32))

    return pl.pallas_call(
        body, name="lru_scan_bwd", grid=(C // cb,), out_shape=[jax.ShapeDtypeStruct((T, C), F32)] * 2,
        in_specs=[pl.BlockSpec((T, cb), lambda j: (0, j))] * 3,
        out_specs=[pl.BlockSpec((T, cb), lambda j: (0, j))] * 2,
        compiler_params=_params(dimension_semantics=("arbitrary",)),
    )(a, hs, dh)


@jax.custom_vjp
def lru_scan(a, u):
    return _scan_fwd_call(a, u)


def _lru_scan_fwd(a, u):
    hs = _scan_fwd_call(a, u)
    return hs, (a, hs)


def _lru_scan_bwd(res, dh):
    a, hs = res
    return tuple(_scan_bwd_call(a, hs, dh))


lru_scan.defvjp(_lru_scan_fwd, _lru_scan_bwd)


def _tri(n, kind):
    r = lax.broadcasted_iota(jnp.int32, (n, n), 0)
    c = lax.broadcasted_iota(jnp.int32, (n, n), 1)
    m = {"gt": r > c, "le": r <= c, "lt": r < c, "ge": r >= c, "eq": r == c}[kind]
    return jnp.where(m, 1.0, 0.0).astype(BF16)


def _sb_fwd_call(q, k, v):
    T, HD = q.shape
    H = HD // HEAD_DIM
    tb = _pick(T, (SB_BLOCK, 128))
    scale = HEAD_DIM ** -0.5

    def body(q_ref, k_ref, v_ref, o_ref, tot_ref):
        i = pl.program_id(1)
        qb = q_ref[...].astype(BF16)
        u_gt = _tri(tb, "gt")
        row = lax.broadcasted_iota(jnp.int32, (tb, tb), 0)
        col = lax.broadcasted_iota(jnp.int32, (tb, tb), 1)

        def step(s, carry):
            acc, run = carry
            j = i - s
            off = pl.multiple_of(j * tb, tb)
            kb = k_ref[pl.ds(off, tb), :].astype(BF16)
            vb = v_ref[pl.ds(off, tb), :].astype(BF16)
            z = lax.dot_general(qb, kb, _DN["nt"], preferred_element_type=F32) * scale
            past = (j * tb + col) < (i * tb + row)
            sp, ls, _ = _softplus_parts(z)
            sp = jnp.where(past, sp, 0.0)
            between = _dot01_raw(sp, u_gt, _DN["nn"], False) + run
            w = jnp.where(past, jnp.exp(ls - between), 0.0)
            acc = acc + lax.dot_general(w.astype(BF16), vb, _DN["nn"], preferred_element_type=F32)
            run = run + jnp.sum(sp, axis=1, keepdims=True)
            return acc, run

        acc, run = lax.fori_loop(0, i + 1, step, (jnp.zeros((tb, HEAD_DIM), F32), jnp.zeros((tb, 1), F32)))
        o_ref[...] = acc
        tot_ref[...] = jnp.broadcast_to(run, (tb, HEAD_DIM))

    blk = pl.BlockSpec((tb, HEAD_DIM), lambda h, i: (i, h))
    full = pl.BlockSpec((T, HEAD_DIM), lambda h, i: (0, h))
    return pl.pallas_call(
        body, name="sb_attn_fwd", grid=(H, T // tb), out_shape=[jax.ShapeDtypeStruct((T, HD), F32)] * 2,
        in_specs=[blk, full, full], out_specs=[blk, blk],
        compiler_params=_params(dimension_semantics=("arbitrary", "arbitrary")),
    )(q, k, v)


def _sb_bwd_call(q, k, v, tot, do):
    T, HD = q.shape
    H = HD // HEAD_DIM
    tb = _pick(T, (SB_BLOCK, 128))
    scale = HEAD_DIM ** -0.5

    def body(q_ref, k_ref, v_ref, tot_ref, do_ref, dq_ref, dk_ref, dv_ref):
        i = pl.program_id(1)

        @pl.when(i == 0)
        def _():
            dk_ref[...] = jnp.zeros(dk_ref.shape, F32)
            dv_ref[...] = jnp.zeros(dv_ref.shape, F32)

        qb = q_ref[...].astype(BF16)
        dob = do_ref[...].astype(BF16)
        tot = tot_ref[:, 0:1]
        u_le = _tri(tb, "le")
        u_lt = _tri(tb, "lt")
        row = lax.broadcasted_iota(jnp.int32, (tb, tb), 0)
        col = lax.broadcasted_iota(jnp.int32, (tb, tb), 1)

        def step(j, carry):
            dq, cs, cd = carry
            off = pl.multiple_of(j * tb, tb)
            kb = k_ref[pl.ds(off, tb), :].astype(BF16)
            vb = v_ref[pl.ds(off, tb), :].astype(BF16)
            z = lax.dot_general(qb, kb, _DN["nt"], preferred_element_type=F32) * scale
            past = (j * tb + col) < (i * tb + row)
            sp, ls, sig = _softplus_parts(z)
            sp = jnp.where(past, sp, 0.0)
            prefix = _dot01_raw(sp, u_le, _DN["nn"], False) + cs
            w = jnp.where(past, jnp.exp(ls - (tot - prefix)), 0.0)
            wb = w.astype(BF16)
            dv_ref[pl.ds(off, tb), :] += lax.dot_general(wb, dob, _DN["tn"], preferred_element_type=F32)
            dw = lax.dot_general(dob, vb, _DN["nt"], preferred_element_type=F32)
            dl = dw * w
            dsp = -(_dot01_raw(dl, u_lt, _DN["nn"], False) + cd)
            dz = jnp.where(past, dl * (1.0 - sig) + dsp * sig, 0.0) * scale
            dzb = dz.astype(BF16)
            dq = dq + lax.dot_general(dzb, kb, _DN["nn"], preferred_element_type=F32)
            dk_ref[pl.ds(off, tb), :] += lax.dot_general(dzb, qb, _DN["tn"], preferred_element_type=F32)
            cs = cs + jnp.sum(sp, axis=1, keepdims=True)
            cd = cd + jnp.sum(dl, axis=1, keepdims=True)
            return dq, cs, cd

        z1 = jnp.zeros((tb, 1), F32)
        dq, _, _ = lax.fori_loop(0, i + 1, step, (jnp.zeros((tb, HEAD_DIM), F32), z1, z1))
        dq_ref[...] = dq

    blk = pl.BlockSpec((tb, HEAD_DIM), lambda h, i: (i, h))
    full = pl.BlockSpec((T, HEAD_DIM), lambda h, i: (0, h))
    return pl.pallas_call(
        body, name="sb_attn_bwd", grid=(H, T // tb), out_shape=[jax.ShapeDtypeStruct((T, HD), F32)] * 3,
        in_specs=[blk, full, full, blk, blk], out_specs=[blk, full, full],
        compiler_params=_params(dimension_semantics=("arbitrary", "arbitrary")),
    )(q, k, v, tot, do)


@jax.custom_vjp
def sb_attn(q, k, v):
    return _sb_fwd_call(q, k, v)[0]


def _sb_attn_fwd(q, k, v):
    o, tot = _sb_fwd_call(q, k, v)
    return o, (q, k, v, tot)


def _sb_attn_bwd(res, do):
    return tuple(_sb_bwd_call(*res, do))


sb_attn.defvjp(_sb_attn_fwd, _sb_attn_bwd)


def _dsw_tile(q, kp, kc, vp, vc, n):
    blk = DSW_BLOCK
    scale = HEAD_DIM ** -0.5
    qi = lax.broadcasted_iota(jnp.int32, (blk, blk), 0)
    kj = lax.broadcasted_iota(jnp.int32, (blk, blk), 1)
    neg = -1e30
    s_p = jnp.where((kj >= qi) & (n > 0), bdot(q, kp, "nt") * scale, neg)
    s_c = jnp.where(kj <= qi, bdot(q, kc, "nt") * scale, neg)
    m = jnp.maximum(jnp.max(s_p, axis=-1, keepdims=True), jnp.max(s_c, axis=-1, keepdims=True))
    p_p, p_c = jnp.exp(s_p - m), jnp.exp(s_c - m)
    den = jnp.sum(p_p, axis=-1, keepdims=True) + jnp.sum(p_c, axis=-1, keepdims=True)
    o = bdot(p_p / den, vp, "nn") + bdot(p_c / den, vc, "nn")
    lse = m + jnp.log(den)
    return o, jnp.broadcast_to(lse, (blk, HEAD_DIM))


def _dsw_specs(nsub):
    cur = pl.BlockSpec((None, DSW_BLOCK, HEAD_DIM), lambda r, h, n: (r, n, h))
    prev = pl.BlockSpec((None, DSW_BLOCK, HEAD_DIM), lambda r, h, n: (r, jnp.maximum(n - 1, 0), h))
    whole = pl.BlockSpec((None, nsub, HEAD_DIM), lambda r, h, n: (r, 0, h))
    return cur, prev, whole


def _dsw_fwd_call(q, k, v):
    d, nsub, HD = q.shape
    cur, prev, _ = _dsw_specs(nsub)

    def body(q_ref, kp_ref, kc_ref, vp_ref, vc_ref, o_ref, l_ref):
        n = pl.program_id(2)
        o, l = _dsw_tile(q_ref[...], kp_ref[...], kc_ref[...], vp_ref[...], vc_ref[...], n)
        o_ref[...] = o
        l_ref[...] = l

    return pl.pallas_call(
        body, name=f"dsw_attn_fwd_d{d}", grid=(d, HD // HEAD_DIM, nsub // DSW_BLOCK),
        out_shape=[jax.ShapeDtypeStruct(q.shape, F32)] * 2,
        in_specs=[cur, prev, cur, prev, cur], out_specs=[cur, cur],
        compiler_params=_params(dimension_semantics=("arbitrary",) * 3),
    )(q, k, k, v, v)


def _dsw_bwd_call(q, k, v, do, dl):
    d, nsub, HD = q.shape
    cur, prev, whole = _dsw_specs(nsub)
    blk = DSW_BLOCK

    def body(q_ref, kp_ref, kc_ref, vp_ref, vc_ref, do_ref, dl_ref, dq_ref, dk_ref, dv_ref):
        n = pl.program_id(2)

        @pl.when(n == 0)
        def _():
            dk_ref[...] = jnp.zeros(dk_ref.shape, F32)
            dv_ref[...] = jnp.zeros(dv_ref.shape, F32)

        _, vjp = jax.vjp(lambda a, b, c, e, f: _dsw_tile(a, b, c, e, f, n),
                         q_ref[...], kp_ref[...], kc_ref[...], vp_ref[...], vc_ref[...])
        dq, dkp, dkc, dvp, dvc = vjp((do_ref[...], dl_ref[...]))
        dq_ref[...] = dq
        c0 = pl.multiple_of(n * blk, blk)
        p0 = pl.multiple_of(jnp.maximum(n - 1, 0) * blk, blk)
        dk_ref[pl.ds(c0, blk), :] += dkc
        dv_ref[pl.ds(c0, blk), :] += dvc
        dk_ref[pl.ds(p0, blk), :] += dkp
        dv_ref[pl.ds(p0, blk), :] += dvp

    return pl.pallas_call(
        body, name=f"dsw_attn_bwd_d{d}", grid=(d, HD // HEAD_DIM, nsub // blk),
        out_shape=[jax.ShapeDtypeStruct(q.shape, F32)] * 3,
        in_specs=[cur, prev, cur, prev, cur, cur, cur], out_specs=[cur, whole, whole],
        compiler_params=_params(dimension_semantics=("arbitrary",) * 3),
    )(q, k, k, v, v, do, dl)


@jax.custom_vjp
def dsw_attn(q, k, v):
    return tuple(_dsw_fwd_call(q, k, v))


def _dsw_attn_fwd(q, k, v):
    return tuple(_dsw_fwd_call(q, k, v)), (q, k, v)


def _dsw_attn_bwd(res, cts):
    return tuple(_dsw_bwd_call(*res, *cts))


dsw_attn.defvjp(_dsw_attn_fwd, _dsw_attn_bwd)


def _gdn_step(state, q, k, v, gb, bb):
    C = GDN_CHUNK
    r = lax.broadcasted_iota(jnp.int32, (C, C), 0)
    c = lax.broadcasted_iota(jnp.int32, (C, C), 1)
    causal, strict = r >= c, r > c
    gc = dot01(gb, _tri(C, "ge"), True)
    g_sq = gc[:, :C]
    g_row = dot01(jnp.where(r == c, g_sq, 0.0), jnp.ones((C, C), BF16), True)
    decay = jnp.where(causal, jnp.exp(jnp.where(causal, g_sq - g_row, 0.0)), 0.0)
    kb, vb = k * bb, v * bb
    a_mat = jnp.where(strict, bdot(kb, k, "nt") * decay, 0.0)
    x = -a_mat
    eye = jnp.where(r == c, 1.0, 0.0)
    t_mat = eye + x
    for _ in range(5):
        x = mm3(x, x)
        t_mat = t_mat + mm3(t_mat, x)
    u = bdot(t_mat, vb, "nn")
    w = bdot(t_mat, kb * jnp.exp(gc), "nn")
    v_new = u - bdot(w, state, "nn")
    attn = bdot(q, k, "nt") * decay
    out = bdot(q * jnp.exp(gc), state, "nn") + bdot(attn, v_new, "nn")
    g_last = gc[C - 1:C, :]
    k_dec = k * jnp.exp(g_last - gc)
    new_state = state * jnp.exp(g_last) + bdot(k_dec, v_new, "tn")
    return new_state, out


def _gdn_fwd_call(q, k, v, gb, bb):
    T, HD = v.shape
    H = HD // HEAD_DIM
    N = T // GDN_CHUNK
    hb = GDN_HEADS_PER_STEP
    W = hb * HEAD_DIM

    def body(q_ref, k_ref, v_ref, g_ref, b_ref, o_ref, s_ref, state):
        n = pl.program_id(1)

        @pl.when(n == 0)
        def _():
            state[...] = jnp.zeros(state.shape, F32)

        s_ref[...] = state[...]
        outs = []
        for h in range(hb):
            sl = slice(h * HEAD_DIM, (h + 1) * HEAD_DIM)
            ns, o = _gdn_step(state[h], q_ref[:, sl], k_ref[:, sl], v_ref[:, sl], g_ref[:, sl], b_ref[:, sl])
            state[h] = ns
            outs.append(o)
        o_ref[...] = jnp.concatenate(outs, axis=1)

    blk = pl.BlockSpec((GDN_CHUNK, W), lambda h, n: (n, h))
    sblk = pl.BlockSpec((None, hb, HEAD_DIM, HEAD_DIM), lambda h, n: (n, h, 0, 0))
    return pl.pallas_call(
        body, name="gdn_chunk_fwd", grid=(H // hb, N),
        out_shape=[jax.ShapeDtypeStruct((T, HD), F32), jax.ShapeDtypeStruct((N, H, HEAD_DIM, HEAD_DIM), F32)],
        in_specs=[blk] * 5, out_specs=[blk, sblk],
        scratch_shapes=[pltpu.VMEM((hb, HEAD_DIM, HEAD_DIM), F32)],
        compiler_params=_params(dimension_semantics=("arbitrary", "arbitrary")),
    )(q, k, v, gb, bb)


def _gdn_bwd_call(q, k, v, gb, bb, states, do):
    T, HD = v.shape
    H = HD // HEAD_DIM
    N = T // GDN_CHUNK
    hb = GDN_HEADS_PER_STEP
    W = hb * HEAD_DIM

    def body(q_ref, k_ref, v_ref, g_ref, b_ref, s_ref, do_ref, dq_ref, dk_ref, dv_ref, dg_ref, db_ref, dstate):
        n = pl.program_id(1)

        @pl.when(n == 0)
        def _():
            dstate[...] = jnp.zeros(dstate.shape, F32)

        grads = [[] for _ in range(5)]
        for h in range(hb):
            sl = slice(h * HEAD_DIM, (h + 1) * HEAD_DIM)
            _, vjp = jax.vjp(_gdn_step, s_ref[h], q_ref[:, sl], k_ref[:, sl], v_ref[:, sl], g_ref[:, sl], b_ref[:, sl])
            ds, *rest = vjp((dstate[h], do_ref[:, sl]))
            dstate[h] = ds
            for lst, g in zip(grads, rest):
                lst.append(g)
        for ref, lst in zip((dq_ref, dk_ref, dv_ref, dg_ref, db_ref), grads):
            ref[...] = jnp.concatenate(lst, axis=1)

    blk = pl.BlockSpec((GDN_CHUNK, W), lambda h, n: (N - 1 - n, h))
    sblk = pl.BlockSpec((None, hb, HEAD_DIM, HEAD_DIM), lambda h, n: (N - 1 - n, h, 0, 0))
    return pl.pallas_call(
        body, name="gdn_chunk_bwd", grid=(H // hb, N), out_shape=[jax.ShapeDtypeStruct((T, HD), F32)] * 5,
        in_specs=[blk] * 5 + [sblk, blk], out_specs=[blk] * 5,
        scratch_shapes=[pltpu.VMEM((hb, HEAD_DIM, HEAD_DIM), F32)],
        compiler_params=_params(dimension_semantics=("arbitrary", "arbitrary")),
    )(q, k, v, gb, bb, states, do)


@jax.custom_vjp
def gdn_core(q, k, v, gb, bb):
    return _gdn_fwd_call(q, k, v, gb, bb)[0]


def _gdn_core_fwd(q, k, v, gb, bb):
    o, states = _gdn_fwd_call(q, k, v, gb, bb)
    return o, (q, k, v, gb, bb, states)


def _gdn_core_bwd(res, do):
    return tuple(_gdn_bwd_call(*res, do))


gdn_core.defvjp(_gdn_core_fwd, _gdn_core_bwd)


def _mixer_sb(h, w):
    qkv = lin_t(h, w["sb_in_t"])
    q, k, v = rowop(_fn_sb_pre, "sb_pre", [qkv], [], [w["sb_q_norm"], w["sb_k_norm"]], [D_MODEL] * 3)
    return lin_n(sb_attn(q, k, v), w["sb_out"])


def _mixer_gdn(h, w):
    wt = w["gdn_in_t"]
    nqkv = 2 * GDN_KEY_DIM + GDN_VAL_DIM
    qkv = lin_t(h, wt[:nqkv])
    z = lin_t(h, wt[nqkv:nqkv + GDN_VAL_DIM])
    w_ba = jnp.pad(wt[nqkv + GDN_VAL_DIM:], ((0, LANES - 2 * GDN_V_HEADS), (0, 0)))
    ba = lin_t(h, w_ba)
    qkv = dwconv(qkv, w["gdn_conv_w"])
    q, k, v, gb, bb = rowop(_fn_gdn_pre, "gdn_pre", [qkv, ba], [], [w["gdn_a_log"], w["gdn_dt_bias"]],
                            [GDN_VAL_DIM] * 5)
    o = gdn_core(q, k, v, gb, bb)
    (y,) = rowop(_fn_gdn_post, "gdn_post", [o, z], [], [w["gdn_o_norm"]], [GDN_VAL_DIM])
    return lin_n(y, w["gdn_out"])


def _to_strided(x, cols, d):
    T = x.shape[0]
    return x[:, cols].reshape(T // d, d, -1).transpose(1, 0, 2)


def _from_strided(x):
    d, n, c = x.shape
    return x.transpose(1, 0, 2).reshape(d * n, c)


def _mixer_dsw(h, cosm, sinm, w):
    qkv = lin_t(h, w["dsw_in_t"])
    nhd = DSW_HEADS * HEAD_DIM
    q, k, v = rowop(_fn_dsw_pre, "dsw_pre", [qkv], [cosm, sinm], [w["dsw_q_norm"], w["dsw_k_norm"]], [nhd] * 3)
    outs, lses = [], []
    for gi, (_, d) in enumerate(DSW_GROUPS):
        cols = slice(gi * DSW_HG * HEAD_DIM, (gi + 1) * DSW_HG * HEAD_DIM)
        o_g, l_g = dsw_attn(_to_strided(q, cols, d), _to_strided(k, cols, d), _to_strided(v, cols, d))
        outs.append(_from_strided(o_g))
        lses.append(_from_strided(l_g))
    (o,) = rowop(_fn_dsw_combine, "dsw_combine", [jnp.concatenate(outs, axis=1), jnp.concatenate(lses, axis=1)],
                 [], [], [nhd])
    return lin_n(o, w["dsw_out"])


def _mixer_lru(h, w):
    wt = w["lru_in_t"]
    gate = lin_t(h, wt[:LRU_WIDTH])
    xr = dwconv(lin_t(h, wt[LRU_WIDTH:]), w["lru_conv_w"])
    a, u = rowop(_fn_lru_gates, "lru_gates", [xr], [],
                 [w["lru_conv_b"], w["lru_w_a"], w["lru_b_a"], w["lru_w_x"], w["lru_b_x"], w["lru_lambda"]],
                 [LRU_WIDTH] * 2)
    hs = lru_scan(a, u)
    (y,) = rowop(_fn_lru_out, "lru_out", [hs, gate], [], [], [LRU_WIDTH])
    return lin_n(y, w["lru_out"])


def _ffn(h, w, i):
    wt = w["ffn_gu_t"][i]
    gate = lin_t(h, wt[:FFN_HIDDEN])
    up = lin_t(h, wt[FFN_HIDDEN:])
    (act,) = rowop(_fn_swiglu, f"swiglu", [gate, up], [], [], [FFN_HIDDEN])
    return lin_n(act, w["ffn_down"][i])


def trunk(x, cosm, sinm, w):
    (h,) = rowop(_fn_norm, "norm", [x], [], [w["mix_norm"][0:1]], [D_MODEL])
    f = None
    for i in range(DEPTH):
        if i > 0:
            x, h = rowop(_fn_add_norm, "add_norm", [x, f], [], [w["mix_norm"][i:i + 1]], [D_MODEL] * 2)
        kind = i % 4
        if kind == 0:
            y = _mixer_sb(h, w)
        elif kind == 1:
            y = _mixer_gdn(h, w)
        elif kind == 2:
            y = _mixer_dsw(h, cosm, sinm, w)
        else:
            y = _mixer_lru(h, w)
        x, h = rowop(_fn_add_norm, "add_norm", [x, y], [], [w["ffn_norm"][i:i + 1]], [D_MODEL] * 2)
        f = _ffn(h, w, i)
    return x, f


ANY = pl.BlockSpec(memory_space=pl.ANY)


def all_gather(shard, name):
    r, C = shard.shape

    def body(x_ref, out_ref, send_sems, recv_sems, local_sem):
        x, y, c = lax.axis_index("x"), lax.axis_index("y"), lax.axis_index("c")
        me, sibling = (x, y, c), (x, y, 1 - c)
        chips = [(1 - x, y), (x, 1 - y), (1 - x, 1 - y)]

        def slot(px, py, pc):
            return out_ref.at[4 * px + 2 * py + pc]

        def copy(k, block, to, src=None):
            return pltpu.make_async_remote_copy(
                src_ref=slot(*block) if src is None else src, dst_ref=slot(*block),
                send_sem=send_sems.at[k], recv_sem=recv_sems.at[k], device_id=to, device_id_type=MESH)

        mine = pltpu.make_async_copy(x_ref, slot(*me), local_sem)
        mine.start()
        first = [copy(0, me, sibling, src=x_ref)]
        first += [copy(1 + j, me, (*chip, c), src=x_ref) for j, chip in enumerate(chips)]
        for cp in first:
            cp.start()
        passed = [copy(4 + j, (*chip, c), sibling) for j, chip in enumerate(chips)]
        for j, chip in enumerate(chips):
            copy(1 + j, (*chip, c), me).wait_recv()
            passed[j].start()
        copy(0, sibling, me).wait_recv()
        for j, chip in enumerate(chips):
            copy(4 + j, (*chip, 1 - c), me).wait_recv()
        for cp in first + passed:
            cp.wait_send()
        mine.wait()

    return pl.pallas_call(
        body, name=name, out_shape=jax.ShapeDtypeStruct((N_DEV, r, C), shard.dtype),
        in_specs=[ANY], out_specs=ANY,
        scratch_shapes=[pltpu.SemaphoreType.DMA((7,)), pltpu.SemaphoreType.DMA((7,)), pltpu.SemaphoreType.DMA],
        compiler_params=pltpu.CompilerParams(has_side_effects=True),
    )(shard)


def exchange_pair(g, name):
    _, _, r, C = g.shape

    def body(g_ref, out_ref, send_sems, recv_sems):
        x, y, c = lax.axis_index("x"), lax.axis_index("y"), lax.axis_index("c")
        copies = [pltpu.make_async_remote_copy(
            src_ref=g_ref.at[q, 1 - c], dst_ref=out_ref.at[q], send_sem=send_sems.at[q], recv_sem=recv_sems.at[q],
            device_id=(x, y, 1 - c), device_id_type=MESH) for q in range(N_CHIPS)]
        for cp in copies:
            cp.start()
        for cp in copies:
            cp.wait()

    return pl.pallas_call(
        body, name=name, out_shape=jax.ShapeDtypeStruct((N_CHIPS, r, C), g.dtype), in_specs=[ANY], out_specs=ANY,
        scratch_shapes=[pltpu.SemaphoreType.DMA((N_CHIPS,)), pltpu.SemaphoreType.DMA((N_CHIPS,))],
        compiler_params=pltpu.CompilerParams(has_side_effects=True),
    )(g)


def pair_add(g, got, name):
    _, _, r, C = g.shape
    tr = _pick(r, (256, 128, 64, 32, 16, 8))
    c = lax.axis_index("c")

    def body(c_ref, a_ref, b_ref, o_ref):
        o_ref[...] = (a_ref[...].astype(F32) + b_ref[...].astype(F32)).astype(o_ref.dtype)

    return pl.pallas_call(
        body, name=name, out_shape=jax.ShapeDtypeStruct((N_CHIPS, r, C), g.dtype),
        grid_spec=pltpu.PrefetchScalarGridSpec(
            num_scalar_prefetch=1, grid=(N_CHIPS, r // tr),
            in_specs=[pl.BlockSpec((None, None, tr, C), lambda q, i, cr: (q, cr[0], i, 0)),
                      pl.BlockSpec((None, tr, C), lambda q, i, cr: (q, i, 0))],
            out_specs=pl.BlockSpec((None, tr, C), lambda q, i, cr: (q, i, 0))),
        compiler_params=_params(dimension_semantics=("arbitrary", "arbitrary")),
    )(jnp.reshape(c, (1,)).astype(jnp.int32), g, got)


def exchange_chips(p, name):
    _, r, C = p.shape

    def body(p_ref, out_ref, send_sems, recv_sems, local_sem):
        x, y, c = lax.axis_index("x"), lax.axis_index("y"), lax.axis_index("c")
        mychip = 2 * x + y
        chips = [(1 - x, y), (x, 1 - y), (1 - x, 1 - y)]
        mine = pltpu.make_async_copy(p_ref.at[mychip], out_ref.at[mychip], local_sem)
        mine.start()
        copies = [pltpu.make_async_remote_copy(
            src_ref=p_ref.at[2 * cx + cy], dst_ref=out_ref.at[mychip], send_sem=send_sems.at[j],
            recv_sem=recv_sems.at[j], device_id=(cx, cy, c), device_id_type=MESH) for j, (cx, cy) in enumerate(chips)]
        for cp in copies:
            cp.start()
        for j, (cx, cy) in enumerate(chips):
            pltpu.make_async_remote_copy(
                src_ref=p_ref.at[mychip], dst_ref=out_ref.at[2 * cx + cy], send_sem=send_sems.at[j],
                recv_sem=recv_sems.at[j], device_id=(cx, cy, c), device_id_type=MESH).wait_recv()
        for cp in copies:
            cp.wait_send()
        mine.wait()

    return pl.pallas_call(
        body, name=name, out_shape=jax.ShapeDtypeStruct((N_CHIPS, r, C), p.dtype), in_specs=[ANY], out_specs=ANY,
        scratch_shapes=[pltpu.SemaphoreType.DMA((3,)), pltpu.SemaphoreType.DMA((3,)), pltpu.SemaphoreType.DMA],
        compiler_params=pltpu.CompilerParams(has_side_effects=True),
    )(p)


def sum_slots(parts, name):
    n, r, C = parts.shape
    tr = _pick(r, (256, 128, 64, 32, 16, 8))

    def body(p_ref, o_ref):
        acc = p_ref[0].astype(F32)
        for q in range(1, n):
            acc = acc + p_ref[q].astype(F32)
        o_ref[...] = acc

    return pl.pallas_call(
        body, name=name, out_shape=jax.ShapeDtypeStruct((r, C), F32), grid=(r // tr,),
        in_specs=[pl.BlockSpec((n, tr, C), lambda i: (0, i, 0))], out_specs=pl.BlockSpec((tr, C), lambda i: (i, 0)),
        compiler_params=_params(dimension_semantics=("arbitrary",)),
    )(parts)


def reduce_scatter(g, tag):
    _, r, C = g.shape
    g4 = g.reshape(N_CHIPS, 2, r, C)
    got = exchange_pair(g4, f"rs_pair_{tag}")
    pairs = pair_add(g4, got, f"rs_pair_add_{tag}")
    parts = exchange_chips(pairs, f"rs_chips_{tag}")
    return sum_slots(parts, f"rs_sum_{tag}")


def all_reduce(v, tag):
    return sum_slots(all_gather(v, f"ar_gather_{tag}"), f"ar_sum_{tag}")


def adamw(w, g, m, v, name):
    R, C = w.shape
    tr = R
    for cand in (512, 256, 128, 64, 32, 16, 8):
        if R % cand == 0 and cand * C * 4 * 7 * 2 <= 40 * 1024 * 1024:
            tr = cand
            break
    c1 = 1.0 - ADAM_B1 ** ADAM_STEP
    c2 = 1.0 - ADAM_B2 ** ADAM_STEP

    def body(w_ref, g_ref, m_ref, v_ref, d_ref, nm_ref, nv_ref):
        gv = g_ref[...]
        nm = ADAM_B1 * m_ref[...] + (1.0 - ADAM_B1) * gv
        nv = ADAM_B2 * v_ref[...] + (1.0 - ADAM_B2) * (gv * gv)
        d_ref[...] = -ADAM_LR * ((nm / c1) / (jnp.sqrt(nv / c2) + ADAM_EPS) + ADAM_WD * w_ref[...])
        nm_ref[...] = nm
        nv_ref[...] = nv

    spec = pl.BlockSpec((tr, C), lambda i: (i, 0))
    return pl.pallas_call(
        body, name=name, out_shape=[jax.ShapeDtypeStruct((R, C), F32)] * 3, grid=(R // tr,),
        in_specs=[spec] * 4, out_specs=[spec] * 3,
        compiler_params=_params(dimension_semantics=("arbitrary",)),
    )(w, g, m, v)


NAMES = ['mix_norm', 'ffn_norm', 'ffn_w_gu', 'ffn_w_down', 'sb_w_in', 'sb_q_norm', 'sb_k_norm', 'sb_w_out',
         'gdn_w_in', 'gdn_conv_w', 'gdn_a_log', 'gdn_dt_bias', 'gdn_o_norm', 'gdn_w_out', 'dsw_w_in', 'dsw_q_norm',
         'dsw_k_norm', 'dsw_w_out', 'lru_w_in', 'lru_conv_w', 'lru_conv_b', 'lru_w_a', 'lru_b_a', 'lru_w_x',
         'lru_b_x', 'lru_lambda', 'lru_w_out']
REPLICATED = ['mix_norm', 'ffn_norm', 'sb_q_norm', 'sb_k_norm', 'gdn_a_log', 'gdn_dt_bias', 'gdn_o_norm',
              'dsw_q_norm', 'dsw_k_norm']
SMALL_SHARDED = ['gdn_conv_w', 'lru_conv_w', 'lru_conv_b', 'lru_b_a', 'lru_b_x', 'lru_lambda']
IN_T = ['sb_w_in', 'gdn_w_in', 'dsw_w_in', 'lru_w_in']
OUT_N = ['sb_w_out', 'gdn_w_out', 'dsw_w_out', 'lru_w_out']


def _pack(arrs, pad_rows_to=8):
    flat = jnp.concatenate([a.reshape(-1) for a in arrs])
    n = flat.shape[0]
    rows = -(-n // LANES)
    rows = -(-rows // pad_rows_to) * pad_rows_to
    return jnp.pad(flat, (0, rows * LANES - n)).reshape(rows, LANES)


def _unpack(buf, shapes):
    flat = buf.reshape(-1)
    out, o = [], 0
    for s in shapes:
        n = math.prod(s)
        out.append(flat[o:o + n].reshape(s))
        o += n
    return out


def kernel(x, positions, mix_norm, ffn_norm, ffn_w_gu, ffn_w_down, sb_w_in, sb_q_norm, sb_k_norm, sb_w_out, gdn_w_in, gdn_conv_w, gdn_a_log, gdn_dt_bias, gdn_o_norm, gdn_w_out, dsw_w_in, dsw_q_norm, dsw_k_norm, dsw_w_out, lru_w_in, lru_conv_w, lru_conv_b, lru_w_a, lru_b_a, lru_w_x, lru_b_x, lru_lambda, lru_w_out, loss_target, m_mix_norm, m_ffn_norm, m_ffn_w_gu, m_ffn_w_down, m_sb_w_in, m_sb_q_norm, m_sb_k_norm, m_sb_w_out, m_gdn_w_in, m_gdn_conv_w, m_gdn_a_log, m_gdn_dt_bias, m_gdn_o_norm, m_gdn_w_out, m_dsw_w_in, m_dsw_q_norm, m_dsw_k_norm, m_dsw_w_out, m_lru_w_in, m_lru_conv_w, m_lru_conv_b, m_lru_w_a, m_lru_b_a, m_lru_w_x, m_lru_b_x, m_lru_lambda, m_lru_w_out, v_mix_norm, v_ffn_norm, v_ffn_w_gu, v_ffn_w_down, v_sb_w_in, v_sb_q_norm, v_sb_k_norm, v_sb_w_out, v_gdn_w_in, v_gdn_conv_w, v_gdn_a_log, v_gdn_dt_bias, v_gdn_o_norm, v_gdn_w_out, v_dsw_w_in, v_dsw_q_norm, v_dsw_k_norm, v_dsw_w_out, v_lru_w_in, v_lru_conv_w, v_lru_conv_b, v_lru_w_a, v_lru_b_a, v_lru_w_x, v_lru_b_x, v_lru_lambda, v_lru_w_out):
    args = locals()
    W = {n: args[n] for n in NAMES}
    M = {n: args["m_" + n] for n in NAMES}
    V = {n: args["v_" + n] for n in NAMES}
    T = x.shape[1]
    x2 = x[0]
    tgt = loss_target[0]

    w = {}
    for n in IN_T:
        loc = W[n][0].T.astype(BF16)
        w[n.replace("_w_in", "_in_t")] = all_gather(loc, f"ag_{n}").reshape(-1, D_MODEL)
    for n in OUT_N:
        w[n.replace("_w_out", "_out")] = all_gather(W[n][0].astype(BF16), f"ag_{n}").reshape(-1, D_MODEL)
    w["ffn_gu_t"] = [all_gather(ffn_w_gu[i].T.astype(BF16), f"ag_ffn_gu{i}").reshape(-1, D_MODEL) for i in range(DEPTH)]
    w["ffn_down"] = [all_gather(ffn_w_down[i].astype(BF16), f"ag_ffn_down{i}").reshape(-1, D_MODEL) for i in range(DEPTH)]
    gates = jnp.concatenate([lru_w_a[0].reshape(-1, LRU_BLOCK_DIM), lru_w_x[0].reshape(-1, LRU_BLOCK_DIM)], axis=0)
    gg = all_gather(gates.astype(BF16), "ag_lru_gates").reshape(N_DEV, 2, LRU_BLOCKS, 32, LRU_BLOCK_DIM)
    gg = gg.transpose(1, 2, 0, 3, 4).reshape(2, LRU_BLOCKS, LRU_BLOCK_DIM, LRU_BLOCK_DIM).astype(F32)
    w["lru_w_a"], w["lru_w_x"] = gg[0], gg[1]
    small_shapes = [W[n].shape for n in SMALL_SHARDED]
    sm = all_gather(_pack([W[n] for n in SMALL_SHARDED]), "ag_small")
    sm = [jnp.stack(parts) for parts in zip(*[_unpack(sm[p], small_shapes) for p in range(N_DEV)])]
    smd = dict(zip(SMALL_SHARDED, sm))
    w["gdn_conv_w"] = smd["gdn_conv_w"][:, 0].transpose(1, 0, 2).reshape(4, -1)
    w["lru_conv_w"] = smd["lru_conv_w"][:, 0].transpose(1, 0, 2).reshape(4, -1)
    w["lru_conv_b"] = smd["lru_conv_b"][:, 0].reshape(1, -1)
    w["lru_lambda"] = smd["lru_lambda"][:, 0].reshape(1, -1)
    w["lru_b_a"] = smd["lru_b_a"][:, 0].transpose(1, 0, 2).reshape(1, -1)
    w["lru_b_x"] = smd["lru_b_x"][:, 0].transpose(1, 0, 2).reshape(1, -1)
    for n in REPLICATED:
        w[n] = W[n]

    half = ROPE_DIM // 2
    inv_freq = ROPE_THETA ** (-jnp.arange(half, dtype=F32) / half)
    ang = positions[0].astype(F32)[:, None] * inv_freq
    cs, sn = jnp.cos(ang), jnp.sin(ang)
    cosm = jnp.concatenate([cs, cs, jnp.ones((T, HEAD_DIM - ROPE_DIM), F32)], axis=1)
    sinm = jnp.concatenate([-sn, sn, jnp.zeros((T, HEAD_DIM - ROPE_DIM), F32)], axis=1)

    (xl, fl), vjp = jax.vjp(lambda xx, ww: trunk(xx, cosm, sinm, ww), x2, w)
    loss_part, dy = loss_head(xl, fl, tgt)
    gx, gw = vjp((dy, dy))
    loss = lax.psum(loss_part, ("x", "y", "c"))

    G = {}
    for n in IN_T:
        r = W[n].shape[2]
        g = gw[n.replace("_w_in", "_in_t")].reshape(N_DEV, r, D_MODEL)
        G[n] = reduce_scatter(g, n).T[None]
    for n in OUT_N:
        r = W[n].shape[1]
        g = gw[n.replace("_w_out", "_out")].reshape(N_DEV, r, D_MODEL)
        G[n] = reduce_scatter(g, n)[None]
    r = ffn_w_gu.shape[2]
    G["ffn_w_gu"] = jnp.stack([reduce_scatter(gw["ffn_gu_t"][i].reshape(N_DEV, r, D_MODEL), f"ffn_gu{i}").T
                               for i in range(DEPTH)])
    r = ffn_w_down.shape[1]
    G["ffn_w_down"] = jnp.stack([reduce_scatter(gw["ffn_down"][i].reshape(N_DEV, r, D_MODEL), f"ffn_down{i}")
                                 for i in range(DEPTH)])
    gg = jnp.stack([gw["lru_w_a"], gw["lru_w_x"]]).reshape(2, LRU_BLOCKS, N_DEV, 32, LRU_BLOCK_DIM)
    gg = gg.transpose(2, 0, 1, 3, 4).reshape(N_DEV, 2 * LRU_BLOCKS * 32, LRU_BLOCK_DIM)
    gg = reduce_scatter(gg, "lru_gates").reshape(2, 1, LRU_BLOCKS, 32, LRU_BLOCK_DIM)
    G["lru_w_a"], G["lru_w_x"] = gg[0], gg[1]
    gs = {
        "gdn_conv_w": gw["gdn_conv_w"].reshape(4, N_DEV, -1).transpose(1, 0, 2)[:, None],
        "lru_conv_w": gw["lru_conv_w"].reshape(4, N_DEV, -1).transpose(1, 0, 2)[:, None],
        "lru_conv_b": gw["lru_conv_b"].reshape(N_DEV, 1, -1),
        "lru_lambda": gw["lru_lambda"].reshape(N_DEV, 1, -1),
        "lru_b_a": gw["lru_b_a"].reshape(LRU_BLOCKS, N_DEV, 32).transpose(1, 0, 2)[:, None],
        "lru_b_x": gw["lru_b_x"].reshape(LRU_BLOCKS, N_DEV, 32).transpose(1, 0, 2)[:, None],
    }
    packed = jnp.stack([_pack([gs[n][p] for n in SMALL_SHARDED]) for p in range(N_DEV)])
    for n, g in zip(SMALL_SHARDED, _unpack(reduce_scatter(packed, "small"), small_shapes)):
        G[n] = g
    rep_shapes = [W[n].shape for n in REPLICATED]
    for n, g in zip(REPLICATED, _unpack(all_reduce(_pack([gw[n] for n in REPLICATED]), "rep"), rep_shapes)):
        G[n] = g

    D, NM, NV = {}, {}, {}
    big = [n for n in NAMES if n not in REPLICATED and n not in SMALL_SHARDED]
    for n in big:
        shp = W[n].shape
        two = (-1, shp[-1])
        d, nm, nv = adamw(W[n].reshape(two), G[n].reshape(two), M[n].reshape(two), V[n].reshape(two), f"adamw_{n}")
        D[n], NM[n], NV[n] = d.reshape(shp), nm.reshape(shp), nv.reshape(shp)
    for group, tag in ((SMALL_SHARDED, "small"), (REPLICATED, "rep")):
        shapes = [W[n].shape for n in group]
        res = adamw(_pack([W[n] for n in group]), _pack([G[n] for n in group]), _pack([M[n] for n in group]),
                    _pack([V[n] for n in group]), f"adamw_{tag}")
        for dst, buf in zip((D, NM, NV), res):
            for n, a in zip(group, _unpack(buf, shapes)):
                dst[n] = a

    return (loss, gx[None], *[G[n] for n in NAMES], *[D[n] for n in NAMES], *[NM[n] for n in NAMES],
            *[NV[n] for n in NAMES])
```

```python
import functools
import math

import jax
import jax.numpy as jnp
from jax import lax
from jax.experimental import pallas as pl
from jax.experimental.pallas import tpu as pltpu

F32 = jnp.float32
BF16 = jnp.bfloat16

D_MODEL = 2048
HEAD_DIM = 128
NORM_EPS = 1e-6
SB_HEADS = 16
SB_BLOCK = 256
GDN_K_HEADS = 16
GDN_V_HEADS = 32
GDN_KEY_DIM = 2048
GDN_VAL_DIM = 4096
GDN_CHUNK = 64
GDN_HEADS_PER_STEP = 4
DSW_GROUPS = ((128, 1), (512, 4), (2048, 16))
DSW_HG = 6
DSW_HEADS = 18
DSW_BLOCK = 128
ROPE_DIM = 32
ROPE_THETA = 500000.0
LRU_WIDTH = 2048
LRU_BLOCKS = 8
LRU_BLOCK_DIM = 256
LRU_C = 8.0
FFN_HIDDEN = 5632
DEPTH = 4
ADAM_LR, ADAM_B1, ADAM_B2, ADAM_EPS, ADAM_WD, ADAM_STEP = 0.001, 0.9, 0.999, 1e-08, 0.01, 10
N_DEV = 8
N_CHIPS = 4

V7X_VMEM_LIMIT = 56 * 1024 * 1024
LANES = 128
MESH = pl.DeviceIdType.MESH


def _params(**kw):
    return pltpu.CompilerParams(vmem_limit_bytes=V7X_VMEM_LIMIT, **kw)


def _pick(n, cands):
    for c in cands:
        if n % c == 0:
            return c
    return n


def _tile(n, cap, unit=LANES):
    best = None
    for t in range(unit, min(n, cap) + 1, unit):
        if n % t == 0:
            best = t
    return best or n


_DN = {"nn": (((1,), (0,)), ((), ())), "nt": (((1,), (1,)), ((), ())), "tn": (((0,), (0,)), ((), ()))}


def _raw_dot(a, b, mode):
    return lax.dot_general(a.astype(BF16), b.astype(BF16), _DN[mode], preferred_element_type=F32)


@functools.partial(jax.custom_vjp, nondiff_argnums=(2,))
def bdot(a, b, mode):
    return _raw_dot(a, b, mode)


def _bdot_fwd(a, b, mode):
    return _raw_dot(a, b, mode), (a, b)


def _bdot_bwd(mode, res, ct):
    a, b = res
    if mode == "nn":
        return bdot(ct, b, "nt"), bdot(a, ct, "tn")
    if mode == "nt":
        return bdot(ct, b, "nn"), bdot(ct, a, "tn")
    return bdot(b, ct, "nt"), bdot(a, ct, "nn")


bdot.defvjp(_bdot_fwd, _bdot_bwd)


def _split2(x):
    hi = x.astype(BF16)
    lo = (x - hi.astype(F32)).astype(BF16)
    return hi, lo


def _dot01_raw(x, m, dn, left):
    hi, lo = _split2(x)
    if left:
        return (lax.dot_general(m, hi, dn, preferred_element_type=F32)
                + lax.dot_general(m, lo, dn, preferred_element_type=F32))
    return (lax.dot_general(hi, m, dn, preferred_element_type=F32)
            + lax.dot_general(lo, m, dn, preferred_element_type=F32))


@functools.partial(jax.custom_vjp, nondiff_argnums=(2,))
def dot01(x, m, left):
    return _dot01_raw(x, m, _DN["nn"], left)


def _dot01_fwd(x, m, left):
    return _dot01_raw(x, m, _DN["nn"], left), m


def _dot01_bwd(left, m, ct):
    dx = _dot01_raw(ct, m, _DN["tn"] if left else _DN["nt"], left)
    return dx, jnp.zeros_like(m)


dot01.defvjp(_dot01_fwd, _dot01_bwd)


def _mm3(a, b):
    ah, al = _split2(a)
    bh, bl = _split2(b)
    dn = _DN["nn"]
    return (lax.dot_general(ah, bh, dn, preferred_element_type=F32)
            + lax.dot_general(ah, bl, dn, preferred_element_type=F32)
            + lax.dot_general(al, bh, dn, preferred_element_type=F32))


@jax.custom_vjp
def mm3(a, b):
    return _mm3(a, b)


def _mm3_fwd(a, b):
    return _mm3(a, b), (a, b)


def _mm3_bwd(res, ct):
    a, b = res
    return _mm3(ct, b.T), _mm3(a.T, ct)


mm3.defvjp(_mm3_fwd, _mm3_bwd)


def _softplus_parts(z):
    e = jnp.exp(-jnp.abs(z))
    l = jnp.log1p(e)
    sp = jnp.maximum(z, 0.0) + l
    ls = jnp.minimum(z, 0.0) - l
    inv = 1.0 / (1.0 + e)
    sig = jnp.where(z >= 0.0, inv, e * inv)
    return sp, ls, sig


def _rms(x, g):
    return x * lax.rsqrt(jnp.mean(x * x, axis=-1, keepdims=True) + NORM_EPS) * g


def _swap16(x):
    lane = lax.broadcasted_iota(jnp.int32, x.shape, 1)
    return jnp.where(lane < 16, pltpu.roll(x, 112, axis=1), jnp.where(lane < 32, pltpu.roll(x, 16, axis=1), 0.0))


@jax.custom_vjp
def rope(x, cosm, sinm):
    return x * cosm + _swap16(x) * sinm


def _rope_fwd(x, cosm, sinm):
    return rope(x, cosm, sinm), (cosm, sinm)


def _rope_bwd(res, ct):
    cosm, sinm = res
    return ct * cosm + _swap16(ct * sinm), jnp.zeros_like(cosm), jnp.zeros_like(sinm)


rope.defvjp(_rope_fwd, _rope_bwd)


def mm(a, b, mode, out_dtype=F32):
    if mode == "nt":
        (M, K), N = a.shape, b.shape[0]
    elif mode == "nn":
        (M, K), N = a.shape, b.shape[1]
    else:
        (K, M), N = a.shape, b.shape[1]
    if mode == "nt":
        tm, tn, tk = _tile(M, 1024), _tile(N, 768), _tile(K, 2048)
    elif mode == "nn":
        tm, tn, tk = _tile(M, 1024), _tile(N, 2048), _tile(K, 1024)
    else:
        tm, tn, tk = _tile(M, 1408), _tile(N, 2048), _tile(K, 512)
    nk = K // tk
    if mode == "nt":
        a_spec = pl.BlockSpec((tm, tk), lambda i, j, k: (i, k))
        b_spec = pl.BlockSpec((tn, tk), lambda i, j, k: (j, k))
    elif mode == "nn":
        a_spec = pl.BlockSpec((tm, tk), lambda i, j, k: (i, k))
        b_spec = pl.BlockSpec((tk, tn), lambda i, j, k: (k, j))
    else:
        a_spec = pl.BlockSpec((tk, tm), lambda i, j, k: (k, i))
        b_spec = pl.BlockSpec((tk, tn), lambda i, j, k: (k, j))

    def body(a_ref, b_ref, o_ref, *scr):
        p = _raw_dot(a_ref[...], b_ref[...], mode)
        if nk == 1:
            o_ref[...] = p.astype(o_ref.dtype)
        else:
            acc = scr[0]
            k = pl.program_id(2)

            @pl.when(k == 0)
            def _():
                acc[...] = p

            @pl.when(k > 0)
            def _():
                acc[...] += p

            @pl.when(k == nk - 1)
            def _():
                o_ref[...] = acc[...].astype(o_ref.dtype)

    return pl.pallas_call(
        body, name=f"mm_{mode}_{M}x{N}x{K}", out_shape=jax.ShapeDtypeStruct((M, N), out_dtype),
        grid=(M // tm, N // tn, nk), in_specs=[a_spec, b_spec],
        out_specs=pl.BlockSpec((tm, tn), lambda i, j, k: (i, j)),
        scratch_shapes=[] if nk == 1 else [pltpu.VMEM((tm, tn), F32)],
        compiler_params=_params(dimension_semantics=("arbitrary", "arbitrary", "arbitrary")),
    )(a, b)


@jax.custom_vjp
def lin_t(x, wt):
    return mm(x, wt, "nt")


def _lin_t_fwd(x, wt):
    return mm(x, wt, "nt"), (x, wt)


def _lin_t_bwd(res, dy):
    x, wt = res
    return mm(dy, wt, "nn"), mm(dy, x, "tn", out_dtype=wt.dtype)


lin_t.defvjp(_lin_t_fwd, _lin_t_bwd)


@jax.custom_vjp
def lin_n(x, w):
    return mm(x, w, "nn")


def _lin_n_fwd(x, w):
    return mm(x, w, "nn"), (x, w)


def _lin_n_bwd(res, dy):
    x, w = res
    return mm(dy, w, "nt"), mm(x, dy, "tn", out_dtype=w.dtype)


lin_n.defvjp(_lin_n_fwd, _lin_n_bwd)


def _full_spec(shape):
    nd = len(shape)
    return pl.BlockSpec(tuple(shape), lambda i: (0,) * nd)


def _row_spec(tr, c):
    return pl.BlockSpec((tr, c), lambda i: (i, 0))


def _rowop_tr(total_cols, T):
    budget = 20 * 1024 * 1024
    for tr in (512, 256, 128, 64, 32, 16, 8):
        if T % tr == 0 and total_cols * tr * 4 * 2 <= budget:
            return tr
    return 8


def rowop(fn, name, rows, consts, params, out_cols):
    nr, nc, npar, nout = len(rows), len(consts), len(params), len(out_cols)
    T = rows[0].shape[0]
    in_cols = [r.shape[1] for r in rows] + [c.shape[1] for c in consts]
    tr_f = _rowop_tr(sum(in_cols) + sum(out_cols), T)
    tr_b = _rowop_tr(sum(in_cols) + sum(out_cols) + sum(r.shape[1] for r in rows), T)

    def fwd_call(rows, consts, params):
        def body(*refs):
            ins = [r[...] for r in refs[:nr + nc + npar]]
            outs = fn(*ins)
            for o_ref, o in zip(refs[nr + nc + npar:], outs):
                o_ref[...] = o.astype(F32)

        return pl.pallas_call(
            body, name=name + "_fwd", grid=(T // tr_f,),
            out_shape=[jax.ShapeDtypeStruct((T, c), F32) for c in out_cols],
            in_specs=[_row_spec(tr_f, c) for c in in_cols] + [_full_spec(p.shape) for p in params],
            out_specs=[_row_spec(tr_f, c) for c in out_cols],
            compiler_params=_params(dimension_semantics=("arbitrary",)),
        )(*rows, *consts, *params)

    def bwd_call(rows, consts, params, douts):
        def body(*refs):
            i = pl.program_id(0)
            rv = [r[...] for r in refs[:nr]]
            cv = [r[...] for r in refs[nr:nr + nc]]
            pv = [r[...] for r in refs[nr + nc:nr + nc + npar]]
            dv = [r[...] for r in refs[nr + nc + npar:nr + nc + npar + nout]]
            orefs = refs[nr + nc + npar + nout:]
            _, vjp = jax.vjp(lambda rr, pp: tuple(fn(*rr, *cv, *pp)), rv, pv)
            drows, dpars = vjp(tuple(dv))
            for o_ref, g in zip(orefs[:nr], drows):
                o_ref[...] = g.astype(F32)

            @pl.when(i == 0)
            def _():
                for o_ref in orefs[nr:]:
                    o_ref[...] = jnp.zeros(o_ref.shape, F32)

            for o_ref, g in zip(orefs[nr:], dpars):
                o_ref[...] += g.astype(F32)

        res = pl.pallas_call(
            body, name=name + "_bwd", grid=(T // tr_b,),
            out_shape=[jax.ShapeDtypeStruct(r.shape, F32) for r in rows]
            + [jax.ShapeDtypeStruct(p.shape, F32) for p in params],
            in_specs=[_row_spec(tr_b, c) for c in in_cols] + [_full_spec(p.shape) for p in params]
            + [_row_spec(tr_b, c) for c in out_cols],
            out_specs=[_row_spec(tr_b, r.shape[1]) for r in rows] + [_full_spec(p.shape) for p in params],
            compiler_params=_params(dimension_semantics=("arbitrary",)),
        )(*rows, *consts, *params, *douts)
        return tuple(res[:nr]), tuple(res[nr:])

    @jax.custom_vjp
    def op(rows, consts, params):
        return tuple(fwd_call(rows, consts, params))

    def op_fwd(rows, consts, params):
        return tuple(fwd_call(rows, consts, params)), (rows, consts, params)

    def op_bwd(res, douts):
        rows, consts, params = res
        drows, dpars = bwd_call(rows, consts, params, douts)
        return drows, tuple(jnp.zeros_like(c) for c in consts), dpars

    op.defvjp(op_fwd, op_bwd)
    return op(tuple(rows), tuple(consts), tuple(params))


def _fn_norm(x, g):
    return (_rms(x, g),)


def _fn_add_norm(x, y, g):
    s = x + y
    return s, _rms(s, g)


def _fn_swiglu(gu):
    return (jax.nn.silu(gu[:, :FFN_HIDDEN]) * gu[:, FFN_HIDDEN:],)


def _heads(x, n, width=HEAD_DIM):
    return [x[:, h * width:(h + 1) * width] for h in range(n)]


def _fn_sb_pre(qkv, qn, kn):
    hs = _heads(qkv, 3 * SB_HEADS)
    q = jnp.concatenate([_rms(h, qn) for h in hs[:SB_HEADS]], axis=1)
    k = jnp.concatenate([_rms(h, kn) for h in hs[SB_HEADS:2 * SB_HEADS]], axis=1)
    v = jnp.concatenate(hs[2 * SB_HEADS:], axis=1)
    return q, k, v


def _fn_dsw_pre(qkv, cosm, sinm, qn, kn):
    hs = _heads(qkv, 3 * DSW_HEADS)
    q = jnp.concatenate([rope(_rms(h, qn), cosm, sinm) for h in hs[:DSW_HEADS]], axis=1)
    k = jnp.concatenate([rope(_rms(h, kn), cosm, sinm) for h in hs[DSW_HEADS:2 * DSW_HEADS]], axis=1)
    v = jnp.concatenate(hs[2 * DSW_HEADS:], axis=1)
    return q, k, v


def _fn_dsw_combine(o, lse):
    os_, ls_ = _heads(o, DSW_HEADS), _heads(lse, DSW_HEADS)
    out = [None] * DSW_HEADS
    for hg in range(DSW_HG):
        l3 = [ls_[g * DSW_HG + hg] for g in range(3)]
        m = jnp.maximum(jnp.maximum(l3[0], l3[1]), l3[2])
        e3 = [jnp.exp(l - m) for l in l3]
        den = e3[0] + e3[1] + e3[2]
        for g in range(3):
            out[g * DSW_HG + hg] = os_[g * DSW_HG + hg] * (e3[g] / den)
    return (jnp.concatenate(out, axis=1),)


def _l2(x):
    return x * lax.rsqrt(jnp.sum(x * x, axis=-1, keepdims=True) + NORM_EPS)


def _fn_gdn_pre(qkv, ba, a_log, dt_bias):
    x = jax.nn.silu(qkv)
    hs = _heads(x, 2 * GDN_K_HEADS + GDN_V_HEADS)
    rep = GDN_V_HEADS // GDN_K_HEADS
    qh = [_l2(h) * HEAD_DIM ** -0.5 for h in hs[:GDN_K_HEADS]]
    kh = [_l2(h) for h in hs[GDN_K_HEADS:2 * GDN_K_HEADS]]
    q = jnp.concatenate([qh[h // rep] for h in range(GDN_V_HEADS)], axis=1)
    k = jnp.concatenate([kh[h // rep] for h in range(GDN_V_HEADS)], axis=1)
    v = jnp.concatenate(hs[2 * GDN_K_HEADS:], axis=1)
    b = ba[:, :GDN_V_HEADS]
    a = ba[:, GDN_V_HEADS:2 * GDN_V_HEADS]
    beta = jax.nn.sigmoid(b)
    g = -jnp.exp(a_log) * jax.nn.softplus(a + dt_bias)
    rows = b.shape[0]
    beta_b = jnp.concatenate([jnp.broadcast_to(beta[:, h:h + 1], (rows, HEAD_DIM)) for h in range(GDN_V_HEADS)], axis=1)
    g_b = jnp.concatenate([jnp.broadcast_to(g[:, h:h + 1], (rows, HEAD_DIM)) for h in range(GDN_V_HEADS)], axis=1)
    return q, k, v, g_b, beta_b


def _fn_gdn_post(o, z, o_norm):
    os_, zs = _heads(o, GDN_V_HEADS), _heads(z, GDN_V_HEADS)
    return (jnp.concatenate([_rms(oh, o_norm) * jax.nn.silu(zh) for oh, zh in zip(os_, zs)], axis=1),)


def _expm1(x):
    return jnp.tanh(0.5 * x) * (jnp.exp(x) + 1.0)


def _fn_lru_gates(xc, conv_b, w_a, b_a, w_x, b_x, lam):
    xr = xc + conv_b
    xs = _heads(xr, LRU_BLOCKS, LRU_BLOCK_DIM)
    r = jnp.concatenate([bdot(xs[n], w_a[n], "nn") for n in range(LRU_BLOCKS)], axis=1) + b_a
    i = jnp.concatenate([bdot(xs[n], w_x[n], "nn") for n in range(LRU_BLOCKS)], axis=1) + b_x
    r = jax.nn.sigmoid(r)
    i = jax.nn.sigmoid(i)
    log_a = -LRU_C * r * jax.nn.softplus(-lam)
    a = jnp.exp(log_a)
    u = jnp.sqrt(-_expm1(2.0 * log_a)) * (i * xr)
    return a, u


def _fn_lru_out(hs, gate):
    c = math.sqrt(2.0 / math.pi)
    gl = 0.5 * gate * (1.0 + jnp.tanh(c * (gate + 0.044715 * (gate * gate * gate))))
    return (hs * gl,)


def loss_head(x, f, target):
    T, D = x.shape
    tr = _pick(T, (256, 128, 64, 32, 16, 8))

    def body(x_ref, f_ref, t_ref, l_ref, dy_ref):
        i = pl.program_id(0)
        err = (x_ref[...] + f_ref[...]) - t_ref[...]
        dy_ref[...] = err * (1.0 / D)
        part = 0.5 * jnp.sum(jnp.mean(err * err, axis=-1, keepdims=True), axis=0, keepdims=True)

        @pl.when(i == 0)
        def _():
            l_ref[...] = jnp.zeros(l_ref.shape, F32)

        l_ref[...] += jnp.broadcast_to(part, l_ref.shape)

    l, dy = pl.pallas_call(
        body, name="loss_head", grid=(T // tr,),
        out_shape=[jax.ShapeDtypeStruct((8, LANES), F32), jax.ShapeDtypeStruct((T, D), F32)],
        in_specs=[_row_spec(tr, D)] * 3, out_specs=[_full_spec((8, LANES)), _row_spec(tr, D)],
        compiler_params=_params(dimension_semantics=("arbitrary",)),
    )(x, f, target)
    return l[0, 0], dy


def _shift_rows(x, s):
    if s == 0:
        return x
    n = x.shape[0]
    row = lax.broadcasted_iota(jnp.int32, x.shape, 0)
    rolled = pltpu.roll(x, s % n, axis=0)
    keep = (row >= s) if s > 0 else (row < n + s)
    return jnp.where(keep, rolled, 0.0)


def _conv_fwd_call(x, w):
    T, C = x.shape
    K = w.shape[0]
    cb = _pick(C, (256, 128))

    def body(x_ref, w_ref, y_ref):
        xv = x_ref[...]
        acc = xv * w_ref[K - 1:K, :]
        for k in range(K - 1):
            acc = acc + _shift_rows(xv, K - 1 - k) * w_ref[k:k + 1, :]
        y_ref[...] = acc

    return pl.pallas_call(
        body, name=f"conv_fwd_{C}", grid=(C // cb,), out_shape=jax.ShapeDtypeStruct((T, C), F32),
        in_specs=[pl.BlockSpec((T, cb), lambda j: (0, j)), pl.BlockSpec((K, cb), lambda j: (0, j))],
        out_specs=pl.BlockSpec((T, cb), lambda j: (0, j)),
        compiler_params=_params(dimension_semantics=("arbitrary",)),
    )(x, w)


def _conv_bwd_call(x, w, dy):
    T, C = x.shape
    K = w.shape[0]
    cb = _pick(C, (256, 128))

    def body(x_ref, w_ref, dy_ref, dx_ref, dw_ref):
        xv, dv = x_ref[...], dy_ref[...]
        acc = dv * w_ref[K - 1:K, :]
        rows = [None] * K
        rows[K - 1] = jnp.sum(dv * xv, axis=0, keepdims=True)
        for k in range(K - 1):
            s = K - 1 - k
            acc = acc + _shift_rows(dv, -s) * w_ref[k:k + 1, :]
            rows[k] = jnp.sum(dv * _shift_rows(xv, s), axis=0, keepdims=True)
        dx_ref[...] = acc
        dw_ref[...] = jnp.concatenate(rows + [jnp.zeros((8 - K, cb), F32)], axis=0)

    dx, dw = pl.pallas_call(
        body, name=f"conv_bwd_{C}", grid=(C // cb,),
        out_shape=[jax.ShapeDtypeStruct((T, C), F32), jax.ShapeDtypeStruct((8, C), F32)],
        in_specs=[pl.BlockSpec((T, cb), lambda j: (0, j)), pl.BlockSpec((K, cb), lambda j: (0, j)),
                  pl.BlockSpec((T, cb), lambda j: (0, j))],
        out_specs=[pl.BlockSpec((T, cb), lambda j: (0, j)), pl.BlockSpec((8, cb), lambda j: (0, j))],
        compiler_params=_params(dimension_semantics=("arbitrary",)),
    )(x, w, dy)
    return dx, dw[:K]


@jax.custom_vjp
def dwconv(x, w):
    return _conv_fwd_call(x, w)


def _dwconv_fwd(x, w):
    return _conv_fwd_call(x, w), (x, w)


def _dwconv_bwd(res, dy):
    return _conv_bwd_call(*res, dy)


dwconv.defvjp(_dwconv_fwd, _dwconv_bwd)


def _scan_fwd_call(a, u):
    T, C = a.shape
    cb = _pick(C, (256, 128))

    def body(a_ref, u_ref, h_ref):
        def step(i, h):
            r = pl.multiple_of(i * 8, 8)
            at, ut = a_ref[pl.ds(r, 8), :], u_ref[pl.ds(r, 8), :]
            rows = []
            for j in range(8):
                h = at[j:j + 1, :] * h + ut[j:j + 1, :]
                rows.append(h)
            h_ref[pl.ds(r, 8), :] = jnp.concatenate(rows, axis=0)
            return h

        lax.fori_loop(0, T // 8, step, jnp.zeros((1, cb), F32))

    return pl.pallas_call(
        body, name="lru_scan_fwd", grid=(C // cb,), out_shape=jax.ShapeDtypeStruct((T, C), F32),
        in_specs=[pl.BlockSpec((T, cb), lambda j: (0, j))] * 2, out_specs=pl.BlockSpec((T, cb), lambda j: (0, j)),
        compiler_params=_params(dimension_semantics=("arbitrary",)),
    )(a, u)


def _scan_bwd_call(a, hs, dh):
    T, C = a.shape
    cb = _pick(C, (256, 128))
    nt = T // 8

    def body(a_ref, h_ref, dh_ref, da_ref, du_ref):
        def step(s, carry):
            i = nt - 1 - s
            r = pl.multiple_of(i * 8, 8)
            rp = pl.multiple_of(jnp.maximum(i - 1, 0) * 8, 8)
            at, ht, dt = a_ref[pl.ds(r, 8), :], h_ref[pl.ds(r, 8), :], dh_ref[pl.ds(r, 8), :]
            hprev_tile = h_ref[pl.ds(rp, 8), :]
            h_before = jnp.where(i > 0, hprev_tile[7:8, :], 0.0)
            da_rows, du_rows = [None] * 8, [None] * 8
            for j in range(7, -1, -1):
                lam = dt[j:j + 1, :] + carry
                du_rows[j] = lam
                hp = ht[j - 1:j, :] if j > 0 else h_before
                da_rows[j] = lam * hp
                carry = at[j:j + 1, :] * lam
            da_ref[pl.ds(r, 8), :] = jnp.concatenate(da_rows, axis=0)
            du_ref[pl.ds(r, 8), :] = jnp.concatenate(du_rows, axis=0)
            return carry

        lax.fori_loop(0, nt, step, jnp.zeros((1, cb), F32))

    return pl.pallas_call(
        body, name="lru_scan_bwd", grid=(C // cb,), out_shape=[jax.ShapeDtypeStruct((T, C), F32)] * 2,
        in_specs=[pl.BlockSpec((T, cb), lambda j: (0, j))] * 3,
        out_specs=[pl.BlockSpec((T, cb), lambda j: (0, j))] * 2,
        compiler_params=_params(dimension_semantics=("arbitrary",)),
    )(a, hs, dh)


@jax.custom_vjp
def lru_scan(a, u):
    return _scan_fwd_call(a, u)


def _lru_scan_fwd(a, u):
    hs = _scan_fwd_call(a, u)
    return hs, (a, hs)


def _lru_scan_bwd(res, dh):
    a, hs = res
    return tuple(_scan_bwd_call(a, hs, dh))


lru_scan.defvjp(_lru_scan_fwd, _lru_scan_bwd)


def _tri(n, kind):
    r = lax.broadcasted_iota(jnp.int32, (n, n), 0)
    c = lax.broadcasted_iota(jnp.int32, (n, n), 1)
    m = {"gt": r > c, "le": r <= c, "lt": r < c, "ge": r >= c, "eq": r == c}[kind]
    return jnp.where(m, 1.0, 0.0).astype(BF16)


def _sb_fwd_call(q, k, v):
    T, HD = q.shape
    H = HD // HEAD_DIM
    tb = _pick(T, (SB_BLOCK, 128))
    scale = HEAD_DIM ** -0.5

    def body(q_ref, k_ref, v_ref, o_ref, tot_ref):
        i = pl.program_id(1)
        qb = q_ref[...].astype(BF16)
        u_gt = _tri(tb, "gt")
        row = lax.broadcasted_iota(jnp.int32, (tb, tb), 0)
        col = lax.broadcasted_iota(jnp.int32, (tb, tb), 1)

        def step(j, carry, diagonal):
            acc, run = carry
            off = pl.multiple_of(j * tb, tb)
            kb = k_ref[pl.ds(off, tb), :].astype(BF16)
            vb = v_ref[pl.ds(off, tb), :].astype(BF16)
            z = lax.dot_general(qb, kb, _DN["nt"], preferred_element_type=F32) * scale
            sp, ls, _ = _softplus_parts(z)
            if diagonal:
                sp = jnp.where(col < row, sp, 0.0)
            between = _dot01_raw(sp, u_gt, _DN["nn"], False) + run
            w = jnp.exp(ls - between)
            if diagonal:
                w = jnp.where(col < row, w, 0.0)
            acc = acc + lax.dot_general(w.astype(BF16), vb, _DN["nn"], preferred_element_type=F32)
            run = run + jnp.sum(sp, axis=1, keepdims=True)
            return acc, run

        carry = step(i, (jnp.zeros((tb, HEAD_DIM), F32), jnp.zeros((tb, 1), F32)), True)
        acc, run = lax.fori_loop(0, i, lambda s, c: step(i - 1 - s, c, False), carry)
        o_ref[...] = acc
        tot_ref[...] = jnp.broadcast_to(run, (tb, HEAD_DIM))

    blk = pl.BlockSpec((tb, HEAD_DIM), lambda h, i: (i, h))
    full = pl.BlockSpec((T, HEAD_DIM), lambda h, i: (0, h))
    return pl.pallas_call(
        body, name="sb_attn_fwd", grid=(H, T // tb), out_shape=[jax.ShapeDtypeStruct((T, HD), F32)] * 2,
        in_specs=[blk, full, full], out_specs=[blk, blk],
        compiler_params=_params(dimension_semantics=("arbitrary", "arbitrary")),
    )(q, k, v)


def _sb_bwd_call(q, k, v, tot, do):
    T, HD = q.shape
    H = HD // HEAD_DIM
    tb = _pick(T, (SB_BLOCK, 128))
    scale = HEAD_DIM ** -0.5

    def body(q_ref, k_ref, v_ref, tot_ref, do_ref, dq_ref, dk_ref, dv_ref):
        i = pl.program_id(1)

        @pl.when(i == 0)
        def _():
            dk_ref[...] = jnp.zeros(dk_ref.shape, F32)
            dv_ref[...] = jnp.zeros(dv_ref.shape, F32)

        qb = q_ref[...].astype(BF16)
        dob = do_ref[...].astype(BF16)
        tot = tot_ref[:, 0:1]
        u_le = _tri(tb, "le")
        u_lt = _tri(tb, "lt")
        row = lax.broadcasted_iota(jnp.int32, (tb, tb), 0)
        col = lax.broadcasted_iota(jnp.int32, (tb, tb), 1)

        def step(j, carry, diagonal):
            dq, cs, cd = carry
            off = pl.multiple_of(j * tb, tb)
            kb = k_ref[pl.ds(off, tb), :].astype(BF16)
            vb = v_ref[pl.ds(off, tb), :].astype(BF16)
            z = lax.dot_general(qb, kb, _DN["nt"], preferred_element_type=F32) * scale
            sp, ls, sig = _softplus_parts(z)
            if diagonal:
                sp = jnp.where(col < row, sp, 0.0)
            prefix = _dot01_raw(sp, u_le, _DN["nn"], False) + cs
            w = jnp.exp(ls - (tot - prefix))
            if diagonal:
                w = jnp.where(col < row, w, 0.0)
            wb = w.astype(BF16)
            dv_ref[pl.ds(off, tb), :] += lax.dot_general(wb, dob, _DN["tn"], preferred_element_type=F32)
            dw = lax.dot_general(dob, vb, _DN["nt"], preferred_element_type=F32)
            dl = dw * w
            dsp = -(_dot01_raw(dl, u_lt, _DN["nn"], False) + cd)
            dz = (dl * (1.0 - sig) + dsp * sig) * scale
            if diagonal:
                dz = jnp.where(col < row, dz, 0.0)
            dzb = dz.astype(BF16)
            dq = dq + lax.dot_general(dzb, kb, _DN["nn"], preferred_element_type=F32)
            dk_ref[pl.ds(off, tb), :] += lax.dot_general(dzb, qb, _DN["tn"], preferred_element_type=F32)
            cs = cs + jnp.sum(sp, axis=1, keepdims=True)
            cd = cd + jnp.sum(dl, axis=1, keepdims=True)
            return dq, cs, cd

        z1 = jnp.zeros((tb, 1), F32)
        carry = lax.fori_loop(0, i, lambda j, c: step(j, c, False), (jnp.zeros((tb, HEAD_DIM), F32), z1, z1))
        dq, _, _ = step(i, carry, True)
        dq_ref[...] = dq

    blk = pl.BlockSpec((tb, HEAD_DIM), lambda h, i: (i, h))
    full = pl.BlockSpec((T, HEAD_DIM), lambda h, i: (0, h))
    return pl.pallas_call(
        body, name="sb_attn_bwd", grid=(H, T // tb), out_shape=[jax.ShapeDtypeStruct((T, HD), F32)] * 3,
        in_specs=[blk, full, full, blk, blk], out_specs=[blk, full, full],
        compiler_params=_params(dimension_semantics=("arbitrary", "arbitrary")),
    )(q, k, v, tot, do)


@jax.custom_vjp
def sb_attn(q, k, v):
    return _sb_fwd_call(q, k, v)[0]


def _sb_attn_fwd(q, k, v):
    o, tot = _sb_fwd_call(q, k, v)
    return o, (q, k, v, tot)


def _sb_attn_bwd(res, do):
    return tuple(_sb_bwd_call(*res, do))


sb_attn.defvjp(_sb_attn_fwd, _sb_attn_bwd)


def _dsw_tile(q, kp, kc, vp, vc, n):
    blk = DSW_BLOCK
    scale = HEAD_DIM ** -0.5
    qi = lax.broadcasted_iota(jnp.int32, (blk, blk), 0)
    kj = lax.broadcasted_iota(jnp.int32, (blk, blk), 1)
    neg = -1e30
    s_p = jnp.where((kj >= qi) & (n > 0), bdot(q, kp, "nt") * scale, neg)
    s_c = jnp.where(kj <= qi, bdot(q, kc, "nt") * scale, neg)
    m = jnp.maximum(jnp.max(s_p, axis=-1, keepdims=True), jnp.max(s_c, axis=-1, keepdims=True))
    p_p, p_c = jnp.exp(s_p - m), jnp.exp(s_c - m)
    den = jnp.sum(p_p, axis=-1, keepdims=True) + jnp.sum(p_c, axis=-1, keepdims=True)
    o = bdot(p_p / den, vp, "nn") + bdot(p_c / den, vc, "nn")
    lse = m + jnp.log(den)
    return o, jnp.broadcast_to(lse, (blk, HEAD_DIM))


def _dsw_specs(nsub):
    cur = pl.BlockSpec((None, DSW_BLOCK, HEAD_DIM), lambda r, h, n: (r, n, h))
    prev = pl.BlockSpec((None, DSW_BLOCK, HEAD_DIM), lambda r, h, n: (r, jnp.maximum(n - 1, 0), h))
    whole = pl.BlockSpec((None, nsub, HEAD_DIM), lambda r, h, n: (r, 0, h))
    return cur, prev, whole


def _dsw_fwd_call(q, k, v):
    d, nsub, HD = q.shape
    cur, prev, _ = _dsw_specs(nsub)

    def body(q_ref, kp_ref, kc_ref, vp_ref, vc_ref, o_ref, l_ref):
        n = pl.program_id(2)
        o, l = _dsw_tile(q_ref[...], kp_ref[...], kc_ref[...], vp_ref[...], vc_ref[...], n)
        o_ref[...] = o
        l_ref[...] = l

    return pl.pallas_call(
        body, name=f"dsw_attn_fwd_d{d}", grid=(d, HD // HEAD_DIM, nsub // DSW_BLOCK),
        out_shape=[jax.ShapeDtypeStruct(q.shape, F32)] * 2,
        in_specs=[cur, prev, cur, prev, cur], out_specs=[cur, cur],
        compiler_params=_params(dimension_semantics=("arbitrary",) * 3),
    )(q, k, k, v, v)


def _dsw_bwd_call(q, k, v, do, dl):
    d, nsub, HD = q.shape
    cur, prev, whole = _dsw_specs(nsub)
    blk = DSW_BLOCK

    def body(q_ref, kp_ref, kc_ref, vp_ref, vc_ref, do_ref, dl_ref, dq_ref, dk_ref, dv_ref):
        n = pl.program_id(2)

        @pl.when(n == 0)
        def _():
            dk_ref[...] = jnp.zeros(dk_ref.shape, F32)
            dv_ref[...] = jnp.zeros(dv_ref.shape, F32)

        _, vjp = jax.vjp(lambda a, b, c, e, f: _dsw_tile(a, b, c, e, f, n),
                         q_ref[...], kp_ref[...], kc_ref[...], vp_ref[...], vc_ref[...])
        dq, dkp, dkc, dvp, dvc = vjp((do_ref[...], dl_ref[...]))
        dq_ref[...] = dq
        c0 = pl.multiple_of(n * blk, blk)
        p0 = pl.multiple_of(jnp.maximum(n - 1, 0) * blk, blk)
        dk_ref[pl.ds(c0, blk), :] += dkc
        dv_ref[pl.ds(c0, blk), :] += dvc
        dk_ref[pl.ds(p0, blk), :] += dkp
        dv_ref[pl.ds(p0, blk), :] += dvp

    return pl.pallas_call(
        body, name=f"dsw_attn_bwd_d{d}", grid=(d, HD // HEAD_DIM, nsub // blk),
        out_shape=[jax.ShapeDtypeStruct(q.shape, F32)] * 3,
        in_specs=[cur, prev, cur, prev, cur, cur, cur], out_specs=[cur, whole, whole],
        compiler_params=_params(dimension_semantics=("arbitrary",) * 3),
    )(q, k, k, v, v, do, dl)


@jax.custom_vjp
def dsw_attn(q, k, v):
    return tuple(_dsw_fwd_call(q, k, v))


def _dsw_attn_fwd(q, k, v):
    return tuple(_dsw_fwd_call(q, k, v)), (q, k, v)


def _dsw_attn_bwd(res, cts):
    return tuple(_dsw_bwd_call(*res, *cts))


dsw_attn.defvjp(_dsw_attn_fwd, _dsw_attn_bwd)


_LOG2_CHUNK = GDN_CHUNK.bit_length() - 1
_LOG2_HEAD_DIM = HEAD_DIM.bit_length() - 1


def _gdn_step(state, q, k, v, gb, bb):
    C, NH = GDN_CHUNK, GDN_HEADS_PER_STEP
    R = NH * C
    r = lax.broadcasted_iota(jnp.int32, (R, R), 0)
    c = lax.broadcasted_iota(jnp.int32, (R, R), 1)
    same = lax.shift_right_logical(r, _LOG2_CHUNK) == lax.shift_right_logical(c, _LOG2_CHUNK)
    causal, strict = same & (r >= c), same & (r > c)
    gc = dot01(gb, jnp.where(causal, 1.0, 0.0).astype(BF16), True)
    g_sq = jnp.concatenate([gc] * (R // HEAD_DIM), axis=1)
    g_row = dot01(jnp.where(r == c, g_sq, 0.0), jnp.ones((R, R), BF16), True)
    decay = jnp.where(causal, jnp.exp(jnp.where(causal, g_sq - g_row, 0.0)), 0.0)
    kb, vb = k * bb, v * bb
    a_mat = jnp.where(strict, bdot(kb, k, "nt") * decay, 0.0)
    x = -a_mat
    t_mat = jnp.where(r == c, 1.0, 0.0) + x
    for _ in range(_LOG2_CHUNK - 1):
        x = mm3(x, x)
        t_mat = t_mat + mm3(t_mat, x)
    uw = bdot(t_mat, jnp.concatenate([vb, kb * jnp.exp(gc)], axis=1), "nn")
    u, w = uw[:, :HEAD_DIM], uw[:, HEAD_DIM:]
    hr = lax.shift_right_logical(lax.broadcasted_iota(jnp.int32, (R, NH * HEAD_DIM), 0), _LOG2_CHUNK)
    hc = lax.shift_right_logical(lax.broadcasted_iota(jnp.int32, (R, NH * HEAD_DIM), 1), _LOG2_HEAD_DIM)

    def widen(m):
        return jnp.where(hr == hc, jnp.concatenate([m] * NH, axis=1), 0.0)

    v_new = u - bdot(widen(w), state, "nn")
    attn = bdot(q, k, "nt") * decay
    out = bdot(widen(q * jnp.exp(gc)), state, "nn") + bdot(attn, v_new, "nn")
    last = [gc[h * C + C - 1:h * C + C, :] for h in range(NH)]
    g_last_rows = jnp.concatenate([jnp.broadcast_to(l, (C, HEAD_DIM)) for l in last], axis=0)
    g_last_state = jnp.concatenate([jnp.broadcast_to(l, (HEAD_DIM, HEAD_DIM)) for l in last], axis=0)
    k_dec = k * jnp.exp(g_last_rows - gc)
    new_state = state * jnp.exp(g_last_state) + bdot(widen(k_dec), v_new, "tn")
    return new_state, out


def _gdn_stack(ref):
    return jnp.concatenate([ref[:, h * HEAD_DIM:(h + 1) * HEAD_DIM] for h in range(GDN_HEADS_PER_STEP)], axis=0)


def _gdn_unstack(ref, val):
    for h in range(GDN_HEADS_PER_STEP):
        ref[:, h * HEAD_DIM:(h + 1) * HEAD_DIM] = val[h * GDN_CHUNK:(h + 1) * GDN_CHUNK]


def _gdn_fwd_call(q, k, v, gb, bb):
    T, HD = v.shape
    H = HD // HEAD_DIM
    N = T // GDN_CHUNK
    hb = GDN_HEADS_PER_STEP
    W = hb * HEAD_DIM

    def body(q_ref, k_ref, v_ref, g_ref, b_ref, o_ref, s_ref, state):
        n = pl.program_id(1)

        @pl.when(n == 0)
        def _():
            state[...] = jnp.zeros(state.shape, F32)

        s_in = state[...]
        s_ref[...] = s_in
        ns, o = _gdn_step(s_in, *[_gdn_stack(ref) for ref in (q_ref, k_ref, v_ref, g_ref, b_ref)])
        state[...] = ns
        _gdn_unstack(o_ref, o)

    blk = pl.BlockSpec((GDN_CHUNK, W), lambda h, n: (n, h))
    sblk = pl.BlockSpec((None, W, HEAD_DIM), lambda h, n: (n, h, 0))
    return pl.pallas_call(
        body, name="gdn_chunk_fwd", grid=(H // hb, N),
        out_shape=[jax.ShapeDtypeStruct((T, HD), F32), jax.ShapeDtypeStruct((N, H * HEAD_DIM, HEAD_DIM), F32)],
        in_specs=[blk] * 5, out_specs=[blk, sblk],
        scratch_shapes=[pltpu.VMEM((W, HEAD_DIM), F32)],
        compiler_params=_params(dimension_semantics=("arbitrary", "arbitrary")),
    )(q, k, v, gb, bb)


def _gdn_bwd_call(q, k, v, gb, bb, states, do):
    T, HD = v.shape
    H = HD // HEAD_DIM
    N = T // GDN_CHUNK
    hb = GDN_HEADS_PER_STEP
    W = hb * HEAD_DIM

    def body(q_ref, k_ref, v_ref, g_ref, b_ref, s_ref, do_ref, dq_ref, dk_ref, dv_ref, dg_ref, db_ref, dstate):
        n = pl.program_id(1)

        @pl.when(n == 0)
        def _():
            dstate[...] = jnp.zeros(dstate.shape, F32)

        _, vjp = jax.vjp(_gdn_step, s_ref[...], *[_gdn_stack(ref) for ref in (q_ref, k_ref, v_ref, g_ref, b_ref)])
        ds, *grads = vjp((dstate[...], _gdn_stack(do_ref)))
        dstate[...] = ds
        for ref, g in zip((dq_ref, dk_ref, dv_ref, dg_ref, db_ref), grads):
            _gdn_unstack(ref, g)

    blk = pl.BlockSpec((GDN_CHUNK, W), lambda h, n: (N - 1 - n, h))
    sblk = pl.BlockSpec((None, W, HEAD_DIM), lambda h, n: (N - 1 - n, h, 0))
    return pl.pallas_call(
        body, name="gdn_chunk_bwd", grid=(H // hb, N), out_shape=[jax.ShapeDtypeStruct((T, HD), F32)] * 5,
        in_specs=[blk] * 5 + [sblk, blk], out_specs=[blk] * 5,
        scratch_shapes=[pltpu.VMEM((W, HEAD_DIM), F32)],
        compiler_params=_params(dimension_semantics=("arbitrary", "arbitrary")),
    )(q, k, v, gb, bb, states, do)


@jax.custom_vjp
def gdn_core(q, k, v, gb, bb):
    return _gdn_fwd_call(q, k, v, gb, bb)[0]


def _gdn_core_fwd(q, k, v, gb, bb):
    o, states = _gdn_fwd_call(q, k, v, gb, bb)
    return o, (q, k, v, gb, bb, states)


def _gdn_core_bwd(res, do):
    return tuple(_gdn_bwd_call(*res, do))


gdn_core.defvjp(_gdn_core_fwd, _gdn_core_bwd)


def _mixer_sb(h, w):
    qkv = lin_t(h, w["sb_in_t"])
    q, k, v = rowop(_fn_sb_pre, "sb_pre", [qkv], [], [w["sb_q_norm"], w["sb_k_norm"]], [D_MODEL] * 3)
    return lin_n(sb_attn(q, k, v), w["sb_out"])


def _mixer_gdn(h, w):
    wt = w["gdn_in_t"]
    nqkv = 2 * GDN_KEY_DIM + GDN_VAL_DIM
    qkv = lin_t(h, wt[:nqkv])
    z = lin_t(h, wt[nqkv:nqkv + GDN_VAL_DIM])
    w_ba = jnp.pad(wt[nqkv + GDN_VAL_DIM:], ((0, LANES - 2 * GDN_V_HEADS), (0, 0)))
    ba = lin_t(h, w_ba)
    qkv = dwconv(qkv, w["gdn_conv_w"])
    q, k, v, gb, bb = rowop(_fn_gdn_pre, "gdn_pre", [qkv, ba], [], [w["gdn_a_log"], w["gdn_dt_bias"]],
                            [GDN_VAL_DIM] * 5)
    o = gdn_core(q, k, v, gb, bb)
    (y,) = rowop(_fn_gdn_post, "gdn_post", [o, z], [], [w["gdn_o_norm"]], [GDN_VAL_DIM])
    return lin_n(y, w["gdn_out"])


def _to_strided(x, cols, d):
    T = x.shape[0]
    return x[:, cols].reshape(T // d, d, -1).transpose(1, 0, 2)


def _from_strided(x):
    d, n, c = x.shape
    return x.transpose(1, 0, 2).reshape(d * n, c)


def _mixer_dsw(h, cosm, sinm, w):
    qkv = lin_t(h, w["dsw_in_t"])
    nhd = DSW_HEADS * HEAD_DIM
    q, k, v = rowop(_fn_dsw_pre, "dsw_pre", [qkv], [cosm, sinm], [w["dsw_q_norm"], w["dsw_k_norm"]], [nhd] * 3)
    outs, lses = [], []
    for gi, (_, d) in enumerate(DSW_GROUPS):
        cols = slice(gi * DSW_HG * HEAD_DIM, (gi + 1) * DSW_HG * HEAD_DIM)
        o_g, l_g = dsw_attn(_to_strided(q, cols, d), _to_strided(k, cols, d), _to_strided(v, cols, d))
        outs.append(_from_strided(o_g))
        lses.append(_from_strided(l_g))
    (o,) = rowop(_fn_dsw_combine, "dsw_combine", [jnp.concatenate(outs, axis=1), jnp.concatenate(lses, axis=1)],
                 [], [], [nhd])
    return lin_n(o, w["dsw_out"])


def _mixer_lru(h, w):
    wt = w["lru_in_t"]
    gate = lin_t(h, wt[:LRU_WIDTH])
    xr = dwconv(lin_t(h, wt[LRU_WIDTH:]), w["lru_conv_w"])
    a, u = rowop(_fn_lru_gates, "lru_gates", [xr], [],
                 [w["lru_conv_b"], w["lru_w_a"], w["lru_b_a"], w["lru_w_x"], w["lru_b_x"], w["lru_lambda"]],
                 [LRU_WIDTH] * 2)
    hs = lru_scan(a, u)
    (y,) = rowop(_fn_lru_out, "lru_out", [hs, gate], [], [], [LRU_WIDTH])
    return lin_n(y, w["lru_out"])


def _ffn(h, w, i):
    gu = lin_t(h, w["ffn_gu_t"][i])
    (act,) = rowop(_fn_swiglu, "swiglu", [gu], [], [], [FFN_HIDDEN])
    return lin_n(act, w["ffn_down"][i])


def trunk(x, cosm, sinm, w):
    (h,) = rowop(_fn_norm, "norm", [x], [], [w["mix_norm"][0:1]], [D_MODEL])
    f = None
    for i in range(DEPTH):
        if i > 0:
            x, h = rowop(_fn_add_norm, "add_norm", [x, f], [], [w["mix_norm"][i:i + 1]], [D_MODEL] * 2)
        kind = i % 4
        if kind == 0:
            y = _mixer_sb(h, w)
        elif kind == 1:
            y = _mixer_gdn(h, w)
        elif kind == 2:
            y = _mixer_dsw(h, cosm, sinm, w)
        else:
            y = _mixer_lru(h, w)
        x, h = rowop(_fn_add_norm, "add_norm", [x, y], [], [w["ffn_norm"][i:i + 1]], [D_MODEL] * 2)
        f = _ffn(h, w, i)
    return x, f


ANY = pl.BlockSpec(memory_space=pl.ANY)


SLAB_BYTES = 4 * 1024 * 1024


def _col_tile(rows, C, itemsize):
    return _tile(C, max(LANES, SLAB_BYTES // (rows * itemsize)))


def all_gather(shard, name):
    r, C = shard.shape

    def body(x_ref, out_ref, send_sems, recv_sems, local_sem):
        x, y, c = lax.axis_index("x"), lax.axis_index("y"), lax.axis_index("c")
        me, sibling = (x, y, c), (x, y, 1 - c)
        chips = [(1 - x, y), (x, 1 - y), (1 - x, 1 - y)]

        def slot(px, py, pc):
            return out_ref.at[4 * px + 2 * py + pc]

        def copy(k, block, to, src=None):
            return pltpu.make_async_remote_copy(
                src_ref=slot(*block) if src is None else src, dst_ref=slot(*block),
                send_sem=send_sems.at[k], recv_sem=recv_sems.at[k], device_id=to, device_id_type=MESH)

        mine = pltpu.make_async_copy(x_ref, slot(*me), local_sem)
        mine.start()
        first = [copy(0, me, sibling, src=x_ref)]
        first += [copy(1 + j, me, (*chip, c), src=x_ref) for j, chip in enumerate(chips)]
        for cp in first:
            cp.start()
        passed = [copy(4 + j, (*chip, c), sibling) for j, chip in enumerate(chips)]
        for j, chip in enumerate(chips):
            copy(1 + j, (*chip, c), me).wait_recv()
            passed[j].start()
        copy(0, sibling, me).wait_recv()
        for j, chip in enumerate(chips):
            copy(4 + j, (*chip, 1 - c), me).wait_recv()
        for cp in first + passed:
            cp.wait_send()
        mine.wait()

    return pl.pallas_call(
        body, name=name, out_shape=jax.ShapeDtypeStruct((N_DEV, r, C), shard.dtype),
        in_specs=[ANY], out_specs=ANY,
        scratch_shapes=[pltpu.SemaphoreType.DMA((7,)), pltpu.SemaphoreType.DMA((7,)), pltpu.SemaphoreType.DMA],
        compiler_params=pltpu.CompilerParams(has_side_effects=True),
    )(shard)


def exchange_pair(g, name):
    _, _, r, C = g.shape

    def body(g_ref, out_ref, send_sems, recv_sems):
        x, y, c = lax.axis_index("x"), lax.axis_index("y"), lax.axis_index("c")
        copies = [pltpu.make_async_remote_copy(
            src_ref=g_ref.at[q, 1 - c], dst_ref=out_ref.at[q], send_sem=send_sems.at[q], recv_sem=recv_sems.at[q],
            device_id=(x, y, 1 - c), device_id_type=MESH) for q in range(N_CHIPS)]
        for cp in copies:
            cp.start()
        for cp in copies:
            cp.wait()

    return pl.pallas_call(
        body, name=name, out_shape=jax.ShapeDtypeStruct((N_CHIPS, r, C), g.dtype), in_specs=[ANY], out_specs=ANY,
        scratch_shapes=[pltpu.SemaphoreType.DMA((N_CHIPS,)), pltpu.SemaphoreType.DMA((N_CHIPS,))],
        compiler_params=pltpu.CompilerParams(has_side_effects=True),
    )(g)


def pair_add(g, got, name):
    _, _, r, C = g.shape
    cb = _col_tile(r, C, g.dtype.itemsize)
    c = lax.axis_index("c")

    def body(c_ref, a_ref, b_ref, o_ref):
        o_ref[...] = (a_ref[...].astype(F32) + b_ref[...].astype(F32)).astype(o_ref.dtype)

    return pl.pallas_call(
        body, name=name, out_shape=jax.ShapeDtypeStruct((N_CHIPS, r, C), g.dtype),
        grid_spec=pltpu.PrefetchScalarGridSpec(
            num_scalar_prefetch=1, grid=(N_CHIPS, C // cb),
            in_specs=[pl.BlockSpec((None, None, r, cb), lambda q, j, cr: (q, cr[0], 0, j)),
                      pl.BlockSpec((None, r, cb), lambda q, j, cr: (q, 0, j))],
            out_specs=pl.BlockSpec((None, r, cb), lambda q, j, cr: (q, 0, j))),
        compiler_params=_params(dimension_semantics=("arbitrary", "arbitrary")),
    )(jnp.reshape(c, (1,)).astype(jnp.int32), g, got)


def exchange_chips(p, name):
    _, r, C = p.shape

    def body(p_ref, out_ref, send_sems, recv_sems, local_sem):
        x, y, c = lax.axis_index("x"), lax.axis_index("y"), lax.axis_index("c")
        mychip = 2 * x + y
        chips = [(1 - x, y), (x, 1 - y), (1 - x, 1 - y)]
        mine = pltpu.make_async_copy(p_ref.at[mychip], out_ref.at[mychip], local_sem)
        mine.start()
        copies = [pltpu.make_async_remote_copy(
            src_ref=p_ref.at[2 * cx + cy], dst_ref=out_ref.at[mychip], send_sem=send_sems.at[j],
            recv_sem=recv_sems.at[j], device_id=(cx, cy, c), device_id_type=MESH) for j, (cx, cy) in enumerate(chips)]
        for cp in copies:
            cp.start()
        for j, (cx, cy) in enumerate(chips):
            pltpu.make_async_remote_copy(
                src_ref=p_ref.at[mychip], dst_ref=out_ref.at[2 * cx + cy], send_sem=send_sems.at[j],
                recv_sem=recv_sems.at[j], device_id=(cx, cy, c), device_id_type=MESH).wait_recv()
        for cp in copies:
            cp.wait_send()
        mine.wait()

    return pl.pallas_call(
        body, name=name, out_shape=jax.ShapeDtypeStruct((N_CHIPS, r, C), p.dtype), in_specs=[ANY], out_specs=ANY,
        scratch_shapes=[pltpu.SemaphoreType.DMA((3,)), pltpu.SemaphoreType.DMA((3,)), pltpu.SemaphoreType.DMA],
        compiler_params=pltpu.CompilerParams(has_side_effects=True),
    )(p)


def sum_slots(parts, name):
    n, r, C = parts.shape
    cb = _col_tile(n * r, C, parts.dtype.itemsize)

    def body(p_ref, o_ref):
        acc = p_ref[0].astype(F32)
        for q in range(1, n):
            acc = acc + p_ref[q].astype(F32)
        o_ref[...] = acc

    return pl.pallas_call(
        body, name=name, out_shape=jax.ShapeDtypeStruct((r, C), F32), grid=(C // cb,),
        in_specs=[pl.BlockSpec((n, r, cb), lambda j: (0, 0, j))], out_specs=pl.BlockSpec((r, cb), lambda j: (0, j)),
        compiler_params=_params(dimension_semantics=("arbitrary",)),
    )(parts)


def reduce_scatter(g, tag):
    _, r, C = g.shape
    g4 = g.reshape(N_CHIPS, 2, r, C)
    got = exchange_pair(g4, f"rs_pair_{tag}")
    pairs = pair_add(g4, got, f"rs_pair_add_{tag}")
    parts = exchange_chips(pairs, f"rs_chips_{tag}")
    return sum_slots(parts, f"rs_sum_{tag}")


def all_reduce(v, tag):
    return sum_slots(all_gather(v, f"ar_gather_{tag}"), f"ar_sum_{tag}")


def adamw(w, g, m, v, name):
    R, C = w.shape
    tr = R
    for cand in (512, 256, 128, 64, 32, 16, 8):
        if R % cand == 0 and cand * C * 4 * 7 * 2 <= 40 * 1024 * 1024:
            tr = cand
            break
    c1 = 1.0 - ADAM_B1 ** ADAM_STEP
    c2 = 1.0 - ADAM_B2 ** ADAM_STEP

    def body(w_ref, g_ref, m_ref, v_ref, d_ref, nm_ref, nv_ref):
        gv = g_ref[...]
        nm = ADAM_B1 * m_ref[...] + (1.0 - ADAM_B1) * gv
        nv = ADAM_B2 * v_ref[...] + (1.0 - ADAM_B2) * (gv * gv)
        d_ref[...] = -ADAM_LR * ((nm / c1) / (jnp.sqrt(nv / c2) + ADAM_EPS) + ADAM_WD * w_ref[...])
        nm_ref[...] = nm
        nv_ref[...] = nv

    spec = pl.BlockSpec((tr, C), lambda i: (i, 0))
    return pl.pallas_call(
        body, name=name, out_shape=[jax.ShapeDtypeStruct((R, C), F32)] * 3, grid=(R // tr,),
        in_specs=[spec] * 4, out_specs=[spec] * 3,
        compiler_params=_params(dimension_semantics=("arbitrary",)),
    )(w, g, m, v)


NAMES = ['mix_norm', 'ffn_norm', 'ffn_w_gu', 'ffn_w_down', 'sb_w_in', 'sb_q_norm', 'sb_k_norm', 'sb_w_out',
         'gdn_w_in', 'gdn_conv_w', 'gdn_a_log', 'gdn_dt_bias', 'gdn_o_norm', 'gdn_w_out', 'dsw_w_in', 'dsw_q_norm',
         'dsw_k_norm', 'dsw_w_out', 'lru_w_in', 'lru_conv_w', 'lru_conv_b', 'lru_w_a', 'lru_b_a', 'lru_w_x',
         'lru_b_x', 'lru_lambda', 'lru_w_out']
REPLICATED = ['mix_norm', 'ffn_norm', 'sb_q_norm', 'sb_k_norm', 'gdn_a_log', 'gdn_dt_bias', 'gdn_o_norm',
              'dsw_q_norm', 'dsw_k_norm']
SMALL_SHARDED = ['gdn_conv_w', 'lru_conv_w', 'lru_conv_b', 'lru_b_a', 'lru_b_x', 'lru_lambda']
IN_T = ['sb_w_in', 'gdn_w_in', 'dsw_w_in', 'lru_w_in']
OUT_N = ['sb_w_out', 'gdn_w_out', 'dsw_w_out', 'lru_w_out']


def _pack(arrs, pad_rows_to=8):
    flat = jnp.concatenate([a.reshape(-1) for a in arrs])
    n = flat.shape[0]
    rows = -(-n // LANES)
    rows = -(-rows // pad_rows_to) * pad_rows_to
    return jnp.pad(flat, (0, rows * LANES - n)).reshape(rows, LANES)


def _unpack(buf, shapes):
    flat = buf.reshape(-1)
    out, o = [], 0
    for s in shapes:
        n = math.prod(s)
        out.append(flat[o:o + n].reshape(s))
        o += n
    return out


def kernel(x, positions, mix_norm, ffn_norm, ffn_w_gu, ffn_w_down, sb_w_in, sb_q_norm, sb_k_norm, sb_w_out, gdn_w_in, gdn_conv_w, gdn_a_log, gdn_dt_bias, gdn_o_norm, gdn_w_out, dsw_w_in, dsw_q_norm, dsw_k_norm, dsw_w_out, lru_w_in, lru_conv_w, lru_conv_b, lru_w_a, lru_b_a, lru_w_x, lru_b_x, lru_lambda, lru_w_out, loss_target, m_mix_norm, m_ffn_norm, m_ffn_w_gu, m_ffn_w_down, m_sb_w_in, m_sb_q_norm, m_sb_k_norm, m_sb_w_out, m_gdn_w_in, m_gdn_conv_w, m_gdn_a_log, m_gdn_dt_bias, m_gdn_o_norm, m_gdn_w_out, m_dsw_w_in, m_dsw_q_norm, m_dsw_k_norm, m_dsw_w_out, m_lru_w_in, m_lru_conv_w, m_lru_conv_b, m_lru_w_a, m_lru_b_a, m_lru_w_x, m_lru_b_x, m_lru_lambda, m_lru_w_out, v_mix_norm, v_ffn_norm, v_ffn_w_gu, v_ffn_w_down, v_sb_w_in, v_sb_q_norm, v_sb_k_norm, v_sb_w_out, v_gdn_w_in, v_gdn_conv_w, v_gdn_a_log, v_gdn_dt_bias, v_gdn_o_norm, v_gdn_w_out, v_dsw_w_in, v_dsw_q_norm, v_dsw_k_norm, v_dsw_w_out, v_lru_w_in, v_lru_conv_w, v_lru_conv_b, v_lru_w_a, v_lru_b_a, v_lru_w_x, v_lru_b_x, v_lru_lambda, v_lru_w_out):
    args = locals()
    W = {n: args[n] for n in NAMES}
    M = {n: args["m_" + n] for n in NAMES}
    V = {n: args["v_" + n] for n in NAMES}
    T = x.shape[1]
    x2 = x[0]
    tgt = loss_target[0]

    w = {}
    for n in IN_T:
        loc = W[n][0].T.astype(BF16)
        w[n.replace("_w_in", "_in_t")] = all_gather(loc, f"ag_{n}").reshape(-1, D_MODEL)
    for n in OUT_N:
        w[n.replace("_w_out", "_out")] = all_gather(W[n][0].astype(BF16), f"ag_{n}").reshape(-1, D_MODEL)
    w["ffn_gu_t"] = [all_gather(ffn_w_gu[i].T.astype(BF16), f"ag_ffn_gu{i}").reshape(-1, D_MODEL) for i in range(DEPTH)]
    w["ffn_down"] = [all_gather(ffn_w_down[i].astype(BF16), f"ag_ffn_down{i}").reshape(-1, D_MODEL) for i in range(DEPTH)]
    gates = jnp.concatenate([lru_w_a[0].reshape(-1, LRU_BLOCK_DIM), lru_w_x[0].reshape(-1, LRU_BLOCK_DIM)], axis=0)
    gg = all_gather(gates.astype(BF16), "ag_lru_gates").reshape(N_DEV, 2, LRU_BLOCKS, 32, LRU_BLOCK_DIM)
    gg = gg.transpose(1, 2, 0, 3, 4).reshape(2, LRU_BLOCKS, LRU_BLOCK_DIM, LRU_BLOCK_DIM).astype(F32)
    w["lru_w_a"], w["lru_w_x"] = gg[0], gg[1]
    small_shapes = [W[n].shape for n in SMALL_SHARDED]
    sm = all_gather(_pack([W[n] for n in SMALL_SHARDED]), "ag_small")
    sm = [jnp.stack(parts) for parts in zip(*[_unpack(sm[p], small_shapes) for p in range(N_DEV)])]
    smd = dict(zip(SMALL_SHARDED, sm))
    w["gdn_conv_w"] = smd["gdn_conv_w"][:, 0].transpose(1, 0, 2).reshape(4, -1)
    w["lru_conv_w"] = smd["lru_conv_w"][:, 0].transpose(1, 0, 2).reshape(4, -1)
    w["lru_conv_b"] = smd["lru_conv_b"][:, 0].reshape(1, -1)
    w["lru_lambda"] = smd["lru_lambda"][:, 0].reshape(1, -1)
    w["lru_b_a"] = smd["lru_b_a"][:, 0].transpose(1, 0, 2).reshape(1, -1)
    w["lru_b_x"] = smd["lru_b_x"][:, 0].transpose(1, 0, 2).reshape(1, -1)
    for n in REPLICATED:
        w[n] = W[n]

    half = ROPE_DIM // 2
    inv_freq = ROPE_THETA ** (-jnp.arange(half, dtype=F32) / half)
    ang = positions[0].astype(F32)[:, None] * inv_freq
    cs, sn = jnp.cos(ang), jnp.sin(ang)
    cosm = jnp.concatenate([cs, cs, jnp.ones((T, HEAD_DIM - ROPE_DIM), F32)], axis=1)
    sinm = jnp.concatenate([-sn, sn, jnp.zeros((T, HEAD_DIM - ROPE_DIM), F32)], axis=1)

    (xl, fl), vjp = jax.vjp(lambda xx, ww: trunk(xx, cosm, sinm, ww), x2, w)
    loss_part, dy = loss_head(xl, fl, tgt)
    gx, gw = vjp((dy, dy))
    loss = lax.psum(loss_part, ("x", "y", "c"))

    G = {}
    for n in IN_T:
        r = W[n].shape[2]
        g = gw[n.replace("_w_in", "_in_t")].reshape(N_DEV, r, D_MODEL)
        G[n] = reduce_scatter(g, n).T[None]
    for n in OUT_N:
        r = W[n].shape[1]
        g = gw[n.replace("_w_out", "_out")].reshape(N_DEV, r, D_MODEL)
        G[n] = reduce_scatter(g, n)[None]
    r = ffn_w_gu.shape[2]
    G["ffn_w_gu"] = jnp.stack([reduce_scatter(gw["ffn_gu_t"][i].reshape(N_DEV, r, D_MODEL), f"ffn_gu{i}").T
                               for i in range(DEPTH)])
    r = ffn_w_down.shape[1]
    G["ffn_w_down"] = jnp.stack([reduce_scatter(gw["ffn_down"][i].reshape(N_DEV, r, D_MODEL), f"ffn_down{i}")
                                 for i in range(DEPTH)])
    gg = jnp.stack([gw["lru_w_a"], gw["lru_w_x"]]).reshape(2, LRU_BLOCKS, N_DEV, 32, LRU_BLOCK_DIM)
    gg = gg.transpose(2, 0, 1, 3, 4).reshape(N_DEV, 2 * LRU_BLOCKS * 32, LRU_BLOCK_DIM)
    gg = reduce_scatter(gg, "lru_gates").reshape(2, 1, LRU_BLOCKS, 32, LRU_BLOCK_DIM)
    G["lru_w_a"], G["lru_w_x"] = gg[0], gg[1]
    gs = {
        "gdn_conv_w": gw["gdn_conv_w"].reshape(4, N_DEV, -1).transpose(1, 0, 2)[:, None],
        "lru_conv_w": gw["lru_conv_w"].reshape(4, N_DEV, -1).transpose(1, 0, 2)[:, None],
        "lru_conv_b": gw["lru_conv_b"].reshape(N_DEV, 1, -1),
        "lru_lambda": gw["lru_lambda"].reshape(N_DEV, 1, -1),
        "lru_b_a": gw["lru_b_a"].reshape(LRU_BLOCKS, N_DEV, 32).transpose(1, 0, 2)[:, None],
        "lru_b_x": gw["lru_b_x"].reshape(LRU_BLOCKS, N_DEV, 32).transpose(1, 0, 2)[:, None],
    }
    packed = jnp.stack([_pack([gs[n][p] for n in SMALL_SHARDED]) for p in range(N_DEV)])
    for n, g in zip(SMALL_SHARDED, _unpack(reduce_scatter(packed, "small"), small_shapes)):
        G[n] = g
    rep_shapes = [W[n].shape for n in REPLICATED]
    for n, g in zip(REPLICATED, _unpack(all_reduce(_pack([gw[n] for n in REPLICATED]), "rep"), rep_shapes)):
        G[n] = g

    D, NM, NV = {}, {}, {}
    big = [n for n in NAMES if n not in REPLICATED and n not in SMALL_SHARDED]
    for n in big:
        shp = W[n].shape
        two = (-1, shp[-1])
        d, nm, nv = adamw(W[n].reshape(two), G[n].reshape(two), M[n].reshape(two), V[n].reshape(two), f"adamw_{n}")
        D[n], NM[n], NV[n] = d.reshape(shp), nm.reshape(shp), nv.reshape(shp)
    for group, tag in ((SMALL_SHARDED, "small"), (REPLICATED, "rep")):
        shapes = [W[n].shape for n in group]
        res = adamw(_pack([W[n] for n in group]), _pack([G[n] for n in group]), _pack([M[n] for n in group]),
                    _pack([V[n] for n in group]), f"adamw_{tag}")
        for dst, buf in zip((D, NM, NV), res):
            for n, a in zip(group, _unpack(buf, shapes)):
                dst[n] = a

    return (loss, gx[None], *[G[n] for n in NAMES], *[D[n] for n in NAMES], *[NM[n] for n in NAMES],
            *[NV[n] for n in NAMES])
```

```python
import functools
import math

import jax
import jax.numpy as jnp
from jax import lax
from jax.experimental import pallas as pl
from jax.experimental.pallas import tpu as pltpu

F32 = jnp.float32
BF16 = jnp.bfloat16

D_MODEL = 2048
HEAD_DIM = 128
NORM_EPS = 1e-6
SB_HEADS = 16
SB_BLOCK = 256
GDN_K_HEADS = 16
GDN_V_HEADS = 32
GDN_KEY_DIM = 2048
GDN_VAL_DIM = 4096
GDN_CHUNK = 64
GDN_HEADS_PER_STEP = 4
DSW_GROUPS = ((128, 1), (512, 4), (2048, 16))
DSW_HG = 6
DSW_HEADS = 18
DSW_BLOCK = 128
ROPE_DIM = 32
ROPE_THETA = 500000.0
LRU_WIDTH = 2048
LRU_BLOCKS = 8
LRU_BLOCK_DIM = 256
LRU_C = 8.0
FFN_HIDDEN = 5632
DEPTH = 4
ADAM_LR, ADAM_B1, ADAM_B2, ADAM_EPS, ADAM_WD, ADAM_STEP = 0.001, 0.9, 0.999, 1e-08, 0.01, 10
N_DEV = 8
N_CHIPS = 4

V7X_VMEM_LIMIT = 56 * 1024 * 1024
LANES = 128
MESH = pl.DeviceIdType.MESH


def _params(**kw):
    return pltpu.CompilerParams(vmem_limit_bytes=V7X_VMEM_LIMIT, **kw)


def _pick(n, cands):
    for c in cands:
        if n % c == 0:
            return c
    return n


def _tile(n, cap, unit=LANES):
    best = None
    for t in range(unit, min(n, cap) + 1, unit):
        if n % t == 0:
            best = t
    return best or n


_DN = {"nn": (((1,), (0,)), ((), ())), "nt": (((1,), (1,)), ((), ())), "tn": (((0,), (0,)), ((), ()))}


def _raw_dot(a, b, mode):
    return lax.dot_general(a.astype(BF16), b.astype(BF16), _DN[mode], preferred_element_type=F32)


@functools.partial(jax.custom_vjp, nondiff_argnums=(2,))
def bdot(a, b, mode):
    return _raw_dot(a, b, mode)


def _bdot_fwd(a, b, mode):
    return _raw_dot(a, b, mode), (a, b)


def _bdot_bwd(mode, res, ct):
    a, b = res
    if mode == "nn":
        return bdot(ct, b, "nt"), bdot(a, ct, "tn")
    if mode == "nt":
        return bdot(ct, b, "nn"), bdot(ct, a, "tn")
    return bdot(b, ct, "nt"), bdot(a, ct, "nn")


bdot.defvjp(_bdot_fwd, _bdot_bwd)


def _split2(x):
    hi = x.astype(BF16)
    lo = (x - hi.astype(F32)).astype(BF16)
    return hi, lo


def _dot01_raw(x, m, dn, left):
    hi, lo = _split2(x)
    if left:
        return (lax.dot_general(m, hi, dn, preferred_element_type=F32)
                + lax.dot_general(m, lo, dn, preferred_element_type=F32))
    return (lax.dot_general(hi, m, dn, preferred_element_type=F32)
            + lax.dot_general(lo, m, dn, preferred_element_type=F32))


@functools.partial(jax.custom_vjp, nondiff_argnums=(2,))
def dot01(x, m, left):
    return _dot01_raw(x, m, _DN["nn"], left)


def _dot01_fwd(x, m, left):
    return _dot01_raw(x, m, _DN["nn"], left), m


def _dot01_bwd(left, m, ct):
    dx = _dot01_raw(ct, m, _DN["tn"] if left else _DN["nt"], left)
    return dx, jnp.zeros_like(m)


dot01.defvjp(_dot01_fwd, _dot01_bwd)


def _mm3(a, b):
    ah, al = _split2(a)
    bh, bl = _split2(b)
    dn = _DN["nn"]
    return (lax.dot_general(ah, bh, dn, preferred_element_type=F32)
            + lax.dot_general(ah, bl, dn, preferred_element_type=F32)
            + lax.dot_general(al, bh, dn, preferred_element_type=F32))


@jax.custom_vjp
def mm3(a, b):
    return _mm3(a, b)


def _mm3_fwd(a, b):
    return _mm3(a, b), (a, b)


def _mm3_bwd(res, ct):
    a, b = res
    return _mm3(ct, b.T), _mm3(a.T, ct)


mm3.defvjp(_mm3_fwd, _mm3_bwd)


def _softplus_parts(z):
    e = jnp.exp(-jnp.abs(z))
    l = jnp.log1p(e)
    sp = jnp.maximum(z, 0.0) + l
    ls = jnp.minimum(z, 0.0) - l
    inv = 1.0 / (1.0 + e)
    sig = jnp.where(z >= 0.0, inv, e * inv)
    return sp, ls, sig


def _rms(x, g):
    return x * lax.rsqrt(jnp.mean(x * x, axis=-1, keepdims=True) + NORM_EPS) * g


def _swap16(x):
    lane = lax.broadcasted_iota(jnp.int32, x.shape, 1)
    return jnp.where(lane < 16, pltpu.roll(x, 112, axis=1), jnp.where(lane < 32, pltpu.roll(x, 16, axis=1), 0.0))


@jax.custom_vjp
def rope(x, cosm, sinm):
    return x * cosm + _swap16(x) * sinm


def _rope_fwd(x, cosm, sinm):
    return rope(x, cosm, sinm), (cosm, sinm)


def _rope_bwd(res, ct):
    cosm, sinm = res
    return ct * cosm + _swap16(ct * sinm), jnp.zeros_like(cosm), jnp.zeros_like(sinm)


rope.defvjp(_rope_fwd, _rope_bwd)


def mm(a, b, mode, out_dtype=F32):
    return mm_carry(a, b, mode, out_dtype, ())[0]


def mm_carry(a, b, mode, out_dtype=F32, jobs=()):
    if mode == "nt":
        (M, K), N = a.shape, b.shape[0]
    elif mode == "nn":
        (M, K), N = a.shape, b.shape[1]
    else:
        (K, M), N = a.shape, b.shape[1]
    if mode == "nt":
        tm, tn, tk = _tile(M, 1024), _tile(N, 768), _tile(K, 2048)
    elif mode == "nn":
        tm, tn, tk = _tile(M, 1024), _tile(N, 2048), _tile(K, 1024)
    else:
        tm, tn, tk = _tile(M, 1408), _tile(N, 2048), _tile(K, 512)
    nk = K // tk
    if mode == "nt":
        a_spec = pl.BlockSpec((tm, tk), lambda i, j, k: (i, k))
        b_spec = pl.BlockSpec((tn, tk), lambda i, j, k: (j, k))
    elif mode == "nn":
        a_spec = pl.BlockSpec((tm, tk), lambda i, j, k: (i, k))
        b_spec = pl.BlockSpec((tk, tn), lambda i, j, k: (k, j))
    else:
        a_spec = pl.BlockSpec((tk, tm), lambda i, j, k: (k, i))
        b_spec = pl.BlockSpec((tk, tn), lambda i, j, k: (k, j))

    def body(a_ref, b_ref, o_ref, *scr):
        p = _raw_dot(a_ref[...], b_ref[...], mode)
        if nk == 1:
            o_ref[...] = p.astype(o_ref.dtype)
        else:
            acc = scr[0]
            k = pl.program_id(2)

            @pl.when(k == 0)
            def _():
                acc[...] = p

            @pl.when(k > 0)
            def _():
                acc[...] += p

            @pl.when(k == nk - 1)
            def _():
                o_ref[...] = acc[...].astype(o_ref.dtype)

    grid = (M // tm, N // tn, nk)
    res = carried_call(
        body, jobs, grid, name=f"mm_{mode}_{M}x{N}x{K}" + ("_c" if jobs else ""),
        out_shape=[jax.ShapeDtypeStruct((M, N), out_dtype)], in_specs=[a_spec, b_spec],
        out_specs=[pl.BlockSpec((tm, tn), lambda i, j, k: (i, j))],
        scratch_shapes=[] if nk == 1 else [pltpu.VMEM((tm, tn), F32)], operands=(a, b))
    return res[0][0], res[1]


@jax.custom_vjp
def lin_t(x, wt):
    return mm(x, wt, "nt")


def _lin_t_fwd(x, wt):
    return mm(x, wt, "nt"), (x, wt)


def _lin_t_bwd(res, dy):
    x, wt = res
    return mm(dy, wt, "nn"), mm(dy, x, "tn", out_dtype=wt.dtype)


lin_t.defvjp(_lin_t_fwd, _lin_t_bwd)


@jax.custom_vjp
def lin_n(x, w):
    return mm(x, w, "nn")


def _lin_n_fwd(x, w):
    return mm(x, w, "nn"), (x, w)


def _lin_n_bwd(res, dy):
    x, w = res
    return mm(dy, w, "nt"), mm(x, dy, "tn", out_dtype=w.dtype)


lin_n.defvjp(_lin_n_fwd, _lin_n_bwd)


def _full_spec(shape):
    nd = len(shape)
    return pl.BlockSpec(tuple(shape), lambda i: (0,) * nd)


def _row_spec(tr, c):
    return pl.BlockSpec((tr, c), lambda i: (i, 0))


def _rowop_tr(total_cols, T):
    budget = 20 * 1024 * 1024
    for tr in (512, 256, 128, 64, 32, 16, 8):
        if T % tr == 0 and total_cols * tr * 4 * 2 <= budget:
            return tr
    return 8


def rowop(fn, name, rows, consts, params, out_cols):
    nr, nc, npar, nout = len(rows), len(consts), len(params), len(out_cols)
    T = rows[0].shape[0]
    in_cols = [r.shape[1] for r in rows] + [c.shape[1] for c in consts]
    tr_f = _rowop_tr(sum(in_cols) + sum(out_cols), T)
    tr_b = _rowop_tr(sum(in_cols) + sum(out_cols) + sum(r.shape[1] for r in rows), T)

    def fwd_call(rows, consts, params):
        def body(*refs):
            ins = [r[...] for r in refs[:nr + nc + npar]]
            outs = fn(*ins)
            for o_ref, o in zip(refs[nr + nc + npar:], outs):
                o_ref[...] = o.astype(F32)

        return pl.pallas_call(
            body, name=name + "_fwd", grid=(T // tr_f,),
            out_shape=[jax.ShapeDtypeStruct((T, c), F32) for c in out_cols],
            in_specs=[_row_spec(tr_f, c) for c in in_cols] + [_full_spec(p.shape) for p in params],
            out_specs=[_row_spec(tr_f, c) for c in out_cols],
            compiler_params=_params(dimension_semantics=("arbitrary",)),
        )(*rows, *consts, *params)

    def bwd_call(rows, consts, params, douts):
        def body(*refs):
            i = pl.program_id(0)
            rv = [r[...] for r in refs[:nr]]
            cv = [r[...] for r in refs[nr:nr + nc]]
            pv = [r[...] for r in refs[nr + nc:nr + nc + npar]]
            dv = [r[...] for r in refs[nr + nc + npar:nr + nc + npar + nout]]
            orefs = refs[nr + nc + npar + nout:]
            _, vjp = jax.vjp(lambda rr, pp: tuple(fn(*rr, *cv, *pp)), rv, pv)
            drows, dpars = vjp(tuple(dv))
            for o_ref, g in zip(orefs[:nr], drows):
                o_ref[...] = g.astype(F32)

            @pl.when(i == 0)
            def _():
                for o_ref in orefs[nr:]:
                    o_ref[...] = jnp.zeros(o_ref.shape, F32)

            for o_ref, g in zip(orefs[nr:], dpars):
                o_ref[...] += g.astype(F32)

        res = pl.pallas_call(
            body, name=name + "_bwd", grid=(T // tr_b,),
            out_shape=[jax.ShapeDtypeStruct(r.shape, F32) for r in rows]
            + [jax.ShapeDtypeStruct(p.shape, F32) for p in params],
            in_specs=[_row_spec(tr_b, c) for c in in_cols] + [_full_spec(p.shape) for p in params]
            + [_row_spec(tr_b, c) for c in out_cols],
            out_specs=[_row_spec(tr_b, r.shape[1]) for r in rows] + [_full_spec(p.shape) for p in params],
            compiler_params=_params(dimension_semantics=("arbitrary",)),
        )(*rows, *consts, *params, *douts)
        return tuple(res[:nr]), tuple(res[nr:])

    @jax.custom_vjp
    def op(rows, consts, params):
        return tuple(fwd_call(rows, consts, params))

    def op_fwd(rows, consts, params):
        return tuple(fwd_call(rows, consts, params)), (rows, consts, params)

    def op_bwd(res, douts):
        rows, consts, params = res
        drows, dpars = bwd_call(rows, consts, params, douts)
        return drows, tuple(jnp.zeros_like(c) for c in consts), dpars

    op.defvjp(op_fwd, op_bwd)
    return op(tuple(rows), tuple(consts), tuple(params))


def _fn_norm(x, g):
    return (_rms(x, g),)


def _fn_add_norm(x, y, g):
    s = x + y
    return s, _rms(s, g)


def _fn_swiglu(gu):
    return (jax.nn.silu(gu[:, :FFN_HIDDEN]) * gu[:, FFN_HIDDEN:],)


def _heads(x, n, width=HEAD_DIM):
    return [x[:, h * width:(h + 1) * width] for h in range(n)]


def _fn_sb_pre(qkv, qn, kn):
    hs = _heads(qkv, 3 * SB_HEADS)
    q = jnp.concatenate([_rms(h, qn) for h in hs[:SB_HEADS]], axis=1)
    k = jnp.concatenate([_rms(h, kn) for h in hs[SB_HEADS:2 * SB_HEADS]], axis=1)
    v = jnp.concatenate(hs[2 * SB_HEADS:], axis=1)
    return q, k, v


def _fn_dsw_pre(qkv, cosm, sinm, qn, kn):
    hs = _heads(qkv, 3 * DSW_HEADS)
    q = jnp.concatenate([rope(_rms(h, qn), cosm, sinm) for h in hs[:DSW_HEADS]], axis=1)
    k = jnp.concatenate([rope(_rms(h, kn), cosm, sinm) for h in hs[DSW_HEADS:2 * DSW_HEADS]], axis=1)
    v = jnp.concatenate(hs[2 * DSW_HEADS:], axis=1)
    return q, k, v


def _fn_dsw_combine(o, lse):
    os_, ls_ = _heads(o, DSW_HEADS), _heads(lse, DSW_HEADS)
    out = [None] * DSW_HEADS
    for hg in range(DSW_HG):
        l3 = [ls_[g * DSW_HG + hg] for g in range(3)]
        m = jnp.maximum(jnp.maximum(l3[0], l3[1]), l3[2])
        e3 = [jnp.exp(l - m) for l in l3]
        den = e3[0] + e3[1] + e3[2]
        for g in range(3):
            out[g * DSW_HG + hg] = os_[g * DSW_HG + hg] * (e3[g] / den)
    return (jnp.concatenate(out, axis=1),)


def _l2(x):
    return x * lax.rsqrt(jnp.sum(x * x, axis=-1, keepdims=True) + NORM_EPS)


def _fn_gdn_pre(qkv, ba, a_log, dt_bias):
    x = jax.nn.silu(qkv)
    hs = _heads(x, 2 * GDN_K_HEADS + GDN_V_HEADS)
    rep = GDN_V_HEADS // GDN_K_HEADS
    qh = [_l2(h) * HEAD_DIM ** -0.5 for h in hs[:GDN_K_HEADS]]
    kh = [_l2(h) for h in hs[GDN_K_HEADS:2 * GDN_K_HEADS]]
    q = jnp.concatenate([qh[h // rep] for h in range(GDN_V_HEADS)], axis=1)
    k = jnp.concatenate([kh[h // rep] for h in range(GDN_V_HEADS)], axis=1)
    v = jnp.concatenate(hs[2 * GDN_K_HEADS:], axis=1)
    b = ba[:, :GDN_V_HEADS]
    a = ba[:, GDN_V_HEADS:2 * GDN_V_HEADS]
    beta = jax.nn.sigmoid(b)
    g = -jnp.exp(a_log) * jax.nn.softplus(a + dt_bias)
    rows = b.shape[0]
    beta_b = jnp.concatenate([jnp.broadcast_to(beta[:, h:h + 1], (rows, HEAD_DIM)) for h in range(GDN_V_HEADS)], axis=1)
    g_b = jnp.concatenate([jnp.broadcast_to(g[:, h:h + 1], (rows, HEAD_DIM)) for h in range(GDN_V_HEADS)], axis=1)
    return q, k, v, g_b, beta_b


def _fn_gdn_post(o, z, o_norm):
    os_, zs = _heads(o, GDN_V_HEADS), _heads(z, GDN_V_HEADS)
    return (jnp.concatenate([_rms(oh, o_norm) * jax.nn.silu(zh) for oh, zh in zip(os_, zs)], axis=1),)


def _expm1(x):
    return jnp.tanh(0.5 * x) * (jnp.exp(x) + 1.0)


def _fn_lru_gates(xc, conv_b, w_a, b_a, w_x, b_x, lam):
    xr = xc + conv_b
    xs = _heads(xr, LRU_BLOCKS, LRU_BLOCK_DIM)
    r = jnp.concatenate([bdot(xs[n], w_a[n], "nn") for n in range(LRU_BLOCKS)], axis=1) + b_a
    i = jnp.concatenate([bdot(xs[n], w_x[n], "nn") for n in range(LRU_BLOCKS)], axis=1) + b_x
    r = jax.nn.sigmoid(r)
    i = jax.nn.sigmoid(i)
    log_a = -LRU_C * r * jax.nn.softplus(-lam)
    a = jnp.exp(log_a)
    u = jnp.sqrt(-_expm1(2.0 * log_a)) * (i * xr)
    return a, u


def _fn_lru_out(hs, gate):
    c = math.sqrt(2.0 / math.pi)
    gl = 0.5 * gate * (1.0 + jnp.tanh(c * (gate + 0.044715 * (gate * gate * gate))))
    return (hs * gl,)


def loss_head(x, f, target):
    T, D = x.shape
    tr = _pick(T, (256, 128, 64, 32, 16, 8))

    def body(x_ref, f_ref, t_ref, l_ref, dy_ref):
        i = pl.program_id(0)
        err = (x_ref[...] + f_ref[...]) - t_ref[...]
        dy_ref[...] = err * (1.0 / D)
        part = 0.5 * jnp.sum(jnp.mean(err * err, axis=-1, keepdims=True), axis=0, keepdims=True)

        @pl.when(i == 0)
        def _():
            l_ref[...] = jnp.zeros(l_ref.shape, F32)

        l_ref[...] += jnp.broadcast_to(part, l_ref.shape)

    l, dy = pl.pallas_call(
        body, name="loss_head", grid=(T // tr,),
        out_shape=[jax.ShapeDtypeStruct((8, LANES), F32), jax.ShapeDtypeStruct((T, D), F32)],
        in_specs=[_row_spec(tr, D)] * 3, out_specs=[_full_spec((8, LANES)), _row_spec(tr, D)],
        compiler_params=_params(dimension_semantics=("arbitrary",)),
    )(x, f, target)
    return l[0, 0], dy


def _as_bf16(x):
    return x.astype(BF16).astype(F32)


def _shift_rows(x, s):
    if s == 0:
        return x
    n = x.shape[0]
    row = lax.broadcasted_iota(jnp.int32, x.shape, 0)
    rolled = pltpu.roll(x, s % n, axis=0)
    keep = (row >= s) if s > 0 else (row < n + s)
    return jnp.where(keep, rolled, 0.0)


def _conv_fwd_call(x, w):
    T, C = x.shape
    K = w.shape[0]
    cb = _pick(C, (256, 128))

    def body(x_ref, w_ref, y_ref):
        xv, wv = _as_bf16(x_ref[...]), _as_bf16(w_ref[...])
        acc = xv * wv[K - 1:K, :]
        for k in range(K - 1):
            acc = acc + _shift_rows(xv, K - 1 - k) * wv[k:k + 1, :]
        y_ref[...] = acc

    return pl.pallas_call(
        body, name=f"conv_fwd_{C}", grid=(C // cb,), out_shape=jax.ShapeDtypeStruct((T, C), F32),
        in_specs=[pl.BlockSpec((T, cb), lambda j: (0, j)), pl.BlockSpec((K, cb), lambda j: (0, j))],
        out_specs=pl.BlockSpec((T, cb), lambda j: (0, j)),
        compiler_params=_params(dimension_semantics=("arbitrary",)),
    )(x, w)


def _conv_bwd_call(x, w, dy):
    T, C = x.shape
    K = w.shape[0]
    cb = _pick(C, (256, 128))

    def body(x_ref, w_ref, dy_ref, dx_ref, dw_ref):
        xv, dv, wv = _as_bf16(x_ref[...]), _as_bf16(dy_ref[...]), _as_bf16(w_ref[...])
        acc = dv * wv[K - 1:K, :]
        rows = [None] * K
        rows[K - 1] = jnp.sum(dv * xv, axis=0, keepdims=True)
        for k in range(K - 1):
            s = K - 1 - k
            acc = acc + _shift_rows(dv, -s) * wv[k:k + 1, :]
            rows[k] = jnp.sum(dv * _shift_rows(xv, s), axis=0, keepdims=True)
        dx_ref[...] = acc
        dw_ref[...] = jnp.concatenate(rows + [jnp.zeros((8 - K, cb), F32)], axis=0)

    dx, dw = pl.pallas_call(
        body, name=f"conv_bwd_{C}", grid=(C // cb,),
        out_shape=[jax.ShapeDtypeStruct((T, C), F32), jax.ShapeDtypeStruct((8, C), F32)],
        in_specs=[pl.BlockSpec((T, cb), lambda j: (0, j)), pl.BlockSpec((K, cb), lambda j: (0, j)),
                  pl.BlockSpec((T, cb), lambda j: (0, j))],
        out_specs=[pl.BlockSpec((T, cb), lambda j: (0, j)), pl.BlockSpec((8, cb), lambda j: (0, j))],
        compiler_params=_params(dimension_semantics=("arbitrary",)),
    )(x, w, dy)
    return dx, dw[:K]


@jax.custom_vjp
def dwconv(x, w):
    return _conv_fwd_call(x, w)


def _dwconv_fwd(x, w):
    return _conv_fwd_call(x, w), (x, w)


def _dwconv_bwd(res, dy):
    return _conv_bwd_call(*res, dy)


dwconv.defvjp(_dwconv_fwd, _dwconv_bwd)


def _scan_fwd_call(a, u):
    T, C = a.shape
    cb = _pick(C, (256, 128))

    def body(a_ref, u_ref, h_ref):
        def step(i, h):
            r = pl.multiple_of(i * 8, 8)
            at, ut = a_ref[pl.ds(r, 8), :], u_ref[pl.ds(r, 8), :]
            rows = []
            for j in range(8):
                h = at[j:j + 1, :] * h + ut[j:j + 1, :]
                rows.append(h)
            h_ref[pl.ds(r, 8), :] = jnp.concatenate(rows, axis=0)
            return h

        lax.fori_loop(0, T // 8, step, jnp.zeros((1, cb), F32))

    return pl.pallas_call(
        body, name="lru_scan_fwd", grid=(C // cb,), out_shape=jax.ShapeDtypeStruct((T, C), F32),
        in_specs=[pl.BlockSpec((T, cb), lambda j: (0, j))] * 2, out_specs=pl.BlockSpec((T, cb), lambda j: (0, j)),
        compiler_params=_params(dimension_semantics=("arbitrary",)),
    )(a, u)


def _scan_bwd_call(a, hs, dh):
    T, C = a.shape
    cb = _pick(C, (256, 128))
    nt = T // 8

    def body(a_ref, h_ref, dh_ref, da_ref, du_ref):
        def step(s, carry):
            i = nt - 1 - s
            r = pl.multiple_of(i * 8, 8)
            rp = pl.multiple_of(jnp.maximum(i - 1, 0) * 8, 8)
            at, ht, dt = a_ref[pl.ds(r, 8), :], h_ref[pl.ds(r, 8), :], dh_ref[pl.ds(r, 8), :]
            hprev_tile = h_ref[pl.ds(rp, 8), :]
            h_before = jnp.where(i > 0, hprev_tile[7:8, :], 0.0)
            da_rows, du_rows = [None] * 8, [None] * 8
            for j in range(7, -1, -1):
                lam = dt[j:j + 1, :] + carry
                du_rows[j] = lam
                hp = ht[j - 1:j, :] if j > 0 else h_before
                da_rows[j] = lam * hp
                carry = at[j:j + 1, :] * lam
            da_ref[pl.ds(r, 8), :] = jnp.concatenate(da_rows, axis=0)
            du_ref[pl.ds(r, 8), :] = jnp.concatenate(du_rows, axis=0)
            return carry

        lax.fori_loop(0, nt, step, jnp.zeros((1, cb), F32))

    return pl.pallas_call(
        body, name="lru_scan_bwd", grid=(C // cb,), out_shape=[jax.ShapeDtypeStruct((T, C), F32)] * 2,
        in_specs=[pl.BlockSpec((T, cb), lambda j: (0, j))] * 3,
        out_specs=[pl.BlockSpec((T, cb), lambda j: (0, j))] * 2,
        compiler_params=_params(dimension_semantics=("arbitrary",)),
    )(a, hs, dh)


@jax.custom_vjp
def lru_scan(a, u):
    return _scan_fwd_call(a, u)


def _lru_scan_fwd(a, u):
    hs = _scan_fwd_call(a, u)
    return hs, (a, hs)


def _lru_scan_bwd(res, dh):
    a, hs = res
    return tuple(_scan_bwd_call(a, hs, dh))


lru_scan.defvjp(_lru_scan_fwd, _lru_scan_bwd)


def _tri(n, kind):
    r = lax.broadcasted_iota(jnp.int32, (n, n), 0)
    c = lax.broadcasted_iota(jnp.int32, (n, n), 1)
    m = {"gt": r > c, "le": r <= c, "lt": r < c, "ge": r >= c, "eq": r == c}[kind]
    return jnp.where(m, 1.0, 0.0).astype(BF16)


def _sb_fwd_call(q, k, v, shards=()):
    T, HD = q.shape
    H = HD // HEAD_DIM
    tb = _pick(T, (SB_BLOCK, 128))
    scale = HEAD_DIM ** -0.5

    def body(q_ref, k_ref, v_ref, o_ref, tot_ref):
        i = pl.program_id(1)
        qb = q_ref[...].astype(BF16)
        u_gt = _tri(tb, "gt")
        row = lax.broadcasted_iota(jnp.int32, (tb, tb), 0)
        col = lax.broadcasted_iota(jnp.int32, (tb, tb), 1)

        def step(j, carry, diagonal):
            acc, run = carry
            off = pl.multiple_of(j * tb, tb)
            kb = k_ref[pl.ds(off, tb), :].astype(BF16)
            vb = v_ref[pl.ds(off, tb), :].astype(BF16)
            z = lax.dot_general(qb, kb, _DN["nt"], preferred_element_type=F32) * scale
            sp, ls, _ = _softplus_parts(z)
            if diagonal:
                sp = jnp.where(col < row, sp, 0.0)
            between = _dot01_raw(sp, u_gt, _DN["nn"], False) + run
            w = jnp.exp(ls - between)
            if diagonal:
                w = jnp.where(col < row, w, 0.0)
            acc = acc + lax.dot_general(w.astype(BF16), vb, _DN["nn"], preferred_element_type=F32)
            run = run + jnp.sum(sp, axis=1, keepdims=True)
            return acc, run

        carry = step(i, (jnp.zeros((tb, HEAD_DIM), F32), jnp.zeros((tb, 1), F32)), True)
        acc, run = lax.fori_loop(0, i, lambda s, c: step(i - 1 - s, c, False), carry)
        o_ref[...] = acc
        tot_ref[...] = jnp.broadcast_to(run, (tb, HEAD_DIM))

    blk = pl.BlockSpec((tb, HEAD_DIM), lambda h, i: (i, h))
    full = pl.BlockSpec((T, HEAD_DIM), lambda h, i: (0, h))
    (o, tot), gathered = carried_call(
        body, [GatherJob(s) for s in shards], (H, T // tb), name="sb_attn_fwd",
        out_shape=[jax.ShapeDtypeStruct((T, HD), F32)] * 2, in_specs=[blk, full, full], out_specs=[blk, blk],
        scratch_shapes=[], operands=(q, k, v))
    return o, tot, tuple(gathered)


def _sb_bwd_call(q, k, v, tot, do):
    T, HD = q.shape
    H = HD // HEAD_DIM
    tb = _pick(T, (SB_BLOCK, 128))
    scale = HEAD_DIM ** -0.5

    def body(q_ref, k_ref, v_ref, tot_ref, do_ref, dq_ref, dk_ref, dv_ref):
        i = pl.program_id(1)

        @pl.when(i == 0)
        def _():
            dk_ref[...] = jnp.zeros(dk_ref.shape, F32)
            dv_ref[...] = jnp.zeros(dv_ref.shape, F32)

        qb = q_ref[...].astype(BF16)
        dob = do_ref[...].astype(BF16)
        tot = tot_ref[:, 0:1]
        u_le = _tri(tb, "le")
        u_lt = _tri(tb, "lt")
        row = lax.broadcasted_iota(jnp.int32, (tb, tb), 0)
        col = lax.broadcasted_iota(jnp.int32, (tb, tb), 1)

        def step(j, carry, diagonal):
            dq, cs, cd = carry
            off = pl.multiple_of(j * tb, tb)
            kb = k_ref[pl.ds(off, tb), :].astype(BF16)
            vb = v_ref[pl.ds(off, tb), :].astype(BF16)
            z = lax.dot_general(qb, kb, _DN["nt"], preferred_element_type=F32) * scale
            sp, ls, sig = _softplus_parts(z)
            if diagonal:
                sp = jnp.where(col < row, sp, 0.0)
            prefix = _dot01_raw(sp, u_le, _DN["nn"], False) + cs
            w = jnp.exp(ls - (tot - prefix))
            if diagonal:
                w = jnp.where(col < row, w, 0.0)
            wb = w.astype(BF16)
            dv_ref[pl.ds(off, tb), :] += lax.dot_general(wb, dob, _DN["tn"], preferred_element_type=F32)
            dw = lax.dot_general(dob, vb, _DN["nt"], preferred_element_type=F32)
            dl = dw * w
            dsp = -(_dot01_raw(dl, u_lt, _DN["nn"], False) + cd)
            dz = (dl * (1.0 - sig) + dsp * sig) * scale
            if diagonal:
                dz = jnp.where(col < row, dz, 0.0)
            dzb = dz.astype(BF16)
            dq = dq + lax.dot_general(dzb, kb, _DN["nn"], preferred_element_type=F32)
            dk_ref[pl.ds(off, tb), :] += lax.dot_general(dzb, qb, _DN["tn"], preferred_element_type=F32)
            cs = cs + jnp.sum(sp, axis=1, keepdims=True)
            cd = cd + jnp.sum(dl, axis=1, keepdims=True)
            return dq, cs, cd

        z1 = jnp.zeros((tb, 1), F32)
        carry = lax.fori_loop(0, i, lambda j, c: step(j, c, False), (jnp.zeros((tb, HEAD_DIM), F32), z1, z1))
        dq, _, _ = step(i, carry, True)
        dq_ref[...] = dq

    blk = pl.BlockSpec((tb, HEAD_DIM), lambda h, i: (i, h))
    full = pl.BlockSpec((T, HEAD_DIM), lambda h, i: (0, h))
    return pl.pallas_call(
        body, name="sb_attn_bwd", grid=(H, T // tb), out_shape=[jax.ShapeDtypeStruct((T, HD), F32)] * 3,
        in_specs=[blk, full, full, blk, blk], out_specs=[blk, full, full],
        compiler_params=_params(dimension_semantics=("arbitrary", "arbitrary")),
    )(q, k, v, tot, do)


@jax.custom_vjp
def sb_attn(q, k, v, shards):
    o, _, gathered = _sb_fwd_call(q, k, v, shards)
    return o, gathered


def _sb_attn_fwd(q, k, v, shards):
    o, tot, gathered = _sb_fwd_call(q, k, v, shards)
    return (o, gathered), (q, k, v, tot, shards)


def _sb_attn_bwd(res, cts):
    q, k, v, tot, shards = res
    return (*_sb_bwd_call(q, k, v, tot, cts[0]), tuple(jnp.zeros_like(s) for s in shards))


sb_attn.defvjp(_sb_attn_fwd, _sb_attn_bwd)


def _dsw_tile(q, kp, kc, vp, vc, n):
    blk = DSW_BLOCK
    scale = HEAD_DIM ** -0.5
    qi = lax.broadcasted_iota(jnp.int32, (blk, blk), 0)
    kj = lax.broadcasted_iota(jnp.int32, (blk, blk), 1)
    neg = -1e30
    s_p = jnp.where((kj >= qi) & (n > 0), bdot(q, kp, "nt") * scale, neg)
    s_c = jnp.where(kj <= qi, bdot(q, kc, "nt") * scale, neg)
    m = jnp.maximum(jnp.max(s_p, axis=-1, keepdims=True), jnp.max(s_c, axis=-1, keepdims=True))
    p_p, p_c = jnp.exp(s_p - m), jnp.exp(s_c - m)
    den = jnp.sum(p_p, axis=-1, keepdims=True) + jnp.sum(p_c, axis=-1, keepdims=True)
    o = bdot(p_p / den, vp, "nn") + bdot(p_c / den, vc, "nn")
    lse = m + jnp.log(den)
    return o, jnp.broadcast_to(lse, (blk, HEAD_DIM))


def _dsw_specs(nsub):
    cur = pl.BlockSpec((None, DSW_BLOCK, HEAD_DIM), lambda r, h, n: (r, n, h))
    prev = pl.BlockSpec((None, DSW_BLOCK, HEAD_DIM), lambda r, h, n: (r, jnp.maximum(n - 1, 0), h))
    whole = pl.BlockSpec((None, nsub, HEAD_DIM), lambda r, h, n: (r, 0, h))
    return cur, prev, whole


def _dsw_fwd_call(q, k, v):
    d, nsub, HD = q.shape
    cur, prev, _ = _dsw_specs(nsub)

    def body(q_ref, kp_ref, kc_ref, vp_ref, vc_ref, o_ref, l_ref):
        n = pl.program_id(2)
        o, l = _dsw_tile(q_ref[...], kp_ref[...], kc_ref[...], vp_ref[...], vc_ref[...], n)
        o_ref[...] = o
        l_ref[...] = l

    return pl.pallas_call(
        body, name=f"dsw_attn_fwd_d{d}", grid=(d, HD // HEAD_DIM, nsub // DSW_BLOCK),
        out_shape=[jax.ShapeDtypeStruct(q.shape, F32)] * 2,
        in_specs=[cur, prev, cur, prev, cur], out_specs=[cur, cur],
        compiler_params=_params(dimension_semantics=("arbitrary",) * 3),
    )(q, k, k, v, v)


def _dsw_bwd_call(q, k, v, do, dl):
    d, nsub, HD = q.shape
    cur, prev, whole = _dsw_specs(nsub)
    blk = DSW_BLOCK

    def body(q_ref, kp_ref, kc_ref, vp_ref, vc_ref, do_ref, dl_ref, dq_ref, dk_ref, dv_ref):
        n = pl.program_id(2)

        @pl.when(n == 0)
        def _():
            dk_ref[...] = jnp.zeros(dk_ref.shape, F32)
            dv_ref[...] = jnp.zeros(dv_ref.shape, F32)

        _, vjp = jax.vjp(lambda a, b, c, e, f: _dsw_tile(a, b, c, e, f, n),
                         q_ref[...], kp_ref[...], kc_ref[...], vp_ref[...], vc_ref[...])
        dq, dkp, dkc, dvp, dvc = vjp((do_ref[...], dl_ref[...]))
        dq_ref[...] = dq
        c0 = pl.multiple_of(n * blk, blk)
        p0 = pl.multiple_of(jnp.maximum(n - 1, 0) * blk, blk)
        dk_ref[pl.ds(c0, blk), :] += dkc
        dv_ref[pl.ds(c0, blk), :] += dvc
        dk_ref[pl.ds(p0, blk), :] += dkp
        dv_ref[pl.ds(p0, blk), :] += dvp

    return pl.pallas_call(
        body, name=f"dsw_attn_bwd_d{d}", grid=(d, HD // HEAD_DIM, nsub // blk),
        out_shape=[jax.ShapeDtypeStruct(q.shape, F32)] * 3,
        in_specs=[cur, prev, cur, prev, cur, cur, cur], out_specs=[cur, whole, whole],
        compiler_params=_params(dimension_semantics=("arbitrary",) * 3),
    )(q, k, k, v, v, do, dl)


@jax.custom_vjp
def dsw_attn(q, k, v):
    return tuple(_dsw_fwd_call(q, k, v))


def _dsw_attn_fwd(q, k, v):
    return tuple(_dsw_fwd_call(q, k, v)), (q, k, v)


def _dsw_attn_bwd(res, cts):
    return tuple(_dsw_bwd_call(*res, *cts))


dsw_attn.defvjp(_dsw_attn_fwd, _dsw_attn_bwd)


_LOG2_CHUNK = GDN_CHUNK.bit_length() - 1
_LOG2_HEAD_DIM = HEAD_DIM.bit_length() - 1


def _gdn_step(state, q, k, v, gb, bb):
    C, NH = GDN_CHUNK, GDN_HEADS_PER_STEP
    R = NH * C
    r = lax.broadcasted_iota(jnp.int32, (R, R), 0)
    c = lax.broadcasted_iota(jnp.int32, (R, R), 1)
    same = lax.shift_right_logical(r, _LOG2_CHUNK) == lax.shift_right_logical(c, _LOG2_CHUNK)
    causal, strict = same & (r >= c), same & (r > c)
    gc = dot01(gb, jnp.where(causal, 1.0, 0.0).astype(BF16), True)
    g_sq = jnp.concatenate([gc] * (R // HEAD_DIM), axis=1)
    g_row = dot01(jnp.where(r == c, g_sq, 0.0), jnp.ones((R, R), BF16), True)
    decay = jnp.where(causal, jnp.exp(jnp.where(causal, g_sq - g_row, 0.0)), 0.0)
    kb, vb = k * bb, v * bb
    a_mat = jnp.where(strict, bdot(kb, k, "nt") * decay, 0.0)
    x = -a_mat
    t_mat = jnp.where(r == c, 1.0, 0.0) + x
    for _ in range(_LOG2_CHUNK - 1):
        x = mm3(x, x)
        t_mat = t_mat + mm3(t_mat, x)
    uw = bdot(t_mat, jnp.concatenate([vb, kb * jnp.exp(gc)], axis=1), "nn")
    u, w = uw[:, :HEAD_DIM], uw[:, HEAD_DIM:]
    hr = lax.shift_right_logical(lax.broadcasted_iota(jnp.int32, (R, NH * HEAD_DIM), 0), _LOG2_CHUNK)
    hc = lax.shift_right_logical(lax.broadcasted_iota(jnp.int32, (R, NH * HEAD_DIM), 1), _LOG2_HEAD_DIM)

    def widen(m):
        return jnp.where(hr == hc, jnp.concatenate([m] * NH, axis=1), 0.0)

    v_new = u - bdot(widen(w), state, "nn")
    attn = bdot(q, k, "nt") * decay
    out = bdot(widen(q * jnp.exp(gc)), state, "nn") + bdot(attn, v_new, "nn")
    last = [gc[h * C + C - 1:h * C + C, :] for h in range(NH)]
    g_last_rows = jnp.concatenate([jnp.broadcast_to(l, (C, HEAD_DIM)) for l in last], axis=0)
    g_last_state = jnp.concatenate([jnp.broadcast_to(l, (HEAD_DIM, HEAD_DIM)) for l in last], axis=0)
    k_dec = k * jnp.exp(g_last_rows - gc)
    new_state = state * jnp.exp(g_last_state) + bdot(widen(k_dec), v_new, "tn")
    return new_state, out


def _gdn_stack(ref):
    return jnp.concatenate([ref[:, h * HEAD_DIM:(h + 1) * HEAD_DIM] for h in range(GDN_HEADS_PER_STEP)], axis=0)


def _gdn_unstack(ref, val):
    for h in range(GDN_HEADS_PER_STEP):
        ref[:, h * HEAD_DIM:(h + 1) * HEAD_DIM] = val[h * GDN_CHUNK:(h + 1) * GDN_CHUNK]


def _gdn_fwd_call(q, k, v, gb, bb, shards=()):
    T, HD = v.shape
    H = HD // HEAD_DIM
    N = T // GDN_CHUNK
    hb = GDN_HEADS_PER_STEP
    W = hb * HEAD_DIM

    def body(q_ref, k_ref, v_ref, g_ref, b_ref, o_ref, s_ref, state):
        n = pl.program_id(1)

        @pl.when(n == 0)
        def _():
            state[...] = jnp.zeros(state.shape, F32)

        s_in = state[...]
        s_ref[...] = s_in
        ns, o = _gdn_step(s_in, *[_gdn_stack(ref) for ref in (q_ref, k_ref, v_ref, g_ref, b_ref)])
        state[...] = ns
        _gdn_unstack(o_ref, o)

    blk = pl.BlockSpec((GDN_CHUNK, W), lambda h, n: (n, h))
    sblk = pl.BlockSpec((None, W, HEAD_DIM), lambda h, n: (n, h, 0))
    (o, states), gathered = carried_call(
        body, [GatherJob(s) for s in shards], (H // hb, N), name="gdn_chunk_fwd",
        out_shape=[jax.ShapeDtypeStruct((T, HD), F32), jax.ShapeDtypeStruct((N, H * HEAD_DIM, HEAD_DIM), F32)],
        in_specs=[blk] * 5, out_specs=[blk, sblk], scratch_shapes=[pltpu.VMEM((W, HEAD_DIM), F32)],
        operands=(q, k, v, gb, bb))
    return o, states, tuple(gathered)


def _gdn_bwd_call(q, k, v, gb, bb, states, do):
    T, HD = v.shape
    H = HD // HEAD_DIM
    N = T // GDN_CHUNK
    hb = GDN_HEADS_PER_STEP
    W = hb * HEAD_DIM

    def body(q_ref, k_ref, v_ref, g_ref, b_ref, s_ref, do_ref, dq_ref, dk_ref, dv_ref, dg_ref, db_ref, dstate):
        n = pl.program_id(1)

        @pl.when(n == 0)
        def _():
            dstate[...] = jnp.zeros(dstate.shape, F32)

        _, vjp = jax.vjp(_gdn_step, s_ref[...], *[_gdn_stack(ref) for ref in (q_ref, k_ref, v_ref, g_ref, b_ref)])
        ds, *grads = vjp((dstate[...], _gdn_stack(do_ref)))
        dstate[...] = ds
        for ref, g in zip((dq_ref, dk_ref, dv_ref, dg_ref, db_ref), grads):
            _gdn_unstack(ref, g)

    blk = pl.BlockSpec((GDN_CHUNK, W), lambda h, n: (N - 1 - n, h))
    sblk = pl.BlockSpec((None, W, HEAD_DIM), lambda h, n: (N - 1 - n, h, 0))
    return pl.pallas_call(
        body, name="gdn_chunk_bwd", grid=(H // hb, N), out_shape=[jax.ShapeDtypeStruct((T, HD), F32)] * 5,
        in_specs=[blk] * 5 + [sblk, blk], out_specs=[blk] * 5,
        scratch_shapes=[pltpu.VMEM((W, HEAD_DIM), F32)],
        compiler_params=_params(dimension_semantics=("arbitrary", "arbitrary")),
    )(q, k, v, gb, bb, states, do)


@jax.custom_vjp
def gdn_core(q, k, v, gb, bb, shards):
    o, _, gathered = _gdn_fwd_call(q, k, v, gb, bb, shards)
    return o, gathered


def _gdn_core_fwd(q, k, v, gb, bb, shards):
    o, states, gathered = _gdn_fwd_call(q, k, v, gb, bb, shards)
    return (o, gathered), (q, k, v, gb, bb, states, shards)


def _gdn_core_bwd(res, cts):
    *core, shards = res
    return (*_gdn_bwd_call(*core, cts[0]), tuple(jnp.zeros_like(s) for s in shards))


gdn_core.defvjp(_gdn_core_fwd, _gdn_core_bwd)


def _mixer_sb(h, w, shards=()):
    qkv = lin_t(h, w["sb_in_t"])
    q, k, v = rowop(_fn_sb_pre, "sb_pre", [qkv], [], [w["sb_q_norm"], w["sb_k_norm"]], [D_MODEL] * 3)
    o, gathered = sb_attn(q, k, v, tuple(shards))
    return lin_n(o, w["sb_out"]), gathered


def _mixer_gdn(h, w, shards=()):
    wt = w["gdn_in_t"]
    nqkv = 2 * GDN_KEY_DIM + GDN_VAL_DIM
    qkv = lin_t(h, wt[:nqkv])
    z = lin_t(h, wt[nqkv:nqkv + GDN_VAL_DIM])
    w_ba = jnp.pad(wt[nqkv + GDN_VAL_DIM:], ((0, LANES - 2 * GDN_V_HEADS), (0, 0)))
    ba = lin_t(h, w_ba)
    qkv = dwconv(qkv, w["gdn_conv_w"])
    q, k, v, gb, bb = rowop(_fn_gdn_pre, "gdn_pre", [qkv, ba], [], [w["gdn_a_log"], w["gdn_dt_bias"]],
                            [GDN_VAL_DIM] * 5)
    o, gathered = gdn_core(q, k, v, gb, bb, tuple(shards))
    (y,) = rowop(_fn_gdn_post, "gdn_post", [o, z], [], [w["gdn_o_norm"]], [GDN_VAL_DIM])
    return lin_n(y, w["gdn_out"]), gathered


def _to_strided(x, cols, d):
    T = x.shape[0]
    return x[:, cols].reshape(T // d, d, -1).transpose(1, 0, 2)


def _from_strided(x):
    d, n, c = x.shape
    return x.transpose(1, 0, 2).reshape(d * n, c)


def _mixer_dsw(h, cosm, sinm, w):
    qkv = lin_t(h, w["dsw_in_t"])
    nhd = DSW_HEADS * HEAD_DIM
    q, k, v = rowop(_fn_dsw_pre, "dsw_pre", [qkv], [cosm, sinm], [w["dsw_q_norm"], w["dsw_k_norm"]], [nhd] * 3)
    outs, lses = [], []
    for gi, (_, d) in enumerate(DSW_GROUPS):
        cols = slice(gi * DSW_HG * HEAD_DIM, (gi + 1) * DSW_HG * HEAD_DIM)
        o_g, l_g = dsw_attn(_to_strided(q, cols, d), _to_strided(k, cols, d), _to_strided(v, cols, d))
        outs.append(_from_strided(o_g))
        lses.append(_from_strided(l_g))
    (o,) = rowop(_fn_dsw_combine, "dsw_combine", [jnp.concatenate(outs, axis=1), jnp.concatenate(lses, axis=1)],
                 [], [], [nhd])
    return lin_n(o, w["dsw_out"])


def _mixer_lru(h, w):
    wt = w["lru_in_t"]
    gate = lin_t(h, wt[:LRU_WIDTH])
    xr = dwconv(lin_t(h, wt[LRU_WIDTH:]), w["lru_conv_w"])
    a, u = rowop(_fn_lru_gates, "lru_gates", [xr], [],
                 [w["lru_conv_b"], w["lru_w_a"], w["lru_b_a"], w["lru_w_x"], w["lru_b_x"], w["lru_lambda"]],
                 [LRU_WIDTH] * 2)
    hs = lru_scan(a, u)
    (y,) = rowop(_fn_lru_out, "lru_out", [hs, gate], [], [], [LRU_WIDTH])
    return lin_n(y, w["lru_out"])


def mixer_segment(i, x, f_prev, cosm, sinm, w, shards):
    if i == 0:
        (h,) = rowop(_fn_norm, "norm", [x], [], [w["mix_norm"]], [D_MODEL])
    else:
        x, h = rowop(_fn_add_norm, "add_norm", [x, f_prev], [], [w["mix_norm"]], [D_MODEL] * 2)
    kind, gathered = i % 4, ()
    if kind == 0:
        y, gathered = _mixer_sb(h, w, shards)
    elif kind == 1:
        y, gathered = _mixer_gdn(h, w, shards)
    elif kind == 2:
        y = _mixer_dsw(h, cosm, sinm, w)
    else:
        y = _mixer_lru(h, w)
    x, h = rowop(_fn_add_norm, "add_norm", [x, y], [], [w["ffn_norm"]], [D_MODEL] * 2)
    return (x, h), gathered


def swiglu_act(gu):
    return rowop(_fn_swiglu, "swiglu", [gu], [], [], [FFN_HIDDEN])[0]


ANY = pl.BlockSpec(memory_space=pl.ANY)


SLAB_BYTES = 4 * 1024 * 1024


def _col_tile(rows, C, itemsize):
    return _tile(C, max(LANES, SLAB_BYTES // (rows * itemsize)))


class GatherJob:
    def __init__(self, shard):
        self.operand = shard
        r, C = shard.shape
        self.out_shape = jax.ShapeDtypeStruct((N_DEV, r, C), shard.dtype)
        self.scratch = [pltpu.SemaphoreType.DMA((7,)), pltpu.SemaphoreType.DMA((7,)), pltpu.SemaphoreType.DMA]

    def _parts(self, x_ref, out_ref, send_sems, recv_sems, local_sem):
        x, y, c = lax.axis_index("x"), lax.axis_index("y"), lax.axis_index("c")
        me, sibling = (x, y, c), (x, y, 1 - c)
        chips = [(1 - x, y), (x, 1 - y), (1 - x, 1 - y)]

        def slot(px, py, pc):
            return out_ref.at[4 * px + 2 * py + pc]

        def copy(k, block, to, src=None):
            return pltpu.make_async_remote_copy(
                src_ref=slot(*block) if src is None else src, dst_ref=slot(*block),
                send_sem=send_sems.at[k], recv_sem=recv_sems.at[k], device_id=to, device_id_type=MESH)

        def mine():
            return pltpu.make_async_copy(x_ref, slot(*me), local_sem)

        def first():
            return [copy(0, me, sibling, src=x_ref)] + [copy(1 + j, me, (*chip, c), src=x_ref)
                                                        for j, chip in enumerate(chips)]

        def passed():
            return [copy(4 + j, (*chip, c), sibling) for j, chip in enumerate(chips)]

        def landed():
            return [copy(1 + j, (*chip, c), me) for j, chip in enumerate(chips)]

        def from_sibling():
            return [copy(0, sibling, me)] + [copy(4 + j, (*chip, 1 - c), me) for j, chip in enumerate(chips)]

        return mine, first, passed, landed, from_sibling

    def start(self, x_ref, out_ref, *sems):
        mine, first, _, _, _ = self._parts(x_ref, out_ref, *sems)
        mine().start()
        for cp in first():
            cp.start()

    def finish(self, x_ref, out_ref, *sems):
        mine, first, passed, landed, from_sibling = self._parts(x_ref, out_ref, *sems)
        onward = passed()
        for arrived, cp in zip(landed(), onward):
            arrived.wait_recv()
            cp.start()
        for cp in from_sibling():
            cp.wait_recv()
        for cp in first() + onward:
            cp.wait_send()
        mine().wait()


class PairJob:
    def __init__(self, g):
        self.operand = g
        _, _, r, C = g.shape
        self.out_shape = jax.ShapeDtypeStruct((N_CHIPS, r, C), g.dtype)
        self.scratch = [pltpu.SemaphoreType.DMA((N_CHIPS,)), pltpu.SemaphoreType.DMA((N_CHIPS,))]

    def _copies(self, g_ref, out_ref, send_sems, recv_sems):
        x, y, c = lax.axis_index("x"), lax.axis_index("y"), lax.axis_index("c")
        return [pltpu.make_async_remote_copy(
            src_ref=g_ref.at[q, 1 - c], dst_ref=out_ref.at[q], send_sem=send_sems.at[q], recv_sem=recv_sems.at[q],
            device_id=(x, y, 1 - c), device_id_type=MESH) for q in range(N_CHIPS)]

    def start(self, *refs):
        for cp in self._copies(*refs):
            cp.start()

    def finish(self, *refs):
        for cp in self._copies(*refs):
            cp.wait()


class ChipsJob:
    def __init__(self, p):
        self.operand = p
        self.out_shape = jax.ShapeDtypeStruct(p.shape, p.dtype)
        self.scratch = [pltpu.SemaphoreType.DMA((3,)), pltpu.SemaphoreType.DMA((3,)), pltpu.SemaphoreType.DMA]

    def _parts(self, p_ref, out_ref, send_sems, recv_sems, local_sem):
        x, y, c = lax.axis_index("x"), lax.axis_index("y"), lax.axis_index("c")
        mychip = 2 * x + y
        chips = [(1 - x, y), (x, 1 - y), (1 - x, 1 - y)]
        def mine():
            return pltpu.make_async_copy(p_ref.at[mychip], out_ref.at[mychip], local_sem)

        def sends():
            return [pltpu.make_async_remote_copy(
                src_ref=p_ref.at[2 * cx + cy], dst_ref=out_ref.at[mychip], send_sem=send_sems.at[j],
                recv_sem=recv_sems.at[j], device_id=(cx, cy, c), device_id_type=MESH)
                for j, (cx, cy) in enumerate(chips)]

        def arrivals():
            return [pltpu.make_async_remote_copy(
                src_ref=p_ref.at[mychip], dst_ref=out_ref.at[2 * cx + cy], send_sem=send_sems.at[j],
                recv_sem=recv_sems.at[j], device_id=(cx, cy, c), device_id_type=MESH)
                for j, (cx, cy) in enumerate(chips)]

        return mine, sends, arrivals

    def start(self, *refs):
        mine, sends, _ = self._parts(*refs)
        mine().start()
        for cp in sends():
            cp.start()

    def finish(self, *refs):
        mine, sends, arrivals = self._parts(*refs)
        for cp in arrivals():
            cp.wait_recv()
        for cp in sends():
            cp.wait_send()
        mine().wait()


def carried_call(body, jobs, grid, *, name, out_shape, in_specs, out_specs, scratch_shapes, operands):
    jobs = list(jobs)
    n_in, n_out, n_scr, nj = len(in_specs), len(out_specs), len(scratch_shapes), len(jobs)
    n_sem = [len(j.scratch) for j in jobs]

    def full_body(*refs):
        core_in = refs[:n_in]
        job_in = refs[n_in:n_in + nj]
        core_out = refs[n_in + nj:n_in + nj + n_out]
        job_out = refs[n_in + nj + n_out:n_in + 2 * nj + n_out]
        rest = refs[n_in + 2 * nj + n_out:]
        core_scr, sems, pos = rest[:n_scr], [], n_scr
        for n in n_sem:
            sems.append(rest[pos:pos + n])
            pos += n
        ids = [pl.program_id(a) for a in range(len(grid))]
        first = functools.reduce(jnp.logical_and, [i == 0 for i in ids])
        last = functools.reduce(jnp.logical_and, [i == g - 1 for i, g in zip(ids, grid)])
        if jobs:
            @pl.when(first)
            def _():
                for j, job in enumerate(jobs):
                    job.start(job_in[j], job_out[j], *sems[j])

        body(*core_in, *core_out, *core_scr)
        if jobs:
            @pl.when(last)
            def _():
                for j, job in enumerate(jobs):
                    job.finish(job_in[j], job_out[j], *sems[j])

    res = pl.pallas_call(
        full_body, name=name, grid=grid,
        out_shape=list(out_shape) + [j.out_shape for j in jobs],
        in_specs=list(in_specs) + [ANY] * nj, out_specs=list(out_specs) + [ANY] * nj,
        scratch_shapes=list(scratch_shapes) + [s for j in jobs for s in j.scratch],
        compiler_params=_params(dimension_semantics=("arbitrary",) * len(grid)),
    )(*operands, *[j.operand for j in jobs])
    return res[:n_out], res[n_out:]


def _exchange(job, name):
    def body(*refs):
        job.start(*refs)
        job.finish(*refs)

    return pl.pallas_call(
        body, name=name, out_shape=job.out_shape, in_specs=[ANY], out_specs=ANY, scratch_shapes=job.scratch,
        compiler_params=pltpu.CompilerParams(has_side_effects=True),
    )(job.operand)


def all_gather(shard, name):
    return _exchange(GatherJob(shard), name)


def exchange_pair(g, name):
    return _exchange(PairJob(g), name)


def exchange_chips(p, name):
    return _exchange(ChipsJob(p), name)


def pair_add(g, got, name):
    _, _, r, C = g.shape
    cb = _col_tile(r, C, g.dtype.itemsize)
    c = lax.axis_index("c")

    def body(c_ref, a_ref, b_ref, o_ref):
        o_ref[...] = (a_ref[...].astype(F32) + b_ref[...].astype(F32)).astype(o_ref.dtype)

    return pl.pallas_call(
        body, name=name, out_shape=jax.ShapeDtypeStruct((N_CHIPS, r, C), g.dtype),
        grid_spec=pltpu.PrefetchScalarGridSpec(
            num_scalar_prefetch=1, grid=(N_CHIPS, C // cb),
            in_specs=[pl.BlockSpec((None, None, r, cb), lambda q, j, cr: (q, cr[0], 0, j)),
                      pl.BlockSpec((None, r, cb), lambda q, j, cr: (q, 0, j))],
            out_specs=pl.BlockSpec((None, r, cb), lambda q, j, cr: (q, 0, j))),
        compiler_params=_params(dimension_semantics=("arbitrary", "arbitrary")),
    )(jnp.reshape(c, (1,)).astype(jnp.int32), g, got)


def sum_slots(parts, name):
    n, r, C = parts.shape
    cb = _col_tile(n * r, C, parts.dtype.itemsize)

    def body(p_ref, o_ref):
        acc = p_ref[0].astype(F32)
        for q in range(1, n):
            acc = acc + p_ref[q].astype(F32)
        o_ref[...] = acc

    return pl.pallas_call(
        body, name=name, out_shape=jax.ShapeDtypeStruct((r, C), F32), grid=(C // cb,),
        in_specs=[pl.BlockSpec((n, r, cb), lambda j: (0, 0, j))], out_specs=pl.BlockSpec((r, cb), lambda j: (0, j)),
        compiler_params=_params(dimension_semantics=("arbitrary",)),
    )(parts)


def reduce_scatter(g, tag):
    _, r, C = g.shape
    g4 = g.reshape(N_CHIPS, 2, r, C)
    got = exchange_pair(g4, f"rs_pair_{tag}")
    pairs = pair_add(g4, got, f"rs_pair_add_{tag}")
    parts = exchange_chips(pairs, f"rs_chips_{tag}")
    return sum_slots(parts, f"rs_sum_{tag}")


def all_reduce(v, tag):
    return sum_slots(all_gather(v, f"ar_gather_{tag}"), f"ar_sum_{tag}")


def adamw(w, g, m, v, name):
    R, C = w.shape
    tr = R
    for cand in (512, 256, 128, 64, 32, 16, 8):
        if R % cand == 0 and cand * C * 4 * 7 * 2 <= 40 * 1024 * 1024:
            tr = cand
            break
    c1 = 1.0 - ADAM_B1 ** ADAM_STEP
    c2 = 1.0 - ADAM_B2 ** ADAM_STEP

    def body(w_ref, g_ref, m_ref, v_ref, d_ref, nm_ref, nv_ref):
        gv = g_ref[...]
        nm = ADAM_B1 * m_ref[...] + (1.0 - ADAM_B1) * gv
        nv = ADAM_B2 * v_ref[...] + (1.0 - ADAM_B2) * (gv * gv)
        d_ref[...] = -ADAM_LR * ((nm / c1) / (jnp.sqrt(nv / c2) + ADAM_EPS) + ADAM_WD * w_ref[...])
        nm_ref[...] = nm
        nv_ref[...] = nv

    spec = pl.BlockSpec((tr, C), lambda i: (i, 0))
    return pl.pallas_call(
        body, name=name, out_shape=[jax.ShapeDtypeStruct((R, C), F32)] * 3, grid=(R // tr,),
        in_specs=[spec] * 4, out_specs=[spec] * 3,
        compiler_params=_params(dimension_semantics=("arbitrary",)),
    )(w, g, m, v)


NAMES = ['mix_norm', 'ffn_norm', 'ffn_w_gu', 'ffn_w_down', 'sb_w_in', 'sb_q_norm', 'sb_k_norm', 'sb_w_out',
         'gdn_w_in', 'gdn_conv_w', 'gdn_a_log', 'gdn_dt_bias', 'gdn_o_norm', 'gdn_w_out', 'dsw_w_in', 'dsw_q_norm',
         'dsw_k_norm', 'dsw_w_out', 'lru_w_in', 'lru_conv_w', 'lru_conv_b', 'lru_w_a', 'lru_b_a', 'lru_w_x',
         'lru_b_x', 'lru_lambda', 'lru_w_out']
REPLICATED = ['mix_norm', 'ffn_norm', 'sb_q_norm', 'sb_k_norm', 'gdn_a_log', 'gdn_dt_bias', 'gdn_o_norm',
              'dsw_q_norm', 'dsw_k_norm']
SMALL_SHARDED = ['gdn_conv_w', 'lru_conv_w', 'lru_conv_b', 'lru_b_a', 'lru_b_x', 'lru_lambda']
IN_T = ['sb_w_in', 'gdn_w_in', 'dsw_w_in', 'lru_w_in']
OUT_N = ['sb_w_out', 'gdn_w_out', 'dsw_w_out', 'lru_w_out']


def _pack(arrs, pad_rows_to=8):
    flat = jnp.concatenate([a.reshape(-1) for a in arrs])
    n = flat.shape[0]
    rows = -(-n // LANES)
    rows = -(-rows // pad_rows_to) * pad_rows_to
    return jnp.pad(flat, (0, rows * LANES - n)).reshape(rows, LANES)


def _unpack(buf, shapes):
    flat = buf.reshape(-1)
    out, o = [], 0
    for s in shapes:
        n = math.prod(s)
        out.append(flat[o:o + n].reshape(s))
        o += n
    return out


def kernel(x, positions, mix_norm, ffn_norm, ffn_w_gu, ffn_w_down, sb_w_in, sb_q_norm, sb_k_norm, sb_w_out, gdn_w_in, gdn_conv_w, gdn_a_log, gdn_dt_bias, gdn_o_norm, gdn_w_out, dsw_w_in, dsw_q_norm, dsw_k_norm, dsw_w_out, lru_w_in, lru_conv_w, lru_conv_b, lru_w_a, lru_b_a, lru_w_x, lru_b_x, lru_lambda, lru_w_out, loss_target, m_mix_norm, m_ffn_norm, m_ffn_w_gu, m_ffn_w_down, m_sb_w_in, m_sb_q_norm, m_sb_k_norm, m_sb_w_out, m_gdn_w_in, m_gdn_conv_w, m_gdn_a_log, m_gdn_dt_bias, m_gdn_o_norm, m_gdn_w_out, m_dsw_w_in, m_dsw_q_norm, m_dsw_k_norm, m_dsw_w_out, m_lru_w_in, m_lru_conv_w, m_lru_conv_b, m_lru_w_a, m_lru_b_a, m_lru_w_x, m_lru_b_x, m_lru_lambda, m_lru_w_out, v_mix_norm, v_ffn_norm, v_ffn_w_gu, v_ffn_w_down, v_sb_w_in, v_sb_q_norm, v_sb_k_norm, v_sb_w_out, v_gdn_w_in, v_gdn_conv_w, v_gdn_a_log, v_gdn_dt_bias, v_gdn_o_norm, v_gdn_w_out, v_dsw_w_in, v_dsw_q_norm, v_dsw_k_norm, v_dsw_w_out, v_lru_w_in, v_lru_conv_w, v_lru_conv_b, v_lru_w_a, v_lru_b_a, v_lru_w_x, v_lru_b_x, v_lru_lambda, v_lru_w_out):
    args = locals()
    W = {n: args[n] for n in NAMES}
    M = {n: args["m_" + n] for n in NAMES}
    V = {n: args["v_" + n] for n in NAMES}
    T = x.shape[1]
    x2 = x[0]
    tgt = loss_target[0]

    S = {}
    for n in IN_T:
        S[n] = W[n][0].T.astype(BF16)
    for n in OUT_N:
        S[n] = W[n][0].astype(BF16)
    for i in range(DEPTH):
        S[f"ffn_gu{i}"] = ffn_w_gu[i].T.astype(BF16)
        S[f"ffn_down{i}"] = ffn_w_down[i].astype(BF16)
    S["lru_gates"] = jnp.concatenate(
        [lru_w_a[0].reshape(-1, LRU_BLOCK_DIM), lru_w_x[0].reshape(-1, LRU_BLOCK_DIM)], axis=0).astype(BF16)
    Gt = {}

    def flat(key):
        return Gt[key].reshape(-1, Gt[key].shape[-1])

    Gt["sb_w_in"] = all_gather(S["sb_w_in"], "ag_sb_w_in")
    Gt["sb_w_out"] = all_gather(S["sb_w_out"], "ag_sb_w_out")
    small_shapes = [W[n].shape for n in SMALL_SHARDED]
    sm = all_gather(_pack([W[n] for n in SMALL_SHARDED]), "ag_small")
    sm = [jnp.stack(parts) for parts in zip(*[_unpack(sm[p], small_shapes) for p in range(N_DEV)])]
    smd = dict(zip(SMALL_SHARDED, sm))

    def mixer_weights(i):
        kind = i % 4
        w = {"mix_norm": mix_norm[i:i + 1], "ffn_norm": ffn_norm[i:i + 1]}
        if kind == 0:
            w.update(sb_in_t=flat("sb_w_in"), sb_out=flat("sb_w_out"), sb_q_norm=sb_q_norm, sb_k_norm=sb_k_norm)
        elif kind == 1:
            w.update(gdn_in_t=flat("gdn_w_in"), gdn_out=flat("gdn_w_out"), gdn_a_log=gdn_a_log,
                     gdn_dt_bias=gdn_dt_bias, gdn_o_norm=gdn_o_norm,
                     gdn_conv_w=smd["gdn_conv_w"][:, 0].transpose(1, 0, 2).reshape(4, -1))
        elif kind == 2:
            w.update(dsw_in_t=flat("dsw_w_in"), dsw_out=flat("dsw_w_out"), dsw_q_norm=dsw_q_norm,
                     dsw_k_norm=dsw_k_norm)
        else:
            gg = Gt["lru_gates"].reshape(N_DEV, 2, LRU_BLOCKS, 32, LRU_BLOCK_DIM)
            gg = gg.transpose(1, 2, 0, 3, 4).reshape(2, LRU_BLOCKS, LRU_BLOCK_DIM, LRU_BLOCK_DIM).astype(F32)
            w.update(lru_in_t=flat("lru_w_in"), lru_out=flat("lru_w_out"), lru_w_a=gg[0], lru_w_x=gg[1],
                     lru_conv_w=smd["lru_conv_w"][:, 0].transpose(1, 0, 2).reshape(4, -1),
                     lru_conv_b=smd["lru_conv_b"][:, 0].reshape(1, -1),
                     lru_lambda=smd["lru_lambda"][:, 0].reshape(1, -1),
                     lru_b_a=smd["lru_b_a"][:, 0].transpose(1, 0, 2).reshape(1, -1),
                     lru_b_x=smd["lru_b_x"][:, 0].transpose(1, 0, 2).reshape(1, -1))
        return w

    carried_by_mixer = {0: ["ffn_gu0", "ffn_down0", "gdn_w_in", "gdn_w_out", "ffn_gu1", "ffn_down1"],
                        1: ["ffn_gu2", "ffn_down2", "lru_w_in", "lru_gates", "ffn_gu3", "ffn_down3"]}
    carried_by_ffn = {0: (["dsw_w_in"], ["dsw_w_out", "lru_w_out"])}

    half = ROPE_DIM // 2
    inv_freq = ROPE_THETA ** (-jnp.arange(half, dtype=F32) / half)
    ang = positions[0].astype(F32)[:, None] * inv_freq
    cs, sn = jnp.cos(ang), jnp.sin(ang)
    cosm = jnp.concatenate([cs, cs, jnp.ones((T, HEAD_DIM - ROPE_DIM), F32)], axis=1)
    sinm = jnp.concatenate([-sn, sn, jnp.zeros((T, HEAD_DIM - ROPE_DIM), F32)], axis=1)

    def gather_jobs(keys):
        return [GatherJob(S[k]) for k in keys]

    xs, f, seg_vjps, ffn_res = x2, None, [], []
    for i in range(DEPTH):
        keys = carried_by_mixer.get(i, [])
        shards = tuple(S[k] for k in keys)
        if i == 0:
            (xs, h), vjp_i, gathered = jax.vjp(
                lambda xx, ww: mixer_segment(0, xx, None, cosm, sinm, ww, shards), xs, mixer_weights(i), has_aux=True)
        else:
            (xs, h), vjp_i, gathered = jax.vjp(
                lambda xx, ff, ww, i=i, shards=shards: mixer_segment(i, xx, ff, cosm, sinm, ww, shards),
                xs, f, mixer_weights(i), has_aux=True)
        Gt.update(zip(keys, gathered))
        seg_vjps.append(vjp_i)
        keys_gu, keys_down = carried_by_ffn.get(i, ([], []))
        gu, got = mm_carry(h, flat(f"ffn_gu{i}"), "nt", F32, gather_jobs(keys_gu))
        Gt.update(zip(keys_gu, got))
        act, sw_vjp = jax.vjp(swiglu_act, gu)
        f, got = mm_carry(act, flat(f"ffn_down{i}"), "nn", F32, gather_jobs(keys_down))
        Gt.update(zip(keys_down, got))
        ffn_res.append((h, act, sw_vjp))
    loss_part, dy = loss_head(xs, f, tgt)
    loss = lax.psum(loss_part, ("x", "y", "c"))

    reduced, gw_small, gw_rep = {}, {}, {}

    def to_partials(i, dw, dwd, dwgu):
        out = {f"ffn_gu{i}": dwgu.reshape(N_DEV, -1, D_MODEL), f"ffn_down{i}": dwd.reshape(N_DEV, -1, D_MODEL)}
        for k, g in dw.items():
            if k.endswith("_in_t"):
                out[k.replace("_in_t", "_w_in")] = g.reshape(N_DEV, -1, D_MODEL)
            elif k.endswith("_out"):
                out[k.replace("_out", "_w_out")] = g.reshape(N_DEV, -1, D_MODEL)
            elif k in ("mix_norm", "ffn_norm"):
                gw_rep[(k, i)] = g
            elif k in REPLICATED:
                gw_rep[k] = g
            elif k not in ("lru_w_a", "lru_w_x"):
                gw_small[k] = g
        if "lru_w_a" in dw:
            gg = jnp.stack([dw["lru_w_a"], dw["lru_w_x"]]).reshape(2, LRU_BLOCKS, N_DEV, 32, LRU_BLOCK_DIM)
            out["lru_gates"] = gg.transpose(2, 0, 1, 3, 4).reshape(N_DEV, 2 * LRU_BLOCKS * 32, LRU_BLOCK_DIM)
        return out

    def spread(keys, sizes, capacities):
        room, bins = list(capacities), [[] for _ in capacities]
        for k in sorted(keys, key=lambda k: -sizes[k]):
            b = max(range(len(room)), key=lambda j: room[j])
            bins[b].append(k)
            room[b] -= sizes[k]
        return bins

    ready, dx, df = {}, dy, dy
    for i in reversed(range(DEPTH)):
        h, act, sw_vjp = ffn_res[i]
        keys = list(ready)
        g4 = {k: ready[k].reshape(N_CHIPS, 2, *ready[k].shape[1:]) for k in keys}
        dact, got = mm_carry(df, flat(f"ffn_down{i}"), "nt", F32, [PairJob(g4[k]) for k in keys])
        pairs = {k: pair_add(g4[k], g, f"rs_pair_add_{k}") for k, g in zip(keys, got)}
        sizes = {k: math.prod(pairs[k].shape[1:]) * pairs[k].dtype.itemsize for k in keys}
        bins = spread(keys, sizes, [c * D_MODEL * 2 for c in (800, 1300, 1550)])
        dwd, p0 = mm_carry(act, df, "tn", BF16, [ChipsJob(pairs[k]) for k in bins[0]])
        (dgu,) = sw_vjp(dact)
        dh, p1 = mm_carry(dgu, flat(f"ffn_gu{i}"), "nn", F32, [ChipsJob(pairs[k]) for k in bins[1]])
        dwgu, p2 = mm_carry(dgu, h, "tn", BF16, [ChipsJob(pairs[k]) for k in bins[2]])
        for k, parts in zip(bins[0] + bins[1] + bins[2], list(p0) + list(p1) + list(p2)):
            reduced[k] = sum_slots(parts, f"rs_sum_{k}")
        if i == 0:
            dx, dw = seg_vjps[i]((dx, dh))
        else:
            dx, df, dw = seg_vjps[i]((dx, dh))
        ready = to_partials(i, dw, dwd, dwgu)
    gx = dx
    for k, g in ready.items():
        reduced[k] = reduce_scatter(g, k)

    G = {}
    for n in IN_T:
        G[n] = reduced[n].T[None]
    for n in OUT_N:
        G[n] = reduced[n][None]
    G["ffn_w_gu"] = jnp.stack([reduced[f"ffn_gu{i}"].T for i in range(DEPTH)])
    G["ffn_w_down"] = jnp.stack([reduced[f"ffn_down{i}"] for i in range(DEPTH)])
    gg = reduced["lru_gates"].reshape(2, 1, LRU_BLOCKS, 32, LRU_BLOCK_DIM)
    G["lru_w_a"], G["lru_w_x"] = gg[0], gg[1]
    gw = dict(gw_small)
    for n in REPLICATED:
        gw[n] = (jnp.concatenate([gw_rep[(n, i)] for i in range(DEPTH)], axis=0) if n in ("mix_norm", "ffn_norm")
                 else gw_rep[n])
    gs = {
        "gdn_conv_w": gw["gdn_conv_w"].reshape(4, N_DEV, -1).transpose(1, 0, 2)[:, None],
        "lru_conv_w": gw["lru_conv_w"].reshape(4, N_DEV, -1).transpose(1, 0, 2)[:, None],
        "lru_conv_b": gw["lru_conv_b"].reshape(N_DEV, 1, -1),
        "lru_lambda": gw["lru_lambda"].reshape(N_DEV, 1, -1),
        "lru_b_a": gw["lru_b_a"].reshape(LRU_BLOCKS, N_DEV, 32).transpose(1, 0, 2)[:, None],
        "lru_b_x": gw["lru_b_x"].reshape(LRU_BLOCKS, N_DEV, 32).transpose(1, 0, 2)[:, None],
    }
    packed = jnp.stack([_pack([gs[n][p] for n in SMALL_SHARDED]) for p in range(N_DEV)])
    for n, g in zip(SMALL_SHARDED, _unpack(reduce_scatter(packed, "small"), small_shapes)):
        G[n] = g
    rep_shapes = [W[n].shape for n in REPLICATED]
    for n, g in zip(REPLICATED, _unpack(all_reduce(_pack([gw[n] for n in REPLICATED]), "rep"), rep_shapes)):
        G[n] = g

    D, NM, NV = {}, {}, {}
    big = [n for n in NAMES if n not in REPLICATED and n not in SMALL_SHARDED]
    for n in big:
        shp = W[n].shape
        two = (-1, shp[-1])
        d, nm, nv = adamw(W[n].reshape(two), G[n].reshape(two), M[n].reshape(two), V[n].reshape(two), f"adamw_{n}")
        D[n], NM[n], NV[n] = d.reshape(shp), nm.reshape(shp), nv.reshape(shp)
    for group, tag in ((SMALL_SHARDED, "small"), (REPLICATED, "rep")):
        shapes = [W[n].shape for n in group]
        res = adamw(_pack([W[n] for n in group]), _pack([G[n] for n in group]), _pack([M[n] for n in group]),
                    _pack([V[n] for n in group]), f"adamw_{tag}")
        for dst, buf in zip((D, NM, NV), res):
            for n, a in zip(group, _unpack(buf, shapes)):
                dst[n] = a

    return (loss, gx[None], *[G[n] for n in NAMES], *[D[n] for n in NAMES], *[NM[n] for n in NAMES],
            *[NV[n] for n in NAMES])
```

```python
import functools
import math

import jax
import jax.numpy as jnp
from jax import lax
from jax.experimental import pallas as pl
from jax.experimental.pallas import tpu as pltpu

F32 = jnp.float32
BF16 = jnp.bfloat16

D_MODEL = 2048
HEAD_DIM = 128
NORM_EPS = 1e-6
SB_HEADS = 16
SB_BLOCK = 256
SB_WIDE = (4, 2, 1)
GDN_K_HEADS = 16
GDN_V_HEADS = 32
GDN_KEY_DIM = 2048
GDN_VAL_DIM = 4096
GDN_CHUNK = 64
GDN_HEADS_PER_STEP = 4
DSW_GROUPS = ((128, 1), (512, 4), (2048, 16))
DSW_HG = 6
DSW_HEADS = 18
DSW_BLOCK = 128
ROPE_DIM = 32
ROPE_THETA = 500000.0
LRU_WIDTH = 2048
LRU_BLOCKS = 8
LRU_BLOCK_DIM = 256
LRU_C = 8.0
FFN_HIDDEN = 5632
DEPTH = 4
ADAM_LR, ADAM_B1, ADAM_B2, ADAM_EPS, ADAM_WD, ADAM_STEP = 0.001, 0.9, 0.999, 1e-08, 0.01, 10
N_DEV = 8
N_CHIPS = 4

V7X_VMEM_LIMIT = 56 * 1024 * 1024
LANES = 128
MESH = pl.DeviceIdType.MESH


def _params(**kw):
    return pltpu.CompilerParams(vmem_limit_bytes=V7X_VMEM_LIMIT, **kw)


def _pick(n, cands):
    for c in cands:
        if n % c == 0:
            return c
    return n


def _tile(n, cap, unit=LANES):
    best = None
    for t in range(unit, min(n, cap) + 1, unit):
        if n % t == 0:
            best = t
    return best or n


_DN = {"nn": (((1,), (0,)), ((), ())), "nt": (((1,), (1,)), ((), ())), "tn": (((0,), (0,)), ((), ()))}


def _raw_dot(a, b, mode):
    return lax.dot_general(a.astype(BF16), b.astype(BF16), _DN[mode], preferred_element_type=F32)


@functools.partial(jax.custom_vjp, nondiff_argnums=(2,))
def bdot(a, b, mode):
    return _raw_dot(a, b, mode)


def _bdot_fwd(a, b, mode):
    return _raw_dot(a, b, mode), (a, b)


def _bdot_bwd(mode, res, ct):
    a, b = res
    if mode == "nn":
        return bdot(ct, b, "nt"), bdot(a, ct, "tn")
    if mode == "nt":
        return bdot(ct, b, "nn"), bdot(ct, a, "tn")
    return bdot(b, ct, "nt"), bdot(a, ct, "nn")


bdot.defvjp(_bdot_fwd, _bdot_bwd)


def _split2(x):
    hi = x.astype(BF16)
    lo = (x - hi.astype(F32)).astype(BF16)
    return hi, lo


def _dot01_raw(x, m, dn, left):
    hi, lo = _split2(x)
    if left:
        return (lax.dot_general(m, hi, dn, preferred_element_type=F32)
                + lax.dot_general(m, lo, dn, preferred_element_type=F32))
    return (lax.dot_general(hi, m, dn, preferred_element_type=F32)
            + lax.dot_general(lo, m, dn, preferred_element_type=F32))


@functools.partial(jax.custom_vjp, nondiff_argnums=(2,))
def dot01(x, m, left):
    return _dot01_raw(x, m, _DN["nn"], left)


def _dot01_fwd(x, m, left):
    return _dot01_raw(x, m, _DN["nn"], left), m


def _dot01_bwd(left, m, ct):
    dx = _dot01_raw(ct, m, _DN["tn"] if left else _DN["nt"], left)
    return dx, jnp.zeros_like(m)


dot01.defvjp(_dot01_fwd, _dot01_bwd)


def _mm3(a, b):
    ah, al = _split2(a)
    bh, bl = _split2(b)
    dn = _DN["nn"]
    return (lax.dot_general(ah, bh, dn, preferred_element_type=F32)
            + lax.dot_general(ah, bl, dn, preferred_element_type=F32)
            + lax.dot_general(al, bh, dn, preferred_element_type=F32))


@jax.custom_vjp
def mm3(a, b):
    return _mm3(a, b)


def _mm3_fwd(a, b):
    return _mm3(a, b), (a, b)


def _mm3_bwd(res, ct):
    a, b = res
    return _mm3(ct, b.T), _mm3(a.T, ct)


mm3.defvjp(_mm3_fwd, _mm3_bwd)


def _softplus_parts(z):
    e = jnp.exp(-jnp.abs(z))
    l = jnp.log(1.0 + e)
    sp = jnp.maximum(z, 0.0) + l
    ls = jnp.minimum(z, 0.0) - l
    inv = 1.0 / (1.0 + e)
    sig = jnp.where(z >= 0.0, inv, e * inv)
    return sp, ls, sig


def _rms(x, g):
    return x * lax.rsqrt(jnp.mean(x * x, axis=-1, keepdims=True) + NORM_EPS) * g


def _swap16(x):
    lane = lax.broadcasted_iota(jnp.int32, x.shape, 1)
    return jnp.where(lane < 16, pltpu.roll(x, 112, axis=1), jnp.where(lane < 32, pltpu.roll(x, 16, axis=1), 0.0))


@jax.custom_vjp
def rope(x, cosm, sinm):
    return x * cosm + _swap16(x) * sinm


def _rope_fwd(x, cosm, sinm):
    return rope(x, cosm, sinm), (cosm, sinm)


def _rope_bwd(res, ct):
    cosm, sinm = res
    return ct * cosm + _swap16(ct * sinm), jnp.zeros_like(cosm), jnp.zeros_like(sinm)


rope.defvjp(_rope_fwd, _rope_bwd)


def mm(a, b, mode, out_dtype=F32):
    return mm_carry(a, b, mode, out_dtype, ())[0]


def mm_carry(a, b, mode, out_dtype=F32, jobs=()):
    if mode == "nt":
        (M, K), N = a.shape, b.shape[0]
    elif mode == "nn":
        (M, K), N = a.shape, b.shape[1]
    else:
        (K, M), N = a.shape, b.shape[1]
    if mode == "nt":
        tm, tn, tk = _tile(M, 1024), _tile(N, 768), _tile(K, 2048)
    elif mode == "nn":
        tm, tn, tk = _tile(M, 1024), _tile(N, 2048), _tile(K, 1024)
    else:
        tm, tn, tk = _tile(M, 1408), _tile(N, 2048), _tile(K, 512)
    nk = K // tk
    if mode == "nt":
        a_spec = pl.BlockSpec((tm, tk), lambda i, j, k: (i, k))
        b_spec = pl.BlockSpec((tn, tk), lambda i, j, k: (j, k))
    elif mode == "nn":
        a_spec = pl.BlockSpec((tm, tk), lambda i, j, k: (i, k))
        b_spec = pl.BlockSpec((tk, tn), lambda i, j, k: (k, j))
    else:
        a_spec = pl.BlockSpec((tk, tm), lambda i, j, k: (k, i))
        b_spec = pl.BlockSpec((tk, tn), lambda i, j, k: (k, j))

    def body(a_ref, b_ref, o_ref, *scr):
        p = _raw_dot(a_ref[...], b_ref[...], mode)
        if nk == 1:
            o_ref[...] = p.astype(o_ref.dtype)
        else:
            acc = scr[0]
            k = pl.program_id(2)

            @pl.when(k == 0)
            def _():
                acc[...] = p

            @pl.when(k > 0)
            def _():
                acc[...] += p

            @pl.when(k == nk - 1)
            def _():
                o_ref[...] = acc[...].astype(o_ref.dtype)

    grid = (M // tm, N // tn, nk)
    res = carried_call(
        body, jobs, grid, name=f"mm_{mode}_{M}x{N}x{K}" + ("_c" if jobs else ""),
        out_shape=[jax.ShapeDtypeStruct((M, N), out_dtype)], in_specs=[a_spec, b_spec],
        out_specs=[pl.BlockSpec((tm, tn), lambda i, j, k: (i, j))],
        scratch_shapes=[] if nk == 1 else [pltpu.VMEM((tm, tn), F32)], operands=(a, b))
    return res[0][0], res[1]


@jax.custom_vjp
def lin_t(x, wt):
    return mm(x, wt, "nt")


def _lin_t_fwd(x, wt):
    return mm(x, wt, "nt"), (x, wt)


def _lin_t_bwd(res, dy):
    x, wt = res
    return mm(dy, wt, "nn"), mm(dy, x, "tn", out_dtype=wt.dtype)


lin_t.defvjp(_lin_t_fwd, _lin_t_bwd)


@jax.custom_vjp
def lin_n(x, w):
    return mm(x, w, "nn")


def _lin_n_fwd(x, w):
    return mm(x, w, "nn"), (x, w)


def _lin_n_bwd(res, dy):
    x, w = res
    return mm(dy, w, "nt"), mm(x, dy, "tn", out_dtype=w.dtype)


lin_n.defvjp(_lin_n_fwd, _lin_n_bwd)


def _full_spec(shape):
    nd = len(shape)
    return pl.BlockSpec(tuple(shape), lambda i: (0,) * nd)


def _row_spec(tr, c):
    return pl.BlockSpec((tr, c), lambda i: (i, 0))


def _rowop_tr(total_cols, T):
    budget = 20 * 1024 * 1024
    for tr in (512, 256, 128, 64, 32, 16, 8):
        if T % tr == 0 and total_cols * tr * 4 * 2 <= budget:
            return tr
    return 8


def rowop(fn, name, rows, consts, params, out_cols):
    nr, nc, npar, nout = len(rows), len(consts), len(params), len(out_cols)
    T = rows[0].shape[0]
    in_cols = [r.shape[1] for r in rows] + [c.shape[1] for c in consts]
    tr_f = _rowop_tr(sum(in_cols) + sum(out_cols), T)
    tr_b = _rowop_tr(sum(in_cols) + sum(out_cols) + sum(r.shape[1] for r in rows), T)

    def fwd_call(rows, consts, params):
        def body(*refs):
            ins = [r[...] for r in refs[:nr + nc + npar]]
            outs = fn(*ins)
            for o_ref, o in zip(refs[nr + nc + npar:], outs):
                o_ref[...] = o.astype(F32)

        return pl.pallas_call(
            body, name=name + "_fwd", grid=(T // tr_f,),
            out_shape=[jax.ShapeDtypeStruct((T, c), F32) for c in out_cols],
            in_specs=[_row_spec(tr_f, c) for c in in_cols] + [_full_spec(p.shape) for p in params],
            out_specs=[_row_spec(tr_f, c) for c in out_cols],
            compiler_params=_params(dimension_semantics=("arbitrary",)),
        )(*rows, *consts, *params)

    def bwd_call(rows, consts, params, douts):
        def body(*refs):
            i = pl.program_id(0)
            rv = [r[...] for r in refs[:nr]]
            cv = [r[...] for r in refs[nr:nr + nc]]
            pv = [r[...] for r in refs[nr + nc:nr + nc + npar]]
            dv = [r[...] for r in refs[nr + nc + npar:nr + nc + npar + nout]]
            orefs = refs[nr + nc + npar + nout:]
            _, vjp = jax.vjp(lambda rr, pp: tuple(fn(*rr, *cv, *pp)), rv, pv)
            drows, dpars = vjp(tuple(dv))
            for o_ref, g in zip(orefs[:nr], drows):
                o_ref[...] = g.astype(F32)

            @pl.when(i == 0)
            def _():
                for o_ref in orefs[nr:]:
                    o_ref[...] = jnp.zeros(o_ref.shape, F32)

            for o_ref, g in zip(orefs[nr:], dpars):
                o_ref[...] += g.astype(F32)

        res = pl.pallas_call(
            body, name=name + "_bwd", grid=(T // tr_b,),
            out_shape=[jax.ShapeDtypeStruct(r.shape, F32) for r in rows]
            + [jax.ShapeDtypeStruct(p.shape, F32) for p in params],
            in_specs=[_row_spec(tr_b, c) for c in in_cols] + [_full_spec(p.shape) for p in params]
            + [_row_spec(tr_b, c) for c in out_cols],
            out_specs=[_row_spec(tr_b, r.shape[1]) for r in rows] + [_full_spec(p.shape) for p in params],
            compiler_params=_params(dimension_semantics=("arbitrary",)),
        )(*rows, *consts, *params, *douts)
        return tuple(res[:nr]), tuple(res[nr:])

    @jax.custom_vjp
    def op(rows, consts, params):
        return tuple(fwd_call(rows, consts, params))

    def op_fwd(rows, consts, params):
        return tuple(fwd_call(rows, consts, params)), (rows, consts, params)

    def op_bwd(res, douts):
        rows, consts, params = res
        drows, dpars = bwd_call(rows, consts, params, douts)
        return drows, tuple(jnp.zeros_like(c) for c in consts), dpars

    op.defvjp(op_fwd, op_bwd)
    return op(tuple(rows), tuple(consts), tuple(params))


def _fn_norm(x, g):
    return (_rms(x, g),)


def _fn_add_norm(x, y, g):
    s = x + y
    return s, _rms(s, g)


def _fn_swiglu(gu):
    return (jax.nn.silu(gu[:, :FFN_HIDDEN]) * gu[:, FFN_HIDDEN:],)


def _heads(x, n, width=HEAD_DIM):
    return [x[:, h * width:(h + 1) * width] for h in range(n)]


def _fn_sb_pre(qkv, qn, kn):
    hs = _heads(qkv, 3 * SB_HEADS)
    q = jnp.concatenate([_rms(h, qn) for h in hs[:SB_HEADS]], axis=1)
    k = jnp.concatenate([_rms(h, kn) for h in hs[SB_HEADS:2 * SB_HEADS]], axis=1)
    v = jnp.concatenate(hs[2 * SB_HEADS:], axis=1)
    return q, k, v


def _fn_dsw_pre(qkv, cosm, sinm, qn, kn):
    hs = _heads(qkv, 3 * DSW_HEADS)
    q = jnp.concatenate([rope(_rms(h, qn), cosm, sinm) for h in hs[:DSW_HEADS]], axis=1)
    k = jnp.concatenate([rope(_rms(h, kn), cosm, sinm) for h in hs[DSW_HEADS:2 * DSW_HEADS]], axis=1)
    v = jnp.concatenate(hs[2 * DSW_HEADS:], axis=1)
    return q, k, v


def _fn_dsw_combine(o, lse):
    os_, ls_ = _heads(o, DSW_HEADS), _heads(lse, DSW_HEADS)
    out = [None] * DSW_HEADS
    for hg in range(DSW_HG):
        l3 = [ls_[g * DSW_HG + hg] for g in range(3)]
        m = jnp.maximum(jnp.maximum(l3[0], l3[1]), l3[2])
        e3 = [jnp.exp(l - m) for l in l3]
        den = e3[0] + e3[1] + e3[2]
        for g in range(3):
            out[g * DSW_HG + hg] = os_[g * DSW_HG + hg] * (e3[g] / den)
    return (jnp.concatenate(out, axis=1),)


def _l2(x):
    return x * lax.rsqrt(jnp.sum(x * x, axis=-1, keepdims=True) + NORM_EPS)


def _fn_gdn_pre(qkv, ba, a_log, dt_bias):
    x = jax.nn.silu(qkv)
    hs = _heads(x, 2 * GDN_K_HEADS + GDN_V_HEADS)
    rep = GDN_V_HEADS // GDN_K_HEADS
    qh = [_l2(h) * HEAD_DIM ** -0.5 for h in hs[:GDN_K_HEADS]]
    kh = [_l2(h) for h in hs[GDN_K_HEADS:2 * GDN_K_HEADS]]
    q = jnp.concatenate([qh[h // rep] for h in range(GDN_V_HEADS)], axis=1)
    k = jnp.concatenate([kh[h // rep] for h in range(GDN_V_HEADS)], axis=1)
    v = jnp.concatenate(hs[2 * GDN_K_HEADS:], axis=1)
    b = ba[:, :GDN_V_HEADS]
    a = ba[:, GDN_V_HEADS:2 * GDN_V_HEADS]
    beta = jax.nn.sigmoid(b)
    g = -jnp.exp(a_log) * jax.nn.softplus(a + dt_bias)
    rows = b.shape[0]
    beta_b = jnp.concatenate([jnp.broadcast_to(beta[:, h:h + 1], (rows, HEAD_DIM)) for h in range(GDN_V_HEADS)], axis=1)
    g_b = jnp.concatenate([jnp.broadcast_to(g[:, h:h + 1], (rows, HEAD_DIM)) for h in range(GDN_V_HEADS)], axis=1)
    return q, k, v, g_b, beta_b


def _fn_gdn_post(o, z, o_norm):
    os_, zs = _heads(o, GDN_V_HEADS), _heads(z, GDN_V_HEADS)
    return (jnp.concatenate([_rms(oh, o_norm) * jax.nn.silu(zh) for oh, zh in zip(os_, zs)], axis=1),)


def _expm1(x):
    return jnp.tanh(0.5 * x) * (jnp.exp(x) + 1.0)


def _fn_lru_gates(xc, conv_b, w_a, b_a, w_x, b_x, lam):
    xr = xc + conv_b
    xs = _heads(xr, LRU_BLOCKS, LRU_BLOCK_DIM)
    r = jnp.concatenate([bdot(xs[n], w_a[n], "nn") for n in range(LRU_BLOCKS)], axis=1) + b_a
    i = jnp.concatenate([bdot(xs[n], w_x[n], "nn") for n in range(LRU_BLOCKS)], axis=1) + b_x
    r = jax.nn.sigmoid(r)
    i = jax.nn.sigmoid(i)
    log_a = -LRU_C * r * jax.nn.softplus(-lam)
    a = jnp.exp(log_a)
    u = jnp.sqrt(-_expm1(2.0 * log_a)) * (i * xr)
    return a, u


def _fn_lru_out(hs, gate):
    c = math.sqrt(2.0 / math.pi)
    gl = 0.5 * gate * (1.0 + jnp.tanh(c * (gate + 0.044715 * (gate * gate * gate))))
    return (hs * gl,)


def loss_head(x, f, target):
    T, D = x.shape
    tr = _pick(T, (256, 128, 64, 32, 16, 8))

    def body(x_ref, f_ref, t_ref, l_ref, dy_ref):
        i = pl.program_id(0)
        err = (x_ref[...] + f_ref[...]) - t_ref[...]
        dy_ref[...] = err * (1.0 / D)
        part = 0.5 * jnp.sum(jnp.mean(err * err, axis=-1, keepdims=True), axis=0, keepdims=True)

        @pl.when(i == 0)
        def _():
            l_ref[...] = jnp.zeros(l_ref.shape, F32)

        l_ref[...] += jnp.broadcast_to(part, l_ref.shape)

    l, dy = pl.pallas_call(
        body, name="loss_head", grid=(T // tr,),
        out_shape=[jax.ShapeDtypeStruct((8, LANES), F32), jax.ShapeDtypeStruct((T, D), F32)],
        in_specs=[_row_spec(tr, D)] * 3, out_specs=[_full_spec((8, LANES)), _row_spec(tr, D)],
        compiler_params=_params(dimension_semantics=("arbitrary",)),
    )(x, f, target)
    return l[0, 0], dy


def _as_bf16(x):
    return x.astype(BF16).astype(F32)


def _shift_rows(x, s):
    if s == 0:
        return x
    n = x.shape[0]
    row = lax.broadcasted_iota(jnp.int32, x.shape, 0)
    rolled = pltpu.roll(x, s % n, axis=0)
    keep = (row >= s) if s > 0 else (row < n + s)
    return jnp.where(keep, rolled, 0.0)


def _conv_fwd_call(x, w):
    T, C = x.shape
    K = w.shape[0]
    cb = _pick(C, (256, 128))

    def body(x_ref, w_ref, y_ref):
        xv, wv = _as_bf16(x_ref[...]), _as_bf16(w_ref[...])
        acc = xv * wv[K - 1:K, :]
        for k in range(K - 1):
            acc = acc + _shift_rows(xv, K - 1 - k) * wv[k:k + 1, :]
        y_ref[...] = acc

    return pl.pallas_call(
        body, name=f"conv_fwd_{C}", grid=(C // cb,), out_shape=jax.ShapeDtypeStruct((T, C), F32),
        in_specs=[pl.BlockSpec((T, cb), lambda j: (0, j)), pl.BlockSpec((K, cb), lambda j: (0, j))],
        out_specs=pl.BlockSpec((T, cb), lambda j: (0, j)),
        compiler_params=_params(dimension_semantics=("arbitrary",)),
    )(x, w)


def _conv_bwd_call(x, w, dy):
    T, C = x.shape
    K = w.shape[0]
    cb = _pick(C, (256, 128))

    def body(x_ref, w_ref, dy_ref, dx_ref, dw_ref):
        xv, dv, wv = _as_bf16(x_ref[...]), _as_bf16(dy_ref[...]), _as_bf16(w_ref[...])
        acc = dv * wv[K - 1:K, :]
        rows = [None] * K
        rows[K - 1] = jnp.sum(dv * xv, axis=0, keepdims=True)
        for k in range(K - 1):
            s = K - 1 - k
            acc = acc + _shift_rows(dv, -s) * wv[k:k + 1, :]
            rows[k] = jnp.sum(dv * _shift_rows(xv, s), axis=0, keepdims=True)
        dx_ref[...] = acc
        dw_ref[...] = jnp.concatenate(rows + [jnp.zeros((8 - K, cb), F32)], axis=0)

    dx, dw = pl.pallas_call(
        body, name=f"conv_bwd_{C}", grid=(C // cb,),
        out_shape=[jax.ShapeDtypeStruct((T, C), F32), jax.ShapeDtypeStruct((8, C), F32)],
        in_specs=[pl.BlockSpec((T, cb), lambda j: (0, j)), pl.BlockSpec((K, cb), lambda j: (0, j)),
                  pl.BlockSpec((T, cb), lambda j: (0, j))],
        out_specs=[pl.BlockSpec((T, cb), lambda j: (0, j)), pl.BlockSpec((8, cb), lambda j: (0, j))],
        compiler_params=_params(dimension_semantics=("arbitrary",)),
    )(x, w, dy)
    return dx, dw[:K]


@jax.custom_vjp
def dwconv(x, w):
    return _conv_fwd_call(x, w)


def _dwconv_fwd(x, w):
    return _conv_fwd_call(x, w), (x, w)


def _dwconv_bwd(res, dy):
    return _conv_bwd_call(*res, dy)


dwconv.defvjp(_dwconv_fwd, _dwconv_bwd)


def _scan_fwd_call(a, u):
    T, C = a.shape
    cb = _pick(C, (256, 128))

    def body(a_ref, u_ref, h_ref):
        def step(i, h):
            r = pl.multiple_of(i * 8, 8)
            at, ut = a_ref[pl.ds(r, 8), :], u_ref[pl.ds(r, 8), :]
            rows = []
            for j in range(8):
                h = at[j:j + 1, :] * h + ut[j:j + 1, :]
                rows.append(h)
            h_ref[pl.ds(r, 8), :] = jnp.concatenate(rows, axis=0)
            return h

        lax.fori_loop(0, T // 8, step, jnp.zeros((1, cb), F32))

    return pl.pallas_call(
        body, name="lru_scan_fwd", grid=(C // cb,), out_shape=jax.ShapeDtypeStruct((T, C), F32),
        in_specs=[pl.BlockSpec((T, cb), lambda j: (0, j))] * 2, out_specs=pl.BlockSpec((T, cb), lambda j: (0, j)),
        compiler_params=_params(dimension_semantics=("arbitrary",)),
    )(a, u)


def _scan_bwd_call(a, hs, dh):
    T, C = a.shape
    cb = _pick(C, (256, 128))
    nt = T // 8

    def body(a_ref, h_ref, dh_ref, da_ref, du_ref):
        def step(s, carry):
            i = nt - 1 - s
            r = pl.multiple_of(i * 8, 8)
            rp = pl.multiple_of(jnp.maximum(i - 1, 0) * 8, 8)
            at, ht, dt = a_ref[pl.ds(r, 8), :], h_ref[pl.ds(r, 8), :], dh_ref[pl.ds(r, 8), :]
            hprev_tile = h_ref[pl.ds(rp, 8), :]
            h_before = jnp.where(i > 0, hprev_tile[7:8, :], 0.0)
            da_rows, du_rows = [None] * 8, [None] * 8
            for j in range(7, -1, -1):
                lam = dt[j:j + 1, :] + carry
                du_rows[j] = lam
                hp = ht[j - 1:j, :] if j > 0 else h_before
                da_rows[j] = lam * hp
                carry = at[j:j + 1, :] * lam
            da_ref[pl.ds(r, 8), :] = jnp.concatenate(da_rows, axis=0)
            du_ref[pl.ds(r, 8), :] = jnp.concatenate(du_rows, axis=0)
            return carry

        lax.fori_loop(0, nt, step, jnp.zeros((1, cb), F32))

    return pl.pallas_call(
        body, name="lru_scan_bwd", grid=(C // cb,), out_shape=[jax.ShapeDtypeStruct((T, C), F32)] * 2,
        in_specs=[pl.BlockSpec((T, cb), lambda j: (0, j))] * 3,
        out_specs=[pl.BlockSpec((T, cb), lambda j: (0, j))] * 2,
        compiler_params=_params(dimension_semantics=("arbitrary",)),
    )(a, hs, dh)


@jax.custom_vjp
def lru_scan(a, u):
    return _scan_fwd_call(a, u)


def _lru_scan_fwd(a, u):
    hs = _scan_fwd_call(a, u)
    return hs, (a, hs)


def _lru_scan_bwd(res, dh):
    a, hs = res
    return tuple(_scan_bwd_call(a, hs, dh))


lru_scan.defvjp(_lru_scan_fwd, _lru_scan_bwd)


def _tri(n, kind):
    r = lax.broadcasted_iota(jnp.int32, (n, n), 0)
    c = lax.broadcasted_iota(jnp.int32, (n, n), 1)
    m = {"gt": r > c, "le": r <= c, "lt": r < c, "ge": r >= c, "eq": r == c}[kind]
    return jnp.where(m, 1.0, 0.0).astype(BF16)


def _sb_fwd_call(q, k, v, shards=()):
    T, HD = q.shape
    H = HD // HEAD_DIM
    tb = _pick(T, (SB_BLOCK, 128))
    scale = HEAD_DIM ** -0.5

    def body(q_ref, k_ref, v_ref, o_ref, tot_ref):
        i = pl.program_id(1)
        qb = q_ref[...].astype(BF16)
        u_gt = _tri(tb, "gt")
        row = lax.broadcasted_iota(jnp.int32, (tb, tb), 0)
        col = lax.broadcasted_iota(jnp.int32, (tb, tb), 1)

        def step(j, n, carry, diagonal):
            acc, run = carry
            off = pl.multiple_of(j * tb, tb)
            kb = k_ref[pl.ds(off, n * tb), :].astype(BF16)
            vb = v_ref[pl.ds(off, n * tb), :].astype(BF16)
            z = lax.dot_general(qb, kb, _DN["nt"], preferred_element_type=F32) * scale
            sp, ls, _ = _softplus_parts(z)
            if diagonal:
                sp = jnp.where(col < row, sp, 0.0)
            parts = [None] * n
            for s in reversed(range(n)):
                xs = sp[:, s * tb:(s + 1) * tb]
                parts[s] = _dot01_raw(xs, u_gt, _DN["nn"], False) + run
                run = run + jnp.sum(xs, axis=1, keepdims=True)
            between = parts[0] if n == 1 else jnp.concatenate(parts, axis=1)
            w = jnp.exp(ls - between)
            if diagonal:
                w = jnp.where(col < row, w, 0.0)
            acc = acc + lax.dot_general(w.astype(BF16), vb, _DN["nn"], preferred_element_type=F32)
            return acc, run

        carry = step(i, 1, (jnp.zeros((tb, HEAD_DIM), F32), jnp.zeros((tb, 1), F32)), True)
        left = i
        for n in SB_WIDE:
            carry = lax.fori_loop(0, lax.div(left, n), lambda t, c, n=n, left=left: step(left - n * (t + 1), n, c, False),
                                  carry)
            left = lax.rem(left, n)
        acc, run = carry
        o_ref[...] = acc
        tot_ref[...] = jnp.broadcast_to(run, (tb, HEAD_DIM))

    blk = pl.BlockSpec((tb, HEAD_DIM), lambda h, i: (i, h))
    full = pl.BlockSpec((T, HEAD_DIM), lambda h, i: (0, h))
    (o, tot), gathered = carried_call(
        body, [GatherJob(s) for s in shards], (H, T // tb), name="sb_attn_fwd",
        out_shape=[jax.ShapeDtypeStruct((T, HD), F32)] * 2, in_specs=[blk, full, full], out_specs=[blk, blk],
        scratch_shapes=[], operands=(q, k, v))
    return o, tot, tuple(gathered)


def _sb_bwd_call(q, k, v, tot, do):
    T, HD = q.shape
    H = HD // HEAD_DIM
    tb = _pick(T, (SB_BLOCK, 128))
    scale = HEAD_DIM ** -0.5

    def body(q_ref, k_ref, v_ref, tot_ref, do_ref, dq_ref, dk_ref, dv_ref):
        i = pl.program_id(1)

        @pl.when(i == 0)
        def _():
            dk_ref[...] = jnp.zeros(dk_ref.shape, F32)
            dv_ref[...] = jnp.zeros(dv_ref.shape, F32)

        qb = q_ref[...].astype(BF16)
        dob = do_ref[...].astype(BF16)
        tot = tot_ref[:, 0:1]
        u_le = _tri(tb, "le")
        u_lt = _tri(tb, "lt")
        row = lax.broadcasted_iota(jnp.int32, (tb, tb), 0)
        col = lax.broadcasted_iota(jnp.int32, (tb, tb), 1)

        def prefix_sums(x, u, start):
            out = []
            for s in range(x.shape[1] // tb):
                xs = x[:, s * tb:(s + 1) * tb]
                out.append(_dot01_raw(xs, u, _DN["nn"], False) + start)
                start = start + jnp.sum(xs, axis=1, keepdims=True)
            return (out[0] if len(out) == 1 else jnp.concatenate(out, axis=1)), start

        def step(j, n, carry, diagonal):
            dq, cs, cd = carry
            off = pl.multiple_of(j * tb, tb)
            kb = k_ref[pl.ds(off, n * tb), :].astype(BF16)
            vb = v_ref[pl.ds(off, n * tb), :].astype(BF16)
            z = lax.dot_general(qb, kb, _DN["nt"], preferred_element_type=F32) * scale
            sp, ls, sig = _softplus_parts(z)
            if diagonal:
                sp = jnp.where(col < row, sp, 0.0)
            prefix, cs = prefix_sums(sp, u_le, cs)
            w = jnp.exp(ls - (tot - prefix))
            if diagonal:
                w = jnp.where(col < row, w, 0.0)
            wb = w.astype(BF16)
            dv_ref[pl.ds(off, n * tb), :] += lax.dot_general(wb, dob, _DN["tn"], preferred_element_type=F32)
            dw = lax.dot_general(dob, vb, _DN["nt"], preferred_element_type=F32)
            dl = dw * w
            before, cd = prefix_sums(dl, u_lt, cd)
            dz = (dl * (1.0 - sig) - before * sig) * scale
            if diagonal:
                dz = jnp.where(col < row, dz, 0.0)
            dzb = dz.astype(BF16)
            dq = dq + lax.dot_general(dzb, kb, _DN["nn"], preferred_element_type=F32)
            dk_ref[pl.ds(off, n * tb), :] += lax.dot_general(dzb, qb, _DN["tn"], preferred_element_type=F32)
            return dq, cs, cd

        z1 = jnp.zeros((tb, 1), F32)
        carry, done = (jnp.zeros((tb, HEAD_DIM), F32), z1, z1), 0
        for n in SB_WIDE:
            trips = lax.div(i - done, n)
            carry = lax.fori_loop(0, trips, lambda t, c, n=n, done=done: step(done + n * t, n, c, False), carry)
            done = done + trips * n
        dq, _, _ = step(i, 1, carry, True)
        dq_ref[...] = dq

    blk = pl.BlockSpec((tb, HEAD_DIM), lambda h, i: (i, h))
    full = pl.BlockSpec((T, HEAD_DIM), lambda h, i: (0, h))
    return pl.pallas_call(
        body, name="sb_attn_bwd", grid=(H, T // tb), out_shape=[jax.ShapeDtypeStruct((T, HD), F32)] * 3,
        in_specs=[blk, full, full, blk, blk], out_specs=[blk, full, full],
        compiler_params=_params(dimension_semantics=("arbitrary", "arbitrary")),
    )(q, k, v, tot, do)


@jax.custom_vjp
def sb_attn(q, k, v, shards):
    o, _, gathered = _sb_fwd_call(q, k, v, shards)
    return o, gathered


def _sb_attn_fwd(q, k, v, shards):
    o, tot, gathered = _sb_fwd_call(q, k, v, shards)
    return (o, gathered), (q, k, v, tot, shards)


def _sb_attn_bwd(res, cts):
    q, k, v, tot, shards = res
    return (*_sb_bwd_call(q, k, v, tot, cts[0]), tuple(jnp.zeros_like(s) for s in shards))


sb_attn.defvjp(_sb_attn_fwd, _sb_attn_bwd)


def _dsw_tile(q, kp, kc, vp, vc, n):
    blk = DSW_BLOCK
    scale = HEAD_DIM ** -0.5
    qi = lax.broadcasted_iota(jnp.int32, (blk, blk), 0)
    kj = lax.broadcasted_iota(jnp.int32, (blk, blk), 1)
    neg = -1e30
    s_p = jnp.where((kj >= qi) & (n > 0), bdot(q, kp, "nt") * scale, neg)
    s_c = jnp.where(kj <= qi, bdot(q, kc, "nt") * scale, neg)
    m = jnp.maximum(jnp.max(s_p, axis=-1, keepdims=True), jnp.max(s_c, axis=-1, keepdims=True))
    p_p, p_c = jnp.exp(s_p - m), jnp.exp(s_c - m)
    den = jnp.sum(p_p, axis=-1, keepdims=True) + jnp.sum(p_c, axis=-1, keepdims=True)
    o = bdot(p_p / den, vp, "nn") + bdot(p_c / den, vc, "nn")
    lse = m + jnp.log(den)
    return o, jnp.broadcast_to(lse, (blk, HEAD_DIM))


def _dsw_specs(nsub):
    cur = pl.BlockSpec((None, DSW_BLOCK, HEAD_DIM), lambda r, h, n: (r, n, h))
    prev = pl.BlockSpec((None, DSW_BLOCK, HEAD_DIM), lambda r, h, n: (r, jnp.maximum(n - 1, 0), h))
    whole = pl.BlockSpec((None, nsub, HEAD_DIM), lambda r, h, n: (r, 0, h))
    return cur, prev, whole


def _dsw_fwd_call(q, k, v):
    d, nsub, HD = q.shape
    cur, prev, _ = _dsw_specs(nsub)

    def body(q_ref, kp_ref, kc_ref, vp_ref, vc_ref, o_ref, l_ref):
        n = pl.program_id(2)
        o, l = _dsw_tile(q_ref[...], kp_ref[...], kc_ref[...], vp_ref[...], vc_ref[...], n)
        o_ref[...] = o
        l_ref[...] = l

    return pl.pallas_call(
        body, name=f"dsw_attn_fwd_d{d}", grid=(d, HD // HEAD_DIM, nsub // DSW_BLOCK),
        out_shape=[jax.ShapeDtypeStruct(q.shape, F32)] * 2,
        in_specs=[cur, prev, cur, prev, cur], out_specs=[cur, cur],
        compiler_params=_params(dimension_semantics=("arbitrary",) * 3),
    )(q, k, k, v, v)


def _dsw_bwd_call(q, k, v, do, dl):
    d, nsub, HD = q.shape
    cur, prev, whole = _dsw_specs(nsub)
    blk = DSW_BLOCK

    def body(q_ref, kp_ref, kc_ref, vp_ref, vc_ref, do_ref, dl_ref, dq_ref, dk_ref, dv_ref):
        n = pl.program_id(2)

        @pl.when(n == 0)
        def _():
            dk_ref[...] = jnp.zeros(dk_ref.shape, F32)
            dv_ref[...] = jnp.zeros(dv_ref.shape, F32)

        _, vjp = jax.vjp(lambda a, b, c, e, f: _dsw_tile(a, b, c, e, f, n),
                         q_ref[...], kp_ref[...], kc_ref[...], vp_ref[...], vc_ref[...])
        dq, dkp, dkc, dvp, dvc = vjp((do_ref[...], dl_ref[...]))
        dq_ref[...] = dq
        c0 = pl.multiple_of(n * blk, blk)
        p0 = pl.multiple_of(jnp.maximum(n - 1, 0) * blk, blk)
        dk_ref[pl.ds(c0, blk), :] += dkc
        dv_ref[pl.ds(c0, blk), :] += dvc
        dk_ref[pl.ds(p0, blk), :] += dkp
        dv_ref[pl.ds(p0, blk), :] += dvp

    return pl.pallas_call(
        body, name=f"dsw_attn_bwd_d{d}", grid=(d, HD // HEAD_DIM, nsub // blk),
        out_shape=[jax.ShapeDtypeStruct(q.shape, F32)] * 3,
        in_specs=[cur, prev, cur, prev, cur, cur, cur], out_specs=[cur, whole, whole],
        compiler_params=_params(dimension_semantics=("arbitrary",) * 3),
    )(q, k, k, v, v, do, dl)


@jax.custom_vjp
def dsw_attn(q, k, v):
    return tuple(_dsw_fwd_call(q, k, v))


def _dsw_attn_fwd(q, k, v):
    return tuple(_dsw_fwd_call(q, k, v)), (q, k, v)


def _dsw_attn_bwd(res, cts):
    return tuple(_dsw_bwd_call(*res, *cts))


dsw_attn.defvjp(_dsw_attn_fwd, _dsw_attn_bwd)


_LOG2_CHUNK = GDN_CHUNK.bit_length() - 1
_LOG2_HEAD_DIM = HEAD_DIM.bit_length() - 1


def _unit_lower_inverse(a):
    n = a.shape[0]
    r = lax.broadcasted_iota(jnp.int32, (n, n), 0)
    c = lax.broadcasted_iota(jnp.int32, (n, n), 1)
    x = -a
    t = jnp.where(r == c, 1.0, 0.0) + x
    for _ in range(_LOG2_CHUNK - 1):
        x = _mm3(x, x)
        t = t + _mm3(t, x)
    return t


@jax.custom_vjp
def unit_lower_inverse(a):
    return _unit_lower_inverse(a)


def _unit_lower_inverse_fwd(a):
    t = _unit_lower_inverse(a)
    return t, t


def _unit_lower_inverse_bwd(t, ct):
    return (-_mm3(_mm3(t.T, ct), t.T),)


unit_lower_inverse.defvjp(_unit_lower_inverse_fwd, _unit_lower_inverse_bwd)


def _gdn_step(state, q, k, v, gb, bb):
    C, NH = GDN_CHUNK, GDN_HEADS_PER_STEP
    R = NH * C
    r = lax.broadcasted_iota(jnp.int32, (R, R), 0)
    c = lax.broadcasted_iota(jnp.int32, (R, R), 1)
    same = lax.shift_right_logical(r, _LOG2_CHUNK) == lax.shift_right_logical(c, _LOG2_CHUNK)
    causal, strict = same & (r >= c), same & (r > c)
    gc = dot01(gb, jnp.where(causal, 1.0, 0.0).astype(BF16), True)
    g_sq = jnp.concatenate([gc] * (R // HEAD_DIM), axis=1)
    g_row = dot01(jnp.where(r == c, g_sq, 0.0), jnp.ones((R, R), BF16), True)
    decay = jnp.where(causal, jnp.exp(jnp.where(causal, g_sq - g_row, 0.0)), 0.0)
    kb, vb = k * bb, v * bb
    a_mat = jnp.where(strict, bdot(kb, k, "nt") * decay, 0.0)
    t_mat = unit_lower_inverse(a_mat)
    uw = bdot(t_mat, jnp.concatenate([vb, kb * jnp.exp(gc)], axis=1), "nn")
    u, w = uw[:, :HEAD_DIM], uw[:, HEAD_DIM:]
    hr = lax.shift_right_logical(lax.broadcasted_iota(jnp.int32, (R, NH * HEAD_DIM), 0), _LOG2_CHUNK)
    hc = lax.shift_right_logical(lax.broadcasted_iota(jnp.int32, (R, NH * HEAD_DIM), 1), _LOG2_HEAD_DIM)

    def widen(m):
        return jnp.where(hr == hc, jnp.concatenate([m] * NH, axis=1), 0.0)

    v_new = u - bdot(widen(w), state, "nn")
    attn = bdot(q, k, "nt") * decay
    out = bdot(widen(q * jnp.exp(gc)), state, "nn") + bdot(attn, v_new, "nn")
    last = [gc[h * C + C - 1:h * C + C, :] for h in range(NH)]
    g_last_rows = jnp.concatenate([jnp.broadcast_to(l, (C, HEAD_DIM)) for l in last], axis=0)
    g_last_state = jnp.concatenate([jnp.broadcast_to(l, (HEAD_DIM, HEAD_DIM)) for l in last], axis=0)
    k_dec = k * jnp.exp(g_last_rows - gc)
    new_state = state * jnp.exp(g_last_state) + bdot(widen(k_dec), v_new, "tn")
    return new_state, out


def _gdn_stack(ref):
    return jnp.concatenate([ref[:, h * HEAD_DIM:(h + 1) * HEAD_DIM] for h in range(GDN_HEADS_PER_STEP)], axis=0)


def _gdn_unstack(ref, val):
    for h in range(GDN_HEADS_PER_STEP):
        ref[:, h * HEAD_DIM:(h + 1) * HEAD_DIM] = val[h * GDN_CHUNK:(h + 1) * GDN_CHUNK]


def _gdn_fwd_call(q, k, v, gb, bb, shards=()):
    T, HD = v.shape
    H = HD // HEAD_DIM
    N = T // GDN_CHUNK
    hb = GDN_HEADS_PER_STEP
    W = hb * HEAD_DIM

    def body(q_ref, k_ref, v_ref, g_ref, b_ref, o_ref, s_ref, state):
        n = pl.program_id(1)

        @pl.when(n == 0)
        def _():
            state[...] = jnp.zeros(state.shape, F32)

        s_in = state[...]
        s_ref[...] = s_in
        ns, o = _gdn_step(s_in, *[_gdn_stack(ref) for ref in (q_ref, k_ref, v_ref, g_ref, b_ref)])
        state[...] = ns
        _gdn_unstack(o_ref, o)

    blk = pl.BlockSpec((GDN_CHUNK, W), lambda h, n: (n, h))
    sblk = pl.BlockSpec((None, W, HEAD_DIM), lambda h, n: (n, h, 0))
    (o, states), gathered = carried_call(
        body, [GatherJob(s) for s in shards], (H // hb, N), name="gdn_chunk_fwd",
        out_shape=[jax.ShapeDtypeStruct((T, HD), F32), jax.ShapeDtypeStruct((N, H * HEAD_DIM, HEAD_DIM), F32)],
        in_specs=[blk] * 5, out_specs=[blk, sblk], scratch_shapes=[pltpu.VMEM((W, HEAD_DIM), F32)],
        operands=(q, k, v, gb, bb))
    return o, states, tuple(gathered)


def _gdn_bwd_call(q, k, v, gb, bb, states, do):
    T, HD = v.shape
    H = HD // HEAD_DIM
    N = T // GDN_CHUNK
    hb = GDN_HEADS_PER_STEP
    W = hb * HEAD_DIM

    def body(q_ref, k_ref, v_ref, g_ref, b_ref, s_ref, do_ref, dq_ref, dk_ref, dv_ref, dg_ref, db_ref, dstate):
        n = pl.program_id(1)

        @pl.when(n == 0)
        def _():
            dstate[...] = jnp.zeros(dstate.shape, F32)

        _, vjp = jax.vjp(_gdn_step, s_ref[...], *[_gdn_stack(ref) for ref in (q_ref, k_ref, v_ref, g_ref, b_ref)])
        ds, *grads = vjp((dstate[...], _gdn_stack(do_ref)))
        dstate[...] = ds
        for ref, g in zip((dq_ref, dk_ref, dv_ref, dg_ref, db_ref), grads):
            _gdn_unstack(ref, g)

    blk = pl.BlockSpec((GDN_CHUNK, W), lambda h, n: (N - 1 - n, h))
    sblk = pl.BlockSpec((None, W, HEAD_DIM), lambda h, n: (N - 1 - n, h, 0))
    return pl.pallas_call(
        body, name="gdn_chunk_bwd", grid=(H // hb, N), out_shape=[jax.ShapeDtypeStruct((T, HD), F32)] * 5,
        in_specs=[blk] * 5 + [sblk, blk], out_specs=[blk] * 5,
        scratch_shapes=[pltpu.VMEM((W, HEAD_DIM), F32)],
        compiler_params=_params(dimension_semantics=("arbitrary", "arbitrary")),
    )(q, k, v, gb, bb, states, do)


@jax.custom_vjp
def gdn_core(q, k, v, gb, bb, shards):
    o, _, gathered = _gdn_fwd_call(q, k, v, gb, bb, shards)
    return o, gathered


def _gdn_core_fwd(q, k, v, gb, bb, shards):
    o, states, gathered = _gdn_fwd_call(q, k, v, gb, bb, shards)
    return (o, gathered), (q, k, v, gb, bb, states, shards)


def _gdn_core_bwd(res, cts):
    *core, shards = res
    return (*_gdn_bwd_call(*core, cts[0]), tuple(jnp.zeros_like(s) for s in shards))


gdn_core.defvjp(_gdn_core_fwd, _gdn_core_bwd)


def _mixer_sb(h, w, shards=()):
    qkv = lin_t(h, w["sb_in_t"])
    q, k, v = rowop(_fn_sb_pre, "sb_pre", [qkv], [], [w["sb_q_norm"], w["sb_k_norm"]], [D_MODEL] * 3)
    o, gathered = sb_attn(q, k, v, tuple(shards))
    return lin_n(o, w["sb_out"]), gathered


def _mixer_gdn(h, w, shards=()):
    wt = w["gdn_in_t"]
    nqkv = 2 * GDN_KEY_DIM + GDN_VAL_DIM
    qkv = lin_t(h, wt[:nqkv])
    z = lin_t(h, wt[nqkv:nqkv + GDN_VAL_DIM])
    w_ba = jnp.pad(wt[nqkv + GDN_VAL_DIM:], ((0, LANES - 2 * GDN_V_HEADS), (0, 0)))
    ba = lin_t(h, w_ba)
    qkv = dwconv(qkv, w["gdn_conv_w"])
    q, k, v, gb, bb = rowop(_fn_gdn_pre, "gdn_pre", [qkv, ba], [], [w["gdn_a_log"], w["gdn_dt_bias"]],
                            [GDN_VAL_DIM] * 5)
    o, gathered = gdn_core(q, k, v, gb, bb, tuple(shards))
    (y,) = rowop(_fn_gdn_post, "gdn_post", [o, z], [], [w["gdn_o_norm"]], [GDN_VAL_DIM])
    return lin_n(y, w["gdn_out"]), gathered


def _to_strided(x, cols, d):
    T = x.shape[0]
    return x[:, cols].reshape(T // d, d, -1).transpose(1, 0, 2)


def _from_strided(x):
    d, n, c = x.shape
    return x.transpose(1, 0, 2).reshape(d * n, c)


def _mixer_dsw(h, cosm, sinm, w):
    qkv = lin_t(h, w["dsw_in_t"])
    nhd = DSW_HEADS * HEAD_DIM
    q, k, v = rowop(_fn_dsw_pre, "dsw_pre", [qkv], [cosm, sinm], [w["dsw_q_norm"], w["dsw_k_norm"]], [nhd] * 3)
    outs, lses = [], []
    for gi, (_, d) in enumerate(DSW_GROUPS):
        cols = slice(gi * DSW_HG * HEAD_DIM, (gi + 1) * DSW_HG * HEAD_DIM)
        o_g, l_g = dsw_attn(_to_strided(q, cols, d), _to_strided(k, cols, d), _to_strided(v, cols, d))
        outs.append(_from_strided(o_g))
        lses.append(_from_strided(l_g))
    (o,) = rowop(_fn_dsw_combine, "dsw_combine", [jnp.concatenate(outs, axis=1), jnp.concatenate(lses, axis=1)],
                 [], [], [nhd])
    return lin_n(o, w["dsw_out"])


def _mixer_lru(h, w):
    wt = w["lru_in_t"]
    gate = lin_t(h, wt[:LRU_WIDTH])
    xr = dwconv(lin_t(h, wt[LRU_WIDTH:]), w["lru_conv_w"])
    a, u = rowop(_fn_lru_gates, "lru_gates", [xr], [],
                 [w["lru_conv_b"], w["lru_w_a"], w["lru_b_a"], w["lru_w_x"], w["lru_b_x"], w["lru_lambda"]],
                 [LRU_WIDTH] * 2)
    hs = lru_scan(a, u)
    (y,) = rowop(_fn_lru_out, "lru_out", [hs, gate], [], [], [LRU_WIDTH])
    return lin_n(y, w["lru_out"])


def mixer_segment(i, x, f_prev, cosm, sinm, w, shards):
    if i == 0:
        (h,) = rowop(_fn_norm, "norm", [x], [], [w["mix_norm"]], [D_MODEL])
    else:
        x, h = rowop(_fn_add_norm, "add_norm", [x, f_prev], [], [w["mix_norm"]], [D_MODEL] * 2)
    kind, gathered = i % 4, ()
    if kind == 0:
        y, gathered = _mixer_sb(h, w, shards)
    elif kind == 1:
        y, gathered = _mixer_gdn(h, w, shards)
    elif kind == 2:
        y = _mixer_dsw(h, cosm, sinm, w)
    else:
        y = _mixer_lru(h, w)
    x, h = rowop(_fn_add_norm, "add_norm", [x, y], [], [w["ffn_norm"]], [D_MODEL] * 2)
    return (x, h), gathered


def swiglu_act(gu):
    return rowop(_fn_swiglu, "swiglu", [gu], [], [], [FFN_HIDDEN])[0]


ANY = pl.BlockSpec(memory_space=pl.ANY)


SLAB_BYTES = 4 * 1024 * 1024


def _col_tile(rows, C, itemsize):
    return _tile(C, max(LANES, SLAB_BYTES // (rows * itemsize)))


class GatherJob:
    def __init__(self, shard):
        self.operand = shard
        r, C = shard.shape
        self.out_shape = jax.ShapeDtypeStruct((N_DEV, r, C), shard.dtype)
        self.scratch = [pltpu.SemaphoreType.DMA((7,)), pltpu.SemaphoreType.DMA((7,)), pltpu.SemaphoreType.DMA]

    def _parts(self, x_ref, out_ref, send_sems, recv_sems, local_sem):
        x, y, c = lax.axis_index("x"), lax.axis_index("y"), lax.axis_index("c")
        me, sibling = (x, y, c), (x, y, 1 - c)
        chips = [(1 - x, y), (x, 1 - y), (1 - x, 1 - y)]

        def slot(px, py, pc):
            return out_ref.at[4 * px + 2 * py + pc]

        def copy(k, block, to, src=None):
            return pltpu.make_async_remote_copy(
                src_ref=slot(*block) if src is None else src, dst_ref=slot(*block),
                send_sem=send_sems.at[k], recv_sem=recv_sems.at[k], device_id=to, device_id_type=MESH)

        def mine():
            return pltpu.make_async_copy(x_ref, slot(*me), local_sem)

        def first():
            return [copy(0, me, sibling, src=x_ref)] + [copy(1 + j, me, (*chip, c), src=x_ref)
                                                        for j, chip in enumerate(chips)]

        def passed():
            return [copy(4 + j, (*chip, c), sibling) for j, chip in enumerate(chips)]

        def landed():
            return [copy(1 + j, (*chip, c), me) for j, chip in enumerate(chips)]

        def from_sibling():
            return [copy(0, sibling, me)] + [copy(4 + j, (*chip, 1 - c), me) for j, chip in enumerate(chips)]

        return mine, first, passed, landed, from_sibling

    def start(self, x_ref, out_ref, *sems):
        mine, first, _, _, _ = self._parts(x_ref, out_ref, *sems)
        mine().start()
        for cp in first():
            cp.start()

    def finish(self, x_ref, out_ref, *sems):
        mine, first, passed, landed, from_sibling = self._parts(x_ref, out_ref, *sems)
        onward = passed()
        for arrived, cp in zip(landed(), onward):
            arrived.wait_recv()
            cp.start()
        for cp in from_sibling():
            cp.wait_recv()
        for cp in first() + onward:
            cp.wait_send()
        mine().wait()


class PairJob:
    def __init__(self, g):
        self.operand = g
        _, _, r, C = g.shape
        self.out_shape = jax.ShapeDtypeStruct((N_CHIPS, r, C), g.dtype)
        self.scratch = [pltpu.SemaphoreType.DMA((N_CHIPS,)), pltpu.SemaphoreType.DMA((N_CHIPS,))]

    def _copies(self, g_ref, out_ref, send_sems, recv_sems):
        x, y, c = lax.axis_index("x"), lax.axis_index("y"), lax.axis_index("c")
        return [pltpu.make_async_remote_copy(
            src_ref=g_ref.at[q, 1 - c], dst_ref=out_ref.at[q], send_sem=send_sems.at[q], recv_sem=recv_sems.at[q],
            device_id=(x, y, 1 - c), device_id_type=MESH) for q in range(N_CHIPS)]

    def start(self, *refs):
        for cp in self._copies(*refs):
            cp.start()

    def finish(self, *refs):
        for cp in self._copies(*refs):
            cp.wait()


class ChipsJob:
    def __init__(self, p):
        self.operand = p
        self.out_shape = jax.ShapeDtypeStruct(p.shape, p.dtype)
        self.scratch = [pltpu.SemaphoreType.DMA((3,)), pltpu.SemaphoreType.DMA((3,)), pltpu.SemaphoreType.DMA]

    def _parts(self, p_ref, out_ref, send_sems, recv_sems, local_sem):
        x, y, c = lax.axis_index("x"), lax.axis_index("y"), lax.axis_index("c")
        mychip = 2 * x + y
        chips = [(1 - x, y), (x, 1 - y), (1 - x, 1 - y)]
        def mine():
            return pltpu.make_async_copy(p_ref.at[mychip], out_ref.at[mychip], local_sem)

        def sends():
            return [pltpu.make_async_remote_copy(
                src_ref=p_ref.at[2 * cx + cy], dst_ref=out_ref.at[mychip], send_sem=send_sems.at[j],
                recv_sem=recv_sems.at[j], device_id=(cx, cy, c), device_id_type=MESH)
                for j, (cx, cy) in enumerate(chips)]

        def arrivals():
            return [pltpu.make_async_remote_copy(
                src_ref=p_ref.at[mychip], dst_ref=out_ref.at[2 * cx + cy], send_sem=send_sems.at[j],
                recv_sem=recv_sems.at[j], device_id=(cx, cy, c), device_id_type=MESH)
                for j, (cx, cy) in enumerate(chips)]

        return mine, sends, arrivals

    def start(self, *refs):
        mine, sends, _ = self._parts(*refs)
        mine().start()
        for cp in sends():
            cp.start()

    def finish(self, *refs):
        mine, sends, arrivals = self._parts(*refs)
        for cp in arrivals():
            cp.wait_recv()
        for cp in sends():
            cp.wait_send()
        mine().wait()


def carried_call(body, jobs, grid, *, name, out_shape, in_specs, out_specs, scratch_shapes, operands):
    jobs = list(jobs)
    n_in, n_out, n_scr, nj = len(in_specs), len(out_specs), len(scratch_shapes), len(jobs)
    n_sem = [len(j.scratch) for j in jobs]

    def full_body(*refs):
        core_in = refs[:n_in]
        job_in = refs[n_in:n_in + nj]
        core_out = refs[n_in + nj:n_in + nj + n_out]
        job_out = refs[n_in + nj + n_out:n_in + 2 * nj + n_out]
        rest = refs[n_in + 2 * nj + n_out:]
        core_scr, sems, pos = rest[:n_scr], [], n_scr
        for n in n_sem:
            sems.append(rest[pos:pos + n])
            pos += n
        ids = [pl.program_id(a) for a in range(len(grid))]
        first = functools.reduce(jnp.logical_and, [i == 0 for i in ids])
        last = functools.reduce(jnp.logical_and, [i == g - 1 for i, g in zip(ids, grid)])
        if jobs:
            @pl.when(first)
            def _():
                for j, job in enumerate(jobs):
                    job.start(job_in[j], job_out[j], *sems[j])

        body(*core_in, *core_out, *core_scr)
        if jobs:
            @pl.when(last)
            def _():
                for j, job in enumerate(jobs):
                    job.finish(job_in[j], job_out[j], *sems[j])

    res = pl.pallas_call(
        full_body, name=name, grid=grid,
        out_shape=list(out_shape) + [j.out_shape for j in jobs],
        in_specs=list(in_specs) + [ANY] * nj, out_specs=list(out_specs) + [ANY] * nj,
        scratch_shapes=list(scratch_shapes) + [s for j in jobs for s in j.scratch],
        compiler_params=_params(dimension_semantics=("arbitrary",) * len(grid)),
    )(*operands, *[j.operand for j in jobs])
    return res[:n_out], res[n_out:]


def _exchange(job, name):
    def body(*refs):
        job.start(*refs)
        job.finish(*refs)

    return pl.pallas_call(
        body, name=name, out_shape=job.out_shape, in_specs=[ANY], out_specs=ANY, scratch_shapes=job.scratch,
        compiler_params=pltpu.CompilerParams(has_side_effects=True),
    )(job.operand)


def all_gather(shard, name):
    return _exchange(GatherJob(shard), name)


def exchange_pair(g, name):
    return _exchange(PairJob(g), name)


def exchange_chips(p, name):
    return _exchange(ChipsJob(p), name)


def pair_add(g, got, name):
    _, _, r, C = g.shape
    cb = _col_tile(r, C, g.dtype.itemsize)
    c = lax.axis_index("c")

    def body(c_ref, a_ref, b_ref, o_ref):
        o_ref[...] = (a_ref[...].astype(F32) + b_ref[...].astype(F32)).astype(o_ref.dtype)

    return pl.pallas_call(
        body, name=name, out_shape=jax.ShapeDtypeStruct((N_CHIPS, r, C), g.dtype),
        grid_spec=pltpu.PrefetchScalarGridSpec(
            num_scalar_prefetch=1, grid=(N_CHIPS, C // cb),
            in_specs=[pl.BlockSpec((None, None, r, cb), lambda q, j, cr: (q, cr[0], 0, j)),
                      pl.BlockSpec((None, r, cb), lambda q, j, cr: (q, 0, j))],
            out_specs=pl.BlockSpec((None, r, cb), lambda q, j, cr: (q, 0, j))),
        compiler_params=_params(dimension_semantics=("arbitrary", "arbitrary")),
    )(jnp.reshape(c, (1,)).astype(jnp.int32), g, got)


def sum_slots(parts, name):
    n, r, C = parts.shape
    cb = _col_tile(n * r, C, parts.dtype.itemsize)

    def body(p_ref, o_ref):
        acc = p_ref[0].astype(F32)
        for q in range(1, n):
            acc = acc + p_ref[q].astype(F32)
        o_ref[...] = acc

    return pl.pallas_call(
        body, name=name, out_shape=jax.ShapeDtypeStruct((r, C), F32), grid=(C // cb,),
        in_specs=[pl.BlockSpec((n, r, cb), lambda j: (0, 0, j))], out_specs=pl.BlockSpec((r, cb), lambda j: (0, j)),
        compiler_params=_params(dimension_semantics=("arbitrary",)),
    )(parts)


def reduce_scatter(g, tag):
    _, r, C = g.shape
    g4 = g.reshape(N_CHIPS, 2, r, C)
    got = exchange_pair(g4, f"rs_pair_{tag}")
    pairs = pair_add(g4, got, f"rs_pair_add_{tag}")
    parts = exchange_chips(pairs, f"rs_chips_{tag}")
    return sum_slots(parts, f"rs_sum_{tag}")


def all_reduce(v, tag):
    return sum_slots(all_gather(v, f"ar_gather_{tag}"), f"ar_sum_{tag}")


def adamw(w, g, m, v, name):
    R, C = w.shape
    tr = R
    for cand in (512, 256, 128, 64, 32, 16, 8):
        if R % cand == 0 and cand * C * 4 * 7 * 2 <= 40 * 1024 * 1024:
            tr = cand
            break
    c1 = 1.0 - ADAM_B1 ** ADAM_STEP
    c2 = 1.0 - ADAM_B2 ** ADAM_STEP

    def body(w_ref, g_ref, m_ref, v_ref, d_ref, nm_ref, nv_ref):
        gv = g_ref[...]
        nm = ADAM_B1 * m_ref[...] + (1.0 - ADAM_B1) * gv
        nv = ADAM_B2 * v_ref[...] + (1.0 - ADAM_B2) * (gv * gv)
        d_ref[...] = -ADAM_LR * ((nm / c1) / (jnp.sqrt(nv / c2) + ADAM_EPS) + ADAM_WD * w_ref[...])
        nm_ref[...] = nm
        nv_ref[...] = nv

    spec = pl.BlockSpec((tr, C), lambda i: (i, 0))
    return pl.pallas_call(
        body, name=name, out_shape=[jax.ShapeDtypeStruct((R, C), F32)] * 3, grid=(R // tr,),
        in_specs=[spec] * 4, out_specs=[spec] * 3,
        compiler_params=_params(dimension_semantics=("arbitrary",)),
    )(w, g, m, v)


NAMES = ['mix_norm', 'ffn_norm', 'ffn_w_gu', 'ffn_w_down', 'sb_w_in', 'sb_q_norm', 'sb_k_norm', 'sb_w_out',
         'gdn_w_in', 'gdn_conv_w', 'gdn_a_log', 'gdn_dt_bias', 'gdn_o_norm', 'gdn_w_out', 'dsw_w_in', 'dsw_q_norm',
         'dsw_k_norm', 'dsw_w_out', 'lru_w_in', 'lru_conv_w', 'lru_conv_b', 'lru_w_a', 'lru_b_a', 'lru_w_x',
         'lru_b_x', 'lru_lambda', 'lru_w_out']
REPLICATED = ['mix_norm', 'ffn_norm', 'sb_q_norm', 'sb_k_norm', 'gdn_a_log', 'gdn_dt_bias', 'gdn_o_norm',
              'dsw_q_norm', 'dsw_k_norm']
SMALL_SHARDED = ['gdn_conv_w', 'lru_conv_w', 'lru_conv_b', 'lru_b_a', 'lru_b_x', 'lru_lambda']
IN_T = ['sb_w_in', 'gdn_w_in', 'dsw_w_in', 'lru_w_in']
OUT_N = ['sb_w_out', 'gdn_w_out', 'dsw_w_out', 'lru_w_out']


def _pack(arrs, pad_rows_to=8):
    flat = jnp.concatenate([a.reshape(-1) for a in arrs])
    n = flat.shape[0]
    rows = -(-n // LANES)
    rows = -(-rows // pad_rows_to) * pad_rows_to
    return jnp.pad(flat, (0, rows * LANES - n)).reshape(rows, LANES)


def _unpack(buf, shapes):
    flat = buf.reshape(-1)
    out, o = [], 0
    for s in shapes:
        n = math.prod(s)
        out.append(flat[o:o + n].reshape(s))
        o += n
    return out


def kernel(x, positions, mix_norm, ffn_norm, ffn_w_gu, ffn_w_down, sb_w_in, sb_q_norm, sb_k_norm, sb_w_out, gdn_w_in, gdn_conv_w, gdn_a_log, gdn_dt_bias, gdn_o_norm, gdn_w_out, dsw_w_in, dsw_q_norm, dsw_k_norm, dsw_w_out, lru_w_in, lru_conv_w, lru_conv_b, lru_w_a, lru_b_a, lru_w_x, lru_b_x, lru_lambda, lru_w_out, loss_target, m_mix_norm, m_ffn_norm, m_ffn_w_gu, m_ffn_w_down, m_sb_w_in, m_sb_q_norm, m_sb_k_norm, m_sb_w_out, m_gdn_w_in, m_gdn_conv_w, m_gdn_a_log, m_gdn_dt_bias, m_gdn_o_norm, m_gdn_w_out, m_dsw_w_in, m_dsw_q_norm, m_dsw_k_norm, m_dsw_w_out, m_lru_w_in, m_lru_conv_w, m_lru_conv_b, m_lru_w_a, m_lru_b_a, m_lru_w_x, m_lru_b_x, m_lru_lambda, m_lru_w_out, v_mix_norm, v_ffn_norm, v_ffn_w_gu, v_ffn_w_down, v_sb_w_in, v_sb_q_norm, v_sb_k_norm, v_sb_w_out, v_gdn_w_in, v_gdn_conv_w, v_gdn_a_log, v_gdn_dt_bias, v_gdn_o_norm, v_gdn_w_out, v_dsw_w_in, v_dsw_q_norm, v_dsw_k_norm, v_dsw_w_out, v_lru_w_in, v_lru_conv_w, v_lru_conv_b, v_lru_w_a, v_lru_b_a, v_lru_w_x, v_lru_b_x, v_lru_lambda, v_lru_w_out):
    args = locals()
    W = {n: args[n] for n in NAMES}
    M = {n: args["m_" + n] for n in NAMES}
    V = {n: args["v_" + n] for n in NAMES}
    T = x.shape[1]
    x2 = x[0]
    tgt = loss_target[0]

    S = {}
    for n in IN_T:
        S[n] = W[n][0].T.astype(BF16)
    for n in OUT_N:
        S[n] = W[n][0].astype(BF16)
    for i in range(DEPTH):
        S[f"ffn_gu{i}"] = ffn_w_gu[i].T.astype(BF16)
        S[f"ffn_down{i}"] = ffn_w_down[i].astype(BF16)
    S["lru_gates"] = jnp.concatenate(
        [lru_w_a[0].reshape(-1, LRU_BLOCK_DIM), lru_w_x[0].reshape(-1, LRU_BLOCK_DIM)], axis=0).astype(BF16)
    Gt = {}

    def flat(key):
        return Gt[key].reshape(-1, Gt[key].shape[-1])

    Gt["sb_w_in"] = all_gather(S["sb_w_in"], "ag_sb_w_in")
    Gt["sb_w_out"] = all_gather(S["sb_w_out"], "ag_sb_w_out")
    small_shapes = [W[n].shape for n in SMALL_SHARDED]
    sm = all_gather(_pack([W[n] for n in SMALL_SHARDED]), "ag_small")
    sm = [jnp.stack(parts) for parts in zip(*[_unpack(sm[p], small_shapes) for p in range(N_DEV)])]
    smd = dict(zip(SMALL_SHARDED, sm))

    def mixer_weights(i):
        kind = i % 4
        w = {"mix_norm": mix_norm[i:i + 1], "ffn_norm": ffn_norm[i:i + 1]}
        if kind == 0:
            w.update(sb_in_t=flat("sb_w_in"), sb_out=flat("sb_w_out"), sb_q_norm=sb_q_norm, sb_k_norm=sb_k_norm)
        elif kind == 1:
            w.update(gdn_in_t=flat("gdn_w_in"), gdn_out=flat("gdn_w_out"), gdn_a_log=gdn_a_log,
                     gdn_dt_bias=gdn_dt_bias, gdn_o_norm=gdn_o_norm,
                     gdn_conv_w=smd["gdn_conv_w"][:, 0].transpose(1, 0, 2).reshape(4, -1))
        elif kind == 2:
            w.update(dsw_in_t=flat("dsw_w_in"), dsw_out=flat("dsw_w_out"), dsw_q_norm=dsw_q_norm,
                     dsw_k_norm=dsw_k_norm)
        else:
            gg = Gt["lru_gates"].reshape(N_DEV, 2, LRU_BLOCKS, 32, LRU_BLOCK_DIM)
            gg = gg.transpose(1, 2, 0, 3, 4).reshape(2, LRU_BLOCKS, LRU_BLOCK_DIM, LRU_BLOCK_DIM).astype(F32)
            w.update(lru_in_t=flat("lru_w_in"), lru_out=flat("lru_w_out"), lru_w_a=gg[0], lru_w_x=gg[1],
                     lru_conv_w=smd["lru_conv_w"][:, 0].transpose(1, 0, 2).reshape(4, -1),
                     lru_conv_b=smd["lru_conv_b"][:, 0].reshape(1, -1),
                     lru_lambda=smd["lru_lambda"][:, 0].reshape(1, -1),
                     lru_b_a=smd["lru_b_a"][:, 0].transpose(1, 0, 2).reshape(1, -1),
                     lru_b_x=smd["lru_b_x"][:, 0].transpose(1, 0, 2).reshape(1, -1))
        return w

    carried_by_mixer = {0: ["ffn_gu0", "ffn_down0", "gdn_w_in", "gdn_w_out", "ffn_gu1", "ffn_down1"],
                        1: ["ffn_gu2", "ffn_down2", "lru_w_in", "lru_gates", "ffn_gu3", "ffn_down3"]}
    carried_by_ffn = {0: (["dsw_w_in"], ["dsw_w_out", "lru_w_out"])}

    half = ROPE_DIM // 2
    inv_freq = ROPE_THETA ** (-jnp.arange(half, dtype=F32) / half)
    ang = positions[0].astype(F32)[:, None] * inv_freq
    cs, sn = jnp.cos(ang), jnp.sin(ang)
    cosm = jnp.concatenate([cs, cs, jnp.ones((T, HEAD_DIM - ROPE_DIM), F32)], axis=1)
    sinm = jnp.concatenate([-sn, sn, jnp.zeros((T, HEAD_DIM - ROPE_DIM), F32)], axis=1)

    def gather_jobs(keys):
        return [GatherJob(S[k]) for k in keys]

    xs, f, seg_vjps, ffn_res = x2, None, [], []
    for i in range(DEPTH):
        keys = carried_by_mixer.get(i, [])
        shards = tuple(S[k] for k in keys)
        if i == 0:
            (xs, h), vjp_i, gathered = jax.vjp(
                lambda xx, ww: mixer_segment(0, xx, None, cosm, sinm, ww, shards), xs, mixer_weights(i), has_aux=True)
        else:
            (xs, h), vjp_i, gathered = jax.vjp(
                lambda xx, ff, ww, i=i, shards=shards: mixer_segment(i, xx, ff, cosm, sinm, ww, shards),
                xs, f, mixer_weights(i), has_aux=True)
        Gt.update(zip(keys, gathered))
        seg_vjps.append(vjp_i)
        keys_gu, keys_down = carried_by_ffn.get(i, ([], []))
        gu, got = mm_carry(h, flat(f"ffn_gu{i}"), "nt", F32, gather_jobs(keys_gu))
        Gt.update(zip(keys_gu, got))
        act, sw_vjp = jax.vjp(swiglu_act, gu)
        f, got = mm_carry(act, flat(f"ffn_down{i}"), "nn", F32, gather_jobs(keys_down))
        Gt.update(zip(keys_down, got))
        ffn_res.append((h, act, sw_vjp))
    loss_part, dy = loss_head(xs, f, tgt)
    loss = lax.psum(loss_part, ("x", "y", "c"))

    reduced, gw_small, gw_rep = {}, {}, {}

    def to_partials(i, dw, dwd, dwgu):
        out = {f"ffn_gu{i}": dwgu.reshape(N_DEV, -1, D_MODEL), f"ffn_down{i}": dwd.reshape(N_DEV, -1, D_MODEL)}
        for k, g in dw.items():
            if k.endswith("_in_t"):
                out[k.replace("_in_t", "_w_in")] = g.reshape(N_DEV, -1, D_MODEL)
            elif k.endswith("_out"):
                out[k.replace("_out", "_w_out")] = g.reshape(N_DEV, -1, D_MODEL)
            elif k in ("mix_norm", "ffn_norm"):
                gw_rep[(k, i)] = g
            elif k in REPLICATED:
                gw_rep[k] = g
            elif k not in ("lru_w_a", "lru_w_x"):
                gw_small[k] = g
        if "lru_w_a" in dw:
            gg = jnp.stack([dw["lru_w_a"], dw["lru_w_x"]]).reshape(2, LRU_BLOCKS, N_DEV, 32, LRU_BLOCK_DIM)
            out["lru_gates"] = gg.transpose(2, 0, 1, 3, 4).reshape(N_DEV, 2 * LRU_BLOCKS * 32, LRU_BLOCK_DIM)
        return out

    def spread(keys, sizes, capacities):
        room, bins = list(capacities), [[] for _ in capacities]
        for k in sorted(keys, key=lambda k: -sizes[k]):
            b = max(range(len(room)), key=lambda j: room[j])
            bins[b].append(k)
            room[b] -= sizes[k]
        return bins

    ready, dx, df = {}, dy, dy
    for i in reversed(range(DEPTH)):
        h, act, sw_vjp = ffn_res[i]
        keys = list(ready)
        g4 = {k: ready[k].reshape(N_CHIPS, 2, *ready[k].shape[1:]) for k in keys}
        dact, got = mm_carry(df, flat(f"ffn_down{i}"), "nt", F32, [PairJob(g4[k]) for k in keys])
        pairs = {k: pair_add(g4[k], g, f"rs_pair_add_{k}") for k, g in zip(keys, got)}
        sizes = {k: math.prod(pairs[k].shape[1:]) * pairs[k].dtype.itemsize for k in keys}
        bins = spread(keys, sizes, [c * D_MODEL * 2 for c in (800, 1300, 1550)])
        dwd, p0 = mm_carry(act, df, "tn", BF16, [ChipsJob(pairs[k]) for k in bins[0]])
        (dgu,) = sw_vjp(dact)
        dh, p1 = mm_carry(dgu, flat(f"ffn_gu{i}"), "nn", F32, [ChipsJob(pairs[k]) for k in bins[1]])
        dwgu, p2 = mm_carry(dgu, h, "tn", BF16, [ChipsJob(pairs[k]) for k in bins[2]])
        for k, parts in zip(bins[0] + bins[1] + bins[2], list(p0) + list(p1) + list(p2)):
            reduced[k] = sum_slots(parts, f"rs_sum_{k}")
        if i == 0:
            dx, dw = seg_vjps[i]((dx, dh))
        else:
            dx, df, dw = seg_vjps[i]((dx, dh))
        ready = to_partials(i, dw, dwd, dwgu)
    gx = dx
    for k, g in ready.items():
        reduced[k] = reduce_scatter(g, k)

    G = {}
    for n in IN_T:
        G[n] = reduced[n].T[None]
    for n in OUT_N:
        G[n] = reduced[n][None]
    G["ffn_w_gu"] = jnp.stack([reduced[f"ffn_gu{i}"].T for i in range(DEPTH)])
    G["ffn_w_down"] = jnp.stack([reduced[f"ffn_down{i}"] for i in range(DEPTH)])
    gg = reduced["lru_gates"].reshape(2, 1, LRU_BLOCKS, 32, LRU_BLOCK_DIM)
    G["lru_w_a"], G["lru_w_x"] = gg[0], gg[1]
    gw = dict(gw_small)
    for n in REPLICATED:
        gw[n] = (jnp.concatenate([gw_rep[(n, i)] for i in range(DEPTH)], axis=0) if n in ("mix_norm", "ffn_norm")
                 else gw_rep[n])
    gs = {
        "gdn_conv_w": gw["gdn_conv_w"].reshape(4, N_DEV, -1).transpose(1, 0, 2)[:, None],
        "lru_conv_w": gw["lru_conv_w"].reshape(4, N_DEV, -1).transpose(1, 0, 2)[:, None],
        "lru_conv_b": gw["lru_conv_b"].reshape(N_DEV, 1, -1),
        "lru_lambda": gw["lru_lambda"].reshape(N_DEV, 1, -1),
        "lru_b_a": gw["lru_b_a"].reshape(LRU_BLOCKS, N_DEV, 32).transpose(1, 0, 2)[:, None],
        "lru_b_x": gw["lru_b_x"].reshape(LRU_BLOCKS, N_DEV, 32).transpose(1, 0, 2)[:, None],
    }
    packed = jnp.stack([_pack([gs[n][p] for n in SMALL_SHARDED]) for p in range(N_DEV)])
    for n, g in zip(SMALL_SHARDED, _unpack(reduce_scatter(packed, "small"), small_shapes)):
        G[n] = g
    rep_shapes = [W[n].shape for n in REPLICATED]
    for n, g in zip(REPLICATED, _unpack(all_reduce(_pack([gw[n] for n in REPLICATED]), "rep"), rep_shapes)):
        G[n] = g

    D, NM, NV = {}, {}, {}
    big = [n for n in NAMES if n not in REPLICATED and n not in SMALL_SHARDED]
    for n in big:
        shp = W[n].shape
        two = (-1, shp[-1])
        d, nm, nv = adamw(W[n].reshape(two), G[n].reshape(two), M[n].reshape(two), V[n].reshape(two), f"adamw_{n}")
        D[n], NM[n], NV[n] = d.reshape(shp), nm.reshape(shp), nv.reshape(shp)
    for group, tag in ((SMALL_SHARDED, "small"), (REPLICATED, "rep")):
        shapes = [W[n].shape for n in group]
        res = adamw(_pack([W[n] for n in group]), _pack([G[n] for n in group]), _pack([M[n] for n in group]),
                    _pack([V[n] for n in group]), f"adamw_{tag}")
        for dst, buf in zip((D, NM, NV), res):
            for n, a in zip(group, _unpack(buf, shapes)):
                dst[n] = a

    return (loss, gx[None], *[G[n] for n in NAMES], *[D[n] for n in NAMES], *[NM[n] for n in NAMES],
            *[NV[n] for n in NAMES])
```

```python
import functools
import math

import jax
import jax.numpy as jnp
from jax import lax
from jax.experimental import pallas as pl
from jax.experimental.pallas import tpu as pltpu

F32 = jnp.float32
BF16 = jnp.bfloat16

D_MODEL = 2048
HEAD_DIM = 128
NORM_EPS = 1e-6
SB_HEADS = 16
SB_BLOCK = 256
SB_WIDE = (4, 2, 1)
GDN_K_HEADS = 16
GDN_V_HEADS = 32
GDN_KEY_DIM = 2048
GDN_VAL_DIM = 4096
GDN_CHUNK = 64
GDN_HEADS_PER_STEP = 4
GDN_GROUPS_PER_STEP = 1
DSW_GROUPS = ((128, 1), (512, 4), (2048, 16))
DSW_HG = 6
DSW_HEADS = 18
DSW_BLOCK = 128
ROPE_DIM = 32
ROPE_THETA = 500000.0
LRU_WIDTH = 2048
LRU_BLOCKS = 8
LRU_BLOCK_DIM = 256
LRU_C = 8.0
FFN_HIDDEN = 5632
DEPTH = 4
ADAM_LR, ADAM_B1, ADAM_B2, ADAM_EPS, ADAM_WD, ADAM_STEP = 0.001, 0.9, 0.999, 1e-08, 0.01, 10
N_DEV = 8
N_CHIPS = 4

V7X_VMEM_LIMIT = 56 * 1024 * 1024
LANES = 128
MESH = pl.DeviceIdType.MESH


def _params(**kw):
    return pltpu.CompilerParams(vmem_limit_bytes=V7X_VMEM_LIMIT, **kw)


def _pick(n, cands):
    for c in cands:
        if n % c == 0:
            return c
    return n


def _tile(n, cap, unit=LANES):
    best = None
    for t in range(unit, min(n, cap) + 1, unit):
        if n % t == 0:
            best = t
    return best or n


_DN = {"nn": (((1,), (0,)), ((), ())), "nt": (((1,), (1,)), ((), ())), "tn": (((0,), (0,)), ((), ()))}


def _raw_dot(a, b, mode):
    return lax.dot_general(a.astype(BF16), b.astype(BF16), _DN[mode], preferred_element_type=F32)


@functools.partial(jax.custom_vjp, nondiff_argnums=(2,))
def bdot(a, b, mode):
    return _raw_dot(a, b, mode)


def _bdot_fwd(a, b, mode):
    return _raw_dot(a, b, mode), (a, b)


def _bdot_bwd(mode, res, ct):
    a, b = res
    if mode == "nn":
        return bdot(ct, b, "nt"), bdot(a, ct, "tn")
    if mode == "nt":
        return bdot(ct, b, "nn"), bdot(ct, a, "tn")
    return bdot(b, ct, "nt"), bdot(a, ct, "nn")


bdot.defvjp(_bdot_fwd, _bdot_bwd)


def _split2(x):
    hi = x.astype(BF16)
    lo = (x - hi.astype(F32)).astype(BF16)
    return hi, lo


def _dot01_raw(x, m, dn, left):
    hi, lo = _split2(x)
    if left:
        return (lax.dot_general(m, hi, dn, preferred_element_type=F32)
                + lax.dot_general(m, lo, dn, preferred_element_type=F32))
    return (lax.dot_general(hi, m, dn, preferred_element_type=F32)
            + lax.dot_general(lo, m, dn, preferred_element_type=F32))


@functools.partial(jax.custom_vjp, nondiff_argnums=(2,))
def dot01(x, m, left):
    return _dot01_raw(x, m, _DN["nn"], left)


def _dot01_fwd(x, m, left):
    return _dot01_raw(x, m, _DN["nn"], left), m


def _dot01_bwd(left, m, ct):
    dx = _dot01_raw(ct, m, _DN["tn"] if left else _DN["nt"], left)
    return dx, jnp.zeros_like(m)


dot01.defvjp(_dot01_fwd, _dot01_bwd)


def _mm3(a, b):
    ah, al = _split2(a)
    bh, bl = _split2(b)
    dn = _DN["nn"]
    return (lax.dot_general(ah, bh, dn, preferred_element_type=F32)
            + lax.dot_general(ah, bl, dn, preferred_element_type=F32)
            + lax.dot_general(al, bh, dn, preferred_element_type=F32))


@jax.custom_vjp
def mm3(a, b):
    return _mm3(a, b)


def _mm3_fwd(a, b):
    return _mm3(a, b), (a, b)


def _mm3_bwd(res, ct):
    a, b = res
    return _mm3(ct, b.T), _mm3(a.T, ct)


mm3.defvjp(_mm3_fwd, _mm3_bwd)


def _softplus_parts(z):
    e = jnp.exp(-jnp.abs(z))
    l = jnp.log(1.0 + e)
    sp = jnp.maximum(z, 0.0) + l
    ls = jnp.minimum(z, 0.0) - l
    inv = 1.0 / (1.0 + e)
    sig = jnp.where(z >= 0.0, inv, e * inv)
    return sp, ls, sig


def _rms(x, g):
    return x * lax.rsqrt(jnp.mean(x * x, axis=-1, keepdims=True) + NORM_EPS) * g


def _swap16(x):
    lane = lax.broadcasted_iota(jnp.int32, x.shape, 1)
    return jnp.where(lane < 16, pltpu.roll(x, 112, axis=1), jnp.where(lane < 32, pltpu.roll(x, 16, axis=1), 0.0))


@jax.custom_vjp
def rope(x, cosm, sinm):
    return x * cosm + _swap16(x) * sinm


def _rope_fwd(x, cosm, sinm):
    return rope(x, cosm, sinm), (cosm, sinm)


def _rope_bwd(res, ct):
    cosm, sinm = res
    return ct * cosm + _swap16(ct * sinm), jnp.zeros_like(cosm), jnp.zeros_like(sinm)


rope.defvjp(_rope_fwd, _rope_bwd)


def mm(a, b, mode, out_dtype=F32):
    return mm_carry(a, b, mode, out_dtype, ())[0]


def mm_carry(a, b, mode, out_dtype=F32, jobs=()):
    if mode == "nt":
        (M, K), N = a.shape, b.shape[0]
    elif mode == "nn":
        (M, K), N = a.shape, b.shape[1]
    else:
        (K, M), N = a.shape, b.shape[1]
    if mode == "nt":
        tm, tn, tk = _tile(M, 1024), _tile(N, 768), _tile(K, 2048)
    elif mode == "nn":
        tm, tn, tk = _tile(M, 1024), _tile(N, 2048), _tile(K, 1408)
    else:
        tm, tn, tk = _tile(M, 1408), _tile(N, 2048), _tile(K, 512)
    nk = K // tk
    if mode == "nt":
        a_spec = pl.BlockSpec((tm, tk), lambda i, j, k: (i, k))
        b_spec = pl.BlockSpec((tn, tk), lambda i, j, k: (j, k))
    elif mode == "nn":
        a_spec = pl.BlockSpec((tm, tk), lambda i, j, k: (i, k))
        b_spec = pl.BlockSpec((tk, tn), lambda i, j, k: (k, j))
    else:
        a_spec = pl.BlockSpec((tk, tm), lambda i, j, k: (k, i))
        b_spec = pl.BlockSpec((tk, tn), lambda i, j, k: (k, j))

    def body(a_ref, b_ref, o_ref, *scr):
        p = _raw_dot(a_ref[...], b_ref[...], mode)
        if nk == 1:
            o_ref[...] = p.astype(o_ref.dtype)
        else:
            acc = scr[0]
            k = pl.program_id(2)

            @pl.when(k == 0)
            def _():
                acc[...] = p

            @pl.when(k > 0)
            def _():
                acc[...] += p

            @pl.when(k == nk - 1)
            def _():
                o_ref[...] = acc[...].astype(o_ref.dtype)

    grid = (M // tm, N // tn, nk)
    res = carried_call(
        body, jobs, grid, name=f"mm_{mode}_{M}x{N}x{K}" + ("_c" if jobs else ""),
        out_shape=[jax.ShapeDtypeStruct((M, N), out_dtype)], in_specs=[a_spec, b_spec],
        out_specs=[pl.BlockSpec((tm, tn), lambda i, j, k: (i, j))],
        scratch_shapes=[] if nk == 1 else [pltpu.VMEM((tm, tn), F32)], operands=(a, b))
    return res[0][0], res[1]


@jax.custom_vjp
def lin_t(x, wt):
    return mm(x, wt, "nt")


def _lin_t_fwd(x, wt):
    return mm(x, wt, "nt"), (x, wt)


def _lin_t_bwd(res, dy):
    x, wt = res
    return mm(dy, wt, "nn"), mm(dy, x, "tn", out_dtype=wt.dtype)


lin_t.defvjp(_lin_t_fwd, _lin_t_bwd)


@jax.custom_vjp
def lin_n(x, w):
    return mm(x, w, "nn")


def _lin_n_fwd(x, w):
    return mm(x, w, "nn"), (x, w)


def _lin_n_bwd(res, dy):
    x, w = res
    return mm(dy, w, "nt"), mm(x, dy, "tn", out_dtype=w.dtype)


lin_n.defvjp(_lin_n_fwd, _lin_n_bwd)


def _full_spec(shape):
    nd = len(shape)
    return pl.BlockSpec(tuple(shape), lambda i: (0,) * nd)


def _row_spec(tr, c):
    return pl.BlockSpec((tr, c), lambda i: (i, 0))


def _rowop_tr(total_cols, T):
    budget = 20 * 1024 * 1024
    for tr in (512, 256, 128, 64, 32, 16, 8):
        if T % tr == 0 and total_cols * tr * 4 * 2 <= budget:
            return tr
    return 8


def rowop(fn, name, rows, consts, params, out_cols):
    nr, nc, npar, nout = len(rows), len(consts), len(params), len(out_cols)
    T = rows[0].shape[0]
    in_cols = [r.shape[1] for r in rows] + [c.shape[1] for c in consts]
    tr_f = _rowop_tr(sum(in_cols) + sum(out_cols), T)
    tr_b = _rowop_tr(sum(in_cols) + sum(out_cols) + sum(r.shape[1] for r in rows), T)

    def fwd_call(rows, consts, params):
        def body(*refs):
            ins = [r[...] for r in refs[:nr + nc + npar]]
            outs = fn(*ins)
            for o_ref, o in zip(refs[nr + nc + npar:], outs):
                o_ref[...] = o.astype(F32)

        return pl.pallas_call(
            body, name=name + "_fwd", grid=(T // tr_f,),
            out_shape=[jax.ShapeDtypeStruct((T, c), F32) for c in out_cols],
            in_specs=[_row_spec(tr_f, c) for c in in_cols] + [_full_spec(p.shape) for p in params],
            out_specs=[_row_spec(tr_f, c) for c in out_cols],
            compiler_params=_params(dimension_semantics=("arbitrary",)),
        )(*rows, *consts, *params)

    def bwd_call(rows, consts, params, douts):
        def body(*refs):
            i = pl.program_id(0)
            rv = [r[...] for r in refs[:nr]]
            cv = [r[...] for r in refs[nr:nr + nc]]
            pv = [r[...] for r in refs[nr + nc:nr + nc + npar]]
            dv = [r[...] for r in refs[nr + nc + npar:nr + nc + npar + nout]]
            orefs = refs[nr + nc + npar + nout:]
            _, vjp = jax.vjp(lambda rr, pp: tuple(fn(*rr, *cv, *pp)), rv, pv)
            drows, dpars = vjp(tuple(dv))
            for o_ref, g in zip(orefs[:nr], drows):
                o_ref[...] = g.astype(F32)

            @pl.when(i == 0)
            def _():
                for o_ref in orefs[nr:]:
                    o_ref[...] = jnp.zeros(o_ref.shape, F32)

            for o_ref, g in zip(orefs[nr:], dpars):
                o_ref[...] += g.astype(F32)

        res = pl.pallas_call(
            body, name=name + "_bwd", grid=(T // tr_b,),
            out_shape=[jax.ShapeDtypeStruct(r.shape, F32) for r in rows]
            + [jax.ShapeDtypeStruct(p.shape, F32) for p in params],
            in_specs=[_row_spec(tr_b, c) for c in in_cols] + [_full_spec(p.shape) for p in params]
            + [_row_spec(tr_b, c) for c in out_cols],
            out_specs=[_row_spec(tr_b, r.shape[1]) for r in rows] + [_full_spec(p.shape) for p in params],
            compiler_params=_params(dimension_semantics=("arbitrary",)),
        )(*rows, *consts, *params, *douts)
        return tuple(res[:nr]), tuple(res[nr:])

    @jax.custom_vjp
    def op(rows, consts, params):
        return tuple(fwd_call(rows, consts, params))

    def op_fwd(rows, consts, params):
        return tuple(fwd_call(rows, consts, params)), (rows, consts, params)

    def op_bwd(res, douts):
        rows, consts, params = res
        drows, dpars = bwd_call(rows, consts, params, douts)
        return drows, tuple(jnp.zeros_like(c) for c in consts), dpars

    op.defvjp(op_fwd, op_bwd)
    return op(tuple(rows), tuple(consts), tuple(params))


def _fn_norm(x, g):
    return (_rms(x, g),)


def _fn_add_norm(x, y, g):
    s = x + y
    return s, _rms(s, g)


def _fn_swiglu(gu):
    return (jax.nn.silu(gu[:, :FFN_HIDDEN]) * gu[:, FFN_HIDDEN:],)


def _heads(x, n, width=HEAD_DIM):
    return [x[:, h * width:(h + 1) * width] for h in range(n)]


def _fn_sb_pre(qkv, qn, kn):
    hs = _heads(qkv, 3 * SB_HEADS)
    q = jnp.concatenate([_rms(h, qn) for h in hs[:SB_HEADS]], axis=1)
    k = jnp.concatenate([_rms(h, kn) for h in hs[SB_HEADS:2 * SB_HEADS]], axis=1)
    v = jnp.concatenate(hs[2 * SB_HEADS:], axis=1)
    return q, k, v


def _fn_dsw_pre(qkv, cosm, sinm, qn, kn):
    hs = _heads(qkv, 3 * DSW_HEADS)
    q = jnp.concatenate([rope(_rms(h, qn), cosm, sinm) for h in hs[:DSW_HEADS]], axis=1)
    k = jnp.concatenate([rope(_rms(h, kn), cosm, sinm) for h in hs[DSW_HEADS:2 * DSW_HEADS]], axis=1)
    v = jnp.concatenate(hs[2 * DSW_HEADS:], axis=1)
    return q, k, v


def _fn_dsw_combine(o, lse):
    os_, ls_ = _heads(o, DSW_HEADS), _heads(lse, DSW_HEADS)
    out = [None] * DSW_HEADS
    for hg in range(DSW_HG):
        l3 = [ls_[g * DSW_HG + hg] for g in range(3)]
        m = jnp.maximum(jnp.maximum(l3[0], l3[1]), l3[2])
        e3 = [jnp.exp(l - m) for l in l3]
        den = e3[0] + e3[1] + e3[2]
        for g in range(3):
            out[g * DSW_HG + hg] = os_[g * DSW_HG + hg] * (e3[g] / den)
    return (jnp.concatenate(out, axis=1),)


def _l2(x):
    return x * lax.rsqrt(jnp.sum(x * x, axis=-1, keepdims=True) + NORM_EPS)


def _fn_gdn_pre(qkv, ba, a_log, dt_bias):
    x = jax.nn.silu(qkv)
    hs = _heads(x, 2 * GDN_K_HEADS + GDN_V_HEADS)
    rep = GDN_V_HEADS // GDN_K_HEADS
    qh = [_l2(h) * HEAD_DIM ** -0.5 for h in hs[:GDN_K_HEADS]]
    kh = [_l2(h) for h in hs[GDN_K_HEADS:2 * GDN_K_HEADS]]
    q = jnp.concatenate([qh[h // rep] for h in range(GDN_V_HEADS)], axis=1)
    k = jnp.concatenate([kh[h // rep] for h in range(GDN_V_HEADS)], axis=1)
    v = jnp.concatenate(hs[2 * GDN_K_HEADS:], axis=1)
    b = ba[:, :GDN_V_HEADS]
    a = ba[:, GDN_V_HEADS:2 * GDN_V_HEADS]
    beta = jax.nn.sigmoid(b)
    g = -jnp.exp(a_log) * jax.nn.softplus(a + dt_bias)
    rows = b.shape[0]
    beta_b = jnp.concatenate([jnp.broadcast_to(beta[:, h:h + 1], (rows, HEAD_DIM)) for h in range(GDN_V_HEADS)], axis=1)
    g_b = jnp.concatenate([jnp.broadcast_to(g[:, h:h + 1], (rows, HEAD_DIM)) for h in range(GDN_V_HEADS)], axis=1)
    return q, k, v, g_b, beta_b


def _fn_gdn_post(o, z, o_norm):
    os_, zs = _heads(o, GDN_V_HEADS), _heads(z, GDN_V_HEADS)
    return (jnp.concatenate([_rms(oh, o_norm) * jax.nn.silu(zh) for oh, zh in zip(os_, zs)], axis=1),)


def _expm1(x):
    return jnp.tanh(0.5 * x) * (jnp.exp(x) + 1.0)


def _fn_lru_gates(xc, conv_b, w_a, b_a, w_x, b_x, lam):
    xr = xc + conv_b
    xs = _heads(xr, LRU_BLOCKS, LRU_BLOCK_DIM)
    r = jnp.concatenate([bdot(xs[n], w_a[n], "nn") for n in range(LRU_BLOCKS)], axis=1) + b_a
    i = jnp.concatenate([bdot(xs[n], w_x[n], "nn") for n in range(LRU_BLOCKS)], axis=1) + b_x
    r = jax.nn.sigmoid(r)
    i = jax.nn.sigmoid(i)
    log_a = -LRU_C * r * jax.nn.softplus(-lam)
    a = jnp.exp(log_a)
    u = jnp.sqrt(-_expm1(2.0 * log_a)) * (i * xr)
    return a, u


def _fn_lru_out(hs, gate):
    c = math.sqrt(2.0 / math.pi)
    gl = 0.5 * gate * (1.0 + jnp.tanh(c * (gate + 0.044715 * (gate * gate * gate))))
    return (hs * gl,)


def loss_head(x, f, target):
    T, D = x.shape
    tr = _pick(T, (256, 128, 64, 32, 16, 8))

    def body(x_ref, f_ref, t_ref, l_ref, dy_ref):
        i = pl.program_id(0)
        err = (x_ref[...] + f_ref[...]) - t_ref[...]
        dy_ref[...] = err * (1.0 / D)
        part = 0.5 * jnp.sum(jnp.mean(err * err, axis=-1, keepdims=True), axis=0, keepdims=True)

        @pl.when(i == 0)
        def _():
            l_ref[...] = jnp.zeros(l_ref.shape, F32)

        l_ref[...] += jnp.broadcast_to(part, l_ref.shape)

    l, dy = pl.pallas_call(
        body, name="loss_head", grid=(T // tr,),
        out_shape=[jax.ShapeDtypeStruct((8, LANES), F32), jax.ShapeDtypeStruct((T, D), F32)],
        in_specs=[_row_spec(tr, D)] * 3, out_specs=[_full_spec((8, LANES)), _row_spec(tr, D)],
        compiler_params=_params(dimension_semantics=("arbitrary",)),
    )(x, f, target)
    return l[0, 0], dy


def _as_bf16(x):
    return x.astype(BF16).astype(F32)


def _shift_rows(x, s):
    if s == 0:
        return x
    n = x.shape[0]
    row = lax.broadcasted_iota(jnp.int32, x.shape, 0)
    rolled = pltpu.roll(x, s % n, axis=0)
    keep = (row >= s) if s > 0 else (row < n + s)
    return jnp.where(keep, rolled, 0.0)


def _conv_fwd_call(x, w):
    T, C = x.shape
    K = w.shape[0]
    cb = _pick(C, (256, 128))

    def body(x_ref, w_ref, y_ref):
        xv, wv = _as_bf16(x_ref[...]), _as_bf16(w_ref[...])
        acc = xv * wv[K - 1:K, :]
        for k in range(K - 1):
            acc = acc + _shift_rows(xv, K - 1 - k) * wv[k:k + 1, :]
        y_ref[...] = acc

    return pl.pallas_call(
        body, name=f"conv_fwd_{C}", grid=(C // cb,), out_shape=jax.ShapeDtypeStruct((T, C), F32),
        in_specs=[pl.BlockSpec((T, cb), lambda j: (0, j)), pl.BlockSpec((K, cb), lambda j: (0, j))],
        out_specs=pl.BlockSpec((T, cb), lambda j: (0, j)),
        compiler_params=_params(dimension_semantics=("arbitrary",)),
    )(x, w)


def _conv_bwd_call(x, w, dy):
    T, C = x.shape
    K = w.shape[0]
    cb = _pick(C, (256, 128))

    def body(x_ref, w_ref, dy_ref, dx_ref, dw_ref):
        xv, dv, wv = _as_bf16(x_ref[...]), _as_bf16(dy_ref[...]), _as_bf16(w_ref[...])
        acc = dv * wv[K - 1:K, :]
        rows = [None] * K
        rows[K - 1] = jnp.sum(dv * xv, axis=0, keepdims=True)
        for k in range(K - 1):
            s = K - 1 - k
            acc = acc + _shift_rows(dv, -s) * wv[k:k + 1, :]
            rows[k] = jnp.sum(dv * _shift_rows(xv, s), axis=0, keepdims=True)
        dx_ref[...] = acc
        dw_ref[...] = jnp.concatenate(rows + [jnp.zeros((8 - K, cb), F32)], axis=0)

    dx, dw = pl.pallas_call(
        body, name=f"conv_bwd_{C}", grid=(C // cb,),
        out_shape=[jax.ShapeDtypeStruct((T, C), F32), jax.ShapeDtypeStruct((8, C), F32)],
        in_specs=[pl.BlockSpec((T, cb), lambda j: (0, j)), pl.BlockSpec((K, cb), lambda j: (0, j)),
                  pl.BlockSpec((T, cb), lambda j: (0, j))],
        out_specs=[pl.BlockSpec((T, cb), lambda j: (0, j)), pl.BlockSpec((8, cb), lambda j: (0, j))],
        compiler_params=_params(dimension_semantics=("arbitrary",)),
    )(x, w, dy)
    return dx, dw[:K]


@jax.custom_vjp
def dwconv(x, w):
    return _conv_fwd_call(x, w)


def _dwconv_fwd(x, w):
    return _conv_fwd_call(x, w), (x, w)


def _dwconv_bwd(res, dy):
    return _conv_bwd_call(*res, dy)


dwconv.defvjp(_dwconv_fwd, _dwconv_bwd)


def _scan_fwd_call(a, u):
    T, C = a.shape
    cb = _pick(C, (256, 128))

    def body(a_ref, u_ref, h_ref):
        def step(i, h):
            r = pl.multiple_of(i * 8, 8)
            at, ut = a_ref[pl.ds(r, 8), :], u_ref[pl.ds(r, 8), :]
            rows = []
            for j in range(8):
                h = at[j:j + 1, :] * h + ut[j:j + 1, :]
                rows.append(h)
            h_ref[pl.ds(r, 8), :] = jnp.concatenate(rows, axis=0)
            return h

        lax.fori_loop(0, T // 8, step, jnp.zeros((1, cb), F32))

    return pl.pallas_call(
        body, name="lru_scan_fwd", grid=(C // cb,), out_shape=jax.ShapeDtypeStruct((T, C), F32),
        in_specs=[pl.BlockSpec((T, cb), lambda j: (0, j))] * 2, out_specs=pl.BlockSpec((T, cb), lambda j: (0, j)),
        compiler_params=_params(dimension_semantics=("arbitrary",)),
    )(a, u)


def _scan_bwd_call(a, hs, dh):
    T, C = a.shape
    cb = _pick(C, (256, 128))
    nt = T // 8

    def body(a_ref, h_ref, dh_ref, da_ref, du_ref):
        def step(s, carry):
            i = nt - 1 - s
            r = pl.multiple_of(i * 8, 8)
            rp = pl.multiple_of(jnp.maximum(i - 1, 0) * 8, 8)
            at, ht, dt = a_ref[pl.ds(r, 8), :], h_ref[pl.ds(r, 8), :], dh_ref[pl.ds(r, 8), :]
            hprev_tile = h_ref[pl.ds(rp, 8), :]
            h_before = jnp.where(i > 0, hprev_tile[7:8, :], 0.0)
            da_rows, du_rows = [None] * 8, [None] * 8
            for j in range(7, -1, -1):
                lam = dt[j:j + 1, :] + carry
                du_rows[j] = lam
                hp = ht[j - 1:j, :] if j > 0 else h_before
                da_rows[j] = lam * hp
                carry = at[j:j + 1, :] * lam
            da_ref[pl.ds(r, 8), :] = jnp.concatenate(da_rows, axis=0)
            du_ref[pl.ds(r, 8), :] = jnp.concatenate(du_rows, axis=0)
            return carry

        lax.fori_loop(0, nt, step, jnp.zeros((1, cb), F32))

    return pl.pallas_call(
        body, name="lru_scan_bwd", grid=(C // cb,), out_shape=[jax.ShapeDtypeStruct((T, C), F32)] * 2,
        in_specs=[pl.BlockSpec((T, cb), lambda j: (0, j))] * 3,
        out_specs=[pl.BlockSpec((T, cb), lambda j: (0, j))] * 2,
        compiler_params=_params(dimension_semantics=("arbitrary",)),
    )(a, hs, dh)


@jax.custom_vjp
def lru_scan(a, u):
    return _scan_fwd_call(a, u)


def _lru_scan_fwd(a, u):
    hs = _scan_fwd_call(a, u)
    return hs, (a, hs)


def _lru_scan_bwd(res, dh):
    a, hs = res
    return tuple(_scan_bwd_call(a, hs, dh))


lru_scan.defvjp(_lru_scan_fwd, _lru_scan_bwd)


def _tri(n, kind):
    r = lax.broadcasted_iota(jnp.int32, (n, n), 0)
    c = lax.broadcasted_iota(jnp.int32, (n, n), 1)
    m = {"gt": r > c, "le": r <= c, "lt": r < c, "ge": r >= c, "eq": r == c}[kind]
    return jnp.where(m, 1.0, 0.0).astype(BF16)


def _sb_fwd_call(q, k, v, shards=()):
    T, HD = q.shape
    H = HD // HEAD_DIM
    tb = _pick(T, (SB_BLOCK, 128))
    scale = HEAD_DIM ** -0.5

    def body(q_ref, k_ref, v_ref, o_ref, tot_ref):
        i = pl.program_id(1)
        qb = q_ref[...].astype(BF16)
        u_gt = _tri(tb, "gt")
        row = lax.broadcasted_iota(jnp.int32, (tb, tb), 0)
        col = lax.broadcasted_iota(jnp.int32, (tb, tb), 1)

        def step(j, n, carry, diagonal):
            acc, run = carry
            off = pl.multiple_of(j * tb, tb)
            kb = k_ref[pl.ds(off, n * tb), :].astype(BF16)
            vb = v_ref[pl.ds(off, n * tb), :].astype(BF16)
            z = lax.dot_general(qb, kb, _DN["nt"], preferred_element_type=F32) * scale
            sp, ls, _ = _softplus_parts(z)
            if diagonal:
                sp = jnp.where(col < row, sp, 0.0)
            parts = [None] * n
            for s in reversed(range(n)):
                xs = sp[:, s * tb:(s + 1) * tb]
                parts[s] = _dot01_raw(xs, u_gt, _DN["nn"], False) + run
                run = run + jnp.sum(xs, axis=1, keepdims=True)
            between = parts[0] if n == 1 else jnp.concatenate(parts, axis=1)
            w = jnp.exp(ls - between)
            if diagonal:
                w = jnp.where(col < row, w, 0.0)
            acc = acc + lax.dot_general(w.astype(BF16), vb, _DN["nn"], preferred_element_type=F32)
            return acc, run

        carry = step(i, 1, (jnp.zeros((tb, HEAD_DIM), F32), jnp.zeros((tb, 1), F32)), True)
        left = i
        for n in SB_WIDE:
            carry = lax.fori_loop(0, lax.div(left, n), lambda t, c, n=n, left=left: step(left - n * (t + 1), n, c, False),
                                  carry)
            left = lax.rem(left, n)
        acc, run = carry
        o_ref[...] = acc
        tot_ref[...] = jnp.broadcast_to(run, (tb, HEAD_DIM))

    blk = pl.BlockSpec((tb, HEAD_DIM), lambda h, i: (i, h))
    full = pl.BlockSpec((T, HEAD_DIM), lambda h, i: (0, h))
    (o, tot), gathered = carried_call(
        body, [GatherJob(s) for s in shards], (H, T // tb), name="sb_attn_fwd",
        out_shape=[jax.ShapeDtypeStruct((T, HD), F32)] * 2, in_specs=[blk, full, full], out_specs=[blk, blk],
        scratch_shapes=[], operands=(q, k, v))
    return o, tot, tuple(gathered)


def _sb_bwd_call(q, k, v, tot, do, jobs=()):
    T, HD = q.shape
    H = HD // HEAD_DIM
    tb = _pick(T, (SB_BLOCK, 128))
    scale = HEAD_DIM ** -0.5

    def body(q_ref, k_ref, v_ref, tot_ref, do_ref, dq_ref, dk_ref, dv_ref):
        i = pl.program_id(1)

        @pl.when(i == 0)
        def _():
            dk_ref[...] = jnp.zeros(dk_ref.shape, F32)
            dv_ref[...] = jnp.zeros(dv_ref.shape, F32)

        qb = q_ref[...].astype(BF16)
        dob = do_ref[...].astype(BF16)
        tot = tot_ref[:, 0:1]
        u_le = _tri(tb, "le")
        u_lt = _tri(tb, "lt")
        row = lax.broadcasted_iota(jnp.int32, (tb, tb), 0)
        col = lax.broadcasted_iota(jnp.int32, (tb, tb), 1)

        def prefix_sums(x, u, start):
            out = []
            for s in range(x.shape[1] // tb):
                xs = x[:, s * tb:(s + 1) * tb]
                out.append(_dot01_raw(xs, u, _DN["nn"], False) + start)
                start = start + jnp.sum(xs, axis=1, keepdims=True)
            return (out[0] if len(out) == 1 else jnp.concatenate(out, axis=1)), start

        def step(j, n, carry, diagonal):
            dq, cs, cd = carry
            off = pl.multiple_of(j * tb, tb)
            kb = k_ref[pl.ds(off, n * tb), :].astype(BF16)
            vb = v_ref[pl.ds(off, n * tb), :].astype(BF16)
            z = lax.dot_general(qb, kb, _DN["nt"], preferred_element_type=F32) * scale
            sp, ls, sig = _softplus_parts(z)
            if diagonal:
                sp = jnp.where(col < row, sp, 0.0)
            prefix, cs = prefix_sums(sp, u_le, cs)
            w = jnp.exp(ls - (tot - prefix))
            if diagonal:
                w = jnp.where(col < row, w, 0.0)
            wb = w.astype(BF16)
            dv_ref[pl.ds(off, n * tb), :] += lax.dot_general(wb, dob, _DN["tn"], preferred_element_type=F32)
            dw = lax.dot_general(dob, vb, _DN["nt"], preferred_element_type=F32)
            dl = dw * w
            before, cd = prefix_sums(dl, u_lt, cd)
            dz = (dl * (1.0 - sig) - before * sig) * scale
            if diagonal:
                dz = jnp.where(col < row, dz, 0.0)
            dzb = dz.astype(BF16)
            dq = dq + lax.dot_general(dzb, kb, _DN["nn"], preferred_element_type=F32)
            dk_ref[pl.ds(off, n * tb), :] += lax.dot_general(dzb, qb, _DN["tn"], preferred_element_type=F32)
            return dq, cs, cd

        z1 = jnp.zeros((tb, 1), F32)
        carry, done = (jnp.zeros((tb, HEAD_DIM), F32), z1, z1), 0
        for n in SB_WIDE:
            trips = lax.div(i - done, n)
            carry = lax.fori_loop(0, trips, lambda t, c, n=n, done=done: step(done + n * t, n, c, False), carry)
            done = done + trips * n
        dq, _, _ = step(i, 1, carry, True)
        dq_ref[...] = dq

    blk = pl.BlockSpec((tb, HEAD_DIM), lambda h, i: (i, h))
    full = pl.BlockSpec((T, HEAD_DIM), lambda h, i: (0, h))
    grads, exchanged = carried_call(
        body, jobs, (H, T // tb), name="sb_attn_bwd", out_shape=[jax.ShapeDtypeStruct((T, HD), F32)] * 3,
        in_specs=[blk, full, full, blk, blk], out_specs=[blk, full, full], scratch_shapes=[],
        operands=(q, k, v, tot, do))
    return (*grads, exchanged) if jobs else tuple(grads)


@jax.custom_vjp
def sb_attn(q, k, v, shards):
    o, _, gathered = _sb_fwd_call(q, k, v, shards)
    return o, gathered


def _sb_attn_fwd(q, k, v, shards):
    o, tot, gathered = _sb_fwd_call(q, k, v, shards)
    return (o, gathered), (q, k, v, tot, shards)


def _sb_attn_bwd(res, cts):
    q, k, v, tot, shards = res
    return (*_sb_bwd_call(q, k, v, tot, cts[0]), tuple(jnp.zeros_like(s) for s in shards))


sb_attn.defvjp(_sb_attn_fwd, _sb_attn_bwd)


def _dsw_tile(q, kp, kc, vp, vc, n):
    blk = DSW_BLOCK
    scale = HEAD_DIM ** -0.5
    qi = lax.broadcasted_iota(jnp.int32, (blk, blk), 0)
    kj = lax.broadcasted_iota(jnp.int32, (blk, blk), 1)
    neg = -1e30
    s_p = jnp.where((kj >= qi) & (n > 0), bdot(q, kp, "nt") * scale, neg)
    s_c = jnp.where(kj <= qi, bdot(q, kc, "nt") * scale, neg)
    m = jnp.maximum(jnp.max(s_p, axis=-1, keepdims=True), jnp.max(s_c, axis=-1, keepdims=True))
    p_p, p_c = jnp.exp(s_p - m), jnp.exp(s_c - m)
    den = jnp.sum(p_p, axis=-1, keepdims=True) + jnp.sum(p_c, axis=-1, keepdims=True)
    o = bdot(p_p / den, vp, "nn") + bdot(p_c / den, vc, "nn")
    lse = m + jnp.log(den)
    return o, jnp.broadcast_to(lse, (blk, HEAD_DIM))


def _dsw_specs(nsub):
    cur = pl.BlockSpec((None, DSW_BLOCK, HEAD_DIM), lambda r, h, n: (r, n, h))
    prev = pl.BlockSpec((None, DSW_BLOCK, HEAD_DIM), lambda r, h, n: (r, jnp.maximum(n - 1, 0), h))
    whole = pl.BlockSpec((None, nsub, HEAD_DIM), lambda r, h, n: (r, 0, h))
    return cur, prev, whole


def _dsw_fwd_call(q, k, v):
    d, nsub, HD = q.shape
    cur, prev, _ = _dsw_specs(nsub)

    def body(q_ref, kp_ref, kc_ref, vp_ref, vc_ref, o_ref, l_ref):
        n = pl.program_id(2)
        o, l = _dsw_tile(q_ref[...], kp_ref[...], kc_ref[...], vp_ref[...], vc_ref[...], n)
        o_ref[...] = o
        l_ref[...] = l

    return pl.pallas_call(
        body, name=f"dsw_attn_fwd_d{d}", grid=(d, HD // HEAD_DIM, nsub // DSW_BLOCK),
        out_shape=[jax.ShapeDtypeStruct(q.shape, F32)] * 2,
        in_specs=[cur, prev, cur, prev, cur], out_specs=[cur, cur],
        compiler_params=_params(dimension_semantics=("arbitrary",) * 3),
    )(q, k, k, v, v)


def _dsw_bwd_call(q, k, v, do, dl):
    d, nsub, HD = q.shape
    cur, prev, whole = _dsw_specs(nsub)
    blk = DSW_BLOCK

    def body(q_ref, kp_ref, kc_ref, vp_ref, vc_ref, do_ref, dl_ref, dq_ref, dk_ref, dv_ref):
        n = pl.program_id(2)

        @pl.when(n == 0)
        def _():
            dk_ref[...] = jnp.zeros(dk_ref.shape, F32)
            dv_ref[...] = jnp.zeros(dv_ref.shape, F32)

        _, vjp = jax.vjp(lambda a, b, c, e, f: _dsw_tile(a, b, c, e, f, n),
                         q_ref[...], kp_ref[...], kc_ref[...], vp_ref[...], vc_ref[...])
        dq, dkp, dkc, dvp, dvc = vjp((do_ref[...], dl_ref[...]))
        dq_ref[...] = dq
        c0 = pl.multiple_of(n * blk, blk)
        p0 = pl.multiple_of(jnp.maximum(n - 1, 0) * blk, blk)
        dk_ref[pl.ds(c0, blk), :] += dkc
        dv_ref[pl.ds(c0, blk), :] += dvc
        dk_ref[pl.ds(p0, blk), :] += dkp
        dv_ref[pl.ds(p0, blk), :] += dvp

    return pl.pallas_call(
        body, name=f"dsw_attn_bwd_d{d}", grid=(d, HD // HEAD_DIM, nsub // blk),
        out_shape=[jax.ShapeDtypeStruct(q.shape, F32)] * 3,
        in_specs=[cur, prev, cur, prev, cur, cur, cur], out_specs=[cur, whole, whole],
        compiler_params=_params(dimension_semantics=("arbitrary",) * 3),
    )(q, k, k, v, v, do, dl)


@jax.custom_vjp
def dsw_attn(q, k, v):
    return tuple(_dsw_fwd_call(q, k, v))


def _dsw_attn_fwd(q, k, v):
    return tuple(_dsw_fwd_call(q, k, v)), (q, k, v)


def _dsw_attn_bwd(res, cts):
    return tuple(_dsw_bwd_call(*res, *cts))


dsw_attn.defvjp(_dsw_attn_fwd, _dsw_attn_bwd)


_LOG2_CHUNK = GDN_CHUNK.bit_length() - 1
_LOG2_HEAD_DIM = HEAD_DIM.bit_length() - 1


def _unit_lower_inverse(a):
    n = a.shape[0]
    r = lax.broadcasted_iota(jnp.int32, (n, n), 0)
    c = lax.broadcasted_iota(jnp.int32, (n, n), 1)
    x = -a
    t = jnp.where(r == c, 1.0, 0.0) + x
    for _ in range(_LOG2_CHUNK - 1):
        x = _mm3(x, x)
        t = t + _mm3(t, x)
    return t


@jax.custom_vjp
def unit_lower_inverse(a):
    return _unit_lower_inverse(a)


def _unit_lower_inverse_fwd(a):
    t = _unit_lower_inverse(a)
    return t, t


def _unit_lower_inverse_bwd(t, ct):
    return (-_mm3(_mm3(t.T, ct), t.T),)


unit_lower_inverse.defvjp(_unit_lower_inverse_fwd, _unit_lower_inverse_bwd)


def _gdn_step(state, q, k, v, gb, bb):
    C, NH = GDN_CHUNK, GDN_HEADS_PER_STEP
    R = NH * C
    r = lax.broadcasted_iota(jnp.int32, (R, R), 0)
    c = lax.broadcasted_iota(jnp.int32, (R, R), 1)
    same = lax.shift_right_logical(r, _LOG2_CHUNK) == lax.shift_right_logical(c, _LOG2_CHUNK)
    causal, strict = same & (r >= c), same & (r > c)
    gc = dot01(gb, jnp.where(causal, 1.0, 0.0).astype(BF16), True)
    g_sq = jnp.concatenate([gc] * (R // HEAD_DIM), axis=1)
    g_row = dot01(jnp.where(r == c, g_sq, 0.0), jnp.ones((R, R), BF16), True)
    decay = jnp.where(causal, jnp.exp(jnp.where(causal, g_sq - g_row, 0.0)), 0.0)
    kb, vb = k * bb, v * bb
    a_mat = jnp.where(strict, bdot(kb, k, "nt") * decay, 0.0)
    t_mat = unit_lower_inverse(a_mat)
    uw = bdot(t_mat, jnp.concatenate([vb, kb * jnp.exp(gc)], axis=1), "nn")
    u, w = uw[:, :HEAD_DIM], uw[:, HEAD_DIM:]
    hr = lax.shift_right_logical(lax.broadcasted_iota(jnp.int32, (R, NH * HEAD_DIM), 0), _LOG2_CHUNK)
    hc = lax.shift_right_logical(lax.broadcasted_iota(jnp.int32, (R, NH * HEAD_DIM), 1), _LOG2_HEAD_DIM)

    def widen(m):
        return jnp.where(hr == hc, jnp.concatenate([m] * NH, axis=1), 0.0)

    v_new = u - bdot(widen(w), state, "nn")
    attn = bdot(q, k, "nt") * decay
    out = bdot(widen(q * jnp.exp(gc)), state, "nn") + bdot(attn, v_new, "nn")
    last = [gc[h * C + C - 1:h * C + C, :] for h in range(NH)]
    g_last_rows = jnp.concatenate([jnp.broadcast_to(l, (C, HEAD_DIM)) for l in last], axis=0)
    g_last_state = jnp.concatenate([jnp.broadcast_to(l, (HEAD_DIM, HEAD_DIM)) for l in last], axis=0)
    k_dec = k * jnp.exp(g_last_rows - gc)
    new_state = state * jnp.exp(g_last_state) + bdot(widen(k_dec), v_new, "tn")
    return new_state, out


def _gdn_stack(ref, g):
    h0 = g * GDN_HEADS_PER_STEP
    return jnp.concatenate([ref[:, (h0 + h) * HEAD_DIM:(h0 + h + 1) * HEAD_DIM] for h in range(GDN_HEADS_PER_STEP)],
                           axis=0)


def _gdn_unstack(ref, g, val):
    h0 = g * GDN_HEADS_PER_STEP
    for h in range(GDN_HEADS_PER_STEP):
        ref[:, (h0 + h) * HEAD_DIM:(h0 + h + 1) * HEAD_DIM] = val[h * GDN_CHUNK:(h + 1) * GDN_CHUNK]


def _gdn_fwd_call(q, k, v, gb, bb, shards=()):
    T, HD = v.shape
    H = HD // HEAD_DIM
    N = T // GDN_CHUNK
    hb = GDN_HEADS_PER_STEP * GDN_GROUPS_PER_STEP
    W = hb * HEAD_DIM
    SW = GDN_HEADS_PER_STEP * HEAD_DIM

    def body(q_ref, k_ref, v_ref, g_ref, b_ref, o_ref, s_ref, state):
        n = pl.program_id(1)

        @pl.when(n == 0)
        def _():
            state[...] = jnp.zeros(state.shape, F32)

        s_in = state[...]
        s_ref[...] = s_in
        res = [_gdn_step(s_in[g * SW:(g + 1) * SW], *[_gdn_stack(ref, g) for ref in (q_ref, k_ref, v_ref, g_ref, b_ref)])
               for g in range(GDN_GROUPS_PER_STEP)]
        for g, (ns, o) in enumerate(res):
            state[g * SW:(g + 1) * SW, :] = ns
            _gdn_unstack(o_ref, g, o)

    blk = pl.BlockSpec((GDN_CHUNK, W), lambda h, n: (n, h))
    sblk = pl.BlockSpec((None, W, HEAD_DIM), lambda h, n: (n, h, 0))
    (o, states), gathered = carried_call(
        body, [GatherJob(s) for s in shards], (H // hb, N), name="gdn_chunk_fwd",
        out_shape=[jax.ShapeDtypeStruct((T, HD), F32), jax.ShapeDtypeStruct((N, H * HEAD_DIM, HEAD_DIM), F32)],
        in_specs=[blk] * 5, out_specs=[blk, sblk], scratch_shapes=[pltpu.VMEM((W, HEAD_DIM), F32)],
        operands=(q, k, v, gb, bb))
    return o, states, tuple(gathered)


def _gdn_bwd_call(q, k, v, gb, bb, states, do):
    T, HD = v.shape
    H = HD // HEAD_DIM
    N = T // GDN_CHUNK
    hb = GDN_HEADS_PER_STEP * GDN_GROUPS_PER_STEP
    W = hb * HEAD_DIM
    SW = GDN_HEADS_PER_STEP * HEAD_DIM

    def body(q_ref, k_ref, v_ref, g_ref, b_ref, s_ref, do_ref, dq_ref, dk_ref, dv_ref, dg_ref, db_ref, dstate):
        n = pl.program_id(1)

        @pl.when(n == 0)
        def _():
            dstate[...] = jnp.zeros(dstate.shape, F32)

        res = []
        for g in range(GDN_GROUPS_PER_STEP):
            rows = slice(g * SW, (g + 1) * SW)
            _, vjp = jax.vjp(_gdn_step, s_ref[rows, :],
                             *[_gdn_stack(ref, g) for ref in (q_ref, k_ref, v_ref, g_ref, b_ref)])
            res.append(vjp((dstate[rows, :], _gdn_stack(do_ref, g))))
        for g, (ds, *grads) in enumerate(res):
            dstate[g * SW:(g + 1) * SW, :] = ds
            for ref, grad in zip((dq_ref, dk_ref, dv_ref, dg_ref, db_ref), grads):
                _gdn_unstack(ref, g, grad)

    blk = pl.BlockSpec((GDN_CHUNK, W), lambda h, n: (N - 1 - n, h))
    sblk = pl.BlockSpec((None, W, HEAD_DIM), lambda h, n: (N - 1 - n, h, 0))
    return pl.pallas_call(
        body, name="gdn_chunk_bwd", grid=(H // hb, N), out_shape=[jax.ShapeDtypeStruct((T, HD), F32)] * 5,
        in_specs=[blk] * 5 + [sblk, blk], out_specs=[blk] * 5,
        scratch_shapes=[pltpu.VMEM((W, HEAD_DIM), F32)],
        compiler_params=_params(dimension_semantics=("arbitrary", "arbitrary")),
    )(q, k, v, gb, bb, states, do)


@jax.custom_vjp
def gdn_core(q, k, v, gb, bb, shards):
    o, _, gathered = _gdn_fwd_call(q, k, v, gb, bb, shards)
    return o, gathered


def _gdn_core_fwd(q, k, v, gb, bb, shards):
    o, states, gathered = _gdn_fwd_call(q, k, v, gb, bb, shards)
    return (o, gathered), (q, k, v, gb, bb, states, shards)


def _gdn_core_bwd(res, cts):
    *core, shards = res
    return (*_gdn_bwd_call(*core, cts[0]), tuple(jnp.zeros_like(s) for s in shards))


gdn_core.defvjp(_gdn_core_fwd, _gdn_core_bwd)


def _mixer_sb(h, w, shards=()):
    qkv = lin_t(h, w["sb_in_t"])
    q, k, v = rowop(_fn_sb_pre, "sb_pre", [qkv], [], [w["sb_q_norm"], w["sb_k_norm"]], [D_MODEL] * 3)
    o, gathered = sb_attn(q, k, v, tuple(shards))
    return lin_n(o, w["sb_out"]), gathered


def _mixer_gdn(h, w, shards=()):
    wt = w["gdn_in_t"]
    nqkv = 2 * GDN_KEY_DIM + GDN_VAL_DIM
    qkv = lin_t(h, wt[:nqkv])
    z = lin_t(h, wt[nqkv:nqkv + GDN_VAL_DIM])
    w_ba = jnp.pad(wt[nqkv + GDN_VAL_DIM:], ((0, LANES - 2 * GDN_V_HEADS), (0, 0)))
    ba = lin_t(h, w_ba)
    qkv = dwconv(qkv, w["gdn_conv_w"])
    q, k, v, gb, bb = rowop(_fn_gdn_pre, "gdn_pre", [qkv, ba], [], [w["gdn_a_log"], w["gdn_dt_bias"]],
                            [GDN_VAL_DIM] * 5)
    o, gathered = gdn_core(q, k, v, gb, bb, tuple(shards))
    (y,) = rowop(_fn_gdn_post, "gdn_post", [o, z], [], [w["gdn_o_norm"]], [GDN_VAL_DIM])
    return lin_n(y, w["gdn_out"]), gathered


def _to_strided(x, cols, d):
    T = x.shape[0]
    return x[:, cols].reshape(T // d, d, -1).transpose(1, 0, 2)


def _from_strided(x):
    d, n, c = x.shape
    return x.transpose(1, 0, 2).reshape(d * n, c)


def _mixer_dsw(h, cosm, sinm, w):
    qkv = lin_t(h, w["dsw_in_t"])
    nhd = DSW_HEADS * HEAD_DIM
    q, k, v = rowop(_fn_dsw_pre, "dsw_pre", [qkv], [cosm, sinm], [w["dsw_q_norm"], w["dsw_k_norm"]], [nhd] * 3)
    outs, lses = [], []
    for gi, (_, d) in enumerate(DSW_GROUPS):
        cols = slice(gi * DSW_HG * HEAD_DIM, (gi + 1) * DSW_HG * HEAD_DIM)
        o_g, l_g = dsw_attn(_to_strided(q, cols, d), _to_strided(k, cols, d), _to_strided(v, cols, d))
        outs.append(_from_strided(o_g))
        lses.append(_from_strided(l_g))
    (o,) = rowop(_fn_dsw_combine, "dsw_combine", [jnp.concatenate(outs, axis=1), jnp.concatenate(lses, axis=1)],
                 [], [], [nhd])
    return lin_n(o, w["dsw_out"])


def _mixer_lru(h, w):
    wt = w["lru_in_t"]
    gate = lin_t(h, wt[:LRU_WIDTH])
    xr = dwconv(lin_t(h, wt[LRU_WIDTH:]), w["lru_conv_w"])
    a, u = rowop(_fn_lru_gates, "lru_gates", [xr], [],
                 [w["lru_conv_b"], w["lru_w_a"], w["lru_b_a"], w["lru_w_x"], w["lru_b_x"], w["lru_lambda"]],
                 [LRU_WIDTH] * 2)
    hs = lru_scan(a, u)
    (y,) = rowop(_fn_lru_out, "lru_out", [hs, gate], [], [], [LRU_WIDTH])
    return lin_n(y, w["lru_out"])


def mixer_segment(i, x, f_prev, cosm, sinm, w, shards):
    if i == 0:
        (h,) = rowop(_fn_norm, "norm", [x], [], [w["mix_norm"]], [D_MODEL])
    else:
        x, h = rowop(_fn_add_norm, "add_norm", [x, f_prev], [], [w["mix_norm"]], [D_MODEL] * 2)
    kind, gathered = i % 4, ()
    if kind == 0:
        y, gathered = _mixer_sb(h, w, shards)
    elif kind == 1:
        y, gathered = _mixer_gdn(h, w, shards)
    elif kind == 2:
        y = _mixer_dsw(h, cosm, sinm, w)
    else:
        y = _mixer_lru(h, w)
    x, h = rowop(_fn_add_norm, "add_norm", [x, y], [], [w["ffn_norm"]], [D_MODEL] * 2)
    return (x, h), gathered


def sb_segment_in(x, w):
    (h,) = rowop(_fn_norm, "norm", [x], [], [w["mix_norm"]], [D_MODEL])
    qkv = lin_t(h, w["sb_in_t"])
    return rowop(_fn_sb_pre, "sb_pre", [qkv], [], [w["sb_q_norm"], w["sb_k_norm"]], [D_MODEL] * 3)


def sb_segment_out(x, o, w):
    return rowop(_fn_add_norm, "add_norm", [x, lin_n(o, w["sb_out"])], [], [w["ffn_norm"]], [D_MODEL] * 2)


def swiglu_act(gu):
    return rowop(_fn_swiglu, "swiglu", [gu], [], [], [FFN_HIDDEN])[0]


ANY = pl.BlockSpec(memory_space=pl.ANY)


SLAB_BYTES = 4 * 1024 * 1024


def _col_tile(rows, C, itemsize):
    return _tile(C, max(LANES, SLAB_BYTES // (rows * itemsize)))


class GatherJob:
    def __init__(self, shard):
        self.operand = shard
        r, C = shard.shape
        self.out_shape = jax.ShapeDtypeStruct((N_DEV, r, C), shard.dtype)
        self.scratch = [pltpu.SemaphoreType.DMA((7,)), pltpu.SemaphoreType.DMA((7,)), pltpu.SemaphoreType.DMA]

    def _parts(self, x_ref, out_ref, send_sems, recv_sems, local_sem):
        x, y, c = lax.axis_index("x"), lax.axis_index("y"), lax.axis_index("c")
        me, sibling = (x, y, c), (x, y, 1 - c)
        chips = [(1 - x, y), (x, 1 - y), (1 - x, 1 - y)]

        def slot(px, py, pc):
            return out_ref.at[4 * px + 2 * py + pc]

        def copy(k, block, to, src=None):
            return pltpu.make_async_remote_copy(
                src_ref=slot(*block) if src is None else src, dst_ref=slot(*block),
                send_sem=send_sems.at[k], recv_sem=recv_sems.at[k], device_id=to, device_id_type=MESH)

        def mine():
            return pltpu.make_async_copy(x_ref, slot(*me), local_sem)

        def first():
            return [copy(0, me, sibling, src=x_ref)] + [copy(1 + j, me, (*chip, c), src=x_ref)
                                                        for j, chip in enumerate(chips)]

        def passed():
            return [copy(4 + j, (*chip, c), sibling) for j, chip in enumerate(chips)]

        def landed():
            return [copy(1 + j, (*chip, c), me) for j, chip in enumerate(chips)]

        def from_sibling():
            return [copy(0, sibling, me)] + [copy(4 + j, (*chip, 1 - c), me) for j, chip in enumerate(chips)]

        return mine, first, passed, landed, from_sibling

    def start(self, x_ref, out_ref, *sems):
        mine, first, _, _, _ = self._parts(x_ref, out_ref, *sems)
        mine().start()
        for cp in first():
            cp.start()

    def finish(self, x_ref, out_ref, *sems):
        mine, first, passed, landed, from_sibling = self._parts(x_ref, out_ref, *sems)
        onward = passed()
        for arrived, cp in zip(landed(), onward):
            arrived.wait_recv()
            cp.start()
        for cp in from_sibling():
            cp.wait_recv()
        for cp in first() + onward:
            cp.wait_send()
        mine().wait()


class PairJob:
    def __init__(self, g):
        self.operand = g
        _, _, r, C = g.shape
        self.out_shape = jax.ShapeDtypeStruct((N_CHIPS, r, C), g.dtype)
        self.scratch = [pltpu.SemaphoreType.DMA((N_CHIPS,)), pltpu.SemaphoreType.DMA((N_CHIPS,))]

    def _copies(self, g_ref, out_ref, send_sems, recv_sems):
        x, y, c = lax.axis_index("x"), lax.axis_index("y"), lax.axis_index("c")
        return [pltpu.make_async_remote_copy(
            src_ref=g_ref.at[q, 1 - c], dst_ref=out_ref.at[q], send_sem=send_sems.at[q], recv_sem=recv_sems.at[q],
            device_id=(x, y, 1 - c), device_id_type=MESH) for q in range(N_CHIPS)]

    def start(self, *refs):
        for cp in self._copies(*refs):
            cp.start()

    def finish(self, *refs):
        for cp in self._copies(*refs):
            cp.wait()


class ChipsJob:
    def __init__(self, p):
        self.operand = p
        self.out_shape = jax.ShapeDtypeStruct(p.shape, p.dtype)
        self.scratch = [pltpu.SemaphoreType.DMA((3,)), pltpu.SemaphoreType.DMA((3,)), pltpu.SemaphoreType.DMA]

    def _parts(self, p_ref, out_ref, send_sems, recv_sems, local_sem):
        x, y, c = lax.axis_index("x"), lax.axis_index("y"), lax.axis_index("c")
        mychip = 2 * x + y
        chips = [(1 - x, y), (x, 1 - y), (1 - x, 1 - y)]
        def mine():
            return pltpu.make_async_copy(p_ref.at[mychip], out_ref.at[mychip], local_sem)

        def sends():
            return [pltpu.make_async_remote_copy(
                src_ref=p_ref.at[2 * cx + cy], dst_ref=out_ref.at[mychip], send_sem=send_sems.at[j],
                recv_sem=recv_sems.at[j], device_id=(cx, cy, c), device_id_type=MESH)
                for j, (cx, cy) in enumerate(chips)]

        def arrivals():
            return [pltpu.make_async_remote_copy(
                src_ref=p_ref.at[mychip], dst_ref=out_ref.at[2 * cx + cy], send_sem=send_sems.at[j],
                recv_sem=recv_sems.at[j], device_id=(cx, cy, c), device_id_type=MESH)
                for j, (cx, cy) in enumerate(chips)]

        return mine, sends, arrivals

    def start(self, *refs):
        mine, sends, _ = self._parts(*refs)
        mine().start()
        for cp in sends():
            cp.start()

    def finish(self, *refs):
        mine, sends, arrivals = self._parts(*refs)
        for cp in arrivals():
            cp.wait_recv()
        for cp in sends():
            cp.wait_send()
        mine().wait()


def carried_call(body, jobs, grid, *, name, out_shape, in_specs, out_specs, scratch_shapes, operands):
    jobs = list(jobs)
    n_in, n_out, n_scr, nj = len(in_specs), len(out_specs), len(scratch_shapes), len(jobs)
    n_sem = [len(j.scratch) for j in jobs]

    def full_body(*refs):
        core_in = refs[:n_in]
        job_in = refs[n_in:n_in + nj]
        core_out = refs[n_in + nj:n_in + nj + n_out]
        job_out = refs[n_in + nj + n_out:n_in + 2 * nj + n_out]
        rest = refs[n_in + 2 * nj + n_out:]
        core_scr, sems, pos = rest[:n_scr], [], n_scr
        for n in n_sem:
            sems.append(rest[pos:pos + n])
            pos += n
        ids = [pl.program_id(a) for a in range(len(grid))]
        first = functools.reduce(jnp.logical_and, [i == 0 for i in ids])
        last = functools.reduce(jnp.logical_and, [i == g - 1 for i, g in zip(ids, grid)])
        if jobs:
            @pl.when(first)
            def _():
                for j, job in enumerate(jobs):
                    job.start(job_in[j], job_out[j], *sems[j])

        body(*core_in, *core_out, *core_scr)
        if jobs:
            @pl.when(last)
            def _():
                for j, job in enumerate(jobs):
                    job.finish(job_in[j], job_out[j], *sems[j])

    res = pl.pallas_call(
        full_body, name=name, grid=grid,
        out_shape=list(out_shape) + [j.out_shape for j in jobs],
        in_specs=list(in_specs) + [ANY] * nj, out_specs=list(out_specs) + [ANY] * nj,
        scratch_shapes=list(scratch_shapes) + [s for j in jobs for s in j.scratch],
        compiler_params=_params(dimension_semantics=("arbitrary",) * len(grid)),
    )(*operands, *[j.operand for j in jobs])
    return res[:n_out], res[n_out:]


def _exchange(job, name):
    def body(*refs):
        job.start(*refs)
        job.finish(*refs)

    return pl.pallas_call(
        body, name=name, out_shape=job.out_shape, in_specs=[ANY], out_specs=ANY, scratch_shapes=job.scratch,
        compiler_params=pltpu.CompilerParams(has_side_effects=True),
    )(job.operand)


def all_gather(shard, name):
    return _exchange(GatherJob(shard), name)


def exchange_pair(g, name):
    return _exchange(PairJob(g), name)


def exchange_chips(p, name):
    return _exchange(ChipsJob(p), name)


def pair_add(g, got, name):
    _, _, r, C = g.shape
    cb = _col_tile(r, C, g.dtype.itemsize)
    c = lax.axis_index("c")

    def body(c_ref, a_ref, b_ref, o_ref):
        o_ref[...] = (a_ref[...].astype(F32) + b_ref[...].astype(F32)).astype(o_ref.dtype)

    return pl.pallas_call(
        body, name=name, out_shape=jax.ShapeDtypeStruct((N_CHIPS, r, C), g.dtype),
        grid_spec=pltpu.PrefetchScalarGridSpec(
            num_scalar_prefetch=1, grid=(N_CHIPS, C // cb),
            in_specs=[pl.BlockSpec((None, None, r, cb), lambda q, j, cr: (q, cr[0], 0, j)),
                      pl.BlockSpec((None, r, cb), lambda q, j, cr: (q, 0, j))],
            out_specs=pl.BlockSpec((None, r, cb), lambda q, j, cr: (q, 0, j))),
        compiler_params=_params(dimension_semantics=("arbitrary", "arbitrary")),
    )(jnp.reshape(c, (1,)).astype(jnp.int32), g, got)


def sum_slots(parts, name):
    n, r, C = parts.shape
    cb = _col_tile(n * r, C, parts.dtype.itemsize)

    def body(p_ref, o_ref):
        acc = p_ref[0].astype(F32)
        for q in range(1, n):
            acc = acc + p_ref[q].astype(F32)
        o_ref[...] = acc

    return pl.pallas_call(
        body, name=name, out_shape=jax.ShapeDtypeStruct((r, C), F32), grid=(C // cb,),
        in_specs=[pl.BlockSpec((n, r, cb), lambda j: (0, 0, j))], out_specs=pl.BlockSpec((r, cb), lambda j: (0, j)),
        compiler_params=_params(dimension_semantics=("arbitrary",)),
    )(parts)


def reduce_scatter(g, tag):
    _, r, C = g.shape
    g4 = g.reshape(N_CHIPS, 2, r, C)
    got = exchange_pair(g4, f"rs_pair_{tag}")
    pairs = pair_add(g4, got, f"rs_pair_add_{tag}")
    parts = exchange_chips(pairs, f"rs_chips_{tag}")
    return sum_slots(parts, f"rs_sum_{tag}")


def all_reduce(v, tag):
    return sum_slots(all_gather(v, f"ar_gather_{tag}"), f"ar_sum_{tag}")


def adamw(w, g, m, v, name):
    R, C = w.shape
    tr = R
    for cand in (512, 256, 128, 64, 32, 16, 8):
        if R % cand == 0 and cand * C * 4 * 7 * 2 <= 40 * 1024 * 1024:
            tr = cand
            break
    c1 = 1.0 - ADAM_B1 ** ADAM_STEP
    c2 = 1.0 - ADAM_B2 ** ADAM_STEP

    def body(w_ref, g_ref, m_ref, v_ref, d_ref, nm_ref, nv_ref):
        gv = g_ref[...]
        nm = ADAM_B1 * m_ref[...] + (1.0 - ADAM_B1) * gv
        nv = ADAM_B2 * v_ref[...] + (1.0 - ADAM_B2) * (gv * gv)
        d_ref[...] = -ADAM_LR * ((nm / c1) / (jnp.sqrt(nv / c2) + ADAM_EPS) + ADAM_WD * w_ref[...])
        nm_ref[...] = nm
        nv_ref[...] = nv

    spec = pl.BlockSpec((tr, C), lambda i: (i, 0))
    return pl.pallas_call(
        body, name=name, out_shape=[jax.ShapeDtypeStruct((R, C), F32)] * 3, grid=(R // tr,),
        in_specs=[spec] * 4, out_specs=[spec] * 3,
        compiler_params=_params(dimension_semantics=("arbitrary",)),
    )(w, g, m, v)


NAMES = ['mix_norm', 'ffn_norm', 'ffn_w_gu', 'ffn_w_down', 'sb_w_in', 'sb_q_norm', 'sb_k_norm', 'sb_w_out',
         'gdn_w_in', 'gdn_conv_w', 'gdn_a_log', 'gdn_dt_bias', 'gdn_o_norm', 'gdn_w_out', 'dsw_w_in', 'dsw_q_norm',
         'dsw_k_norm', 'dsw_w_out', 'lru_w_in', 'lru_conv_w', 'lru_conv_b', 'lru_w_a', 'lru_b_a', 'lru_w_x',
         'lru_b_x', 'lru_lambda', 'lru_w_out']
REPLICATED = ['mix_norm', 'ffn_norm', 'sb_q_norm', 'sb_k_norm', 'gdn_a_log', 'gdn_dt_bias', 'gdn_o_norm',
              'dsw_q_norm', 'dsw_k_norm']
SMALL_SHARDED = ['gdn_conv_w', 'lru_conv_w', 'lru_conv_b', 'lru_b_a', 'lru_b_x', 'lru_lambda']
IN_T = ['sb_w_in', 'gdn_w_in', 'dsw_w_in', 'lru_w_in']
OUT_N = ['sb_w_out', 'gdn_w_out', 'dsw_w_out', 'lru_w_out']


def _pack(arrs, pad_rows_to=8):
    flat = jnp.concatenate([a.reshape(-1) for a in arrs])
    n = flat.shape[0]
    rows = -(-n // LANES)
    rows = -(-rows // pad_rows_to) * pad_rows_to
    return jnp.pad(flat, (0, rows * LANES - n)).reshape(rows, LANES)


def _unpack(buf, shapes):
    flat = buf.reshape(-1)
    out, o = [], 0
    for s in shapes:
        n = math.prod(s)
        out.append(flat[o:o + n].reshape(s))
        o += n
    return out


def kernel(x, positions, mix_norm, ffn_norm, ffn_w_gu, ffn_w_down, sb_w_in, sb_q_norm, sb_k_norm, sb_w_out, gdn_w_in, gdn_conv_w, gdn_a_log, gdn_dt_bias, gdn_o_norm, gdn_w_out, dsw_w_in, dsw_q_norm, dsw_k_norm, dsw_w_out, lru_w_in, lru_conv_w, lru_conv_b, lru_w_a, lru_b_a, lru_w_x, lru_b_x, lru_lambda, lru_w_out, loss_target, m_mix_norm, m_ffn_norm, m_ffn_w_gu, m_ffn_w_down, m_sb_w_in, m_sb_q_norm, m_sb_k_norm, m_sb_w_out, m_gdn_w_in, m_gdn_conv_w, m_gdn_a_log, m_gdn_dt_bias, m_gdn_o_norm, m_gdn_w_out, m_dsw_w_in, m_dsw_q_norm, m_dsw_k_norm, m_dsw_w_out, m_lru_w_in, m_lru_conv_w, m_lru_conv_b, m_lru_w_a, m_lru_b_a, m_lru_w_x, m_lru_b_x, m_lru_lambda, m_lru_w_out, v_mix_norm, v_ffn_norm, v_ffn_w_gu, v_ffn_w_down, v_sb_w_in, v_sb_q_norm, v_sb_k_norm, v_sb_w_out, v_gdn_w_in, v_gdn_conv_w, v_gdn_a_log, v_gdn_dt_bias, v_gdn_o_norm, v_gdn_w_out, v_dsw_w_in, v_dsw_q_norm, v_dsw_k_norm, v_dsw_w_out, v_lru_w_in, v_lru_conv_w, v_lru_conv_b, v_lru_w_a, v_lru_b_a, v_lru_w_x, v_lru_b_x, v_lru_lambda, v_lru_w_out):
    args = locals()
    W = {n: args[n] for n in NAMES}
    M = {n: args["m_" + n] for n in NAMES}
    V = {n: args["v_" + n] for n in NAMES}
    T = x.shape[1]
    x2 = x[0]
    tgt = loss_target[0]

    S = {}
    for n in IN_T:
        S[n] = W[n][0].T.astype(BF16)
    for n in OUT_N:
        S[n] = W[n][0].astype(BF16)
    for i in range(DEPTH):
        S[f"ffn_gu{i}"] = ffn_w_gu[i].T.astype(BF16)
        S[f"ffn_down{i}"] = ffn_w_down[i].astype(BF16)
    S["lru_gates"] = jnp.concatenate(
        [lru_w_a[0].reshape(-1, LRU_BLOCK_DIM), lru_w_x[0].reshape(-1, LRU_BLOCK_DIM)], axis=0).astype(BF16)
    Gt = {}

    def flat(key):
        return Gt[key].reshape(-1, Gt[key].shape[-1])

    Gt["sb_w_in"] = all_gather(S["sb_w_in"], "ag_sb_w_in")
    Gt["sb_w_out"] = all_gather(S["sb_w_out"], "ag_sb_w_out")
    small_shapes = [W[n].shape for n in SMALL_SHARDED]
    sm = all_gather(_pack([W[n] for n in SMALL_SHARDED]), "ag_small")
    sm = [jnp.stack(parts) for parts in zip(*[_unpack(sm[p], small_shapes) for p in range(N_DEV)])]
    smd = dict(zip(SMALL_SHARDED, sm))

    def mixer_weights(i):
        kind = i % 4
        w = {"mix_norm": mix_norm[i:i + 1], "ffn_norm": ffn_norm[i:i + 1]}
        if kind == 0:
            w.update(sb_in_t=flat("sb_w_in"), sb_out=flat("sb_w_out"), sb_q_norm=sb_q_norm, sb_k_norm=sb_k_norm)
        elif kind == 1:
            w.update(gdn_in_t=flat("gdn_w_in"), gdn_out=flat("gdn_w_out"), gdn_a_log=gdn_a_log,
                     gdn_dt_bias=gdn_dt_bias, gdn_o_norm=gdn_o_norm,
                     gdn_conv_w=smd["gdn_conv_w"][:, 0].transpose(1, 0, 2).reshape(4, -1))
        elif kind == 2:
            w.update(dsw_in_t=flat("dsw_w_in"), dsw_out=flat("dsw_w_out"), dsw_q_norm=dsw_q_norm,
                     dsw_k_norm=dsw_k_norm)
        else:
            gg = Gt["lru_gates"].reshape(N_DEV, 2, LRU_BLOCKS, 32, LRU_BLOCK_DIM)
            gg = gg.transpose(1, 2, 0, 3, 4).reshape(2, LRU_BLOCKS, LRU_BLOCK_DIM, LRU_BLOCK_DIM).astype(F32)
            w.update(lru_in_t=flat("lru_w_in"), lru_out=flat("lru_w_out"), lru_w_a=gg[0], lru_w_x=gg[1],
                     lru_conv_w=smd["lru_conv_w"][:, 0].transpose(1, 0, 2).reshape(4, -1),
                     lru_conv_b=smd["lru_conv_b"][:, 0].reshape(1, -1),
                     lru_lambda=smd["lru_lambda"][:, 0].reshape(1, -1),
                     lru_b_a=smd["lru_b_a"][:, 0].transpose(1, 0, 2).reshape(1, -1),
                     lru_b_x=smd["lru_b_x"][:, 0].transpose(1, 0, 2).reshape(1, -1))
        return w

    carried_by_mixer = {0: ["ffn_gu0", "ffn_down0", "gdn_w_in", "gdn_w_out", "ffn_gu1", "ffn_down1"],
                        1: ["ffn_gu2", "ffn_down2", "lru_w_in", "lru_gates", "ffn_gu3", "ffn_down3"]}
    carried_by_ffn = {0: (["dsw_w_in"], ["dsw_w_out", "lru_w_out"])}

    half = ROPE_DIM // 2
    inv_freq = ROPE_THETA ** (-jnp.arange(half, dtype=F32) / half)
    ang = positions[0].astype(F32)[:, None] * inv_freq
    cs, sn = jnp.cos(ang), jnp.sin(ang)
    cosm = jnp.concatenate([cs, cs, jnp.ones((T, HEAD_DIM - ROPE_DIM), F32)], axis=1)
    sinm = jnp.concatenate([-sn, sn, jnp.zeros((T, HEAD_DIM - ROPE_DIM), F32)], axis=1)

    def gather_jobs(keys):
        return [GatherJob(S[k]) for k in keys]

    xs, f, seg_vjps, ffn_res = x2, None, [], []
    for i in range(DEPTH):
        keys = carried_by_mixer.get(i, [])
        shards = tuple(S[k] for k in keys)
        if i == 0:
            w0 = mixer_weights(0)
            qkv0, vjp_in = jax.vjp(sb_segment_in, xs, {k: w0[k] for k in ("mix_norm", "sb_in_t", "sb_q_norm", "sb_k_norm")})
            o0, tot0, gathered = _sb_fwd_call(*qkv0, shards)
            (xs, h), vjp_i = jax.vjp(sb_segment_out, xs, o0, {k: w0[k] for k in ("sb_out", "ffn_norm")})
        else:
            (xs, h), vjp_i, gathered = jax.vjp(
                lambda xx, ff, ww, i=i, shards=shards: mixer_segment(i, xx, ff, cosm, sinm, ww, shards),
                xs, f, mixer_weights(i), has_aux=True)
        Gt.update(zip(keys, gathered))
        seg_vjps.append(vjp_i)
        keys_gu, keys_down = carried_by_ffn.get(i, ([], []))
        gu, got = mm_carry(h, flat(f"ffn_gu{i}"), "nt", F32, gather_jobs(keys_gu))
        Gt.update(zip(keys_gu, got))
        act, sw_vjp = jax.vjp(swiglu_act, gu)
        f, got = mm_carry(act, flat(f"ffn_down{i}"), "nn", F32, gather_jobs(keys_down))
        Gt.update(zip(keys_down, got))
        ffn_res.append((h, act, sw_vjp))
    loss_part, dy = loss_head(xs, f, tgt)
    loss = lax.psum(loss_part, ("x", "y", "c"))

    reduced, gw_small, gw_rep = {}, {}, {}

    def to_partials(i, dw, dwd, dwgu):
        out = {}
        if dwgu is not None:
            out = {f"ffn_gu{i}": dwgu.reshape(N_DEV, -1, D_MODEL), f"ffn_down{i}": dwd.reshape(N_DEV, -1, D_MODEL)}
        for k, g in dw.items():
            if k.endswith("_in_t"):
                out[k.replace("_in_t", "_w_in")] = g.reshape(N_DEV, -1, D_MODEL)
            elif k.endswith("_out"):
                out[k.replace("_out", "_w_out")] = g.reshape(N_DEV, -1, D_MODEL)
            elif k in ("mix_norm", "ffn_norm"):
                gw_rep[(k, i)] = g
            elif k in REPLICATED:
                gw_rep[k] = g
            elif k not in ("lru_w_a", "lru_w_x"):
                gw_small[k] = g
        if "lru_w_a" in dw:
            gg = jnp.stack([dw["lru_w_a"], dw["lru_w_x"]]).reshape(2, LRU_BLOCKS, N_DEV, 32, LRU_BLOCK_DIM)
            out["lru_gates"] = gg.transpose(2, 0, 1, 3, 4).reshape(N_DEV, 2 * LRU_BLOCKS * 32, LRU_BLOCK_DIM)
        return out

    def spread(keys, sizes, capacities):
        room, bins = list(capacities), [[] for _ in capacities]
        for k in sorted(keys, key=lambda k: -sizes[k]):
            b = max(range(len(room)), key=lambda j: room[j])
            bins[b].append(k)
            room[b] -= sizes[k]
        return bins

    ready, dx, df = {}, dy, dy
    for i in reversed(range(DEPTH)):
        h, act, sw_vjp = ffn_res[i]
        keys = list(ready)
        g4 = {k: ready[k].reshape(N_CHIPS, 2, *ready[k].shape[1:]) for k in keys}
        dact, got = mm_carry(df, flat(f"ffn_down{i}"), "nt", F32, [PairJob(g4[k]) for k in keys])
        pairs = {k: pair_add(g4[k], g, f"rs_pair_add_{k}") for k, g in zip(keys, got)}
        sizes = {k: math.prod(pairs[k].shape[1:]) * pairs[k].dtype.itemsize for k in keys}
        bins = spread(keys, sizes, [c * D_MODEL * 2 for c in (800, 1300, 1550)])
        dwd, p0 = mm_carry(act, df, "tn", BF16, [ChipsJob(pairs[k]) for k in bins[0]])
        (dgu,) = sw_vjp(dact)
        dh, p1 = mm_carry(dgu, flat(f"ffn_gu{i}"), "nn", F32, [ChipsJob(pairs[k]) for k in bins[1]])
        dwgu, p2 = mm_carry(dgu, h, "tn", BF16, [ChipsJob(pairs[k]) for k in bins[2]])
        for k, parts in zip(bins[0] + bins[1] + bins[2], list(p0) + list(p1) + list(p2)):
            reduced[k] = sum_slots(parts, f"rs_sum_{k}")
        if i > 0:
            dx, df, dw = seg_vjps[i]((dx, dh))
            ready = to_partials(i, dw, dwd, dwgu)
    dx_out, do0, dw = seg_vjps[0]((dx, dh))
    ready = to_partials(0, dw, dwd, dwgu)
    keys = list(ready)
    g4 = {k: ready[k].reshape(N_CHIPS, 2, *ready[k].shape[1:]) for k in keys}
    pairs = {k: pair_add(g4[k], exchange_pair(g4[k], f"rs_pair_{k}"), f"rs_pair_add_{k}") for k in keys}
    dq0, dk0, dv0, parts = _sb_bwd_call(*qkv0, tot0, do0, [ChipsJob(pairs[k]) for k in keys])
    for k, p in zip(keys, parts):
        reduced[k] = sum_slots(p, f"rs_sum_{k}")
    dx_in, dw = vjp_in((dq0, dk0, dv0))
    gx = dx_in + dx_out
    for k, g in to_partials(0, dw, None, None).items():
        reduced[k] = reduce_scatter(g, k)

    G = {}
    for n in IN_T:
        G[n] = reduced[n].T[None]
    for n in OUT_N:
        G[n] = reduced[n][None]
    G["ffn_w_gu"] = jnp.stack([reduced[f"ffn_gu{i}"].T for i in range(DEPTH)])
    G["ffn_w_down"] = jnp.stack([reduced[f"ffn_down{i}"] for i in range(DEPTH)])
    gg = reduced["lru_gates"].reshape(2, 1, LRU_BLOCKS, 32, LRU_BLOCK_DIM)
    G["lru_w_a"], G["lru_w_x"] = gg[0], gg[1]
    gw = dict(gw_small)
    for n in REPLICATED:
        gw[n] = (jnp.concatenate([gw_rep[(n, i)] for i in range(DEPTH)], axis=0) if n in ("mix_norm", "ffn_norm")
                 else gw_rep[n])
    gs = {
        "gdn_conv_w": gw["gdn_conv_w"].reshape(4, N_DEV, -1).transpose(1, 0, 2)[:, None],
        "lru_conv_w": gw["lru_conv_w"].reshape(4, N_DEV, -1).transpose(1, 0, 2)[:, None],
        "lru_conv_b": gw["lru_conv_b"].reshape(N_DEV, 1, -1),
        "lru_lambda": gw["lru_lambda"].reshape(N_DEV, 1, -1),
        "lru_b_a": gw["lru_b_a"].reshape(LRU_BLOCKS, N_DEV, 32).transpose(1, 0, 2)[:, None],
        "lru_b_x": gw["lru_b_x"].reshape(LRU_BLOCKS, N_DEV, 32).transpose(1, 0, 2)[:, None],
    }
    packed = jnp.stack([_pack([gs[n][p] for n in SMALL_SHARDED]) for p in range(N_DEV)])
    for n, g in zip(SMALL_SHARDED, _unpack(reduce_scatter(packed, "small"), small_shapes)):
        G[n] = g
    rep_shapes = [W[n].shape for n in REPLICATED]
    for n, g in zip(REPLICATED, _unpack(all_reduce(_pack([gw[n] for n in REPLICATED]), "rep"), rep_shapes)):
        G[n] = g

    D, NM, NV = {}, {}, {}
    big = [n for n in NAMES if n not in REPLICATED and n not in SMALL_SHARDED]
    for n in big:
        shp = W[n].shape
        two = (-1, shp[-1])
        d, nm, nv = adamw(W[n].reshape(two), G[n].reshape(two), M[n].reshape(two), V[n].reshape(two), f"adamw_{n}")
        D[n], NM[n], NV[n] = d.reshape(shp), nm.reshape(shp), nv.reshape(shp)
    for group, tag in ((SMALL_SHARDED, "small"), (REPLICATED, "rep")):
        shapes = [W[n].shape for n in group]
        res = adamw(_pack([W[n] for n in group]), _pack([G[n] for n in group]), _pack([M[n] for n in group]),
                    _pack([V[n] for n in group]), f"adamw_{tag}")
        for dst, buf in zip((D, NM, NV), res):
            for n, a in zip(group, _unpack(buf, shapes)):
                dst[n] = a

    return (loss, gx[None], *[G[n] for n in NAMES], *[D[n] for n in NAMES], *[NM[n] for n in NAMES],
            *[NV[n] for n in NAMES])
```

```python
import functools
import math

import jax
import jax.numpy as jnp
from jax import lax
from jax.experimental import pallas as pl
from jax.experimental.pallas import tpu as pltpu

F32 = jnp.float32
BF16 = jnp.bfloat16

D_MODEL = 2048
HEAD_DIM = 128
NORM_EPS = 1e-6
SB_HEADS = 16
SB_BLOCK = 256
SB_WIDE = (4, 2, 1)
GDN_K_HEADS = 16
GDN_V_HEADS = 32
GDN_KEY_DIM = 2048
GDN_VAL_DIM = 4096
GDN_CHUNK = 64
GDN_HEADS_PER_STEP = 4
GDN_GROUPS_PER_STEP = 1
DSW_GROUPS = ((128, 1), (512, 4), (2048, 16))
DSW_HG = 6
DSW_HEADS = 18
DSW_BLOCK = 128
ROPE_DIM = 32
ROPE_THETA = 500000.0
LRU_WIDTH = 2048
LRU_BLOCKS = 8
LRU_BLOCK_DIM = 256
LRU_C = 8.0
FFN_HIDDEN = 5632
DEPTH = 4
ADAM_LR, ADAM_B1, ADAM_B2, ADAM_EPS, ADAM_WD, ADAM_STEP = 0.001, 0.9, 0.999, 1e-08, 0.01, 10
N_DEV = 8
N_CHIPS = 4

V7X_VMEM_LIMIT = 56 * 1024 * 1024
LANES = 128
MESH = pl.DeviceIdType.MESH


def _params(**kw):
    return pltpu.CompilerParams(vmem_limit_bytes=V7X_VMEM_LIMIT, **kw)


def _pick(n, cands):
    for c in cands:
        if n % c == 0:
            return c
    return n


def _tile(n, cap, unit=LANES):
    best = None
    for t in range(unit, min(n, cap) + 1, unit):
        if n % t == 0:
            best = t
    return best or n


_DN = {"nn": (((1,), (0,)), ((), ())), "nt": (((1,), (1,)), ((), ())), "tn": (((0,), (0,)), ((), ()))}


def _raw_dot(a, b, mode):
    return lax.dot_general(a.astype(BF16), b.astype(BF16), _DN[mode], preferred_element_type=F32)


@functools.partial(jax.custom_vjp, nondiff_argnums=(2,))
def bdot(a, b, mode):
    return _raw_dot(a, b, mode)


def _bdot_fwd(a, b, mode):
    return _raw_dot(a, b, mode), (a, b)


def _bdot_bwd(mode, res, ct):
    a, b = res
    if mode == "nn":
        return bdot(ct, b, "nt"), bdot(a, ct, "tn")
    if mode == "nt":
        return bdot(ct, b, "nn"), bdot(ct, a, "tn")
    return bdot(b, ct, "nt"), bdot(a, ct, "nn")


bdot.defvjp(_bdot_fwd, _bdot_bwd)


def _split2(x):
    hi = x.astype(BF16)
    lo = (x - hi.astype(F32)).astype(BF16)
    return hi, lo


def _dot01_raw(x, m, dn, left):
    hi, lo = _split2(x)
    if left:
        return (lax.dot_general(m, hi, dn, preferred_element_type=F32)
                + lax.dot_general(m, lo, dn, preferred_element_type=F32))
    return (lax.dot_general(hi, m, dn, preferred_element_type=F32)
            + lax.dot_general(lo, m, dn, preferred_element_type=F32))


@functools.partial(jax.custom_vjp, nondiff_argnums=(2,))
def dot01(x, m, left):
    return _dot01_raw(x, m, _DN["nn"], left)


def _dot01_fwd(x, m, left):
    return _dot01_raw(x, m, _DN["nn"], left), m


def _dot01_bwd(left, m, ct):
    dx = _dot01_raw(ct, m, _DN["tn"] if left else _DN["nt"], left)
    return dx, jnp.zeros_like(m)


dot01.defvjp(_dot01_fwd, _dot01_bwd)


def _mm3(a, b):
    ah, al = _split2(a)
    bh, bl = _split2(b)
    dn = _DN["nn"]
    return (lax.dot_general(ah, bh, dn, preferred_element_type=F32)
            + lax.dot_general(ah, bl, dn, preferred_element_type=F32)
            + lax.dot_general(al, bh, dn, preferred_element_type=F32))


@jax.custom_vjp
def mm3(a, b):
    return _mm3(a, b)


def _mm3_fwd(a, b):
    return _mm3(a, b), (a, b)


def _mm3_bwd(res, ct):
    a, b = res
    return _mm3(ct, b.T), _mm3(a.T, ct)


mm3.defvjp(_mm3_fwd, _mm3_bwd)


def _softplus_parts(z):
    e = jnp.exp(-jnp.abs(z))
    l = jnp.log(1.0 + e)
    sp = jnp.maximum(z, 0.0) + l
    ls = jnp.minimum(z, 0.0) - l
    inv = 1.0 / (1.0 + e)
    sig = jnp.where(z >= 0.0, inv, e * inv)
    return sp, ls, sig


def _rms(x, g):
    return x * lax.rsqrt(jnp.mean(x * x, axis=-1, keepdims=True) + NORM_EPS) * g


def _swap16(x):
    lane = lax.broadcasted_iota(jnp.int32, x.shape, 1)
    return jnp.where(lane < 16, pltpu.roll(x, 112, axis=1), jnp.where(lane < 32, pltpu.roll(x, 16, axis=1), 0.0))


@jax.custom_vjp
def rope(x, cosm, sinm):
    return x * cosm + _swap16(x) * sinm


def _rope_fwd(x, cosm, sinm):
    return rope(x, cosm, sinm), (cosm, sinm)


def _rope_bwd(res, ct):
    cosm, sinm = res
    return ct * cosm + _swap16(ct * sinm), jnp.zeros_like(cosm), jnp.zeros_like(sinm)


rope.defvjp(_rope_fwd, _rope_bwd)


def mm(a, b, mode, out_dtype=F32):
    return mm_carry(a, b, mode, out_dtype, ())[0]


def mm_carry(a, b, mode, out_dtype=F32, jobs=()):
    if mode == "nt":
        (M, K), N = a.shape, b.shape[0]
    elif mode == "nn":
        (M, K), N = a.shape, b.shape[1]
    else:
        (K, M), N = a.shape, b.shape[1]
    if mode == "nt":
        tm, tn, tk = _tile(M, 1024), _tile(N, 768), _tile(K, 2048)
    elif mode == "nn":
        tm, tn, tk = _tile(M, 1024), _tile(N, 2048), _tile(K, 1408)
    else:
        tm, tn = _tile(M, 1408), _tile(N, 2048)
        tk = _tile(K, 1024 if a.dtype == BF16 and b.dtype == BF16 else 512)
    nk = K // tk
    if mode == "nt":
        a_spec = pl.BlockSpec((tm, tk), lambda i, j, k: (i, k))
        b_spec = pl.BlockSpec((tn, tk), lambda i, j, k: (j, k))
    elif mode == "nn":
        a_spec = pl.BlockSpec((tm, tk), lambda i, j, k: (i, k))
        b_spec = pl.BlockSpec((tk, tn), lambda i, j, k: (k, j))
    else:
        a_spec = pl.BlockSpec((tk, tm), lambda i, j, k: (k, i))
        b_spec = pl.BlockSpec((tk, tn), lambda i, j, k: (k, j))

    def body(a_ref, b_ref, o_ref, *scr):
        p = _raw_dot(a_ref[...], b_ref[...], mode)
        if nk == 1:
            o_ref[...] = p.astype(o_ref.dtype)
        else:
            acc = scr[0]
            k = pl.program_id(2)

            @pl.when(k == 0)
            def _():
                acc[...] = p

            @pl.when(k > 0)
            def _():
                acc[...] += p

            @pl.when(k == nk - 1)
            def _():
                o_ref[...] = acc[...].astype(o_ref.dtype)

    grid = (M // tm, N // tn, nk)
    res = carried_call(
        body, jobs, grid, name=f"mm_{mode}_{M}x{N}x{K}" + ("_c" if jobs else ""),
        out_shape=[jax.ShapeDtypeStruct((M, N), out_dtype)], in_specs=[a_spec, b_spec],
        out_specs=[pl.BlockSpec((tm, tn), lambda i, j, k: (i, j))],
        scratch_shapes=[] if nk == 1 else [pltpu.VMEM((tm, tn), F32)], operands=(a, b))
    return res[0][0], res[1]


@jax.custom_vjp
def lin_t(x, wt):
    return mm(x, wt, "nt")


def _lin_t_fwd(x, wt):
    return mm(x, wt, "nt"), (x, wt)


def _lin_t_bwd(res, dy):
    x, wt = res
    return mm(dy, wt, "nn"), mm(dy, x, "tn", out_dtype=wt.dtype)


lin_t.defvjp(_lin_t_fwd, _lin_t_bwd)


@jax.custom_vjp
def lin_n(x, w):
    return mm(x, w, "nn")


def _lin_n_fwd(x, w):
    return mm(x, w, "nn"), (x, w)


def _lin_n_bwd(res, dy):
    x, w = res
    return mm(dy, w, "nt"), mm(x, dy, "tn", out_dtype=w.dtype)


lin_n.defvjp(_lin_n_fwd, _lin_n_bwd)


def _full_spec(shape):
    nd = len(shape)
    return pl.BlockSpec(tuple(shape), lambda i: (0,) * nd)


def _row_spec(tr, c):
    return pl.BlockSpec((tr, c), lambda i: (i, 0))


def _rowop_tr(total_cols, T):
    budget = 20 * 1024 * 1024
    for tr in (512, 256, 128, 64, 32, 16, 8):
        if T % tr == 0 and total_cols * tr * 4 * 2 <= budget:
            return tr
    return 8


def rowop(fn, name, rows, consts, params, out_cols):
    nr, nc, npar, nout = len(rows), len(consts), len(params), len(out_cols)
    T = rows[0].shape[0]
    in_cols = [r.shape[1] for r in rows] + [c.shape[1] for c in consts]
    tr_f = _rowop_tr(sum(in_cols) + sum(out_cols), T)
    tr_b = _rowop_tr(sum(in_cols) + sum(out_cols) + sum(r.shape[1] for r in rows), T)

    def fwd_call(rows, consts, params):
        def body(*refs):
            ins = [r[...] for r in refs[:nr + nc + npar]]
            outs = fn(*ins)
            for o_ref, o in zip(refs[nr + nc + npar:], outs):
                o_ref[...] = o.astype(F32)

        return pl.pallas_call(
            body, name=name + "_fwd", grid=(T // tr_f,),
            out_shape=[jax.ShapeDtypeStruct((T, c), F32) for c in out_cols],
            in_specs=[_row_spec(tr_f, c) for c in in_cols] + [_full_spec(p.shape) for p in params],
            out_specs=[_row_spec(tr_f, c) for c in out_cols],
            compiler_params=_params(dimension_semantics=("arbitrary",)),
        )(*rows, *consts, *params)

    def bwd_call(rows, consts, params, douts):
        def body(*refs):
            i = pl.program_id(0)
            rv = [r[...] for r in refs[:nr]]
            cv = [r[...] for r in refs[nr:nr + nc]]
            pv = [r[...] for r in refs[nr + nc:nr + nc + npar]]
            dv = [r[...] for r in refs[nr + nc + npar:nr + nc + npar + nout]]
            orefs = refs[nr + nc + npar + nout:]
            _, vjp = jax.vjp(lambda rr, pp: tuple(fn(*rr, *cv, *pp)), rv, pv)
            drows, dpars = vjp(tuple(dv))
            for o_ref, g in zip(orefs[:nr], drows):
                o_ref[...] = g.astype(F32)

            @pl.when(i == 0)
            def _():
                for o_ref in orefs[nr:]:
                    o_ref[...] = jnp.zeros(o_ref.shape, F32)

            for o_ref, g in zip(orefs[nr:], dpars):
                o_ref[...] += g.astype(F32)

        res = pl.pallas_call(
            body, name=name + "_bwd", grid=(T // tr_b,),
            out_shape=[jax.ShapeDtypeStruct(r.shape, F32) for r in rows]
            + [jax.ShapeDtypeStruct(p.shape, F32) for p in params],
            in_specs=[_row_spec(tr_b, c) for c in in_cols] + [_full_spec(p.shape) for p in params]
            + [_row_spec(tr_b, c) for c in out_cols],
            out_specs=[_row_spec(tr_b, r.shape[1]) for r in rows] + [_full_spec(p.shape) for p in params],
            compiler_params=_params(dimension_semantics=("arbitrary",)),
        )(*rows, *consts, *params, *douts)
        return tuple(res[:nr]), tuple(res[nr:])

    @jax.custom_vjp
    def op(rows, consts, params):
        return tuple(fwd_call(rows, consts, params))

    def op_fwd(rows, consts, params):
        return tuple(fwd_call(rows, consts, params)), (rows, consts, params)

    def op_bwd(res, douts):
        rows, consts, params = res
        drows, dpars = bwd_call(rows, consts, params, douts)
        return drows, tuple(jnp.zeros_like(c) for c in consts), dpars

    op.defvjp(op_fwd, op_bwd)
    return op(tuple(rows), tuple(consts), tuple(params))


def _fn_norm(x, g):
    return (_rms(x, g),)


def _fn_add_norm(x, y, g):
    s = x + y
    return s, _rms(s, g)


def _fn_swiglu(gu):
    return (jax.nn.silu(gu[:, :FFN_HIDDEN]) * gu[:, FFN_HIDDEN:],)


def _heads(x, n, width=HEAD_DIM):
    return [x[:, h * width:(h + 1) * width] for h in range(n)]


def _fn_sb_pre(qkv, qn, kn):
    hs = _heads(qkv, 3 * SB_HEADS)
    q = jnp.concatenate([_rms(h, qn) for h in hs[:SB_HEADS]], axis=1)
    k = jnp.concatenate([_rms(h, kn) for h in hs[SB_HEADS:2 * SB_HEADS]], axis=1)
    v = jnp.concatenate(hs[2 * SB_HEADS:], axis=1)
    return q, k, v


def _fn_dsw_pre(qkv, cosm, sinm, qn, kn):
    hs = _heads(qkv, 3 * DSW_HEADS)
    q = jnp.concatenate([rope(_rms(h, qn), cosm, sinm) for h in hs[:DSW_HEADS]], axis=1)
    k = jnp.concatenate([rope(_rms(h, kn), cosm, sinm) for h in hs[DSW_HEADS:2 * DSW_HEADS]], axis=1)
    v = jnp.concatenate(hs[2 * DSW_HEADS:], axis=1)
    return q, k, v


def _fn_dsw_combine(o, lse):
    os_, ls_ = _heads(o, DSW_HEADS), _heads(lse, DSW_HEADS)
    out = [None] * DSW_HEADS
    for hg in range(DSW_HG):
        l3 = [ls_[g * DSW_HG + hg] for g in range(3)]
        m = jnp.maximum(jnp.maximum(l3[0], l3[1]), l3[2])
        e3 = [jnp.exp(l - m) for l in l3]
        den = e3[0] + e3[1] + e3[2]
        for g in range(3):
            out[g * DSW_HG + hg] = os_[g * DSW_HG + hg] * (e3[g] / den)
    return (jnp.concatenate(out, axis=1),)


def _l2(x):
    return x * lax.rsqrt(jnp.sum(x * x, axis=-1, keepdims=True) + NORM_EPS)


def _fn_gdn_pre(qkv, ba, a_log, dt_bias):
    x = jax.nn.silu(qkv)
    hs = _heads(x, 2 * GDN_K_HEADS + GDN_V_HEADS)
    rep = GDN_V_HEADS // GDN_K_HEADS
    qh = [_l2(h) * HEAD_DIM ** -0.5 for h in hs[:GDN_K_HEADS]]
    kh = [_l2(h) for h in hs[GDN_K_HEADS:2 * GDN_K_HEADS]]
    q = jnp.concatenate([qh[h // rep] for h in range(GDN_V_HEADS)], axis=1)
    k = jnp.concatenate([kh[h // rep] for h in range(GDN_V_HEADS)], axis=1)
    v = jnp.concatenate(hs[2 * GDN_K_HEADS:], axis=1)
    b = ba[:, :GDN_V_HEADS]
    a = ba[:, GDN_V_HEADS:2 * GDN_V_HEADS]
    beta = jax.nn.sigmoid(b)
    g = -jnp.exp(a_log) * jax.nn.softplus(a + dt_bias)
    rows = b.shape[0]
    beta_b = jnp.concatenate([jnp.broadcast_to(beta[:, h:h + 1], (rows, HEAD_DIM)) for h in range(GDN_V_HEADS)], axis=1)
    g_b = jnp.concatenate([jnp.broadcast_to(g[:, h:h + 1], (rows, HEAD_DIM)) for h in range(GDN_V_HEADS)], axis=1)
    return q, k, v, g_b, beta_b


def _fn_gdn_post(o, z, o_norm):
    os_, zs = _heads(o, GDN_V_HEADS), _heads(z, GDN_V_HEADS)
    return (jnp.concatenate([_rms(oh, o_norm) * jax.nn.silu(zh) for oh, zh in zip(os_, zs)], axis=1),)


def _expm1(x):
    return jnp.tanh(0.5 * x) * (jnp.exp(x) + 1.0)


def _fn_lru_gates(xc, conv_b, w_a, b_a, w_x, b_x, lam):
    xr = xc + conv_b
    xs = _heads(xr, LRU_BLOCKS, LRU_BLOCK_DIM)
    r = jnp.concatenate([bdot(xs[n], w_a[n], "nn") for n in range(LRU_BLOCKS)], axis=1) + b_a
    i = jnp.concatenate([bdot(xs[n], w_x[n], "nn") for n in range(LRU_BLOCKS)], axis=1) + b_x
    r = jax.nn.sigmoid(r)
    i = jax.nn.sigmoid(i)
    log_a = -LRU_C * r * jax.nn.softplus(-lam)
    a = jnp.exp(log_a)
    u = jnp.sqrt(-_expm1(2.0 * log_a)) * (i * xr)
    return a, u


def _fn_lru_out(hs, gate):
    c = math.sqrt(2.0 / math.pi)
    gl = 0.5 * gate * (1.0 + jnp.tanh(c * (gate + 0.044715 * (gate * gate * gate))))
    return (hs * gl,)


def loss_head(x, f, target):
    T, D = x.shape
    tr = _pick(T, (256, 128, 64, 32, 16, 8))

    def body(x_ref, f_ref, t_ref, l_ref, dy_ref):
        i = pl.program_id(0)
        err = (x_ref[...] + f_ref[...]) - t_ref[...]
        dy_ref[...] = err * (1.0 / D)
        part = 0.5 * jnp.sum(jnp.mean(err * err, axis=-1, keepdims=True), axis=0, keepdims=True)

        @pl.when(i == 0)
        def _():
            l_ref[...] = jnp.zeros(l_ref.shape, F32)

        l_ref[...] += jnp.broadcast_to(part, l_ref.shape)

    l, dy = pl.pallas_call(
        body, name="loss_head", grid=(T // tr,),
        out_shape=[jax.ShapeDtypeStruct((8, LANES), F32), jax.ShapeDtypeStruct((T, D), F32)],
        in_specs=[_row_spec(tr, D)] * 3, out_specs=[_full_spec((8, LANES)), _row_spec(tr, D)],
        compiler_params=_params(dimension_semantics=("arbitrary",)),
    )(x, f, target)
    return l[0, 0], dy


def _as_bf16(x):
    return x.astype(BF16).astype(F32)


def _shift_rows(x, s):
    if s == 0:
        return x
    n = x.shape[0]
    row = lax.broadcasted_iota(jnp.int32, x.shape, 0)
    rolled = pltpu.roll(x, s % n, axis=0)
    keep = (row >= s) if s > 0 else (row < n + s)
    return jnp.where(keep, rolled, 0.0)


def _conv_fwd_call(x, w):
    T, C = x.shape
    K = w.shape[0]
    cb = _pick(C, (256, 128))

    def body(x_ref, w_ref, y_ref):
        xv, wv = _as_bf16(x_ref[...]), _as_bf16(w_ref[...])
        acc = xv * wv[K - 1:K, :]
        for k in range(K - 1):
            acc = acc + _shift_rows(xv, K - 1 - k) * wv[k:k + 1, :]
        y_ref[...] = acc

    return pl.pallas_call(
        body, name=f"conv_fwd_{C}", grid=(C // cb,), out_shape=jax.ShapeDtypeStruct((T, C), F32),
        in_specs=[pl.BlockSpec((T, cb), lambda j: (0, j)), pl.BlockSpec((K, cb), lambda j: (0, j))],
        out_specs=pl.BlockSpec((T, cb), lambda j: (0, j)),
        compiler_params=_params(dimension_semantics=("arbitrary",)),
    )(x, w)


def _conv_bwd_call(x, w, dy):
    T, C = x.shape
    K = w.shape[0]
    cb = _pick(C, (256, 128))

    def body(x_ref, w_ref, dy_ref, dx_ref, dw_ref):
        xv, dv, wv = _as_bf16(x_ref[...]), _as_bf16(dy_ref[...]), _as_bf16(w_ref[...])
        acc = dv * wv[K - 1:K, :]
        rows = [None] * K
        rows[K - 1] = jnp.sum(dv * xv, axis=0, keepdims=True)
        for k in range(K - 1):
            s = K - 1 - k
            acc = acc + _shift_rows(dv, -s) * wv[k:k + 1, :]
            rows[k] = jnp.sum(dv * _shift_rows(xv, s), axis=0, keepdims=True)
        dx_ref[...] = acc
        dw_ref[...] = jnp.concatenate(rows + [jnp.zeros((8 - K, cb), F32)], axis=0)

    dx, dw = pl.pallas_call(
        body, name=f"conv_bwd_{C}", grid=(C // cb,),
        out_shape=[jax.ShapeDtypeStruct((T, C), F32), jax.ShapeDtypeStruct((8, C), F32)],
        in_specs=[pl.BlockSpec((T, cb), lambda j: (0, j)), pl.BlockSpec((K, cb), lambda j: (0, j)),
                  pl.BlockSpec((T, cb), lambda j: (0, j))],
        out_specs=[pl.BlockSpec((T, cb), lambda j: (0, j)), pl.BlockSpec((8, cb), lambda j: (0, j))],
        compiler_params=_params(dimension_semantics=("arbitrary",)),
    )(x, w, dy)
    return dx, dw[:K]


@jax.custom_vjp
def dwconv(x, w):
    return _conv_fwd_call(x, w)


def _dwconv_fwd(x, w):
    return _conv_fwd_call(x, w), (x, w)


def _dwconv_bwd(res, dy):
    return _conv_bwd_call(*res, dy)


dwconv.defvjp(_dwconv_fwd, _dwconv_bwd)


def _scan_fwd_call(a, u):
    T, C = a.shape
    cb = _pick(C, (256, 128))

    def body(a_ref, u_ref, h_ref):
        def step(i, h):
            r = pl.multiple_of(i * 8, 8)
            at, ut = a_ref[pl.ds(r, 8), :], u_ref[pl.ds(r, 8), :]
            rows = []
            for j in range(8):
                h = at[j:j + 1, :] * h + ut[j:j + 1, :]
                rows.append(h)
            h_ref[pl.ds(r, 8), :] = jnp.concatenate(rows, axis=0)
            return h

        lax.fori_loop(0, T // 8, step, jnp.zeros((1, cb), F32))

    return pl.pallas_call(
        body, name="lru_scan_fwd", grid=(C // cb,), out_shape=jax.ShapeDtypeStruct((T, C), F32),
        in_specs=[pl.BlockSpec((T, cb), lambda j: (0, j))] * 2, out_specs=pl.BlockSpec((T, cb), lambda j: (0, j)),
        compiler_params=_params(dimension_semantics=("arbitrary",)),
    )(a, u)


def _scan_bwd_call(a, hs, dh):
    T, C = a.shape
    cb = _pick(C, (256, 128))
    nt = T // 8

    def body(a_ref, h_ref, dh_ref, da_ref, du_ref):
        def step(s, carry):
            i = nt - 1 - s
            r = pl.multiple_of(i * 8, 8)
            rp = pl.multiple_of(jnp.maximum(i - 1, 0) * 8, 8)
            at, ht, dt = a_ref[pl.ds(r, 8), :], h_ref[pl.ds(r, 8), :], dh_ref[pl.ds(r, 8), :]
            hprev_tile = h_ref[pl.ds(rp, 8), :]
            h_before = jnp.where(i > 0, hprev_tile[7:8, :], 0.0)
            da_rows, du_rows = [None] * 8, [None] * 8
            for j in range(7, -1, -1):
                lam = dt[j:j + 1, :] + carry
                du_rows[j] = lam
                hp = ht[j - 1:j, :] if j > 0 else h_before
                da_rows[j] = lam * hp
                carry = at[j:j + 1, :] * lam
            da_ref[pl.ds(r, 8), :] = jnp.concatenate(da_rows, axis=0)
            du_ref[pl.ds(r, 8), :] = jnp.concatenate(du_rows, axis=0)
            return carry

        lax.fori_loop(0, nt, step, jnp.zeros((1, cb), F32))

    return pl.pallas_call(
        body, name="lru_scan_bwd", grid=(C // cb,), out_shape=[jax.ShapeDtypeStruct((T, C), F32)] * 2,
        in_specs=[pl.BlockSpec((T, cb), lambda j: (0, j))] * 3,
        out_specs=[pl.BlockSpec((T, cb), lambda j: (0, j))] * 2,
        compiler_params=_params(dimension_semantics=("arbitrary",)),
    )(a, hs, dh)


@jax.custom_vjp
def lru_scan(a, u):
    return _scan_fwd_call(a, u)


def _lru_scan_fwd(a, u):
    hs = _scan_fwd_call(a, u)
    return hs, (a, hs)


def _lru_scan_bwd(res, dh):
    a, hs = res
    return tuple(_scan_bwd_call(a, hs, dh))


lru_scan.defvjp(_lru_scan_fwd, _lru_scan_bwd)


def _tri(n, kind):
    r = lax.broadcasted_iota(jnp.int32, (n, n), 0)
    c = lax.broadcasted_iota(jnp.int32, (n, n), 1)
    m = {"gt": r > c, "le": r <= c, "lt": r < c, "ge": r >= c, "eq": r == c}[kind]
    return jnp.where(m, 1.0, 0.0).astype(BF16)


def _sb_fwd_call(q, k, v, shards=()):
    T, HD = q.shape
    H = HD // HEAD_DIM
    tb = _pick(T, (SB_BLOCK, 128))
    scale = HEAD_DIM ** -0.5

    def body(q_ref, k_ref, v_ref, o_ref, tot_ref):
        i = pl.program_id(1)
        qb = q_ref[...].astype(BF16)
        u_gt = _tri(tb, "gt")
        row = lax.broadcasted_iota(jnp.int32, (tb, tb), 0)
        col = lax.broadcasted_iota(jnp.int32, (tb, tb), 1)

        def step(j, n, carry, diagonal):
            acc, run = carry
            off = pl.multiple_of(j * tb, tb)
            kb = k_ref[pl.ds(off, n * tb), :].astype(BF16)
            vb = v_ref[pl.ds(off, n * tb), :].astype(BF16)
            z = lax.dot_general(qb, kb, _DN["nt"], preferred_element_type=F32) * scale
            sp, ls, _ = _softplus_parts(z)
            if diagonal:
                sp = jnp.where(col < row, sp, 0.0)
            parts = [None] * n
            for s in reversed(range(n)):
                xs = sp[:, s * tb:(s + 1) * tb]
                parts[s] = _dot01_raw(xs, u_gt, _DN["nn"], False) + run
                run = run + jnp.sum(xs, axis=1, keepdims=True)
            between = parts[0] if n == 1 else jnp.concatenate(parts, axis=1)
            w = jnp.exp(ls - between)
            if diagonal:
                w = jnp.where(col < row, w, 0.0)
            acc = acc + lax.dot_general(w.astype(BF16), vb, _DN["nn"], preferred_element_type=F32)
            return acc, run

        carry = step(i, 1, (jnp.zeros((tb, HEAD_DIM), F32), jnp.zeros((tb, 1), F32)), True)
        left = i
        for n in SB_WIDE:
            carry = lax.fori_loop(0, lax.div(left, n), lambda t, c, n=n, left=left: step(left - n * (t + 1), n, c, False),
                                  carry)
            left = lax.rem(left, n)
        acc, run = carry
        o_ref[...] = acc
        tot_ref[...] = jnp.broadcast_to(run, (tb, HEAD_DIM))

    blk = pl.BlockSpec((tb, HEAD_DIM), lambda h, i: (i, h))
    full = pl.BlockSpec((T, HEAD_DIM), lambda h, i: (0, h))
    (o, tot), gathered = carried_call(
        body, [GatherJob(s) for s in shards], (H, T // tb), name="sb_attn_fwd",
        out_shape=[jax.ShapeDtypeStruct((T, HD), F32)] * 2, in_specs=[blk, full, full], out_specs=[blk, blk],
        scratch_shapes=[], operands=(q, k, v))
    return o, tot, tuple(gathered)


def _sb_bwd_call(q, k, v, tot, do, jobs=()):
    T, HD = q.shape
    H = HD // HEAD_DIM
    tb = _pick(T, (SB_BLOCK, 128))
    scale = HEAD_DIM ** -0.5

    def body(q_ref, k_ref, v_ref, tot_ref, do_ref, dq_ref, dk_ref, dv_ref):
        i = pl.program_id(1)

        @pl.when(i == 0)
        def _():
            dk_ref[...] = jnp.zeros(dk_ref.shape, F32)
            dv_ref[...] = jnp.zeros(dv_ref.shape, F32)

        qb = q_ref[...].astype(BF16)
        dob = do_ref[...].astype(BF16)
        tot = tot_ref[:, 0:1]
        u_le = _tri(tb, "le")
        u_lt = _tri(tb, "lt")
        row = lax.broadcasted_iota(jnp.int32, (tb, tb), 0)
        col = lax.broadcasted_iota(jnp.int32, (tb, tb), 1)

        def prefix_sums(x, u, start):
            out = []
            for s in range(x.shape[1] // tb):
                xs = x[:, s * tb:(s + 1) * tb]
                out.append(_dot01_raw(xs, u, _DN["nn"], False) + start)
                start = start + jnp.sum(xs, axis=1, keepdims=True)
            return (out[0] if len(out) == 1 else jnp.concatenate(out, axis=1)), start

        def step(j, n, carry, diagonal):
            dq, cs, cd = carry
            off = pl.multiple_of(j * tb, tb)
            kb = k_ref[pl.ds(off, n * tb), :].astype(BF16)
            vb = v_ref[pl.ds(off, n * tb), :].astype(BF16)
            z = lax.dot_general(qb, kb, _DN["nt"], preferred_element_type=F32) * scale
            sp, ls, sig = _softplus_parts(z)
            if diagonal:
                sp = jnp.where(col < row, sp, 0.0)
            prefix, cs = prefix_sums(sp, u_le, cs)
            w = jnp.exp(ls - (tot - prefix))
            if diagonal:
                w = jnp.where(col < row, w, 0.0)
            wb = w.astype(BF16)
            dv_ref[pl.ds(off, n * tb), :] += lax.dot_general(wb, dob, _DN["tn"], preferred_element_type=F32)
            dw = lax.dot_general(dob, vb, _DN["nt"], preferred_element_type=F32)
            dl = dw * w
            before, cd = prefix_sums(dl, u_lt, cd)
            dz = (dl * (1.0 - sig) - before * sig) * scale
            if diagonal:
                dz = jnp.where(col < row, dz, 0.0)
            dzb = dz.astype(BF16)
            dq = dq + lax.dot_general(dzb, kb, _DN["nn"], preferred_element_type=F32)
            dk_ref[pl.ds(off, n * tb), :] += lax.dot_general(dzb, qb, _DN["tn"], preferred_element_type=F32)
            return dq, cs, cd

        z1 = jnp.zeros((tb, 1), F32)
        carry, done = (jnp.zeros((tb, HEAD_DIM), F32), z1, z1), 0
        for n in SB_WIDE:
            trips = lax.div(i - done, n)
            carry = lax.fori_loop(0, trips, lambda t, c, n=n, done=done: step(done + n * t, n, c, False), carry)
            done = done + trips * n
        dq, _, _ = step(i, 1, carry, True)
        dq_ref[...] = dq

    blk = pl.BlockSpec((tb, HEAD_DIM), lambda h, i: (i, h))
    full = pl.BlockSpec((T, HEAD_DIM), lambda h, i: (0, h))
    grads, exchanged = carried_call(
        body, jobs, (H, T // tb), name="sb_attn_bwd", out_shape=[jax.ShapeDtypeStruct((T, HD), F32)] * 3,
        in_specs=[blk, full, full, blk, blk], out_specs=[blk, full, full], scratch_shapes=[],
        operands=(q, k, v, tot, do))
    return (*grads, exchanged) if jobs else tuple(grads)


@jax.custom_vjp
def sb_attn(q, k, v, shards):
    o, _, gathered = _sb_fwd_call(q, k, v, shards)
    return o, gathered


def _sb_attn_fwd(q, k, v, shards):
    o, tot, gathered = _sb_fwd_call(q, k, v, shards)
    return (o, gathered), (q, k, v, tot, shards)


def _sb_attn_bwd(res, cts):
    q, k, v, tot, shards = res
    return (*_sb_bwd_call(q, k, v, tot, cts[0]), tuple(jnp.zeros_like(s) for s in shards))


sb_attn.defvjp(_sb_attn_fwd, _sb_attn_bwd)


def _dsw_tile(q, kp, kc, vp, vc, n):
    blk = DSW_BLOCK
    scale = HEAD_DIM ** -0.5
    qi = lax.broadcasted_iota(jnp.int32, (blk, blk), 0)
    kj = lax.broadcasted_iota(jnp.int32, (blk, blk), 1)
    neg = -1e30
    s_p = jnp.where((kj >= qi) & (n > 0), bdot(q, kp, "nt") * scale, neg)
    s_c = jnp.where(kj <= qi, bdot(q, kc, "nt") * scale, neg)
    m = jnp.maximum(jnp.max(s_p, axis=-1, keepdims=True), jnp.max(s_c, axis=-1, keepdims=True))
    p_p, p_c = jnp.exp(s_p - m), jnp.exp(s_c - m)
    den = jnp.sum(p_p, axis=-1, keepdims=True) + jnp.sum(p_c, axis=-1, keepdims=True)
    o = bdot(p_p / den, vp, "nn") + bdot(p_c / den, vc, "nn")
    lse = m + jnp.log(den)
    return o, jnp.broadcast_to(lse, (blk, HEAD_DIM))


def _dsw_specs(nsub):
    cur = pl.BlockSpec((None, DSW_BLOCK, HEAD_DIM), lambda r, h, n: (r, n, h))
    prev = pl.BlockSpec((None, DSW_BLOCK, HEAD_DIM), lambda r, h, n: (r, jnp.maximum(n - 1, 0), h))
    whole = pl.BlockSpec((None, nsub, HEAD_DIM), lambda r, h, n: (r, 0, h))
    return cur, prev, whole


def _dsw_fwd_call(q, k, v):
    d, nsub, HD = q.shape
    cur, prev, _ = _dsw_specs(nsub)

    def body(q_ref, kp_ref, kc_ref, vp_ref, vc_ref, o_ref, l_ref):
        n = pl.program_id(2)
        o, l = _dsw_tile(q_ref[...], kp_ref[...], kc_ref[...], vp_ref[...], vc_ref[...], n)
        o_ref[...] = o
        l_ref[...] = l

    return pl.pallas_call(
        body, name=f"dsw_attn_fwd_d{d}", grid=(d, HD // HEAD_DIM, nsub // DSW_BLOCK),
        out_shape=[jax.ShapeDtypeStruct(q.shape, F32)] * 2,
        in_specs=[cur, prev, cur, prev, cur], out_specs=[cur, cur],
        compiler_params=_params(dimension_semantics=("arbitrary",) * 3),
    )(q, k, k, v, v)


def _dsw_bwd_call(q, k, v, do, dl):
    d, nsub, HD = q.shape
    cur, prev, whole = _dsw_specs(nsub)
    blk = DSW_BLOCK

    def body(q_ref, kp_ref, kc_ref, vp_ref, vc_ref, do_ref, dl_ref, dq_ref, dk_ref, dv_ref):
        n = pl.program_id(2)

        @pl.when(n == 0)
        def _():
            dk_ref[...] = jnp.zeros(dk_ref.shape, F32)
            dv_ref[...] = jnp.zeros(dv_ref.shape, F32)

        _, vjp = jax.vjp(lambda a, b, c, e, f: _dsw_tile(a, b, c, e, f, n),
                         q_ref[...], kp_ref[...], kc_ref[...], vp_ref[...], vc_ref[...])
        dq, dkp, dkc, dvp, dvc = vjp((do_ref[...], dl_ref[...]))
        dq_ref[...] = dq
        c0 = pl.multiple_of(n * blk, blk)
        p0 = pl.multiple_of(jnp.maximum(n - 1, 0) * blk, blk)
        dk_ref[pl.ds(c0, blk), :] += dkc
        dv_ref[pl.ds(c0, blk), :] += dvc
        dk_ref[pl.ds(p0, blk), :] += dkp
        dv_ref[pl.ds(p0, blk), :] += dvp

    return pl.pallas_call(
        body, name=f"dsw_attn_bwd_d{d}", grid=(d, HD // HEAD_DIM, nsub // blk),
        out_shape=[jax.ShapeDtypeStruct(q.shape, F32)] * 3,
        in_specs=[cur, prev, cur, prev, cur, cur, cur], out_specs=[cur, whole, whole],
        compiler_params=_params(dimension_semantics=("arbitrary",) * 3),
    )(q, k, k, v, v, do, dl)


@jax.custom_vjp
def dsw_attn(q, k, v):
    return tuple(_dsw_fwd_call(q, k, v))


def _dsw_attn_fwd(q, k, v):
    return tuple(_dsw_fwd_call(q, k, v)), (q, k, v)


def _dsw_attn_bwd(res, cts):
    return tuple(_dsw_bwd_call(*res, *cts))


dsw_attn.defvjp(_dsw_attn_fwd, _dsw_attn_bwd)


_LOG2_CHUNK = GDN_CHUNK.bit_length() - 1
_LOG2_HEAD_DIM = HEAD_DIM.bit_length() - 1


def _unit_lower_inverse(a):
    n = a.shape[0]
    r = lax.broadcasted_iota(jnp.int32, (n, n), 0)
    c = lax.broadcasted_iota(jnp.int32, (n, n), 1)
    x = -a
    t = jnp.where(r == c, 1.0, 0.0) + x
    for _ in range(_LOG2_CHUNK - 1):
        x = _mm3(x, x)
        t = t + _mm3(t, x)
    return t


@jax.custom_vjp
def unit_lower_inverse(a):
    return _unit_lower_inverse(a)


def _unit_lower_inverse_fwd(a):
    t = _unit_lower_inverse(a)
    return t, t


def _unit_lower_inverse_bwd(t, ct):
    return (-_mm3(_mm3(t.T, ct), t.T),)


unit_lower_inverse.defvjp(_unit_lower_inverse_fwd, _unit_lower_inverse_bwd)


@jax.custom_vjp
def known_inverse(a, t):
    return t


def _known_inverse_fwd(a, t):
    return t, t


def _known_inverse_bwd(t, ct):
    return _unit_lower_inverse_bwd(t, ct)[0], jnp.zeros_like(t)


known_inverse.defvjp(_known_inverse_fwd, _known_inverse_bwd)


def _gdn_step(state, q, k, v, gb, bb, t_known=None):
    new_state, out, _ = _gdn_step_all(state, q, k, v, gb, bb, t_known)
    return new_state, out


def _gdn_step_all(state, q, k, v, gb, bb, t_known=None):
    C, NH = GDN_CHUNK, GDN_HEADS_PER_STEP
    R = NH * C
    r = lax.broadcasted_iota(jnp.int32, (R, R), 0)
    c = lax.broadcasted_iota(jnp.int32, (R, R), 1)
    same = lax.shift_right_logical(r, _LOG2_CHUNK) == lax.shift_right_logical(c, _LOG2_CHUNK)
    causal, strict = same & (r >= c), same & (r > c)
    gc = dot01(gb, jnp.where(causal, 1.0, 0.0).astype(BF16), True)
    g_sq = jnp.concatenate([gc] * (R // HEAD_DIM), axis=1)
    g_row = dot01(jnp.where(r == c, g_sq, 0.0), jnp.ones((R, R), BF16), True)
    decay = jnp.where(causal, jnp.exp(jnp.where(causal, g_sq - g_row, 0.0)), 0.0)
    kb, vb = k * bb, v * bb
    a_mat = jnp.where(strict, bdot(kb, k, "nt") * decay, 0.0)
    t_mat = unit_lower_inverse(a_mat) if t_known is None else known_inverse(a_mat, t_known)
    uw = bdot(t_mat, jnp.concatenate([vb, kb * jnp.exp(gc)], axis=1), "nn")
    u, w = uw[:, :HEAD_DIM], uw[:, HEAD_DIM:]
    hr = lax.shift_right_logical(lax.broadcasted_iota(jnp.int32, (R, NH * HEAD_DIM), 0), _LOG2_CHUNK)
    hc = lax.shift_right_logical(lax.broadcasted_iota(jnp.int32, (R, NH * HEAD_DIM), 1), _LOG2_HEAD_DIM)

    def widen(m):
        return jnp.where(hr == hc, jnp.concatenate([m] * NH, axis=1), 0.0)

    v_new = u - bdot(widen(w), state, "nn")
    attn = bdot(q, k, "nt") * decay
    out = bdot(widen(q * jnp.exp(gc)), state, "nn") + bdot(attn, v_new, "nn")
    last = [gc[h * C + C - 1:h * C + C, :] for h in range(NH)]
    g_last_rows = jnp.concatenate([jnp.broadcast_to(l, (C, HEAD_DIM)) for l in last], axis=0)
    g_last_state = jnp.concatenate([jnp.broadcast_to(l, (HEAD_DIM, HEAD_DIM)) for l in last], axis=0)
    k_dec = k * jnp.exp(g_last_rows - gc)
    new_state = state * jnp.exp(g_last_state) + bdot(widen(k_dec), v_new, "tn")
    return new_state, out, t_mat


def _gdn_stack(ref, g):
    h0 = g * GDN_HEADS_PER_STEP
    return jnp.concatenate([ref[:, (h0 + h) * HEAD_DIM:(h0 + h + 1) * HEAD_DIM] for h in range(GDN_HEADS_PER_STEP)],
                           axis=0)


def _gdn_unstack(ref, g, val):
    h0 = g * GDN_HEADS_PER_STEP
    for h in range(GDN_HEADS_PER_STEP):
        ref[:, (h0 + h) * HEAD_DIM:(h0 + h + 1) * HEAD_DIM] = val[h * GDN_CHUNK:(h + 1) * GDN_CHUNK]


def _gdn_fwd_call(q, k, v, gb, bb, shards=()):
    T, HD = v.shape
    H = HD // HEAD_DIM
    N = T // GDN_CHUNK
    hb = GDN_HEADS_PER_STEP * GDN_GROUPS_PER_STEP
    W = hb * HEAD_DIM
    SW = GDN_HEADS_PER_STEP * HEAD_DIM

    R = GDN_HEADS_PER_STEP * GDN_CHUNK

    def body(q_ref, k_ref, v_ref, g_ref, b_ref, o_ref, s_ref, t_ref, state):
        n = pl.program_id(1)

        @pl.when(n == 0)
        def _():
            state[...] = jnp.zeros(state.shape, F32)

        s_in = state[...]
        s_ref[...] = s_in
        res = [_gdn_step_all(s_in[g * SW:(g + 1) * SW],
                             *[_gdn_stack(ref, g) for ref in (q_ref, k_ref, v_ref, g_ref, b_ref)])
               for g in range(GDN_GROUPS_PER_STEP)]
        for g, (ns, o, t_mat) in enumerate(res):
            state[g * SW:(g + 1) * SW, :] = ns
            t_ref[g] = t_mat
            _gdn_unstack(o_ref, g, o)

    blk = pl.BlockSpec((GDN_CHUNK, W), lambda h, n: (n, h))
    sblk = pl.BlockSpec((None, W, HEAD_DIM), lambda h, n: (n, h, 0))
    tblk = pl.BlockSpec((None, GDN_GROUPS_PER_STEP, R, R), lambda h, n: (n, h, 0, 0))
    (o, states, inverses), gathered = carried_call(
        body, [GatherJob(s) for s in shards], (H // hb, N), name="gdn_chunk_fwd",
        out_shape=[jax.ShapeDtypeStruct((T, HD), F32), jax.ShapeDtypeStruct((N, H * HEAD_DIM, HEAD_DIM), F32),
                   jax.ShapeDtypeStruct((N, H // GDN_HEADS_PER_STEP, R, R), F32)],
        in_specs=[blk] * 5, out_specs=[blk, sblk, tblk], scratch_shapes=[pltpu.VMEM((W, HEAD_DIM), F32)],
        operands=(q, k, v, gb, bb))
    return o, states, inverses, tuple(gathered)


def _gdn_bwd_call(q, k, v, gb, bb, states, inverses, do):
    T, HD = v.shape
    H = HD // HEAD_DIM
    N = T // GDN_CHUNK
    hb = GDN_HEADS_PER_STEP * GDN_GROUPS_PER_STEP
    W = hb * HEAD_DIM
    SW = GDN_HEADS_PER_STEP * HEAD_DIM

    def body(q_ref, k_ref, v_ref, g_ref, b_ref, s_ref, t_ref, do_ref, dq_ref, dk_ref, dv_ref, dg_ref, db_ref, dstate):
        n = pl.program_id(1)

        @pl.when(n == 0)
        def _():
            dstate[...] = jnp.zeros(dstate.shape, F32)

        res = []
        for g in range(GDN_GROUPS_PER_STEP):
            rows = slice(g * SW, (g + 1) * SW)
            t_known = t_ref[g]
            _, vjp = jax.vjp(lambda *a: _gdn_step(*a, t_known=t_known), s_ref[rows, :],
                             *[_gdn_stack(ref, g) for ref in (q_ref, k_ref, v_ref, g_ref, b_ref)])
            res.append(vjp((dstate[rows, :], _gdn_stack(do_ref, g))))
        for g, (ds, *grads) in enumerate(res):
            dstate[g * SW:(g + 1) * SW, :] = ds
            for ref, grad in zip((dq_ref, dk_ref, dv_ref, dg_ref, db_ref), grads):
                _gdn_unstack(ref, g, grad)

    R = GDN_HEADS_PER_STEP * GDN_CHUNK
    blk = pl.BlockSpec((GDN_CHUNK, W), lambda h, n: (N - 1 - n, h))
    sblk = pl.BlockSpec((None, W, HEAD_DIM), lambda h, n: (N - 1 - n, h, 0))
    tblk = pl.BlockSpec((None, GDN_GROUPS_PER_STEP, R, R), lambda h, n: (N - 1 - n, h, 0, 0))
    return pl.pallas_call(
        body, name="gdn_chunk_bwd", grid=(H // hb, N), out_shape=[jax.ShapeDtypeStruct((T, HD), F32)] * 5,
        in_specs=[blk] * 5 + [sblk, tblk, blk], out_specs=[blk] * 5,
        scratch_shapes=[pltpu.VMEM((W, HEAD_DIM), F32)],
        compiler_params=_params(dimension_semantics=("arbitrary", "arbitrary")),
    )(q, k, v, gb, bb, states, inverses, do)


@jax.custom_vjp
def gdn_core(q, k, v, gb, bb, shards):
    o, _, _, gathered = _gdn_fwd_call(q, k, v, gb, bb, shards)
    return o, gathered


def _gdn_core_fwd(q, k, v, gb, bb, shards):
    o, states, inverses, gathered = _gdn_fwd_call(q, k, v, gb, bb, shards)
    return (o, gathered), (q, k, v, gb, bb, states, inverses, shards)


def _gdn_core_bwd(res, cts):
    *core, shards = res
    return (*_gdn_bwd_call(*core, cts[0]), tuple(jnp.zeros_like(s) for s in shards))


gdn_core.defvjp(_gdn_core_fwd, _gdn_core_bwd)


def _mixer_sb(h, w, shards=()):
    qkv = lin_t(h, w["sb_in_t"])
    q, k, v = rowop(_fn_sb_pre, "sb_pre", [qkv], [], [w["sb_q_norm"], w["sb_k_norm"]], [D_MODEL] * 3)
    o, gathered = sb_attn(q, k, v, tuple(shards))
    return lin_n(o, w["sb_out"]), gathered


def _mixer_gdn(h, w, shards=()):
    wt = w["gdn_in_t"]
    nqkv = 2 * GDN_KEY_DIM + GDN_VAL_DIM
    qkv = lin_t(h, wt[:nqkv])
    z = lin_t(h, wt[nqkv:nqkv + GDN_VAL_DIM])
    w_ba = jnp.pad(wt[nqkv + GDN_VAL_DIM:], ((0, LANES - 2 * GDN_V_HEADS), (0, 0)))
    ba = lin_t(h, w_ba)
    qkv = dwconv(qkv, w["gdn_conv_w"])
    q, k, v, gb, bb = rowop(_fn_gdn_pre, "gdn_pre", [qkv, ba], [], [w["gdn_a_log"], w["gdn_dt_bias"]],
                            [GDN_VAL_DIM] * 5)
    o, gathered = gdn_core(q, k, v, gb, bb, tuple(shards))
    (y,) = rowop(_fn_gdn_post, "gdn_post", [o, z], [], [w["gdn_o_norm"]], [GDN_VAL_DIM])
    return lin_n(y, w["gdn_out"]), gathered


def _to_strided(x, cols, d):
    T = x.shape[0]
    return x[:, cols].reshape(T // d, d, -1).transpose(1, 0, 2)


def _from_strided(x):
    d, n, c = x.shape
    return x.transpose(1, 0, 2).reshape(d * n, c)


def _mixer_dsw(h, cosm, sinm, w):
    qkv = lin_t(h, w["dsw_in_t"])
    nhd = DSW_HEADS * HEAD_DIM
    q, k, v = rowop(_fn_dsw_pre, "dsw_pre", [qkv], [cosm, sinm], [w["dsw_q_norm"], w["dsw_k_norm"]], [nhd] * 3)
    outs, lses = [], []
    for gi, (_, d) in enumerate(DSW_GROUPS):
        cols = slice(gi * DSW_HG * HEAD_DIM, (gi + 1) * DSW_HG * HEAD_DIM)
        o_g, l_g = dsw_attn(_to_strided(q, cols, d), _to_strided(k, cols, d), _to_strided(v, cols, d))
        outs.append(_from_strided(o_g))
        lses.append(_from_strided(l_g))
    (o,) = rowop(_fn_dsw_combine, "dsw_combine", [jnp.concatenate(outs, axis=1), jnp.concatenate(lses, axis=1)],
                 [], [], [nhd])
    return lin_n(o, w["dsw_out"])


def _mixer_lru(h, w):
    wt = w["lru_in_t"]
    gate = lin_t(h, wt[:LRU_WIDTH])
    xr = dwconv(lin_t(h, wt[LRU_WIDTH:]), w["lru_conv_w"])
    a, u = rowop(_fn_lru_gates, "lru_gates", [xr], [],
                 [w["lru_conv_b"], w["lru_w_a"], w["lru_b_a"], w["lru_w_x"], w["lru_b_x"], w["lru_lambda"]],
                 [LRU_WIDTH] * 2)
    hs = lru_scan(a, u)
    (y,) = rowop(_fn_lru_out, "lru_out", [hs, gate], [], [], [LRU_WIDTH])
    return lin_n(y, w["lru_out"])


def mixer_segment(i, x, f_prev, cosm, sinm, w, shards):
    if i == 0:
        (h,) = rowop(_fn_norm, "norm", [x], [], [w["mix_norm"]], [D_MODEL])
    else:
        x, h = rowop(_fn_add_norm, "add_norm", [x, f_prev], [], [w["mix_norm"]], [D_MODEL] * 2)
    kind, gathered = i % 4, ()
    if kind == 0:
        y, gathered = _mixer_sb(h, w, shards)
    elif kind == 1:
        y, gathered = _mixer_gdn(h, w, shards)
    elif kind == 2:
        y = _mixer_dsw(h, cosm, sinm, w)
    else:
        y = _mixer_lru(h, w)
    x, h = rowop(_fn_add_norm, "add_norm", [x, y], [], [w["ffn_norm"]], [D_MODEL] * 2)
    return (x, h), gathered


def sb_segment_in(x, w):
    (h,) = rowop(_fn_norm, "norm", [x], [], [w["mix_norm"]], [D_MODEL])
    qkv = lin_t(h, w["sb_in_t"])
    return rowop(_fn_sb_pre, "sb_pre", [qkv], [], [w["sb_q_norm"], w["sb_k_norm"]], [D_MODEL] * 3)


def sb_segment_out(x, o, w):
    return rowop(_fn_add_norm, "add_norm", [x, lin_n(o, w["sb_out"])], [], [w["ffn_norm"]], [D_MODEL] * 2)


def swiglu_act(gu):
    return rowop(_fn_swiglu, "swiglu", [gu], [], [], [FFN_HIDDEN])[0]


def swiglu_fwd(gu):
    T, F2 = gu.shape
    F = F2 // 2
    tr = _pick(T, (128, 64, 32, 16))

    def body(gu_ref, o_ref):
        o_ref[...] = (jax.nn.silu(gu_ref[:, :F]) * gu_ref[:, F:]).astype(o_ref.dtype)

    return pl.pallas_call(
        body, name="swiglu_fwd", grid=(T // tr,), out_shape=jax.ShapeDtypeStruct((T, F), BF16),
        in_specs=[_row_spec(tr, F2)], out_specs=_row_spec(tr, F),
        compiler_params=_params(dimension_semantics=("arbitrary",)),
    )(gu)


def swiglu_bwd(gu, dact):
    T, F2 = gu.shape
    F = F2 // 2
    tr = _pick(T, (128, 64, 32, 16))

    def body(gu_ref, d_ref, o_ref):
        g, up, d = gu_ref[:, :F], gu_ref[:, F:], d_ref[...]
        s = jax.nn.sigmoid(g)
        o_ref[:, :F] = (d * up * (s * (1.0 + g * (1.0 - s)))).astype(o_ref.dtype)
        o_ref[:, F:] = (d * (g * s)).astype(o_ref.dtype)

    return pl.pallas_call(
        body, name="swiglu_bwd", grid=(T // tr,), out_shape=jax.ShapeDtypeStruct((T, F2), BF16),
        in_specs=[_row_spec(tr, F2), _row_spec(tr, F)], out_specs=_row_spec(tr, F2),
        compiler_params=_params(dimension_semantics=("arbitrary",)),
    )(gu, dact)


ANY = pl.BlockSpec(memory_space=pl.ANY)


SLAB_BYTES = 4 * 1024 * 1024


def _col_tile(rows, C, itemsize):
    return _tile(C, max(LANES, SLAB_BYTES // (rows * itemsize)))


class GatherJob:
    def __init__(self, shard):
        self.operand = shard
        r, C = shard.shape
        self.out_shape = jax.ShapeDtypeStruct((N_DEV, r, C), shard.dtype)
        self.scratch = [pltpu.SemaphoreType.DMA((7,)), pltpu.SemaphoreType.DMA((7,)), pltpu.SemaphoreType.DMA]

    def _parts(self, x_ref, out_ref, send_sems, recv_sems, local_sem):
        x, y, c = lax.axis_index("x"), lax.axis_index("y"), lax.axis_index("c")
        me, sibling = (x, y, c), (x, y, 1 - c)
        chips = [(1 - x, y), (x, 1 - y), (1 - x, 1 - y)]

        def slot(px, py, pc):
            return out_ref.at[4 * px + 2 * py + pc]

        def copy(k, block, to, src=None):
            return pltpu.make_async_remote_copy(
                src_ref=slot(*block) if src is None else src, dst_ref=slot(*block),
                send_sem=send_sems.at[k], recv_sem=recv_sems.at[k], device_id=to, device_id_type=MESH)

        def mine():
            return pltpu.make_async_copy(x_ref, slot(*me), local_sem)

        def first():
            return [copy(0, me, sibling, src=x_ref)] + [copy(1 + j, me, (*chip, c), src=x_ref)
                                                        for j, chip in enumerate(chips)]

        def passed():
            return [copy(4 + j, (*chip, c), sibling) for j, chip in enumerate(chips)]

        def landed():
            return [copy(1 + j, (*chip, c), me) for j, chip in enumerate(chips)]

        def from_sibling():
            return [copy(0, sibling, me)] + [copy(4 + j, (*chip, 1 - c), me) for j, chip in enumerate(chips)]

        return mine, first, passed, landed, from_sibling

    def start(self, x_ref, out_ref, *sems):
        mine, first, _, _, _ = self._parts(x_ref, out_ref, *sems)
        mine().start()
        for cp in first():
            cp.start()

    def finish(self, x_ref, out_ref, *sems):
        mine, first, passed, landed, from_sibling = self._parts(x_ref, out_ref, *sems)
        onward = passed()
        for arrived, cp in zip(landed(), onward):
            arrived.wait_recv()
            cp.start()
        for cp in from_sibling():
            cp.wait_recv()
        for cp in first() + onward:
            cp.wait_send()
        mine().wait()


class PairJob:
    def __init__(self, g):
        self.operand = g
        _, _, r, C = g.shape
        self.out_shape = jax.ShapeDtypeStruct((N_CHIPS, r, C), g.dtype)
        self.scratch = [pltpu.SemaphoreType.DMA((N_CHIPS,)), pltpu.SemaphoreType.DMA((N_CHIPS,))]

    def _copies(self, g_ref, out_ref, send_sems, recv_sems):
        x, y, c = lax.axis_index("x"), lax.axis_index("y"), lax.axis_index("c")
        return [pltpu.make_async_remote_copy(
            src_ref=g_ref.at[q, 1 - c], dst_ref=out_ref.at[q], send_sem=send_sems.at[q], recv_sem=recv_sems.at[q],
            device_id=(x, y, 1 - c), device_id_type=MESH) for q in range(N_CHIPS)]

    def start(self, *refs):
        for cp in self._copies(*refs):
            cp.start()

    def finish(self, *refs):
        for cp in self._copies(*refs):
            cp.wait()


class ChipsJob:
    def __init__(self, p):
        self.operand = p
        self.out_shape = jax.ShapeDtypeStruct(p.shape, p.dtype)
        self.scratch = [pltpu.SemaphoreType.DMA((3,)), pltpu.SemaphoreType.DMA((3,)), pltpu.SemaphoreType.DMA]

    def _parts(self, p_ref, out_ref, send_sems, recv_sems, local_sem):
        x, y, c = lax.axis_index("x"), lax.axis_index("y"), lax.axis_index("c")
        mychip = 2 * x + y
        chips = [(1 - x, y), (x, 1 - y), (1 - x, 1 - y)]
        def mine():
            return pltpu.make_async_copy(p_ref.at[mychip], out_ref.at[mychip], local_sem)

        def sends():
            return [pltpu.make_async_remote_copy(
                src_ref=p_ref.at[2 * cx + cy], dst_ref=out_ref.at[mychip], send_sem=send_sems.at[j],
                recv_sem=recv_sems.at[j], device_id=(cx, cy, c), device_id_type=MESH)
                for j, (cx, cy) in enumerate(chips)]

        def arrivals():
            return [pltpu.make_async_remote_copy(
                src_ref=p_ref.at[mychip], dst_ref=out_ref.at[2 * cx + cy], send_sem=send_sems.at[j],
                recv_sem=recv_sems.at[j], device_id=(cx, cy, c), device_id_type=MESH)
                for j, (cx, cy) in enumerate(chips)]

        return mine, sends, arrivals

    def start(self, *refs):
        mine, sends, _ = self._parts(*refs)
        mine().start()
        for cp in sends():
            cp.start()

    def finish(self, *refs):
        mine, sends, arrivals = self._parts(*refs)
        for cp in arrivals():
            cp.wait_recv()
        for cp in sends():
            cp.wait_send()
        mine().wait()


def carried_call(body, jobs, grid, *, name, out_shape, in_specs, out_specs, scratch_shapes, operands):
    jobs = list(jobs)
    n_in, n_out, n_scr, nj = len(in_specs), len(out_specs), len(scratch_shapes), len(jobs)
    n_sem = [len(j.scratch) for j in jobs]

    def full_body(*refs):
        core_in = refs[:n_in]
        job_in = refs[n_in:n_in + nj]
        core_out = refs[n_in + nj:n_in + nj + n_out]
        job_out = refs[n_in + nj + n_out:n_in + 2 * nj + n_out]
        rest = refs[n_in + 2 * nj + n_out:]
        core_scr, sems, pos = rest[:n_scr], [], n_scr
        for n in n_sem:
            sems.append(rest[pos:pos + n])
            pos += n
        ids = [pl.program_id(a) for a in range(len(grid))]
        first = functools.reduce(jnp.logical_and, [i == 0 for i in ids])
        last = functools.reduce(jnp.logical_and, [i == g - 1 for i, g in zip(ids, grid)])
        if jobs:
            @pl.when(first)
            def _():
                for j, job in enumerate(jobs):
                    job.start(job_in[j], job_out[j], *sems[j])

        body(*core_in, *core_out, *core_scr)
        if jobs:
            @pl.when(last)
            def _():
                for j, job in enumerate(jobs):
                    job.finish(job_in[j], job_out[j], *sems[j])

    res = pl.pallas_call(
        full_body, name=name, grid=grid,
        out_shape=list(out_shape) + [j.out_shape for j in jobs],
        in_specs=list(in_specs) + [ANY] * nj, out_specs=list(out_specs) + [ANY] * nj,
        scratch_shapes=list(scratch_shapes) + [s for j in jobs for s in j.scratch],
        compiler_params=_params(dimension_semantics=("arbitrary",) * len(grid)),
    )(*operands, *[j.operand for j in jobs])
    return res[:n_out], res[n_out:]


def _exchange(job, name):
    def body(*refs):
        job.start(*refs)
        job.finish(*refs)

    return pl.pallas_call(
        body, name=name, out_shape=job.out_shape, in_specs=[ANY], out_specs=ANY, scratch_shapes=job.scratch,
        compiler_params=pltpu.CompilerParams(has_side_effects=True),
    )(job.operand)


def all_gather(shard, name):
    return _exchange(GatherJob(shard), name)


def exchange_pair(g, name):
    return _exchange(PairJob(g), name)


def exchange_chips(p, name):
    return _exchange(ChipsJob(p), name)


def pair_add(g, got, name):
    _, _, r, C = g.shape
    cb = _col_tile(r, C, g.dtype.itemsize)
    c = lax.axis_index("c")

    def body(c_ref, a_ref, b_ref, o_ref):
        o_ref[...] = (a_ref[...].astype(F32) + b_ref[...].astype(F32)).astype(o_ref.dtype)

    return pl.pallas_call(
        body, name=name, out_shape=jax.ShapeDtypeStruct((N_CHIPS, r, C), g.dtype),
        grid_spec=pltpu.PrefetchScalarGridSpec(
            num_scalar_prefetch=1, grid=(N_CHIPS, C // cb),
            in_specs=[pl.BlockSpec((None, None, r, cb), lambda q, j, cr: (q, cr[0], 0, j)),
                      pl.BlockSpec((None, r, cb), lambda q, j, cr: (q, 0, j))],
            out_specs=pl.BlockSpec((None, r, cb), lambda q, j, cr: (q, 0, j))),
        compiler_params=_params(dimension_semantics=("arbitrary", "arbitrary")),
    )(jnp.reshape(c, (1,)).astype(jnp.int32), g, got)


def sum_slots(parts, name):
    n, r, C = parts.shape
    cb = _col_tile(n * r, C, parts.dtype.itemsize)

    def body(p_ref, o_ref):
        acc = p_ref[0].astype(F32)
        for q in range(1, n):
            acc = acc + p_ref[q].astype(F32)
        o_ref[...] = acc

    return pl.pallas_call(
        body, name=name, out_shape=jax.ShapeDtypeStruct((r, C), F32), grid=(C // cb,),
        in_specs=[pl.BlockSpec((n, r, cb), lambda j: (0, 0, j))], out_specs=pl.BlockSpec((r, cb), lambda j: (0, j)),
        compiler_params=_params(dimension_semantics=("arbitrary",)),
    )(parts)


def reduce_scatter(g, tag):
    _, r, C = g.shape
    g4 = g.reshape(N_CHIPS, 2, r, C)
    got = exchange_pair(g4, f"rs_pair_{tag}")
    pairs = pair_add(g4, got, f"rs_pair_add_{tag}")
    parts = exchange_chips(pairs, f"rs_chips_{tag}")
    return sum_slots(parts, f"rs_sum_{tag}")


def all_reduce(v, tag):
    return sum_slots(all_gather(v, f"ar_gather_{tag}"), f"ar_sum_{tag}")


def adamw(w, g, m, v, name):
    R, C = w.shape
    tr = R
    for cand in (512, 256, 128, 64, 32, 16, 8):
        if R % cand == 0 and cand * C * 4 * 7 * 2 <= 40 * 1024 * 1024:
            tr = cand
            break
    c1 = 1.0 - ADAM_B1 ** ADAM_STEP
    c2 = 1.0 - ADAM_B2 ** ADAM_STEP

    def body(w_ref, g_ref, m_ref, v_ref, d_ref, nm_ref, nv_ref):
        gv = g_ref[...]
        nm = ADAM_B1 * m_ref[...] + (1.0 - ADAM_B1) * gv
        nv = ADAM_B2 * v_ref[...] + (1.0 - ADAM_B2) * (gv * gv)
        d_ref[...] = -ADAM_LR * ((nm / c1) / (jnp.sqrt(nv / c2) + ADAM_EPS) + ADAM_WD * w_ref[...])
        nm_ref[...] = nm
        nv_ref[...] = nv

    spec = pl.BlockSpec((tr, C), lambda i: (i, 0))
    return pl.pallas_call(
        body, name=name, out_shape=[jax.ShapeDtypeStruct((R, C), F32)] * 3, grid=(R // tr,),
        in_specs=[spec] * 4, out_specs=[spec] * 3,
        compiler_params=_params(dimension_semantics=("arbitrary",)),
    )(w, g, m, v)


NAMES = ['mix_norm', 'ffn_norm', 'ffn_w_gu', 'ffn_w_down', 'sb_w_in', 'sb_q_norm', 'sb_k_norm', 'sb_w_out',
         'gdn_w_in', 'gdn_conv_w', 'gdn_a_log', 'gdn_dt_bias', 'gdn_o_norm', 'gdn_w_out', 'dsw_w_in', 'dsw_q_norm',
         'dsw_k_norm', 'dsw_w_out', 'lru_w_in', 'lru_conv_w', 'lru_conv_b', 'lru_w_a', 'lru_b_a', 'lru_w_x',
         'lru_b_x', 'lru_lambda', 'lru_w_out']
REPLICATED = ['mix_norm', 'ffn_norm', 'sb_q_norm', 'sb_k_norm', 'gdn_a_log', 'gdn_dt_bias', 'gdn_o_norm',
              'dsw_q_norm', 'dsw_k_norm']
SMALL_SHARDED = ['gdn_conv_w', 'lru_conv_w', 'lru_conv_b', 'lru_b_a', 'lru_b_x', 'lru_lambda']
IN_T = ['sb_w_in', 'gdn_w_in', 'dsw_w_in', 'lru_w_in']
OUT_N = ['sb_w_out', 'gdn_w_out', 'dsw_w_out', 'lru_w_out']


def _pack(arrs, pad_rows_to=8):
    flat = jnp.concatenate([a.reshape(-1) for a in arrs])
    n = flat.shape[0]
    rows = -(-n // LANES)
    rows = -(-rows // pad_rows_to) * pad_rows_to
    return jnp.pad(flat, (0, rows * LANES - n)).reshape(rows, LANES)


def _unpack(buf, shapes):
    flat = buf.reshape(-1)
    out, o = [], 0
    for s in shapes:
        n = math.prod(s)
        out.append(flat[o:o + n].reshape(s))
        o += n
    return out


def kernel(x, positions, mix_norm, ffn_norm, ffn_w_gu, ffn_w_down, sb_w_in, sb_q_norm, sb_k_norm, sb_w_out, gdn_w_in, gdn_conv_w, gdn_a_log, gdn_dt_bias, gdn_o_norm, gdn_w_out, dsw_w_in, dsw_q_norm, dsw_k_norm, dsw_w_out, lru_w_in, lru_conv_w, lru_conv_b, lru_w_a, lru_b_a, lru_w_x, lru_b_x, lru_lambda, lru_w_out, loss_target, m_mix_norm, m_ffn_norm, m_ffn_w_gu, m_ffn_w_down, m_sb_w_in, m_sb_q_norm, m_sb_k_norm, m_sb_w_out, m_gdn_w_in, m_gdn_conv_w, m_gdn_a_log, m_gdn_dt_bias, m_gdn_o_norm, m_gdn_w_out, m_dsw_w_in, m_dsw_q_norm, m_dsw_k_norm, m_dsw_w_out, m_lru_w_in, m_lru_conv_w, m_lru_conv_b, m_lru_w_a, m_lru_b_a, m_lru_w_x, m_lru_b_x, m_lru_lambda, m_lru_w_out, v_mix_norm, v_ffn_norm, v_ffn_w_gu, v_ffn_w_down, v_sb_w_in, v_sb_q_norm, v_sb_k_norm, v_sb_w_out, v_gdn_w_in, v_gdn_conv_w, v_gdn_a_log, v_gdn_dt_bias, v_gdn_o_norm, v_gdn_w_out, v_dsw_w_in, v_dsw_q_norm, v_dsw_k_norm, v_dsw_w_out, v_lru_w_in, v_lru_conv_w, v_lru_conv_b, v_lru_w_a, v_lru_b_a, v_lru_w_x, v_lru_b_x, v_lru_lambda, v_lru_w_out):
    args = locals()
    W = {n: args[n] for n in NAMES}
    M = {n: args["m_" + n] for n in NAMES}
    V = {n: args["v_" + n] for n in NAMES}
    T = x.shape[1]
    x2 = x[0]
    tgt = loss_target[0]

    S = {}
    for n in IN_T:
        S[n] = W[n][0].T.astype(BF16)
    for n in OUT_N:
        S[n] = W[n][0].astype(BF16)
    for i in range(DEPTH):
        S[f"ffn_gu{i}"] = ffn_w_gu[i].T.astype(BF16)
        S[f"ffn_down{i}"] = ffn_w_down[i].astype(BF16)
    S["lru_gates"] = jnp.concatenate(
        [lru_w_a[0].reshape(-1, LRU_BLOCK_DIM), lru_w_x[0].reshape(-1, LRU_BLOCK_DIM)], axis=0).astype(BF16)
    Gt = {}

    def flat(key):
        return Gt[key].reshape(-1, Gt[key].shape[-1])

    Gt["sb_w_in"] = all_gather(S["sb_w_in"], "ag_sb_w_in")
    Gt["sb_w_out"] = all_gather(S["sb_w_out"], "ag_sb_w_out")
    small_shapes = [W[n].shape for n in SMALL_SHARDED]
    sm = all_gather(_pack([W[n] for n in SMALL_SHARDED]), "ag_small")
    sm = [jnp.stack(parts) for parts in zip(*[_unpack(sm[p], small_shapes) for p in range(N_DEV)])]
    smd = dict(zip(SMALL_SHARDED, sm))

    def mixer_weights(i):
        kind = i % 4
        w = {"mix_norm": mix_norm[i:i + 1], "ffn_norm": ffn_norm[i:i + 1]}
        if kind == 0:
            w.update(sb_in_t=flat("sb_w_in"), sb_out=flat("sb_w_out"), sb_q_norm=sb_q_norm, sb_k_norm=sb_k_norm)
        elif kind == 1:
            w.update(gdn_in_t=flat("gdn_w_in"), gdn_out=flat("gdn_w_out"), gdn_a_log=gdn_a_log,
                     gdn_dt_bias=gdn_dt_bias, gdn_o_norm=gdn_o_norm,
                     gdn_conv_w=smd["gdn_conv_w"][:, 0].transpose(1, 0, 2).reshape(4, -1))
        elif kind == 2:
            w.update(dsw_in_t=flat("dsw_w_in"), dsw_out=flat("dsw_w_out"), dsw_q_norm=dsw_q_norm,
                     dsw_k_norm=dsw_k_norm)
        else:
            gg = Gt["lru_gates"].reshape(N_DEV, 2, LRU_BLOCKS, 32, LRU_BLOCK_DIM)
            gg = gg.transpose(1, 2, 0, 3, 4).reshape(2, LRU_BLOCKS, LRU_BLOCK_DIM, LRU_BLOCK_DIM).astype(F32)
            w.update(lru_in_t=flat("lru_w_in"), lru_out=flat("lru_w_out"), lru_w_a=gg[0], lru_w_x=gg[1],
                     lru_conv_w=smd["lru_conv_w"][:, 0].transpose(1, 0, 2).reshape(4, -1),
                     lru_conv_b=smd["lru_conv_b"][:, 0].reshape(1, -1),
                     lru_lambda=smd["lru_lambda"][:, 0].reshape(1, -1),
                     lru_b_a=smd["lru_b_a"][:, 0].transpose(1, 0, 2).reshape(1, -1),
                     lru_b_x=smd["lru_b_x"][:, 0].transpose(1, 0, 2).reshape(1, -1))
        return w

    carried_by_mixer = {0: ["ffn_gu0", "ffn_down0", "gdn_w_in", "gdn_w_out", "ffn_gu1", "ffn_down1"],
                        1: ["ffn_gu2", "ffn_down2", "lru_w_in", "lru_gates", "ffn_gu3", "ffn_down3"]}
    carried_by_ffn = {0: (["dsw_w_in"], ["dsw_w_out", "lru_w_out"])}

    half = ROPE_DIM // 2
    inv_freq = ROPE_THETA ** (-jnp.arange(half, dtype=F32) / half)
    ang = positions[0].astype(F32)[:, None] * inv_freq
    cs, sn = jnp.cos(ang), jnp.sin(ang)
    cosm = jnp.concatenate([cs, cs, jnp.ones((T, HEAD_DIM - ROPE_DIM), F32)], axis=1)
    sinm = jnp.concatenate([-sn, sn, jnp.zeros((T, HEAD_DIM - ROPE_DIM), F32)], axis=1)

    def gather_jobs(keys):
        return [GatherJob(S[k]) for k in keys]

    xs, f, seg_vjps, ffn_res = x2, None, [], []
    for i in range(DEPTH):
        keys = carried_by_mixer.get(i, [])
        shards = tuple(S[k] for k in keys)
        if i == 0:
            w0 = mixer_weights(0)
            qkv0, vjp_in = jax.vjp(sb_segment_in, xs, {k: w0[k] for k in ("mix_norm", "sb_in_t", "sb_q_norm", "sb_k_norm")})
            o0, tot0, gathered = _sb_fwd_call(*qkv0, shards)
            (xs, h), vjp_i = jax.vjp(sb_segment_out, xs, o0, {k: w0[k] for k in ("sb_out", "ffn_norm")})
        else:
            (xs, h), vjp_i, gathered = jax.vjp(
                lambda xx, ff, ww, i=i, shards=shards: mixer_segment(i, xx, ff, cosm, sinm, ww, shards),
                xs, f, mixer_weights(i), has_aux=True)
        Gt.update(zip(keys, gathered))
        seg_vjps.append(vjp_i)
        keys_gu, keys_down = carried_by_ffn.get(i, ([], []))
        h = h.astype(BF16)
        gu, got = mm_carry(h, flat(f"ffn_gu{i}"), "nt", F32, gather_jobs(keys_gu))
        Gt.update(zip(keys_gu, got))
        act = swiglu_fwd(gu)
        f, got = mm_carry(act, flat(f"ffn_down{i}"), "nn", F32, gather_jobs(keys_down))
        Gt.update(zip(keys_down, got))
        ffn_res.append((h, act, gu))
    loss_part, dy = loss_head(xs, f, tgt)
    loss = lax.psum(loss_part, ("x", "y", "c"))

    reduced, gw_small, gw_rep = {}, {}, {}

    def to_partials(i, dw, dwd, dwgu):
        out = {}
        if dwgu is not None:
            out = {f"ffn_gu{i}": dwgu.reshape(N_DEV, -1, D_MODEL), f"ffn_down{i}": dwd.reshape(N_DEV, -1, D_MODEL)}
        for k, g in dw.items():
            if k.endswith("_in_t"):
                out[k.replace("_in_t", "_w_in")] = g.reshape(N_DEV, -1, D_MODEL)
            elif k.endswith("_out"):
                out[k.replace("_out", "_w_out")] = g.reshape(N_DEV, -1, D_MODEL)
            elif k in ("mix_norm", "ffn_norm"):
                gw_rep[(k, i)] = g
            elif k in REPLICATED:
                gw_rep[k] = g
            elif k not in ("lru_w_a", "lru_w_x"):
                gw_small[k] = g
        if "lru_w_a" in dw:
            gg = jnp.stack([dw["lru_w_a"], dw["lru_w_x"]]).reshape(2, LRU_BLOCKS, N_DEV, 32, LRU_BLOCK_DIM)
            out["lru_gates"] = gg.transpose(2, 0, 1, 3, 4).reshape(N_DEV, 2 * LRU_BLOCKS * 32, LRU_BLOCK_DIM)
        return out

    def spread(keys, sizes, capacities):
        room, bins = list(capacities), [[] for _ in capacities]
        for k in sorted(keys, key=lambda k: -sizes[k]):
            b = max(range(len(room)), key=lambda j: room[j])
            bins[b].append(k)
            room[b] -= sizes[k]
        return bins

    ready, dx, df = {}, dy, dy
    for i in reversed(range(DEPTH)):
        h, act, gu = ffn_res[i]
        keys = list(ready)
        g4 = {k: ready[k].reshape(N_CHIPS, 2, *ready[k].shape[1:]) for k in keys}
        dact, got = mm_carry(df, flat(f"ffn_down{i}"), "nt", F32, [PairJob(g4[k]) for k in keys])
        pairs = {k: pair_add(g4[k], g, f"rs_pair_add_{k}") for k, g in zip(keys, got)}
        sizes = {k: math.prod(pairs[k].shape[1:]) * pairs[k].dtype.itemsize for k in keys}
        bins = spread(keys, sizes, [c * D_MODEL * 2 for c in (800, 1300, 1550)])
        dwd, p0 = mm_carry(act, df, "tn", BF16, [ChipsJob(pairs[k]) for k in bins[0]])
        dgu = swiglu_bwd(gu, dact)
        dh, p1 = mm_carry(dgu, flat(f"ffn_gu{i}"), "nn", F32, [ChipsJob(pairs[k]) for k in bins[1]])
        dwgu, p2 = mm_carry(dgu, h, "tn", BF16, [ChipsJob(pairs[k]) for k in bins[2]])
        for k, parts in zip(bins[0] + bins[1] + bins[2], list(p0) + list(p1) + list(p2)):
            reduced[k] = sum_slots(parts, f"rs_sum_{k}")
        if i > 0:
            dx, df, dw = seg_vjps[i]((dx, dh))
            ready = to_partials(i, dw, dwd, dwgu)
    dx_out, do0, dw = seg_vjps[0]((dx, dh))
    ready = to_partials(0, dw, dwd, dwgu)
    keys = list(ready)
    g4 = {k: ready[k].reshape(N_CHIPS, 2, *ready[k].shape[1:]) for k in keys}
    pairs = {k: pair_add(g4[k], exchange_pair(g4[k], f"rs_pair_{k}"), f"rs_pair_add_{k}") for k in keys}
    dq0, dk0, dv0, parts = _sb_bwd_call(*qkv0, tot0, do0, [ChipsJob(pairs[k]) for k in keys])
    for k, p in zip(keys, parts):
        reduced[k] = sum_slots(p, f"rs_sum_{k}")
    dx_in, dw = vjp_in((dq0, dk0, dv0))
    gx = dx_in + dx_out
    for k, g in to_partials(0, dw, None, None).items():
        reduced[k] = reduce_scatter(g, k)

    G = {}
    for n in IN_T:
        G[n] = reduced[n].T[None]
    for n in OUT_N:
        G[n] = reduced[n][None]
    G["ffn_w_gu"] = jnp.stack([reduced[f"ffn_gu{i}"].T for i in range(DEPTH)])
    G["ffn_w_down"] = jnp.stack([reduced[f"ffn_down{i}"] for i in range(DEPTH)])
    gg = reduced["lru_gates"].reshape(2, 1, LRU_BLOCKS, 32, LRU_BLOCK_DIM)
    G["lru_w_a"], G["lru_w_x"] = gg[0], gg[1]
    gw = dict(gw_small)
    for n in REPLICATED:
        gw[n] = (jnp.concatenate([gw_rep[(n, i)] for i in range(DEPTH)], axis=0) if n in ("mix_norm", "ffn_norm")
                 else gw_rep[n])
    gs = {
        "gdn_conv_w": gw["gdn_conv_w"].reshape(4, N_DEV, -1).transpose(1, 0, 2)[:, None],
        "lru_conv_w": gw["lru_conv_w"].reshape(4, N_DEV, -1).transpose(1, 0, 2)[:, None],
        "lru_conv_b": gw["lru_conv_b"].reshape(N_DEV, 1, -1),
        "lru_lambda": gw["lru_lambda"].reshape(N_DEV, 1, -1),
        "lru_b_a": gw["lru_b_a"].reshape(LRU_BLOCKS, N_DEV, 32).transpose(1, 0, 2)[:, None],
        "lru_b_x": gw["lru_b_x"].reshape(LRU_BLOCKS, N_DEV, 32).transpose(1, 0, 2)[:, None],
    }
    packed = jnp.stack([_pack([gs[n][p] for n in SMALL_SHARDED]) for p in range(N_DEV)])
    for n, g in zip(SMALL_SHARDED, _unpack(reduce_scatter(packed, "small"), small_shapes)):
        G[n] = g
    rep_shapes = [W[n].shape for n in REPLICATED]
    for n, g in zip(REPLICATED, _unpack(all_reduce(_pack([gw[n] for n in REPLICATED]), "rep"), rep_shapes)):
        G[n] = g

    D, NM, NV = {}, {}, {}
    big = [n for n in NAMES if n not in REPLICATED and n not in SMALL_SHARDED]
    for n in big:
        shp = W[n].shape
        two = (-1, shp[-1])
        d, nm, nv = adamw(W[n].reshape(two), G[n].reshape(two), M[n].reshape(two), V[n].reshape(two), f"adamw_{n}")
        D[n], NM[n], NV[n] = d.reshape(shp), nm.reshape(shp), nv.reshape(shp)
    for group, tag in ((SMALL_SHARDED, "small"), (REPLICATED, "rep")):
        shapes = [W[n].shape for n in group]
        res = adamw(_pack([W[n] for n in group]), _pack([G[n] for n in group]), _pack([M[n] for n in group]),
                    _pack([V[n] for n in group]), f"adamw_{tag}")
        for dst, buf in zip((D, NM, NV), res):
            for n, a in zip(group, _unpack(buf, shapes)):
                dst[n] = a

    return (loss, gx[None], *[G[n] for n in NAMES], *[D[n] for n in NAMES], *[NM[n] for n in NAMES],
            *[NV[n] for n in NAMES])
```

```python
import functools
import math

import jax
import jax.numpy as jnp
from jax import lax
from jax.experimental import pallas as pl
from jax.experimental.pallas import tpu as pltpu

F32 = jnp.float32
BF16 = jnp.bfloat16

D_MODEL = 2048
HEAD_DIM = 128
NORM_EPS = 1e-6
SB_HEADS = 16
SB_BLOCK = 256
SB_WIDE = (4, 2, 1)
GDN_K_HEADS = 16
GDN_V_HEADS = 32
GDN_KEY_DIM = 2048
GDN_VAL_DIM = 4096
GDN_CHUNK = 64
GDN_HEADS_PER_STEP = 4
GDN_GROUPS_PER_STEP = 1
DSW_GROUPS = ((128, 1), (512, 4), (2048, 16))
DSW_HG = 6
DSW_HEADS = 18
DSW_BLOCK = 128
ROPE_DIM = 32
ROPE_THETA = 500000.0
LRU_WIDTH = 2048
LRU_BLOCKS = 8
LRU_BLOCK_DIM = 256
LRU_C = 8.0
FFN_HIDDEN = 5632
DEPTH = 4
ADAM_LR, ADAM_B1, ADAM_B2, ADAM_EPS, ADAM_WD, ADAM_STEP = 0.001, 0.9, 0.999, 1e-08, 0.01, 10
N_DEV = 8
N_CHIPS = 4

V7X_VMEM_LIMIT = 56 * 1024 * 1024
LANES = 128
MESH = pl.DeviceIdType.MESH


def _params(**kw):
    return pltpu.CompilerParams(vmem_limit_bytes=V7X_VMEM_LIMIT, **kw)


def _pick(n, cands):
    for c in cands:
        if n % c == 0:
            return c
    return n


def _tile(n, cap, unit=LANES):
    best = None
    for t in range(unit, min(n, cap) + 1, unit):
        if n % t == 0:
            best = t
    return best or n


_DN = {"nn": (((1,), (0,)), ((), ())), "nt": (((1,), (1,)), ((), ())), "tn": (((0,), (0,)), ((), ()))}


def _raw_dot(a, b, mode):
    return lax.dot_general(a.astype(BF16), b.astype(BF16), _DN[mode], preferred_element_type=F32)


@functools.partial(jax.custom_vjp, nondiff_argnums=(2,))
def bdot(a, b, mode):
    return _raw_dot(a, b, mode)


def _bdot_fwd(a, b, mode):
    return _raw_dot(a, b, mode), (a, b)


def _bdot_bwd(mode, res, ct):
    a, b = res
    if mode == "nn":
        return bdot(ct, b, "nt"), bdot(a, ct, "tn")
    if mode == "nt":
        return bdot(ct, b, "nn"), bdot(ct, a, "tn")
    return bdot(b, ct, "nt"), bdot(a, ct, "nn")


bdot.defvjp(_bdot_fwd, _bdot_bwd)


def _split2(x):
    hi = x.astype(BF16)
    lo = (x - hi.astype(F32)).astype(BF16)
    return hi, lo


def _dot01_raw(x, m, dn, left):
    hi, lo = _split2(x)
    if left:
        return (lax.dot_general(m, hi, dn, preferred_element_type=F32)
                + lax.dot_general(m, lo, dn, preferred_element_type=F32))
    return (lax.dot_general(hi, m, dn, preferred_element_type=F32)
            + lax.dot_general(lo, m, dn, preferred_element_type=F32))


@functools.partial(jax.custom_vjp, nondiff_argnums=(2,))
def dot01(x, m, left):
    return _dot01_raw(x, m, _DN["nn"], left)


def _dot01_fwd(x, m, left):
    return _dot01_raw(x, m, _DN["nn"], left), m


def _dot01_bwd(left, m, ct):
    dx = _dot01_raw(ct, m, _DN["tn"] if left else _DN["nt"], left)
    return dx, jnp.zeros_like(m)


dot01.defvjp(_dot01_fwd, _dot01_bwd)


def _mm3(a, b):
    ah, al = _split2(a)
    bh, bl = _split2(b)
    dn = _DN["nn"]
    return (lax.dot_general(ah, bh, dn, preferred_element_type=F32)
            + lax.dot_general(ah, bl, dn, preferred_element_type=F32)
            + lax.dot_general(al, bh, dn, preferred_element_type=F32))


@jax.custom_vjp
def mm3(a, b):
    return _mm3(a, b)


def _mm3_fwd(a, b):
    return _mm3(a, b), (a, b)


def _mm3_bwd(res, ct):
    a, b = res
    return _mm3(ct, b.T), _mm3(a.T, ct)


mm3.defvjp(_mm3_fwd, _mm3_bwd)


def _softplus_parts(z):
    e = jnp.exp(-jnp.abs(z))
    l = jnp.log(1.0 + e)
    sp = jnp.maximum(z, 0.0) + l
    ls = jnp.minimum(z, 0.0) - l
    inv = 1.0 / (1.0 + e)
    sig = jnp.where(z >= 0.0, inv, e * inv)
    return sp, ls, sig


def _rms(x, g):
    return x * lax.rsqrt(jnp.mean(x * x, axis=-1, keepdims=True) + NORM_EPS) * g


def _swap16(x):
    lane = lax.broadcasted_iota(jnp.int32, x.shape, 1)
    return jnp.where(lane < 16, pltpu.roll(x, 112, axis=1), jnp.where(lane < 32, pltpu.roll(x, 16, axis=1), 0.0))


@jax.custom_vjp
def rope(x, cosm, sinm):
    return x * cosm + _swap16(x) * sinm


def _rope_fwd(x, cosm, sinm):
    return rope(x, cosm, sinm), (cosm, sinm)


def _rope_bwd(res, ct):
    cosm, sinm = res
    return ct * cosm + _swap16(ct * sinm), jnp.zeros_like(cosm), jnp.zeros_like(sinm)


rope.defvjp(_rope_fwd, _rope_bwd)


def mm(a, b, mode, out_dtype=F32):
    return mm_carry(a, b, mode, out_dtype, ())[0]


def mm_carry(a, b, mode, out_dtype=F32, jobs=()):
    if mode == "nt":
        (M, K), N = a.shape, b.shape[0]
    elif mode == "nn":
        (M, K), N = a.shape, b.shape[1]
    else:
        (K, M), N = a.shape, b.shape[1]
    if mode == "nt":
        tm, tn, tk = _tile(M, 1024), _tile(N, 768), _tile(K, 2048)
    elif mode == "nn":
        tm, tn, tk = _tile(M, 1024), _tile(N, 2048), _tile(K, 1408)
    else:
        tm, tn = _tile(M, 1408), _tile(N, 2048)
        tk = _tile(K, 1024 if a.dtype == BF16 and b.dtype == BF16 else 512)
    nk = K // tk
    if mode == "nt":
        a_spec = pl.BlockSpec((tm, tk), lambda i, j, k: (i, k))
        b_spec = pl.BlockSpec((tn, tk), lambda i, j, k: (j, k))
    elif mode == "nn":
        a_spec = pl.BlockSpec((tm, tk), lambda i, j, k: (i, k))
        b_spec = pl.BlockSpec((tk, tn), lambda i, j, k: (k, j))
    else:
        a_spec = pl.BlockSpec((tk, tm), lambda i, j, k: (k, i))
        b_spec = pl.BlockSpec((tk, tn), lambda i, j, k: (k, j))

    def body(a_ref, b_ref, o_ref, *scr):
        p = _raw_dot(a_ref[...], b_ref[...], mode)
        if nk == 1:
            o_ref[...] = p.astype(o_ref.dtype)
        else:
            acc = scr[0]
            k = pl.program_id(2)

            @pl.when(k == 0)
            def _():
                acc[...] = p

            @pl.when(k > 0)
            def _():
                acc[...] += p

            @pl.when(k == nk - 1)
            def _():
                o_ref[...] = acc[...].astype(o_ref.dtype)

    grid = (M // tm, N // tn, nk)
    res = carried_call(
        body, jobs, grid, name=f"mm_{mode}_{M}x{N}x{K}" + ("_c" if jobs else ""),
        out_shape=[jax.ShapeDtypeStruct((M, N), out_dtype)], in_specs=[a_spec, b_spec],
        out_specs=[pl.BlockSpec((tm, tn), lambda i, j, k: (i, j))],
        scratch_shapes=[] if nk == 1 else [pltpu.VMEM((tm, tn), F32)], operands=(a, b))
    return res[0][0], res[1]


@jax.custom_vjp
def lin_t(x, wt):
    return mm(x, wt, "nt")


def _lin_t_fwd(x, wt):
    return mm(x, wt, "nt"), (x, wt)


def _lin_t_bwd(res, dy):
    x, wt = res
    return mm(dy, wt, "nn"), mm(dy, x, "tn", out_dtype=wt.dtype)


lin_t.defvjp(_lin_t_fwd, _lin_t_bwd)


@jax.custom_vjp
def lin_n(x, w):
    return mm(x, w, "nn")


def _lin_n_fwd(x, w):
    return mm(x, w, "nn"), (x, w)


def _lin_n_bwd(res, dy):
    x, w = res
    return mm(dy, w, "nt"), mm(x, dy, "tn", out_dtype=w.dtype)


lin_n.defvjp(_lin_n_fwd, _lin_n_bwd)


def _full_spec(shape):
    nd = len(shape)
    return pl.BlockSpec(tuple(shape), lambda i: (0,) * nd)


def _row_spec(tr, c):
    return pl.BlockSpec((tr, c), lambda i: (i, 0))


def _rowop_tr(total_cols, T):
    budget = 20 * 1024 * 1024
    for tr in (512, 256, 128, 64, 32, 16, 8):
        if T % tr == 0 and total_cols * tr * 4 * 2 <= budget:
            return tr
    return 8


def rowop(fn, name, rows, consts, params, out_cols):
    nr, nc, npar, nout = len(rows), len(consts), len(params), len(out_cols)
    T = rows[0].shape[0]
    in_cols = [r.shape[1] for r in rows] + [c.shape[1] for c in consts]
    tr_f = _rowop_tr(sum(in_cols) + sum(out_cols), T)
    tr_b = _rowop_tr(sum(in_cols) + sum(out_cols) + sum(r.shape[1] for r in rows), T)

    def fwd_call(rows, consts, params):
        def body(*refs):
            ins = [r[...] for r in refs[:nr + nc + npar]]
            outs = fn(*ins)
            for o_ref, o in zip(refs[nr + nc + npar:], outs):
                o_ref[...] = o.astype(F32)

        return pl.pallas_call(
            body, name=name + "_fwd", grid=(T // tr_f,),
            out_shape=[jax.ShapeDtypeStruct((T, c), F32) for c in out_cols],
            in_specs=[_row_spec(tr_f, c) for c in in_cols] + [_full_spec(p.shape) for p in params],
            out_specs=[_row_spec(tr_f, c) for c in out_cols],
            compiler_params=_params(dimension_semantics=("arbitrary",)),
        )(*rows, *consts, *params)

    def bwd_call(rows, consts, params, douts):
        def body(*refs):
            i = pl.program_id(0)
            rv = [r[...] for r in refs[:nr]]
            cv = [r[...] for r in refs[nr:nr + nc]]
            pv = [r[...] for r in refs[nr + nc:nr + nc + npar]]
            dv = [r[...] for r in refs[nr + nc + npar:nr + nc + npar + nout]]
            orefs = refs[nr + nc + npar + nout:]
            _, vjp = jax.vjp(lambda rr, pp: tuple(fn(*rr, *cv, *pp)), rv, pv)
            drows, dpars = vjp(tuple(dv))
            for o_ref, g in zip(orefs[:nr], drows):
                o_ref[...] = g.astype(F32)

            @pl.when(i == 0)
            def _():
                for o_ref in orefs[nr:]:
                    o_ref[...] = jnp.zeros(o_ref.shape, F32)

            for o_ref, g in zip(orefs[nr:], dpars):
                o_ref[...] += g.astype(F32)

        res = pl.pallas_call(
            body, name=name + "_bwd", grid=(T // tr_b,),
            out_shape=[jax.ShapeDtypeStruct(r.shape, F32) for r in rows]
            + [jax.ShapeDtypeStruct(p.shape, F32) for p in params],
            in_specs=[_row_spec(tr_b, c) for c in in_cols] + [_full_spec(p.shape) for p in params]
            + [_row_spec(tr_b, c) for c in out_cols],
            out_specs=[_row_spec(tr_b, r.shape[1]) for r in rows] + [_full_spec(p.shape) for p in params],
            compiler_params=_params(dimension_semantics=("arbitrary",)),
        )(*rows, *consts, *params, *douts)
        return tuple(res[:nr]), tuple(res[nr:])

    @jax.custom_vjp
    def op(rows, consts, params):
        return tuple(fwd_call(rows, consts, params))

    def op_fwd(rows, consts, params):
        return tuple(fwd_call(rows, consts, params)), (rows, consts, params)

    def op_bwd(res, douts):
        rows, consts, params = res
        drows, dpars = bwd_call(rows, consts, params, douts)
        return drows, tuple(jnp.zeros_like(c) for c in consts), dpars

    op.defvjp(op_fwd, op_bwd)
    return op(tuple(rows), tuple(consts), tuple(params))


def _fn_norm(x, g):
    return (_rms(x, g),)


def _fn_add_norm(x, y, g):
    s = x + y
    return s, _rms(s, g)


def _fn_swiglu(gu):
    return (jax.nn.silu(gu[:, :FFN_HIDDEN]) * gu[:, FFN_HIDDEN:],)


def _heads(x, n, width=HEAD_DIM):
    return [x[:, h * width:(h + 1) * width] for h in range(n)]


def _fn_sb_pre(qkv, qn, kn):
    hs = _heads(qkv, 3 * SB_HEADS)
    q = jnp.concatenate([_rms(h, qn) for h in hs[:SB_HEADS]], axis=1)
    k = jnp.concatenate([_rms(h, kn) for h in hs[SB_HEADS:2 * SB_HEADS]], axis=1)
    v = jnp.concatenate(hs[2 * SB_HEADS:], axis=1)
    return q, k, v


def _fn_dsw_pre(qkv, cosm, sinm, qn, kn):
    hs = _heads(qkv, 3 * DSW_HEADS)
    q = jnp.concatenate([rope(_rms(h, qn), cosm, sinm) for h in hs[:DSW_HEADS]], axis=1)
    k = jnp.concatenate([rope(_rms(h, kn), cosm, sinm) for h in hs[DSW_HEADS:2 * DSW_HEADS]], axis=1)
    v = jnp.concatenate(hs[2 * DSW_HEADS:], axis=1)
    return q, k, v


def _fn_dsw_combine(o, lse):
    os_, ls_ = _heads(o, DSW_HEADS), _heads(lse, DSW_HEADS)
    out = [None] * DSW_HEADS
    for hg in range(DSW_HG):
        l3 = [ls_[g * DSW_HG + hg] for g in range(3)]
        m = jnp.maximum(jnp.maximum(l3[0], l3[1]), l3[2])
        e3 = [jnp.exp(l - m) for l in l3]
        den = e3[0] + e3[1] + e3[2]
        for g in range(3):
            out[g * DSW_HG + hg] = os_[g * DSW_HG + hg] * (e3[g] / den)
    return (jnp.concatenate(out, axis=1),)


def _l2(x):
    return x * lax.rsqrt(jnp.sum(x * x, axis=-1, keepdims=True) + NORM_EPS)


def _fn_gdn_pre(qkv, ba, a_log, dt_bias):
    x = jax.nn.silu(qkv)
    hs = _heads(x, 2 * GDN_K_HEADS + GDN_V_HEADS)
    rep = GDN_V_HEADS // GDN_K_HEADS
    qh = [_l2(h) * HEAD_DIM ** -0.5 for h in hs[:GDN_K_HEADS]]
    kh = [_l2(h) for h in hs[GDN_K_HEADS:2 * GDN_K_HEADS]]
    q = jnp.concatenate([qh[h // rep] for h in range(GDN_V_HEADS)], axis=1)
    k = jnp.concatenate([kh[h // rep] for h in range(GDN_V_HEADS)], axis=1)
    v = jnp.concatenate(hs[2 * GDN_K_HEADS:], axis=1)
    b = ba[:, :GDN_V_HEADS]
    a = ba[:, GDN_V_HEADS:2 * GDN_V_HEADS]
    beta = jax.nn.sigmoid(b)
    g = -jnp.exp(a_log) * jax.nn.softplus(a + dt_bias)
    rows = b.shape[0]
    beta_b = jnp.concatenate([jnp.broadcast_to(beta[:, h:h + 1], (rows, HEAD_DIM)) for h in range(GDN_V_HEADS)], axis=1)
    g_b = jnp.concatenate([jnp.broadcast_to(g[:, h:h + 1], (rows, HEAD_DIM)) for h in range(GDN_V_HEADS)], axis=1)
    return q, k, v, g_b, beta_b


def _fn_gdn_post(o, z, o_norm):
    os_, zs = _heads(o, GDN_V_HEADS), _heads(z, GDN_V_HEADS)
    return (jnp.concatenate([_rms(oh, o_norm) * jax.nn.silu(zh) for oh, zh in zip(os_, zs)], axis=1),)


def _expm1(x):
    return jnp.tanh(0.5 * x) * (jnp.exp(x) + 1.0)


def _fn_lru_gates(xc, conv_b, w_a, b_a, w_x, b_x, lam):
    xr = xc + conv_b
    xs = _heads(xr, LRU_BLOCKS, LRU_BLOCK_DIM)
    r = jnp.concatenate([bdot(xs[n], w_a[n], "nn") for n in range(LRU_BLOCKS)], axis=1) + b_a
    i = jnp.concatenate([bdot(xs[n], w_x[n], "nn") for n in range(LRU_BLOCKS)], axis=1) + b_x
    r = jax.nn.sigmoid(r)
    i = jax.nn.sigmoid(i)
    log_a = -LRU_C * r * jax.nn.softplus(-lam)
    a = jnp.exp(log_a)
    u = jnp.sqrt(-_expm1(2.0 * log_a)) * (i * xr)
    return a, u


def _fn_lru_out(hs, gate):
    c = math.sqrt(2.0 / math.pi)
    gl = 0.5 * gate * (1.0 + jnp.tanh(c * (gate + 0.044715 * (gate * gate * gate))))
    return (hs * gl,)


def loss_head(x, f, target):
    T, D = x.shape
    tr = _pick(T, (256, 128, 64, 32, 16, 8))

    def body(x_ref, f_ref, t_ref, l_ref, dy_ref):
        i = pl.program_id(0)
        err = (x_ref[...] + f_ref[...]) - t_ref[...]
        dy_ref[...] = err * (1.0 / D)
        part = 0.5 * jnp.sum(jnp.mean(err * err, axis=-1, keepdims=True), axis=0, keepdims=True)

        @pl.when(i == 0)
        def _():
            l_ref[...] = jnp.zeros(l_ref.shape, F32)

        l_ref[...] += jnp.broadcast_to(part, l_ref.shape)

    l, dy = pl.pallas_call(
        body, name="loss_head", grid=(T // tr,),
        out_shape=[jax.ShapeDtypeStruct((8, LANES), F32), jax.ShapeDtypeStruct((T, D), F32)],
        in_specs=[_row_spec(tr, D)] * 3, out_specs=[_full_spec((8, LANES)), _row_spec(tr, D)],
        compiler_params=_params(dimension_semantics=("arbitrary",)),
    )(x, f, target)
    return l[0, 0], dy


def _as_bf16(x):
    return x.astype(BF16).astype(F32)


def _shift_rows(x, s):
    if s == 0:
        return x
    n = x.shape[0]
    row = lax.broadcasted_iota(jnp.int32, x.shape, 0)
    rolled = pltpu.roll(x, s % n, axis=0)
    keep = (row >= s) if s > 0 else (row < n + s)
    return jnp.where(keep, rolled, 0.0)


def _conv_fwd_call(x, w):
    T, C = x.shape
    K = w.shape[0]
    cb = _pick(C, (256, 128))

    def body(x_ref, w_ref, y_ref):
        xv, wv = _as_bf16(x_ref[...]), _as_bf16(w_ref[...])
        acc = xv * wv[K - 1:K, :]
        for k in range(K - 1):
            acc = acc + _shift_rows(xv, K - 1 - k) * wv[k:k + 1, :]
        y_ref[...] = acc

    return pl.pallas_call(
        body, name=f"conv_fwd_{C}", grid=(C // cb,), out_shape=jax.ShapeDtypeStruct((T, C), F32),
        in_specs=[pl.BlockSpec((T, cb), lambda j: (0, j)), pl.BlockSpec((K, cb), lambda j: (0, j))],
        out_specs=pl.BlockSpec((T, cb), lambda j: (0, j)),
        compiler_params=_params(dimension_semantics=("arbitrary",)),
    )(x, w)


def _conv_bwd_call(x, w, dy):
    T, C = x.shape
    K = w.shape[0]
    cb = _pick(C, (256, 128))

    def body(x_ref, w_ref, dy_ref, dx_ref, dw_ref):
        xv, dv, wv = _as_bf16(x_ref[...]), _as_bf16(dy_ref[...]), _as_bf16(w_ref[...])
        acc = dv * wv[K - 1:K, :]
        rows = [None] * K
        rows[K - 1] = jnp.sum(dv * xv, axis=0, keepdims=True)
        for k in range(K - 1):
            s = K - 1 - k
            acc = acc + _shift_rows(dv, -s) * wv[k:k + 1, :]
            rows[k] = jnp.sum(dv * _shift_rows(xv, s), axis=0, keepdims=True)
        dx_ref[...] = acc
        dw_ref[...] = jnp.concatenate(rows + [jnp.zeros((8 - K, cb), F32)], axis=0)

    dx, dw = pl.pallas_call(
        body, name=f"conv_bwd_{C}", grid=(C // cb,),
        out_shape=[jax.ShapeDtypeStruct((T, C), F32), jax.ShapeDtypeStruct((8, C), F32)],
        in_specs=[pl.BlockSpec((T, cb), lambda j: (0, j)), pl.BlockSpec((K, cb), lambda j: (0, j)),
                  pl.BlockSpec((T, cb), lambda j: (0, j))],
        out_specs=[pl.BlockSpec((T, cb), lambda j: (0, j)), pl.BlockSpec((8, cb), lambda j: (0, j))],
        compiler_params=_params(dimension_semantics=("arbitrary",)),
    )(x, w, dy)
    return dx, dw[:K]


@jax.custom_vjp
def dwconv(x, w):
    return _conv_fwd_call(x, w)


def _dwconv_fwd(x, w):
    return _conv_fwd_call(x, w), (x, w)


def _dwconv_bwd(res, dy):
    return _conv_bwd_call(*res, dy)


dwconv.defvjp(_dwconv_fwd, _dwconv_bwd)


def _scan_fwd_call(a, u):
    T, C = a.shape
    cb = _pick(C, (256, 128))

    def body(a_ref, u_ref, h_ref):
        def step(i, h):
            r = pl.multiple_of(i * 8, 8)
            at, ut = a_ref[pl.ds(r, 8), :], u_ref[pl.ds(r, 8), :]
            rows = []
            for j in range(8):
                h = at[j:j + 1, :] * h + ut[j:j + 1, :]
                rows.append(h)
            h_ref[pl.ds(r, 8), :] = jnp.concatenate(rows, axis=0)
            return h

        lax.fori_loop(0, T // 8, step, jnp.zeros((1, cb), F32))

    return pl.pallas_call(
        body, name="lru_scan_fwd", grid=(C // cb,), out_shape=jax.ShapeDtypeStruct((T, C), F32),
        in_specs=[pl.BlockSpec((T, cb), lambda j: (0, j))] * 2, out_specs=pl.BlockSpec((T, cb), lambda j: (0, j)),
        compiler_params=_params(dimension_semantics=("arbitrary",)),
    )(a, u)


def _scan_bwd_call(a, hs, dh):
    T, C = a.shape
    cb = _pick(C, (256, 128))
    nt = T // 8

    def body(a_ref, h_ref, dh_ref, da_ref, du_ref):
        def step(s, carry):
            i = nt - 1 - s
            r = pl.multiple_of(i * 8, 8)
            rp = pl.multiple_of(jnp.maximum(i - 1, 0) * 8, 8)
            at, ht, dt = a_ref[pl.ds(r, 8), :], h_ref[pl.ds(r, 8), :], dh_ref[pl.ds(r, 8), :]
            hprev_tile = h_ref[pl.ds(rp, 8), :]
            h_before = jnp.where(i > 0, hprev_tile[7:8, :], 0.0)
            da_rows, du_rows = [None] * 8, [None] * 8
            for j in range(7, -1, -1):
                lam = dt[j:j + 1, :] + carry
                du_rows[j] = lam
                hp = ht[j - 1:j, :] if j > 0 else h_before
                da_rows[j] = lam * hp
                carry = at[j:j + 1, :] * lam
            da_ref[pl.ds(r, 8), :] = jnp.concatenate(da_rows, axis=0)
            du_ref[pl.ds(r, 8), :] = jnp.concatenate(du_rows, axis=0)
            return carry

        lax.fori_loop(0, nt, step, jnp.zeros((1, cb), F32))

    return pl.pallas_call(
        body, name="lru_scan_bwd", grid=(C // cb,), out_shape=[jax.ShapeDtypeStruct((T, C), F32)] * 2,
        in_specs=[pl.BlockSpec((T, cb), lambda j: (0, j))] * 3,
        out_specs=[pl.BlockSpec((T, cb), lambda j: (0, j))] * 2,
        compiler_params=_params(dimension_semantics=("arbitrary",)),
    )(a, hs, dh)


@jax.custom_vjp
def lru_scan(a, u):
    return _scan_fwd_call(a, u)


def _lru_scan_fwd(a, u):
    hs = _scan_fwd_call(a, u)
    return hs, (a, hs)


def _lru_scan_bwd(res, dh):
    a, hs = res
    return tuple(_scan_bwd_call(a, hs, dh))


lru_scan.defvjp(_lru_scan_fwd, _lru_scan_bwd)


def _tri(n, kind):
    r = lax.broadcasted_iota(jnp.int32, (n, n), 0)
    c = lax.broadcasted_iota(jnp.int32, (n, n), 1)
    m = {"gt": r > c, "le": r <= c, "lt": r < c, "ge": r >= c, "eq": r == c}[kind]
    return jnp.where(m, 1.0, 0.0).astype(BF16)


def _sb_fwd_call(q, k, v, shards=()):
    T, HD = q.shape
    H = HD // HEAD_DIM
    tb = _pick(T, (SB_BLOCK, 128))
    scale = HEAD_DIM ** -0.5

    def body(q_ref, k_ref, v_ref, o_ref, tot_ref):
        i = pl.program_id(1)
        qb = q_ref[...].astype(BF16)
        u_gt = _tri(tb, "gt")
        row = lax.broadcasted_iota(jnp.int32, (tb, tb), 0)
        col = lax.broadcasted_iota(jnp.int32, (tb, tb), 1)

        def step(j, n, carry, diagonal):
            acc, run = carry
            off = pl.multiple_of(j * tb, tb)
            kb = k_ref[pl.ds(off, n * tb), :].astype(BF16)
            vb = v_ref[pl.ds(off, n * tb), :].astype(BF16)
            z = lax.dot_general(qb, kb, _DN["nt"], preferred_element_type=F32) * scale
            sp, ls, _ = _softplus_parts(z)
            if diagonal:
                sp = jnp.where(col < row, sp, 0.0)
            parts = [None] * n
            for s in reversed(range(n)):
                xs = sp[:, s * tb:(s + 1) * tb]
                parts[s] = _dot01_raw(xs, u_gt, _DN["nn"], False) + run
                run = run + jnp.sum(xs, axis=1, keepdims=True)
            between = parts[0] if n == 1 else jnp.concatenate(parts, axis=1)
            w = jnp.exp(ls - between)
            if diagonal:
                w = jnp.where(col < row, w, 0.0)
            acc = acc + lax.dot_general(w.astype(BF16), vb, _DN["nn"], preferred_element_type=F32)
            return acc, run

        carry = step(i, 1, (jnp.zeros((tb, HEAD_DIM), F32), jnp.zeros((tb, 1), F32)), True)
        left = i
        for n in SB_WIDE:
            carry = lax.fori_loop(0, lax.div(left, n), lambda t, c, n=n, left=left: step(left - n * (t + 1), n, c, False),
                                  carry)
            left = lax.rem(left, n)
        acc, run = carry
        o_ref[...] = acc
        tot_ref[...] = jnp.broadcast_to(run, (tb, HEAD_DIM))

    blk = pl.BlockSpec((tb, HEAD_DIM), lambda h, i: (i, h))
    full = pl.BlockSpec((T, HEAD_DIM), lambda h, i: (0, h))
    (o, tot), gathered = carried_call(
        body, [GatherJob(s) for s in shards], (H, T // tb), name="sb_attn_fwd",
        out_shape=[jax.ShapeDtypeStruct((T, HD), F32)] * 2, in_specs=[blk, full, full], out_specs=[blk, blk],
        scratch_shapes=[], operands=(q, k, v))
    return o, tot, tuple(gathered)


def _sb_bwd_call(q, k, v, tot, do, jobs=()):
    T, HD = q.shape
    H = HD // HEAD_DIM
    tb = _pick(T, (SB_BLOCK, 128))
    scale = HEAD_DIM ** -0.5

    def body(q_ref, k_ref, v_ref, tot_ref, do_ref, dq_ref, dk_ref, dv_ref):
        i = pl.program_id(1)

        @pl.when(i == 0)
        def _():
            dk_ref[...] = jnp.zeros(dk_ref.shape, F32)
            dv_ref[...] = jnp.zeros(dv_ref.shape, F32)

        qb = q_ref[...].astype(BF16)
        dob = do_ref[...].astype(BF16)
        tot = tot_ref[:, 0:1]
        u_le = _tri(tb, "le")
        u_lt = _tri(tb, "lt")
        row = lax.broadcasted_iota(jnp.int32, (tb, tb), 0)
        col = lax.broadcasted_iota(jnp.int32, (tb, tb), 1)

        def prefix_sums(x, u, start):
            out = []
            for s in range(x.shape[1] // tb):
                xs = x[:, s * tb:(s + 1) * tb]
                out.append(_dot01_raw(xs, u, _DN["nn"], False) + start)
                start = start + jnp.sum(xs, axis=1, keepdims=True)
            return (out[0] if len(out) == 1 else jnp.concatenate(out, axis=1)), start

        def step(j, n, carry, diagonal):
            dq, cs, cd = carry
            off = pl.multiple_of(j * tb, tb)
            kb = k_ref[pl.ds(off, n * tb), :].astype(BF16)
            vb = v_ref[pl.ds(off, n * tb), :].astype(BF16)
            z = lax.dot_general(qb, kb, _DN["nt"], preferred_element_type=F32) * scale
            sp, ls, sig = _softplus_parts(z)
            if diagonal:
                sp = jnp.where(col < row, sp, 0.0)
            prefix, cs = prefix_sums(sp, u_le, cs)
            w = jnp.exp(ls - (tot - prefix))
            if diagonal:
                w = jnp.where(col < row, w, 0.0)
            wb = w.astype(BF16)
            dv_ref[pl.ds(off, n * tb), :] += lax.dot_general(wb, dob, _DN["tn"], preferred_element_type=F32)
            dw = lax.dot_general(dob, vb, _DN["nt"], preferred_element_type=F32)
            dl = dw * w
            before, cd = prefix_sums(dl, u_lt, cd)
            dz = (dl * (1.0 - sig) - before * sig) * scale
            if diagonal:
                dz = jnp.where(col < row, dz, 0.0)
            dzb = dz.astype(BF16)
            dq = dq + lax.dot_general(dzb, kb, _DN["nn"], preferred_element_type=F32)
            dk_ref[pl.ds(off, n * tb), :] += lax.dot_general(dzb, qb, _DN["tn"], preferred_element_type=F32)
            return dq, cs, cd

        z1 = jnp.zeros((tb, 1), F32)
        carry, done = (jnp.zeros((tb, HEAD_DIM), F32), z1, z1), 0
        for n in SB_WIDE:
            trips = lax.div(i - done, n)
            carry = lax.fori_loop(0, trips, lambda t, c, n=n, done=done: step(done + n * t, n, c, False), carry)
            done = done + trips * n
        dq, _, _ = step(i, 1, carry, True)
        dq_ref[...] = dq

    blk = pl.BlockSpec((tb, HEAD_DIM), lambda h, i: (i, h))
    full = pl.BlockSpec((T, HEAD_DIM), lambda h, i: (0, h))
    grads, exchanged = carried_call(
        body, jobs, (H, T // tb), name="sb_attn_bwd", out_shape=[jax.ShapeDtypeStruct((T, HD), F32)] * 3,
        in_specs=[blk, full, full, blk, blk], out_specs=[blk, full, full], scratch_shapes=[],
        operands=(q, k, v, tot, do))
    return (*grads, exchanged) if jobs else tuple(grads)


@jax.custom_vjp
def sb_attn(q, k, v, shards):
    o, _, gathered = _sb_fwd_call(q, k, v, shards)
    return o, gathered


def _sb_attn_fwd(q, k, v, shards):
    o, tot, gathered = _sb_fwd_call(q, k, v, shards)
    return (o, gathered), (q, k, v, tot, shards)


def _sb_attn_bwd(res, cts):
    q, k, v, tot, shards = res
    return (*_sb_bwd_call(q, k, v, tot, cts[0]), tuple(jnp.zeros_like(s) for s in shards))


sb_attn.defvjp(_sb_attn_fwd, _sb_attn_bwd)


def _dsw_tile(q, kp, kc, vp, vc, n):
    blk = DSW_BLOCK
    scale = HEAD_DIM ** -0.5
    qi = lax.broadcasted_iota(jnp.int32, (blk, blk), 0)
    kj = lax.broadcasted_iota(jnp.int32, (blk, blk), 1)
    neg = -1e30
    s_p = jnp.where((kj >= qi) & (n > 0), bdot(q, kp, "nt") * scale, neg)
    s_c = jnp.where(kj <= qi, bdot(q, kc, "nt") * scale, neg)
    m = jnp.maximum(jnp.max(s_p, axis=-1, keepdims=True), jnp.max(s_c, axis=-1, keepdims=True))
    p_p, p_c = jnp.exp(s_p - m), jnp.exp(s_c - m)
    den = jnp.sum(p_p, axis=-1, keepdims=True) + jnp.sum(p_c, axis=-1, keepdims=True)
    o = bdot(p_p / den, vp, "nn") + bdot(p_c / den, vc, "nn")
    lse = m + jnp.log(den)
    return o, jnp.broadcast_to(lse, (blk, HEAD_DIM))


DSW_HEADS_PER_STEP_FWD = 6
DSW_HEADS_PER_STEP_BWD = 3


def _dsw_specs(nsub, hps):
    w = hps * HEAD_DIM
    cur = pl.BlockSpec((None, DSW_BLOCK, w), lambda r, h, n: (r, n, h))
    prev = pl.BlockSpec((None, DSW_BLOCK, w), lambda r, h, n: (r, jnp.maximum(n - 1, 0), h))
    whole = pl.BlockSpec((None, nsub, w), lambda r, h, n: (r, 0, h))
    return cur, prev, whole


def _head(ref, h, rows=slice(None)):
    return ref[rows, h * HEAD_DIM:(h + 1) * HEAD_DIM]


def _dsw_fwd_call(q, k, v):
    d, nsub, HD = q.shape
    hps = DSW_HEADS_PER_STEP_FWD
    cur, prev, _ = _dsw_specs(nsub, hps)

    def body(q_ref, kp_ref, kc_ref, vp_ref, vc_ref, o_ref, l_ref):
        n = pl.program_id(2)
        for h in range(hps):
            o, l = _dsw_tile(*[_head(ref, h) for ref in (q_ref, kp_ref, kc_ref, vp_ref, vc_ref)], n)
            o_ref[:, h * HEAD_DIM:(h + 1) * HEAD_DIM] = o
            l_ref[:, h * HEAD_DIM:(h + 1) * HEAD_DIM] = l

    return pl.pallas_call(
        body, name=f"dsw_attn_fwd_d{d}", grid=(d, HD // HEAD_DIM // hps, nsub // DSW_BLOCK),
        out_shape=[jax.ShapeDtypeStruct(q.shape, F32)] * 2,
        in_specs=[cur, prev, cur, prev, cur], out_specs=[cur, cur],
        compiler_params=_params(dimension_semantics=("arbitrary",) * 3),
    )(q, k, k, v, v)


def _dsw_bwd_call(q, k, v, do, dl):
    d, nsub, HD = q.shape
    hps = DSW_HEADS_PER_STEP_BWD
    cur, prev, whole = _dsw_specs(nsub, hps)
    blk = DSW_BLOCK

    def body(q_ref, kp_ref, kc_ref, vp_ref, vc_ref, do_ref, dl_ref, dq_ref, dk_ref, dv_ref):
        n = pl.program_id(2)

        @pl.when(n == 0)
        def _():
            dk_ref[...] = jnp.zeros(dk_ref.shape, F32)
            dv_ref[...] = jnp.zeros(dv_ref.shape, F32)

        c0 = pl.multiple_of(n * blk, blk)
        p0 = pl.multiple_of(jnp.maximum(n - 1, 0) * blk, blk)
        for h in range(hps):
            cols = slice(h * HEAD_DIM, (h + 1) * HEAD_DIM)
            _, vjp = jax.vjp(lambda a, b, c, e, f: _dsw_tile(a, b, c, e, f, n),
                             *[_head(ref, h) for ref in (q_ref, kp_ref, kc_ref, vp_ref, vc_ref)])
            dq, dkp, dkc, dvp, dvc = vjp((_head(do_ref, h), _head(dl_ref, h)))
            dq_ref[:, cols] = dq
            dk_ref[pl.ds(c0, blk), cols] += dkc
            dv_ref[pl.ds(c0, blk), cols] += dvc
            dk_ref[pl.ds(p0, blk), cols] += dkp
            dv_ref[pl.ds(p0, blk), cols] += dvp

    return pl.pallas_call(
        body, name=f"dsw_attn_bwd_d{d}", grid=(d, HD // HEAD_DIM // hps, nsub // blk),
        out_shape=[jax.ShapeDtypeStruct(q.shape, F32)] * 3,
        in_specs=[cur, prev, cur, prev, cur, cur, cur], out_specs=[cur, whole, whole],
        compiler_params=_params(dimension_semantics=("arbitrary",) * 3),
    )(q, k, k, v, v, do, dl)


@jax.custom_vjp
def dsw_attn(q, k, v):
    return tuple(_dsw_fwd_call(q, k, v))


def _dsw_attn_fwd(q, k, v):
    return tuple(_dsw_fwd_call(q, k, v)), (q, k, v)


def _dsw_attn_bwd(res, cts):
    return tuple(_dsw_bwd_call(*res, *cts))


dsw_attn.defvjp(_dsw_attn_fwd, _dsw_attn_bwd)


_LOG2_CHUNK = GDN_CHUNK.bit_length() - 1
_LOG2_HEAD_DIM = HEAD_DIM.bit_length() - 1


def _unit_lower_inverse(a):
    n = a.shape[0]
    r = lax.broadcasted_iota(jnp.int32, (n, n), 0)
    c = lax.broadcasted_iota(jnp.int32, (n, n), 1)
    x = -a
    t = jnp.where(r == c, 1.0, 0.0) + x
    for _ in range(_LOG2_CHUNK - 1):
        x = _mm3(x, x)
        t = t + _mm3(t, x)
    return t


@jax.custom_vjp
def unit_lower_inverse(a):
    return _unit_lower_inverse(a)


def _unit_lower_inverse_fwd(a):
    t = _unit_lower_inverse(a)
    return t, t


def _unit_lower_inverse_bwd(t, ct):
    return (-_mm3(_mm3(t.T, ct), t.T),)


unit_lower_inverse.defvjp(_unit_lower_inverse_fwd, _unit_lower_inverse_bwd)


@jax.custom_vjp
def known_inverse(a, t):
    return t


def _known_inverse_fwd(a, t):
    return t, t


def _known_inverse_bwd(t, ct):
    return _unit_lower_inverse_bwd(t, ct)[0], jnp.zeros_like(t)


known_inverse.defvjp(_known_inverse_fwd, _known_inverse_bwd)


def _gdn_step(state, q, k, v, gb, bb, t_known=None):
    new_state, out, _ = _gdn_step_all(state, q, k, v, gb, bb, t_known)
    return new_state, out


def _gdn_step_all(state, q, k, v, gb, bb, t_known=None):
    C, NH = GDN_CHUNK, GDN_HEADS_PER_STEP
    R = NH * C
    r = lax.broadcasted_iota(jnp.int32, (R, R), 0)
    c = lax.broadcasted_iota(jnp.int32, (R, R), 1)
    same = lax.shift_right_logical(r, _LOG2_CHUNK) == lax.shift_right_logical(c, _LOG2_CHUNK)
    causal, strict = same & (r >= c), same & (r > c)
    gc = dot01(gb, jnp.where(causal, 1.0, 0.0).astype(BF16), True)
    g_sq = jnp.concatenate([gc] * (R // HEAD_DIM), axis=1)
    g_row = dot01(jnp.where(r == c, g_sq, 0.0), jnp.ones((R, R), BF16), True)
    decay = jnp.where(causal, jnp.exp(jnp.where(causal, g_sq - g_row, 0.0)), 0.0)
    kb, vb = k * bb, v * bb
    a_mat = jnp.where(strict, bdot(kb, k, "nt") * decay, 0.0)
    t_mat = unit_lower_inverse(a_mat) if t_known is None else known_inverse(a_mat, t_known)
    uw = bdot(t_mat, jnp.concatenate([vb, kb * jnp.exp(gc)], axis=1), "nn")
    u, w = uw[:, :HEAD_DIM], uw[:, HEAD_DIM:]
    hr = lax.shift_right_logical(lax.broadcasted_iota(jnp.int32, (R, NH * HEAD_DIM), 0), _LOG2_CHUNK)
    hc = lax.shift_right_logical(lax.broadcasted_iota(jnp.int32, (R, NH * HEAD_DIM), 1), _LOG2_HEAD_DIM)

    def widen(m):
        return jnp.where(hr == hc, jnp.concatenate([m] * NH, axis=1), 0.0)

    v_new = u - bdot(widen(w), state, "nn")
    attn = bdot(q, k, "nt") * decay
    out = bdot(widen(q * jnp.exp(gc)), state, "nn") + bdot(attn, v_new, "nn")
    last = [gc[h * C + C - 1:h * C + C, :] for h in range(NH)]
    g_last_rows = jnp.concatenate([jnp.broadcast_to(l, (C, HEAD_DIM)) for l in last], axis=0)
    g_last_state = jnp.concatenate([jnp.broadcast_to(l, (HEAD_DIM, HEAD_DIM)) for l in last], axis=0)
    k_dec = k * jnp.exp(g_last_rows - gc)
    new_state = state * jnp.exp(g_last_state) + bdot(widen(k_dec), v_new, "tn")
    return new_state, out, t_mat


def _gdn_stack(ref, g):
    h0 = g * GDN_HEADS_PER_STEP
    return jnp.concatenate([ref[:, (h0 + h) * HEAD_DIM:(h0 + h + 1) * HEAD_DIM] for h in range(GDN_HEADS_PER_STEP)],
                           axis=0)


def _gdn_unstack(ref, g, val):
    h0 = g * GDN_HEADS_PER_STEP
    for h in range(GDN_HEADS_PER_STEP):
        ref[:, (h0 + h) * HEAD_DIM:(h0 + h + 1) * HEAD_DIM] = val[h * GDN_CHUNK:(h + 1) * GDN_CHUNK]


def _gdn_fwd_call(q, k, v, gb, bb, shards=()):
    T, HD = v.shape
    H = HD // HEAD_DIM
    N = T // GDN_CHUNK
    hb = GDN_HEADS_PER_STEP * GDN_GROUPS_PER_STEP
    W = hb * HEAD_DIM
    SW = GDN_HEADS_PER_STEP * HEAD_DIM

    R = GDN_HEADS_PER_STEP * GDN_CHUNK

    def body(q_ref, k_ref, v_ref, g_ref, b_ref, o_ref, s_ref, t_ref, state):
        n = pl.program_id(1)

        @pl.when(n == 0)
        def _():
            state[...] = jnp.zeros(state.shape, F32)

        s_in = state[...]
        s_ref[...] = s_in
        res = [_gdn_step_all(s_in[g * SW:(g + 1) * SW],
                             *[_gdn_stack(ref, g) for ref in (q_ref, k_ref, v_ref, g_ref, b_ref)])
               for g in range(GDN_GROUPS_PER_STEP)]
        for g, (ns, o, t_mat) in enumerate(res):
            state[g * SW:(g + 1) * SW, :] = ns
            t_ref[g] = t_mat
            _gdn_unstack(o_ref, g, o)

    blk = pl.BlockSpec((GDN_CHUNK, W), lambda h, n: (n, h))
    sblk = pl.BlockSpec((None, W, HEAD_DIM), lambda h, n: (n, h, 0))
    tblk = pl.BlockSpec((None, GDN_GROUPS_PER_STEP, R, R), lambda h, n: (n, h, 0, 0))
    (o, states, inverses), gathered = carried_call(
        body, [GatherJob(s) for s in shards], (H // hb, N), name="gdn_chunk_fwd",
        out_shape=[jax.ShapeDtypeStruct((T, HD), F32), jax.ShapeDtypeStruct((N, H * HEAD_DIM, HEAD_DIM), F32),
                   jax.ShapeDtypeStruct((N, H // GDN_HEADS_PER_STEP, R, R), F32)],
        in_specs=[blk] * 5, out_specs=[blk, sblk, tblk], scratch_shapes=[pltpu.VMEM((W, HEAD_DIM), F32)],
        operands=(q, k, v, gb, bb))
    return o, states, inverses, tuple(gathered)


def _gdn_bwd_call(q, k, v, gb, bb, states, inverses, do):
    T, HD = v.shape
    H = HD // HEAD_DIM
    N = T // GDN_CHUNK
    hb = GDN_HEADS_PER_STEP * GDN_GROUPS_PER_STEP
    W = hb * HEAD_DIM
    SW = GDN_HEADS_PER_STEP * HEAD_DIM

    def body(q_ref, k_ref, v_ref, g_ref, b_ref, s_ref, t_ref, do_ref, dq_ref, dk_ref, dv_ref, dg_ref, db_ref, dstate):
        n = pl.program_id(1)

        @pl.when(n == 0)
        def _():
            dstate[...] = jnp.zeros(dstate.shape, F32)

        res = []
        for g in range(GDN_GROUPS_PER_STEP):
            rows = slice(g * SW, (g + 1) * SW)
            t_known = t_ref[g]
            _, vjp = jax.vjp(lambda *a: _gdn_step(*a, t_known=t_known), s_ref[rows, :],
                             *[_gdn_stack(ref, g) for ref in (q_ref, k_ref, v_ref, g_ref, b_ref)])
            res.append(vjp((dstate[rows, :], _gdn_stack(do_ref, g))))
        for g, (ds, *grads) in enumerate(res):
            dstate[g * SW:(g + 1) * SW, :] = ds
            for ref, grad in zip((dq_ref, dk_ref, dv_ref, dg_ref, db_ref), grads):
                _gdn_unstack(ref, g, grad)

    R = GDN_HEADS_PER_STEP * GDN_CHUNK
    blk = pl.BlockSpec((GDN_CHUNK, W), lambda h, n: (N - 1 - n, h))
    sblk = pl.BlockSpec((None, W, HEAD_DIM), lambda h, n: (N - 1 - n, h, 0))
    tblk = pl.BlockSpec((None, GDN_GROUPS_PER_STEP, R, R), lambda h, n: (N - 1 - n, h, 0, 0))
    return pl.pallas_call(
        body, name="gdn_chunk_bwd", grid=(H // hb, N), out_shape=[jax.ShapeDtypeStruct((T, HD), F32)] * 5,
        in_specs=[blk] * 5 + [sblk, tblk, blk], out_specs=[blk] * 5,
        scratch_shapes=[pltpu.VMEM((W, HEAD_DIM), F32)],
        compiler_params=_params(dimension_semantics=("arbitrary", "arbitrary")),
    )(q, k, v, gb, bb, states, inverses, do)


@jax.custom_vjp
def gdn_core(q, k, v, gb, bb, shards):
    o, _, _, gathered = _gdn_fwd_call(q, k, v, gb, bb, shards)
    return o, gathered


def _gdn_core_fwd(q, k, v, gb, bb, shards):
    o, states, inverses, gathered = _gdn_fwd_call(q, k, v, gb, bb, shards)
    return (o, gathered), (q, k, v, gb, bb, states, inverses, shards)


def _gdn_core_bwd(res, cts):
    *core, shards = res
    return (*_gdn_bwd_call(*core, cts[0]), tuple(jnp.zeros_like(s) for s in shards))


gdn_core.defvjp(_gdn_core_fwd, _gdn_core_bwd)


def _mixer_sb(h, w, shards=()):
    qkv = lin_t(h, w["sb_in_t"])
    q, k, v = rowop(_fn_sb_pre, "sb_pre", [qkv], [], [w["sb_q_norm"], w["sb_k_norm"]], [D_MODEL] * 3)
    o, gathered = sb_attn(q, k, v, tuple(shards))
    return lin_n(o, w["sb_out"]), gathered


def _mixer_gdn(h, w, shards=()):
    wt = w["gdn_in_t"]
    nqkv = 2 * GDN_KEY_DIM + GDN_VAL_DIM
    qkv = lin_t(h, wt[:nqkv])
    z = lin_t(h, wt[nqkv:nqkv + GDN_VAL_DIM])
    w_ba = jnp.pad(wt[nqkv + GDN_VAL_DIM:], ((0, LANES - 2 * GDN_V_HEADS), (0, 0)))
    ba = lin_t(h, w_ba)
    qkv = dwconv(qkv, w["gdn_conv_w"])
    q, k, v, gb, bb = rowop(_fn_gdn_pre, "gdn_pre", [qkv, ba], [], [w["gdn_a_log"], w["gdn_dt_bias"]],
                            [GDN_VAL_DIM] * 5)
    o, gathered = gdn_core(q, k, v, gb, bb, tuple(shards))
    (y,) = rowop(_fn_gdn_post, "gdn_post", [o, z], [], [w["gdn_o_norm"]], [GDN_VAL_DIM])
    return lin_n(y, w["gdn_out"]), gathered


def _to_strided(x, cols, d):
    T = x.shape[0]
    return x[:, cols].reshape(T // d, d, -1).transpose(1, 0, 2)


def _from_strided(x):
    d, n, c = x.shape
    return x.transpose(1, 0, 2).reshape(d * n, c)


def _mixer_dsw(h, cosm, sinm, w):
    qkv = lin_t(h, w["dsw_in_t"])
    nhd = DSW_HEADS * HEAD_DIM
    q, k, v = rowop(_fn_dsw_pre, "dsw_pre", [qkv], [cosm, sinm], [w["dsw_q_norm"], w["dsw_k_norm"]], [nhd] * 3)
    outs, lses = [], []
    for gi, (_, d) in enumerate(DSW_GROUPS):
        cols = slice(gi * DSW_HG * HEAD_DIM, (gi + 1) * DSW_HG * HEAD_DIM)
        o_g, l_g = dsw_attn(_to_strided(q, cols, d), _to_strided(k, cols, d), _to_strided(v, cols, d))
        outs.append(_from_strided(o_g))
        lses.append(_from_strided(l_g))
    (o,) = rowop(_fn_dsw_combine, "dsw_combine", [jnp.concatenate(outs, axis=1), jnp.concatenate(lses, axis=1)],
                 [], [], [nhd])
    return lin_n(o, w["dsw_out"])


def _mixer_lru(h, w):
    wt = w["lru_in_t"]
    gate = lin_t(h, wt[:LRU_WIDTH])
    xr = dwconv(lin_t(h, wt[LRU_WIDTH:]), w["lru_conv_w"])
    a, u = rowop(_fn_lru_gates, "lru_gates", [xr], [],
                 [w["lru_conv_b"], w["lru_w_a"], w["lru_b_a"], w["lru_w_x"], w["lru_b_x"], w["lru_lambda"]],
                 [LRU_WIDTH] * 2)
    hs = lru_scan(a, u)
    (y,) = rowop(_fn_lru_out, "lru_out", [hs, gate], [], [], [LRU_WIDTH])
    return lin_n(y, w["lru_out"])


def mixer_segment(i, x, f_prev, cosm, sinm, w, shards):
    if i == 0:
        (h,) = rowop(_fn_norm, "norm", [x], [], [w["mix_norm"]], [D_MODEL])
    else:
        x, h = rowop(_fn_add_norm, "add_norm", [x, f_prev], [], [w["mix_norm"]], [D_MODEL] * 2)
    kind, gathered = i % 4, ()
    if kind == 0:
        y, gathered = _mixer_sb(h, w, shards)
    elif kind == 1:
        y, gathered = _mixer_gdn(h, w, shards)
    elif kind == 2:
        y = _mixer_dsw(h, cosm, sinm, w)
    else:
        y = _mixer_lru(h, w)
    x, h = rowop(_fn_add_norm, "add_norm", [x, y], [], [w["ffn_norm"]], [D_MODEL] * 2)
    return (x, h), gathered


def sb_segment_in(x, w):
    (h,) = rowop(_fn_norm, "norm", [x], [], [w["mix_norm"]], [D_MODEL])
    qkv = lin_t(h, w["sb_in_t"])
    return rowop(_fn_sb_pre, "sb_pre", [qkv], [], [w["sb_q_norm"], w["sb_k_norm"]], [D_MODEL] * 3)


def sb_segment_out(x, o, w):
    return rowop(_fn_add_norm, "add_norm", [x, lin_n(o, w["sb_out"])], [], [w["ffn_norm"]], [D_MODEL] * 2)


def swiglu_act(gu):
    return rowop(_fn_swiglu, "swiglu", [gu], [], [], [FFN_HIDDEN])[0]


def swiglu_fwd(gu):
    T, F2 = gu.shape
    F = F2 // 2
    tr = _pick(T, (128, 64, 32, 16))

    def body(gu_ref, o_ref):
        o_ref[...] = (jax.nn.silu(gu_ref[:, :F]) * gu_ref[:, F:]).astype(o_ref.dtype)

    return pl.pallas_call(
        body, name="swiglu_fwd", grid=(T // tr,), out_shape=jax.ShapeDtypeStruct((T, F), BF16),
        in_specs=[_row_spec(tr, F2)], out_specs=_row_spec(tr, F),
        compiler_params=_params(dimension_semantics=("arbitrary",)),
    )(gu)


def swiglu_bwd(gu, dact):
    T, F2 = gu.shape
    F = F2 // 2
    tr = _pick(T, (128, 64, 32, 16))

    def body(gu_ref, d_ref, o_ref):
        g, up, d = gu_ref[:, :F], gu_ref[:, F:], d_ref[...]
        s = jax.nn.sigmoid(g)
        o_ref[:, :F] = (d * up * (s * (1.0 + g * (1.0 - s)))).astype(o_ref.dtype)
        o_ref[:, F:] = (d * (g * s)).astype(o_ref.dtype)

    return pl.pallas_call(
        body, name="swiglu_bwd", grid=(T // tr,), out_shape=jax.ShapeDtypeStruct((T, F2), BF16),
        in_specs=[_row_spec(tr, F2), _row_spec(tr, F)], out_specs=_row_spec(tr, F2),
        compiler_params=_params(dimension_semantics=("arbitrary",)),
    )(gu, dact)


ANY = pl.BlockSpec(memory_space=pl.ANY)


SLAB_BYTES = 4 * 1024 * 1024


def _col_tile(rows, C, itemsize):
    return _tile(C, max(LANES, SLAB_BYTES // (rows * itemsize)))


class GatherJob:
    def __init__(self, shard):
        self.operand = shard
        r, C = shard.shape
        self.out_shape = jax.ShapeDtypeStruct((N_DEV, r, C), shard.dtype)
        self.scratch = [pltpu.SemaphoreType.DMA((7,)), pltpu.SemaphoreType.DMA((7,)), pltpu.SemaphoreType.DMA]

    def _parts(self, x_ref, out_ref, send_sems, recv_sems, local_sem):
        x, y, c = lax.axis_index("x"), lax.axis_index("y"), lax.axis_index("c")
        me, sibling = (x, y, c), (x, y, 1 - c)
        chips = [(1 - x, y), (x, 1 - y), (1 - x, 1 - y)]

        def slot(px, py, pc):
            return out_ref.at[4 * px + 2 * py + pc]

        def copy(k, block, to, src=None):
            return pltpu.make_async_remote_copy(
                src_ref=slot(*block) if src is None else src, dst_ref=slot(*block),
                send_sem=send_sems.at[k], recv_sem=recv_sems.at[k], device_id=to, device_id_type=MESH)

        def mine():
            return pltpu.make_async_copy(x_ref, slot(*me), local_sem)

        def first():
            return [copy(0, me, sibling, src=x_ref)] + [copy(1 + j, me, (*chip, c), src=x_ref)
                                                        for j, chip in enumerate(chips)]

        def passed():
            return [copy(4 + j, (*chip, c), sibling) for j, chip in enumerate(chips)]

        def landed():
            return [copy(1 + j, (*chip, c), me) for j, chip in enumerate(chips)]

        def from_sibling():
            return [copy(0, sibling, me)] + [copy(4 + j, (*chip, 1 - c), me) for j, chip in enumerate(chips)]

        return mine, first, passed, landed, from_sibling

    def start(self, x_ref, out_ref, *sems):
        mine, first, _, _, _ = self._parts(x_ref, out_ref, *sems)
        mine().start()
        for cp in first():
            cp.start()

    def finish(self, x_ref, out_ref, *sems):
        mine, first, passed, landed, from_sibling = self._parts(x_ref, out_ref, *sems)
        onward = passed()
        for arrived, cp in zip(landed(), onward):
            arrived.wait_recv()
            cp.start()
        for cp in from_sibling():
            cp.wait_recv()
        for cp in first() + onward:
            cp.wait_send()
        mine().wait()


class PairJob:
    def __init__(self, g):
        self.operand = g
        _, _, r, C = g.shape
        self.out_shape = jax.ShapeDtypeStruct((N_CHIPS, r, C), g.dtype)
        self.scratch = [pltpu.SemaphoreType.DMA((N_CHIPS,)), pltpu.SemaphoreType.DMA((N_CHIPS,))]

    def _copies(self, g_ref, out_ref, send_sems, recv_sems):
        x, y, c = lax.axis_index("x"), lax.axis_index("y"), lax.axis_index("c")
        return [pltpu.make_async_remote_copy(
            src_ref=g_ref.at[q, 1 - c], dst_ref=out_ref.at[q], send_sem=send_sems.at[q], recv_sem=recv_sems.at[q],
            device_id=(x, y, 1 - c), device_id_type=MESH) for q in range(N_CHIPS)]

    def start(self, *refs):
        for cp in self._copies(*refs):
            cp.start()

    def finish(self, *refs):
        for cp in self._copies(*refs):
            cp.wait()


class ChipsJob:
    def __init__(self, p):
        self.operand = p
        self.out_shape = jax.ShapeDtypeStruct(p.shape, p.dtype)
        self.scratch = [pltpu.SemaphoreType.DMA((3,)), pltpu.SemaphoreType.DMA((3,)), pltpu.SemaphoreType.DMA]

    def _parts(self, p_ref, out_ref, send_sems, recv_sems, local_sem):
        x, y, c = lax.axis_index("x"), lax.axis_index("y"), lax.axis_index("c")
        mychip = 2 * x + y
        chips = [(1 - x, y), (x, 1 - y), (1 - x, 1 - y)]
        def mine():
            return pltpu.make_async_copy(p_ref.at[mychip], out_ref.at[mychip], local_sem)

        def sends():
            return [pltpu.make_async_remote_copy(
                src_ref=p_ref.at[2 * cx + cy], dst_ref=out_ref.at[mychip], send_sem=send_sems.at[j],
                recv_sem=recv_sems.at[j], device_id=(cx, cy, c), device_id_type=MESH)
                for j, (cx, cy) in enumerate(chips)]

        def arrivals():
            return [pltpu.make_async_remote_copy(
                src_ref=p_ref.at[mychip], dst_ref=out_ref.at[2 * cx + cy], send_sem=send_sems.at[j],
                recv_sem=recv_sems.at[j], device_id=(cx, cy, c), device_id_type=MESH)
                for j, (cx, cy) in enumerate(chips)]

        return mine, sends, arrivals

    def start(self, *refs):
        mine, sends, _ = self._parts(*refs)
        mine().start()
        for cp in sends():
            cp.start()

    def finish(self, *refs):
        mine, sends, arrivals = self._parts(*refs)
        for cp in arrivals():
            cp.wait_recv()
        for cp in sends():
            cp.wait_send()
        mine().wait()


def carried_call(body, jobs, grid, *, name, out_shape, in_specs, out_specs, scratch_shapes, operands):
    jobs = list(jobs)
    n_in, n_out, n_scr, nj = len(in_specs), len(out_specs), len(scratch_shapes), len(jobs)
    n_sem = [len(j.scratch) for j in jobs]

    def full_body(*refs):
        core_in = refs[:n_in]
        job_in = refs[n_in:n_in + nj]
        core_out = refs[n_in + nj:n_in + nj + n_out]
        job_out = refs[n_in + nj + n_out:n_in + 2 * nj + n_out]
        rest = refs[n_in + 2 * nj + n_out:]
        core_scr, sems, pos = rest[:n_scr], [], n_scr
        for n in n_sem:
            sems.append(rest[pos:pos + n])
            pos += n
        ids = [pl.program_id(a) for a in range(len(grid))]
        first = functools.reduce(jnp.logical_and, [i == 0 for i in ids])
        last = functools.reduce(jnp.logical_and, [i == g - 1 for i, g in zip(ids, grid)])
        if jobs:
            @pl.when(first)
            def _():
                for j, job in enumerate(jobs):
                    job.start(job_in[j], job_out[j], *sems[j])

        body(*core_in, *core_out, *core_scr)
        if jobs:
            @pl.when(last)
            def _():
                for j, job in enumerate(jobs):
                    job.finish(job_in[j], job_out[j], *sems[j])

    res = pl.pallas_call(
        full_body, name=name, grid=grid,
        out_shape=list(out_shape) + [j.out_shape for j in jobs],
        in_specs=list(in_specs) + [ANY] * nj, out_specs=list(out_specs) + [ANY] * nj,
        scratch_shapes=list(scratch_shapes) + [s for j in jobs for s in j.scratch],
        compiler_params=_params(dimension_semantics=("arbitrary",) * len(grid)),
    )(*operands, *[j.operand for j in jobs])
    return res[:n_out], res[n_out:]


def _exchange(job, name):
    def body(*refs):
        job.start(*refs)
        job.finish(*refs)

    return pl.pallas_call(
        body, name=name, out_shape=job.out_shape, in_specs=[ANY], out_specs=ANY, scratch_shapes=job.scratch,
        compiler_params=pltpu.CompilerParams(has_side_effects=True),
    )(job.operand)


def all_gather(shard, name):
    return _exchange(GatherJob(shard), name)


def exchange_pair(g, name):
    return _exchange(PairJob(g), name)


def exchange_chips(p, name):
    return _exchange(ChipsJob(p), name)


def pair_add(g, got, name):
    _, _, r, C = g.shape
    cb = _col_tile(r, C, g.dtype.itemsize)
    c = lax.axis_index("c")

    def body(c_ref, a_ref, b_ref, o_ref):
        o_ref[...] = (a_ref[...].astype(F32) + b_ref[...].astype(F32)).astype(o_ref.dtype)

    return pl.pallas_call(
        body, name=name, out_shape=jax.ShapeDtypeStruct((N_CHIPS, r, C), g.dtype),
        grid_spec=pltpu.PrefetchScalarGridSpec(
            num_scalar_prefetch=1, grid=(N_CHIPS, C // cb),
            in_specs=[pl.BlockSpec((None, None, r, cb), lambda q, j, cr: (q, cr[0], 0, j)),
                      pl.BlockSpec((None, r, cb), lambda q, j, cr: (q, 0, j))],
            out_specs=pl.BlockSpec((None, r, cb), lambda q, j, cr: (q, 0, j))),
        compiler_params=_params(dimension_semantics=("arbitrary", "arbitrary")),
    )(jnp.reshape(c, (1,)).astype(jnp.int32), g, got)


def sum_slots(parts, name):
    n, r, C = parts.shape
    cb = _col_tile(n * r, C, parts.dtype.itemsize)

    def body(p_ref, o_ref):
        acc = p_ref[0].astype(F32)
        for q in range(1, n):
            acc = acc + p_ref[q].astype(F32)
        o_ref[...] = acc

    return pl.pallas_call(
        body, name=name, out_shape=jax.ShapeDtypeStruct((r, C), F32), grid=(C // cb,),
        in_specs=[pl.BlockSpec((n, r, cb), lambda j: (0, 0, j))], out_specs=pl.BlockSpec((r, cb), lambda j: (0, j)),
        compiler_params=_params(dimension_semantics=("arbitrary",)),
    )(parts)


def reduce_scatter(g, tag):
    _, r, C = g.shape
    g4 = g.reshape(N_CHIPS, 2, r, C)
    got = exchange_pair(g4, f"rs_pair_{tag}")
    pairs = pair_add(g4, got, f"rs_pair_add_{tag}")
    parts = exchange_chips(pairs, f"rs_chips_{tag}")
    return sum_slots(parts, f"rs_sum_{tag}")


def all_reduce(v, tag):
    return sum_slots(all_gather(v, f"ar_gather_{tag}"), f"ar_sum_{tag}")


def adamw(w, g, m, v, name):
    R, C = w.shape
    tr = R
    for cand in (512, 256, 128, 64, 32, 16, 8):
        if R % cand == 0 and cand * C * 4 * 7 * 2 <= 40 * 1024 * 1024:
            tr = cand
            break
    c1 = 1.0 - ADAM_B1 ** ADAM_STEP
    c2 = 1.0 - ADAM_B2 ** ADAM_STEP

    def body(w_ref, g_ref, m_ref, v_ref, d_ref, nm_ref, nv_ref):
        gv = g_ref[...]
        nm = ADAM_B1 * m_ref[...] + (1.0 - ADAM_B1) * gv
        nv = ADAM_B2 * v_ref[...] + (1.0 - ADAM_B2) * (gv * gv)
        d_ref[...] = -ADAM_LR * ((nm / c1) / (jnp.sqrt(nv / c2) + ADAM_EPS) + ADAM_WD * w_ref[...])
        nm_ref[...] = nm
        nv_ref[...] = nv

    spec = pl.BlockSpec((tr, C), lambda i: (i, 0))
    return pl.pallas_call(
        body, name=name, out_shape=[jax.ShapeDtypeStruct((R, C), F32)] * 3, grid=(R // tr,),
        in_specs=[spec] * 4, out_specs=[spec] * 3,
        compiler_params=_params(dimension_semantics=("arbitrary",)),
    )(w, g, m, v)


NAMES = ['mix_norm', 'ffn_norm', 'ffn_w_gu', 'ffn_w_down', 'sb_w_in', 'sb_q_norm', 'sb_k_norm', 'sb_w_out',
         'gdn_w_in', 'gdn_conv_w', 'gdn_a_log', 'gdn_dt_bias', 'gdn_o_norm', 'gdn_w_out', 'dsw_w_in', 'dsw_q_norm',
         'dsw_k_norm', 'dsw_w_out', 'lru_w_in', 'lru_conv_w', 'lru_conv_b', 'lru_w_a', 'lru_b_a', 'lru_w_x',
         'lru_b_x', 'lru_lambda', 'lru_w_out']
REPLICATED = ['mix_norm', 'ffn_norm', 'sb_q_norm', 'sb_k_norm', 'gdn_a_log', 'gdn_dt_bias', 'gdn_o_norm',
              'dsw_q_norm', 'dsw_k_norm']
SMALL_SHARDED = ['gdn_conv_w', 'lru_conv_w', 'lru_conv_b', 'lru_b_a', 'lru_b_x', 'lru_lambda']
IN_T = ['sb_w_in', 'gdn_w_in', 'dsw_w_in', 'lru_w_in']
OUT_N = ['sb_w_out', 'gdn_w_out', 'dsw_w_out', 'lru_w_out']


def _pack(arrs, pad_rows_to=8):
    flat = jnp.concatenate([a.reshape(-1) for a in arrs])
    n = flat.shape[0]
    rows = -(-n // LANES)
    rows = -(-rows // pad_rows_to) * pad_rows_to
    return jnp.pad(flat, (0, rows * LANES - n)).reshape(rows, LANES)


def _unpack(buf, shapes):
    flat = buf.reshape(-1)
    out, o = [], 0
    for s in shapes:
        n = math.prod(s)
        out.append(flat[o:o + n].reshape(s))
        o += n
    return out


def kernel(x, positions, mix_norm, ffn_norm, ffn_w_gu, ffn_w_down, sb_w_in, sb_q_norm, sb_k_norm, sb_w_out, gdn_w_in, gdn_conv_w, gdn_a_log, gdn_dt_bias, gdn_o_norm, gdn_w_out, dsw_w_in, dsw_q_norm, dsw_k_norm, dsw_w_out, lru_w_in, lru_conv_w, lru_conv_b, lru_w_a, lru_b_a, lru_w_x, lru_b_x, lru_lambda, lru_w_out, loss_target, m_mix_norm, m_ffn_norm, m_ffn_w_gu, m_ffn_w_down, m_sb_w_in, m_sb_q_norm, m_sb_k_norm, m_sb_w_out, m_gdn_w_in, m_gdn_conv_w, m_gdn_a_log, m_gdn_dt_bias, m_gdn_o_norm, m_gdn_w_out, m_dsw_w_in, m_dsw_q_norm, m_dsw_k_norm, m_dsw_w_out, m_lru_w_in, m_lru_conv_w, m_lru_conv_b, m_lru_w_a, m_lru_b_a, m_lru_w_x, m_lru_b_x, m_lru_lambda, m_lru_w_out, v_mix_norm, v_ffn_norm, v_ffn_w_gu, v_ffn_w_down, v_sb_w_in, v_sb_q_norm, v_sb_k_norm, v_sb_w_out, v_gdn_w_in, v_gdn_conv_w, v_gdn_a_log, v_gdn_dt_bias, v_gdn_o_norm, v_gdn_w_out, v_dsw_w_in, v_dsw_q_norm, v_dsw_k_norm, v_dsw_w_out, v_lru_w_in, v_lru_conv_w, v_lru_conv_b, v_lru_w_a, v_lru_b_a, v_lru_w_x, v_lru_b_x, v_lru_lambda, v_lru_w_out):
    args = locals()
    W = {n: args[n] for n in NAMES}
    M = {n: args["m_" + n] for n in NAMES}
    V = {n: args["v_" + n] for n in NAMES}
    T = x.shape[1]
    x2 = x[0]
    tgt = loss_target[0]

    S = {}
    for n in IN_T:
        S[n] = W[n][0].T.astype(BF16)
    for n in OUT_N:
        S[n] = W[n][0].astype(BF16)
    for i in range(DEPTH):
        S[f"ffn_gu{i}"] = ffn_w_gu[i].T.astype(BF16)
        S[f"ffn_down{i}"] = ffn_w_down[i].astype(BF16)
    S["lru_gates"] = jnp.concatenate(
        [lru_w_a[0].reshape(-1, LRU_BLOCK_DIM), lru_w_x[0].reshape(-1, LRU_BLOCK_DIM)], axis=0).astype(BF16)
    Gt = {}

    def flat(key):
        return Gt[key].reshape(-1, Gt[key].shape[-1])

    Gt["sb_w_in"] = all_gather(S["sb_w_in"], "ag_sb_w_in")
    Gt["sb_w_out"] = all_gather(S["sb_w_out"], "ag_sb_w_out")
    small_shapes = [W[n].shape for n in SMALL_SHARDED]
    sm = all_gather(_pack([W[n] for n in SMALL_SHARDED]), "ag_small")
    sm = [jnp.stack(parts) for parts in zip(*[_unpack(sm[p], small_shapes) for p in range(N_DEV)])]
    smd = dict(zip(SMALL_SHARDED, sm))

    def mixer_weights(i):
        kind = i % 4
        w = {"mix_norm": mix_norm[i:i + 1], "ffn_norm": ffn_norm[i:i + 1]}
        if kind == 0:
            w.update(sb_in_t=flat("sb_w_in"), sb_out=flat("sb_w_out"), sb_q_norm=sb_q_norm, sb_k_norm=sb_k_norm)
        elif kind == 1:
            w.update(gdn_in_t=flat("gdn_w_in"), gdn_out=flat("gdn_w_out"), gdn_a_log=gdn_a_log,
                     gdn_dt_bias=gdn_dt_bias, gdn_o_norm=gdn_o_norm,
                     gdn_conv_w=smd["gdn_conv_w"][:, 0].transpose(1, 0, 2).reshape(4, -1))
        elif kind == 2:
            w.update(dsw_in_t=flat("dsw_w_in"), dsw_out=flat("dsw_w_out"), dsw_q_norm=dsw_q_norm,
                     dsw_k_norm=dsw_k_norm)
        else:
            gg = Gt["lru_gates"].reshape(N_DEV, 2, LRU_BLOCKS, 32, LRU_BLOCK_DIM)
            gg = gg.transpose(1, 2, 0, 3, 4).reshape(2, LRU_BLOCKS, LRU_BLOCK_DIM, LRU_BLOCK_DIM).astype(F32)
            w.update(lru_in_t=flat("lru_w_in"), lru_out=flat("lru_w_out"), lru_w_a=gg[0], lru_w_x=gg[1],
                     lru_conv_w=smd["lru_conv_w"][:, 0].transpose(1, 0, 2).reshape(4, -1),
                     lru_conv_b=smd["lru_conv_b"][:, 0].reshape(1, -1),
                     lru_lambda=smd["lru_lambda"][:, 0].reshape(1, -1),
                     lru_b_a=smd["lru_b_a"][:, 0].transpose(1, 0, 2).reshape(1, -1),
                     lru_b_x=smd["lru_b_x"][:, 0].transpose(1, 0, 2).reshape(1, -1))
        return w

    carried_by_mixer = {0: ["ffn_gu0", "ffn_down0", "gdn_w_in", "gdn_w_out", "ffn_gu1", "ffn_down1"],
                        1: ["ffn_gu2", "ffn_down2", "lru_w_in", "lru_gates", "ffn_gu3", "ffn_down3"]}
    carried_by_ffn = {0: (["dsw_w_in"], ["dsw_w_out", "lru_w_out"])}

    half = ROPE_DIM // 2
    inv_freq = ROPE_THETA ** (-jnp.arange(half, dtype=F32) / half)
    ang = positions[0].astype(F32)[:, None] * inv_freq
    cs, sn = jnp.cos(ang), jnp.sin(ang)
    cosm = jnp.concatenate([cs, cs, jnp.ones((T, HEAD_DIM - ROPE_DIM), F32)], axis=1)
    sinm = jnp.concatenate([-sn, sn, jnp.zeros((T, HEAD_DIM - ROPE_DIM), F32)], axis=1)

    def gather_jobs(keys):
        return [GatherJob(S[k]) for k in keys]

    xs, f, seg_vjps, ffn_res = x2, None, [], []
    for i in range(DEPTH):
        keys = carried_by_mixer.get(i, [])
        shards = tuple(S[k] for k in keys)
        if i == 0:
            w0 = mixer_weights(0)
            qkv0, vjp_in = jax.vjp(sb_segment_in, xs, {k: w0[k] for k in ("mix_norm", "sb_in_t", "sb_q_norm", "sb_k_norm")})
            o0, tot0, gathered = _sb_fwd_call(*qkv0, shards)
            (xs, h), vjp_i = jax.vjp(sb_segment_out, xs, o0, {k: w0[k] for k in ("sb_out", "ffn_norm")})
        else:
            (xs, h), vjp_i, gathered = jax.vjp(
                lambda xx, ff, ww, i=i, shards=shards: mixer_segment(i, xx, ff, cosm, sinm, ww, shards),
                xs, f, mixer_weights(i), has_aux=True)
        Gt.update(zip(keys, gathered))
        seg_vjps.append(vjp_i)
        keys_gu, keys_down = carried_by_ffn.get(i, ([], []))
        h = h.astype(BF16)
        gu, got = mm_carry(h, flat(f"ffn_gu{i}"), "nt", F32, gather_jobs(keys_gu))
        Gt.update(zip(keys_gu, got))
        act = swiglu_fwd(gu)
        f, got = mm_carry(act, flat(f"ffn_down{i}"), "nn", F32, gather_jobs(keys_down))
        Gt.update(zip(keys_down, got))
        ffn_res.append((h, act, gu))
    loss_part, dy = loss_head(xs, f, tgt)
    loss = lax.psum(loss_part, ("x", "y", "c"))

    reduced, gw_small, gw_rep = {}, {}, {}

    def to_partials(i, dw, dwd, dwgu):
        out = {}
        if dwgu is not None:
            out = {f"ffn_gu{i}": dwgu.reshape(N_DEV, -1, D_MODEL), f"ffn_down{i}": dwd.reshape(N_DEV, -1, D_MODEL)}
        for k, g in dw.items():
            if k.endswith("_in_t"):
                out[k.replace("_in_t", "_w_in")] = g.reshape(N_DEV, -1, D_MODEL)
            elif k.endswith("_out"):
                out[k.replace("_out", "_w_out")] = g.reshape(N_DEV, -1, D_MODEL)
            elif k in ("mix_norm", "ffn_norm"):
                gw_rep[(k, i)] = g
            elif k in REPLICATED:
                gw_rep[k] = g
            elif k not in ("lru_w_a", "lru_w_x"):
                gw_small[k] = g
        if "lru_w_a" in dw:
            gg = jnp.stack([dw["lru_w_a"], dw["lru_w_x"]]).reshape(2, LRU_BLOCKS, N_DEV, 32, LRU_BLOCK_DIM)
            out["lru_gates"] = gg.transpose(2, 0, 1, 3, 4).reshape(N_DEV, 2 * LRU_BLOCKS * 32, LRU_BLOCK_DIM)
        return out

    def spread(keys, sizes, capacities):
        room, bins = list(capacities), [[] for _ in capacities]
        for k in sorted(keys, key=lambda k: -sizes[k]):
            b = max(range(len(room)), key=lambda j: room[j])
            bins[b].append(k)
            room[b] -= sizes[k]
        return bins

    ready, dx, df = {}, dy, dy
    for i in reversed(range(DEPTH)):
        h, act, gu = ffn_res[i]
        keys = list(ready)
        g4 = {k: ready[k].reshape(N_CHIPS, 2, *ready[k].shape[1:]) for k in keys}
        dact, got = mm_carry(df, flat(f"ffn_down{i}"), "nt", F32, [PairJob(g4[k]) for k in keys])
        pairs = {k: pair_add(g4[k], g, f"rs_pair_add_{k}") for k, g in zip(keys, got)}
        sizes = {k: math.prod(pairs[k].shape[1:]) * pairs[k].dtype.itemsize for k in keys}
        bins = spread(keys, sizes, [c * D_MODEL * 2 for c in (800, 1300, 1550)])
        dwd, p0 = mm_carry(act, df, "tn", BF16, [ChipsJob(pairs[k]) for k in bins[0]])
        dgu = swiglu_bwd(gu, dact)
        dh, p1 = mm_carry(dgu, flat(f"ffn_gu{i}"), "nn", F32, [ChipsJob(pairs[k]) for k in bins[1]])
        dwgu, p2 = mm_carry(dgu, h, "tn", BF16, [ChipsJob(pairs[k]) for k in bins[2]])
        for k, parts in zip(bins[0] + bins[1] + bins[2], list(p0) + list(p1) + list(p2)):
            reduced[k] = sum_slots(parts, f"rs_sum_{k}")
        if i > 0:
            dx, df, dw = seg_vjps[i]((dx, dh))
            ready = to_partials(i, dw, dwd, dwgu)
    dx_out, do0, dw = seg_vjps[0]((dx, dh))
    ready = to_partials(0, dw, dwd, dwgu)
    keys = list(ready)
    g4 = {k: ready[k].reshape(N_CHIPS, 2, *ready[k].shape[1:]) for k in keys}
    pairs = {k: pair_add(g4[k], exchange_pair(g4[k], f"rs_pair_{k}"), f"rs_pair_add_{k}") for k in keys}
    dq0, dk0, dv0, parts = _sb_bwd_call(*qkv0, tot0, do0, [ChipsJob(pairs[k]) for k in keys])
    for k, p in zip(keys, parts):
        reduced[k] = sum_slots(p, f"rs_sum_{k}")
    dx_in, dw = vjp_in((dq0, dk0, dv0))
    gx = dx_in + dx_out
    for k, g in to_partials(0, dw, None, None).items():
        reduced[k] = reduce_scatter(g, k)

    G = {}
    for n in IN_T:
        G[n] = reduced[n].T[None]
    for n in OUT_N:
        G[n] = reduced[n][None]
    G["ffn_w_gu"] = jnp.stack([reduced[f"ffn_gu{i}"].T for i in range(DEPTH)])
    G["ffn_w_down"] = jnp.stack([reduced[f"ffn_down{i}"] for i in range(DEPTH)])
    gg = reduced["lru_gates"].reshape(2, 1, LRU_BLOCKS, 32, LRU_BLOCK_DIM)
    G["lru_w_a"], G["lru_w_x"] = gg[0], gg[1]
    gw = dict(gw_small)
    for n in REPLICATED:
        gw[n] = (jnp.concatenate([gw_rep[(n, i)] for i in range(DEPTH)], axis=0) if n in ("mix_norm", "ffn_norm")
                 else gw_rep[n])
    gs = {
        "gdn_conv_w": gw["gdn_conv_w"].reshape(4, N_DEV, -1).transpose(1, 0, 2)[:, None],
        "lru_conv_w": gw["lru_conv_w"].reshape(4, N_DEV, -1).transpose(1, 0, 2)[:, None],
        "lru_conv_b": gw["lru_conv_b"].reshape(N_DEV, 1, -1),
        "lru_lambda": gw["lru_lambda"].reshape(N_DEV, 1, -1),
        "lru_b_a": gw["lru_b_a"].reshape(LRU_BLOCKS, N_DEV, 32).transpose(1, 0, 2)[:, None],
        "lru_b_x": gw["lru_b_x"].reshape(LRU_BLOCKS, N_DEV, 32).transpose(1, 0, 2)[:, None],
    }
    packed = jnp.stack([_pack([gs[n][p] for n in SMALL_SHARDED]) for p in range(N_DEV)])
    for n, g in zip(SMALL_SHARDED, _unpack(reduce_scatter(packed, "small"), small_shapes)):
        G[n] = g
    rep_shapes = [W[n].shape for n in REPLICATED]
    for n, g in zip(REPLICATED, _unpack(all_reduce(_pack([gw[n] for n in REPLICATED]), "rep"), rep_shapes)):
        G[n] = g

    D, NM, NV = {}, {}, {}
    big = [n for n in NAMES if n not in REPLICATED and n not in SMALL_SHARDED]
    for n in big:
        shp = W[n].shape
        two = (-1, shp[-1])
        d, nm, nv = adamw(W[n].reshape(two), G[n].reshape(two), M[n].reshape(two), V[n].reshape(two), f"adamw_{n}")
        D[n], NM[n], NV[n] = d.reshape(shp), nm.reshape(shp), nv.reshape(shp)
    for group, tag in ((SMALL_SHARDED, "small"), (REPLICATED, "rep")):
        shapes = [W[n].shape for n in group]
        res = adamw(_pack([W[n] for n in group]), _pack([G[n] for n in group]), _pack([M[n] for n in group]),
                    _pack([V[n] for n in group]), f"adamw_{tag}")
        for dst, buf in zip((D, NM, NV), res):
            for n, a in zip(group, _unpack(buf, shapes)):
                dst[n] = a

    return (loss, gx[None], *[G[n] for n in NAMES], *[D[n] for n in NAMES], *[NM[n] for n in NAMES],
            *[NV[n] for n in NAMES])
```

```python
import functools
import math

import jax
import jax.numpy as jnp
from jax import lax
from jax.experimental import pallas as pl
from jax.experimental.pallas import tpu as pltpu

F32 = jnp.float32
BF16 = jnp.bfloat16

D_MODEL = 2048
HEAD_DIM = 128
NORM_EPS = 1e-6
SB_HEADS = 16
SB_BLOCK = 256
SB_WIDE = (4, 2, 1)
GDN_K_HEADS = 16
GDN_V_HEADS = 32
GDN_KEY_DIM = 2048
GDN_VAL_DIM = 4096
GDN_CHUNK = 64
GDN_HEADS_PER_STEP = 4
GDN_GROUPS_PER_STEP = 1
DSW_GROUPS = ((128, 1), (512, 4), (2048, 16))
DSW_HG = 6
DSW_HEADS = 18
DSW_BLOCK = 128
ROPE_DIM = 32
ROPE_THETA = 500000.0
LRU_WIDTH = 2048
LRU_BLOCKS = 8
LRU_BLOCK_DIM = 256
LRU_C = 8.0
FFN_HIDDEN = 5632
DEPTH = 4
ADAM_LR, ADAM_B1, ADAM_B2, ADAM_EPS, ADAM_WD, ADAM_STEP = 0.001, 0.9, 0.999, 1e-08, 0.01, 10
N_DEV = 8
N_CHIPS = 4

V7X_VMEM_LIMIT = 56 * 1024 * 1024
LANES = 128
MESH = pl.DeviceIdType.MESH


def _params(**kw):
    return pltpu.CompilerParams(vmem_limit_bytes=V7X_VMEM_LIMIT, **kw)


def _pick(n, cands):
    for c in cands:
        if n % c == 0:
            return c
    return n


def _tile(n, cap, unit=LANES):
    best = None
    for t in range(unit, min(n, cap) + 1, unit):
        if n % t == 0:
            best = t
    return best or n


_DN = {"nn": (((1,), (0,)), ((), ())), "nt": (((1,), (1,)), ((), ())), "tn": (((0,), (0,)), ((), ()))}


def _raw_dot(a, b, mode):
    return lax.dot_general(a.astype(BF16), b.astype(BF16), _DN[mode], preferred_element_type=F32)


@functools.partial(jax.custom_vjp, nondiff_argnums=(2,))
def bdot(a, b, mode):
    return _raw_dot(a, b, mode)


def _bdot_fwd(a, b, mode):
    return _raw_dot(a, b, mode), (a, b)


def _bdot_bwd(mode, res, ct):
    a, b = res
    if mode == "nn":
        return bdot(ct, b, "nt"), bdot(a, ct, "tn")
    if mode == "nt":
        return bdot(ct, b, "nn"), bdot(ct, a, "tn")
    return bdot(b, ct, "nt"), bdot(a, ct, "nn")


bdot.defvjp(_bdot_fwd, _bdot_bwd)


def _split2(x):
    hi = x.astype(BF16)
    lo = (x - hi.astype(F32)).astype(BF16)
    return hi, lo


def _dot01_raw(x, m, dn, left):
    hi, lo = _split2(x)
    if left:
        return (lax.dot_general(m, hi, dn, preferred_element_type=F32)
                + lax.dot_general(m, lo, dn, preferred_element_type=F32))
    return (lax.dot_general(hi, m, dn, preferred_element_type=F32)
            + lax.dot_general(lo, m, dn, preferred_element_type=F32))


@functools.partial(jax.custom_vjp, nondiff_argnums=(2,))
def dot01(x, m, left):
    return _dot01_raw(x, m, _DN["nn"], left)


def _dot01_fwd(x, m, left):
    return _dot01_raw(x, m, _DN["nn"], left), m


def _dot01_bwd(left, m, ct):
    dx = _dot01_raw(ct, m, _DN["tn"] if left else _DN["nt"], left)
    return dx, jnp.zeros_like(m)


dot01.defvjp(_dot01_fwd, _dot01_bwd)


def _mm3(a, b):
    ah, al = _split2(a)
    bh, bl = _split2(b)
    dn = _DN["nn"]
    return (lax.dot_general(ah, bh, dn, preferred_element_type=F32)
            + lax.dot_general(ah, bl, dn, preferred_element_type=F32)
            + lax.dot_general(al, bh, dn, preferred_element_type=F32))


def _softplus_parts(z):
    e = jnp.exp(-jnp.abs(z))
    l = jnp.log(1.0 + e)
    sp = jnp.maximum(z, 0.0) + l
    ls = jnp.minimum(z, 0.0) - l
    inv = 1.0 / (1.0 + e)
    sig = jnp.where(z >= 0.0, inv, e * inv)
    return sp, ls, sig


def _rms(x, g):
    return x * lax.rsqrt(jnp.mean(x * x, axis=-1, keepdims=True) + NORM_EPS) * g


def _swap16(x):
    lane = lax.broadcasted_iota(jnp.int32, x.shape, 1)
    return jnp.where(lane < 16, pltpu.roll(x, 112, axis=1), jnp.where(lane < 32, pltpu.roll(x, 16, axis=1), 0.0))


@jax.custom_vjp
def rope(x, cosm, sinm):
    return x * cosm + _swap16(x) * sinm


def _rope_fwd(x, cosm, sinm):
    return rope(x, cosm, sinm), (cosm, sinm)


def _rope_bwd(res, ct):
    cosm, sinm = res
    return ct * cosm + _swap16(ct * sinm), jnp.zeros_like(cosm), jnp.zeros_like(sinm)


rope.defvjp(_rope_fwd, _rope_bwd)


def mm(a, b, mode, out_dtype=F32):
    return mm_carry(a, b, mode, out_dtype, ())[0]


def mm_carry(a, b, mode, out_dtype=F32, jobs=()):
    if mode == "nt":
        (M, K), N = a.shape, b.shape[0]
    elif mode == "nn":
        (M, K), N = a.shape, b.shape[1]
    else:
        (K, M), N = a.shape, b.shape[1]
    if mode == "nt":
        tm, tn, tk = _tile(M, 1024), _tile(N, 768), _tile(K, 2048)
    elif mode == "nn":
        tm, tn, tk = _tile(M, 1024), _tile(N, 2048), _tile(K, 1408)
    else:
        tm, tn = _tile(M, 1408), _tile(N, 2048)
        tk = _tile(K, 1024 if a.dtype == BF16 and b.dtype == BF16 else 512)
    nk = K // tk
    if mode == "nt":
        a_spec = pl.BlockSpec((tm, tk), lambda i, j, k: (i, k))
        b_spec = pl.BlockSpec((tn, tk), lambda i, j, k: (j, k))
    elif mode == "nn":
        a_spec = pl.BlockSpec((tm, tk), lambda i, j, k: (i, k))
        b_spec = pl.BlockSpec((tk, tn), lambda i, j, k: (k, j))
    else:
        a_spec = pl.BlockSpec((tk, tm), lambda i, j, k: (k, i))
        b_spec = pl.BlockSpec((tk, tn), lambda i, j, k: (k, j))

    def body(a_ref, b_ref, o_ref, *scr):
        p = _raw_dot(a_ref[...], b_ref[...], mode)
        if nk == 1:
            o_ref[...] = p.astype(o_ref.dtype)
        else:
            acc = scr[0]
            k = pl.program_id(2)

            @pl.when(k == 0)
            def _():
                acc[...] = p

            @pl.when(k > 0)
            def _():
                acc[...] += p

            @pl.when(k == nk - 1)
            def _():
                o_ref[...] = acc[...].astype(o_ref.dtype)

    grid = (M // tm, N // tn, nk)
    res = carried_call(
        body, jobs, grid, name=f"mm_{mode}_{M}x{N}x{K}" + ("_c" if jobs else ""),
        out_shape=[jax.ShapeDtypeStruct((M, N), out_dtype)], in_specs=[a_spec, b_spec],
        out_specs=[pl.BlockSpec((tm, tn), lambda i, j, k: (i, j))],
        scratch_shapes=[] if nk == 1 else [pltpu.VMEM((tm, tn), F32)], operands=(a, b))
    return res[0][0], res[1]


@jax.custom_vjp
def lin_t(x, wt):
    return mm(x, wt, "nt")


def _lin_t_fwd(x, wt):
    return mm(x, wt, "nt"), (x, wt)


def _lin_t_bwd(res, dy):
    x, wt = res
    return mm(dy, wt, "nn"), mm(dy, x, "tn", out_dtype=wt.dtype)


lin_t.defvjp(_lin_t_fwd, _lin_t_bwd)


@jax.custom_vjp
def lin_n(x, w):
    return mm(x, w, "nn")


def _lin_n_fwd(x, w):
    return mm(x, w, "nn"), (x, w)


def _lin_n_bwd(res, dy):
    x, w = res
    return mm(dy, w, "nt"), mm(x, dy, "tn", out_dtype=w.dtype)


lin_n.defvjp(_lin_n_fwd, _lin_n_bwd)


def _full_spec(shape):
    nd = len(shape)
    return pl.BlockSpec(tuple(shape), lambda i: (0,) * nd)


def _row_spec(tr, c):
    return pl.BlockSpec((tr, c), lambda i: (i, 0))


def _rowop_tr(total_cols, T):
    budget = 20 * 1024 * 1024
    for tr in (512, 256, 128, 64, 32, 16, 8):
        if T % tr == 0 and total_cols * tr * 4 * 2 <= budget:
            return tr
    return 8


def rowop(fn, name, rows, consts, params, out_cols):
    nr, nc, npar, nout = len(rows), len(consts), len(params), len(out_cols)
    T = rows[0].shape[0]
    in_cols = [r.shape[1] for r in rows] + [c.shape[1] for c in consts]
    tr_f = _rowop_tr(sum(in_cols) + sum(out_cols), T)
    tr_b = _rowop_tr(sum(in_cols) + sum(out_cols) + sum(r.shape[1] for r in rows), T)

    def fwd_call(rows, consts, params):
        def body(*refs):
            ins = [r[...] for r in refs[:nr + nc + npar]]
            outs = fn(*ins)
            for o_ref, o in zip(refs[nr + nc + npar:], outs):
                o_ref[...] = o.astype(F32)

        return pl.pallas_call(
            body, name=name + "_fwd", grid=(T // tr_f,),
            out_shape=[jax.ShapeDtypeStruct((T, c), F32) for c in out_cols],
            in_specs=[_row_spec(tr_f, c) for c in in_cols] + [_full_spec(p.shape) for p in params],
            out_specs=[_row_spec(tr_f, c) for c in out_cols],
            compiler_params=_params(dimension_semantics=("arbitrary",)),
        )(*rows, *consts, *params)

    def bwd_call(rows, consts, params, douts):
        def body(*refs):
            i = pl.program_id(0)
            rv = [r[...] for r in refs[:nr]]
            cv = [r[...] for r in refs[nr:nr + nc]]
            pv = [r[...] for r in refs[nr + nc:nr + nc + npar]]
            dv = [r[...] for r in refs[nr + nc + npar:nr + nc + npar + nout]]
            orefs = refs[nr + nc + npar + nout:]
            _, vjp = jax.vjp(lambda rr, pp: tuple(fn(*rr, *cv, *pp)), rv, pv)
            drows, dpars = vjp(tuple(dv))
            for o_ref, g in zip(orefs[:nr], drows):
                o_ref[...] = g.astype(F32)

            @pl.when(i == 0)
            def _():
                for o_ref in orefs[nr:]:
                    o_ref[...] = jnp.zeros(o_ref.shape, F32)

            for o_ref, g in zip(orefs[nr:], dpars):
                o_ref[...] += g.astype(F32)

        res = pl.pallas_call(
            body, name=name + "_bwd", grid=(T // tr_b,),
            out_shape=[jax.ShapeDtypeStruct(r.shape, F32) for r in rows]
            + [jax.ShapeDtypeStruct(p.shape, F32) for p in params],
            in_specs=[_row_spec(tr_b, c) for c in in_cols] + [_full_spec(p.shape) for p in params]
            + [_row_spec(tr_b, c) for c in out_cols],
            out_specs=[_row_spec(tr_b, r.shape[1]) for r in rows] + [_full_spec(p.shape) for p in params],
            compiler_params=_params(dimension_semantics=("arbitrary",)),
        )(*rows, *consts, *params, *douts)
        return tuple(res[:nr]), tuple(res[nr:])

    @jax.custom_vjp
    def op(rows, consts, params):
        return tuple(fwd_call(rows, consts, params))

    def op_fwd(rows, consts, params):
        return tuple(fwd_call(rows, consts, params)), (rows, consts, params)

    def op_bwd(res, douts):
        rows, consts, params = res
        drows, dpars = bwd_call(rows, consts, params, douts)
        return drows, tuple(jnp.zeros_like(c) for c in consts), dpars

    op.defvjp(op_fwd, op_bwd)
    return op(tuple(rows), tuple(consts), tuple(params))


def _fn_norm(x, g):
    return (_rms(x, g),)


def _fn_add_norm(x, y, g):
    s = x + y
    return s, _rms(s, g)


def _heads(x, n, width=HEAD_DIM):
    return [x[:, h * width:(h + 1) * width] for h in range(n)]


def _fn_sb_pre(qkv, qn, kn):
    hs = _heads(qkv, 3 * SB_HEADS)
    q = jnp.concatenate([_rms(h, qn) for h in hs[:SB_HEADS]], axis=1)
    k = jnp.concatenate([_rms(h, kn) for h in hs[SB_HEADS:2 * SB_HEADS]], axis=1)
    v = jnp.concatenate(hs[2 * SB_HEADS:], axis=1)
    return q, k, v


def _fn_dsw_pre(qkv, cosm, sinm, qn, kn):
    hs = _heads(qkv, 3 * DSW_HEADS)
    q = jnp.concatenate([rope(_rms(h, qn), cosm, sinm) for h in hs[:DSW_HEADS]], axis=1)
    k = jnp.concatenate([rope(_rms(h, kn), cosm, sinm) for h in hs[DSW_HEADS:2 * DSW_HEADS]], axis=1)
    v = jnp.concatenate(hs[2 * DSW_HEADS:], axis=1)
    return q, k, v


def _fn_dsw_combine(o, lse):
    os_, ls_ = _heads(o, DSW_HEADS), _heads(lse, DSW_HEADS)
    out = [None] * DSW_HEADS
    for hg in range(DSW_HG):
        l3 = [ls_[g * DSW_HG + hg] for g in range(3)]
        m = jnp.maximum(jnp.maximum(l3[0], l3[1]), l3[2])
        e3 = [jnp.exp(l - m) for l in l3]
        den = e3[0] + e3[1] + e3[2]
        for g in range(3):
            out[g * DSW_HG + hg] = os_[g * DSW_HG + hg] * (e3[g] / den)
    return (jnp.concatenate(out, axis=1),)


def _l2(x):
    return x * lax.rsqrt(jnp.sum(x * x, axis=-1, keepdims=True) + NORM_EPS)


def _fn_gdn_pre(qkv, ba, a_log, dt_bias):
    x = jax.nn.silu(qkv)
    hs = _heads(x, 2 * GDN_K_HEADS + GDN_V_HEADS)
    rep = GDN_V_HEADS // GDN_K_HEADS
    qh = [_l2(h) * HEAD_DIM ** -0.5 for h in hs[:GDN_K_HEADS]]
    kh = [_l2(h) for h in hs[GDN_K_HEADS:2 * GDN_K_HEADS]]
    q = jnp.concatenate([qh[h // rep] for h in range(GDN_V_HEADS)], axis=1)
    k = jnp.concatenate([kh[h // rep] for h in range(GDN_V_HEADS)], axis=1)
    v = jnp.concatenate(hs[2 * GDN_K_HEADS:], axis=1)
    b = ba[:, :GDN_V_HEADS]
    a = ba[:, GDN_V_HEADS:2 * GDN_V_HEADS]
    beta = jax.nn.sigmoid(b)
    g = -jnp.exp(a_log) * jax.nn.softplus(a + dt_bias)
    rows = b.shape[0]
    beta_b = jnp.concatenate([jnp.broadcast_to(beta[:, h:h + 1], (rows, HEAD_DIM)) for h in range(GDN_V_HEADS)], axis=1)
    g_b = jnp.concatenate([jnp.broadcast_to(g[:, h:h + 1], (rows, HEAD_DIM)) for h in range(GDN_V_HEADS)], axis=1)
    return q, k, v, g_b, beta_b


def _fn_gdn_post(o, z, o_norm):
    os_, zs = _heads(o, GDN_V_HEADS), _heads(z, GDN_V_HEADS)
    return (jnp.concatenate([_rms(oh, o_norm) * jax.nn.silu(zh) for oh, zh in zip(os_, zs)], axis=1),)


def _expm1(x):
    return jnp.tanh(0.5 * x) * (jnp.exp(x) + 1.0)


def _fn_lru_gates(xc, conv_b, w_a, b_a, w_x, b_x, lam):
    xr = xc + conv_b
    xs = _heads(xr, LRU_BLOCKS, LRU_BLOCK_DIM)
    r = jnp.concatenate([bdot(xs[n], w_a[n], "nn") for n in range(LRU_BLOCKS)], axis=1) + b_a
    i = jnp.concatenate([bdot(xs[n], w_x[n], "nn") for n in range(LRU_BLOCKS)], axis=1) + b_x
    r = jax.nn.sigmoid(r)
    i = jax.nn.sigmoid(i)
    log_a = -LRU_C * r * jax.nn.softplus(-lam)
    a = jnp.exp(log_a)
    u = jnp.sqrt(-_expm1(2.0 * log_a)) * (i * xr)
    return a, u


def _fn_lru_out(hs, gate):
    c = math.sqrt(2.0 / math.pi)
    gl = 0.5 * gate * (1.0 + jnp.tanh(c * (gate + 0.044715 * (gate * gate * gate))))
    return (hs * gl,)


def loss_head(x, f, target):
    T, D = x.shape
    tr = _pick(T, (256, 128, 64, 32, 16, 8))

    def body(x_ref, f_ref, t_ref, l_ref, dy_ref):
        i = pl.program_id(0)
        err = (x_ref[...] + f_ref[...]) - t_ref[...]
        dy_ref[...] = err * (1.0 / D)
        part = 0.5 * jnp.sum(jnp.mean(err * err, axis=-1, keepdims=True), axis=0, keepdims=True)

        @pl.when(i == 0)
        def _():
            l_ref[...] = jnp.zeros(l_ref.shape, F32)

        l_ref[...] += jnp.broadcast_to(part, l_ref.shape)

    l, dy = pl.pallas_call(
        body, name="loss_head", grid=(T // tr,),
        out_shape=[jax.ShapeDtypeStruct((8, LANES), F32), jax.ShapeDtypeStruct((T, D), F32)],
        in_specs=[_row_spec(tr, D)] * 3, out_specs=[_full_spec((8, LANES)), _row_spec(tr, D)],
        compiler_params=_params(dimension_semantics=("arbitrary",)),
    )(x, f, target)
    return l[0, 0], dy


def _as_bf16(x):
    return x.astype(BF16).astype(F32)


def _shift_rows(x, s):
    if s == 0:
        return x
    n = x.shape[0]
    row = lax.broadcasted_iota(jnp.int32, x.shape, 0)
    rolled = pltpu.roll(x, s % n, axis=0)
    keep = (row >= s) if s > 0 else (row < n + s)
    return jnp.where(keep, rolled, 0.0)


def _conv_fwd_call(x, w):
    T, C = x.shape
    K = w.shape[0]
    cb = _pick(C, (256, 128))

    def body(x_ref, w_ref, y_ref):
        xv, wv = _as_bf16(x_ref[...]), _as_bf16(w_ref[...])
        acc = xv * wv[K - 1:K, :]
        for k in range(K - 1):
            acc = acc + _shift_rows(xv, K - 1 - k) * wv[k:k + 1, :]
        y_ref[...] = acc

    return pl.pallas_call(
        body, name=f"conv_fwd_{C}", grid=(C // cb,), out_shape=jax.ShapeDtypeStruct((T, C), F32),
        in_specs=[pl.BlockSpec((T, cb), lambda j: (0, j)), pl.BlockSpec((K, cb), lambda j: (0, j))],
        out_specs=pl.BlockSpec((T, cb), lambda j: (0, j)),
        compiler_params=_params(dimension_semantics=("arbitrary",)),
    )(x, w)


def _conv_bwd_call(x, w, dy):
    T, C = x.shape
    K = w.shape[0]
    cb = _pick(C, (256, 128))

    def body(x_ref, w_ref, dy_ref, dx_ref, dw_ref):
        xv, dv, wv = _as_bf16(x_ref[...]), _as_bf16(dy_ref[...]), _as_bf16(w_ref[...])
        acc = dv * wv[K - 1:K, :]
        rows = [None] * K
        rows[K - 1] = jnp.sum(dv * xv, axis=0, keepdims=True)
        for k in range(K - 1):
            s = K - 1 - k
            acc = acc + _shift_rows(dv, -s) * wv[k:k + 1, :]
            rows[k] = jnp.sum(dv * _shift_rows(xv, s), axis=0, keepdims=True)
        dx_ref[...] = acc
        dw_ref[...] = jnp.concatenate(rows + [jnp.zeros((8 - K, cb), F32)], axis=0)

    dx, dw = pl.pallas_call(
        body, name=f"conv_bwd_{C}", grid=(C // cb,),
        out_shape=[jax.ShapeDtypeStruct((T, C), F32), jax.ShapeDtypeStruct((8, C), F32)],
        in_specs=[pl.BlockSpec((T, cb), lambda j: (0, j)), pl.BlockSpec((K, cb), lambda j: (0, j)),
                  pl.BlockSpec((T, cb), lambda j: (0, j))],
        out_specs=[pl.BlockSpec((T, cb), lambda j: (0, j)), pl.BlockSpec((8, cb), lambda j: (0, j))],
        compiler_params=_params(dimension_semantics=("arbitrary",)),
    )(x, w, dy)
    return dx, dw[:K]


@jax.custom_vjp
def dwconv(x, w):
    return _conv_fwd_call(x, w)


def _dwconv_fwd(x, w):
    return _conv_fwd_call(x, w), (x, w)


def _dwconv_bwd(res, dy):
    return _conv_bwd_call(*res, dy)


dwconv.defvjp(_dwconv_fwd, _dwconv_bwd)


def _scan_fwd_call(a, u):
    T, C = a.shape
    cb = _pick(C, (256, 128))

    def body(a_ref, u_ref, h_ref):
        def step(i, h):
            r = pl.multiple_of(i * 8, 8)
            at, ut = a_ref[pl.ds(r, 8), :], u_ref[pl.ds(r, 8), :]
            rows = []
            for j in range(8):
                h = at[j:j + 1, :] * h + ut[j:j + 1, :]
                rows.append(h)
            h_ref[pl.ds(r, 8), :] = jnp.concatenate(rows, axis=0)
            return h

        lax.fori_loop(0, T // 8, step, jnp.zeros((1, cb), F32))

    return pl.pallas_call(
        body, name="lru_scan_fwd", grid=(C // cb,), out_shape=jax.ShapeDtypeStruct((T, C), F32),
        in_specs=[pl.BlockSpec((T, cb), lambda j: (0, j))] * 2, out_specs=pl.BlockSpec((T, cb), lambda j: (0, j)),
        compiler_params=_params(dimension_semantics=("arbitrary",)),
    )(a, u)


def _scan_bwd_call(a, hs, dh):
    T, C = a.shape
    cb = _pick(C, (256, 128))
    nt = T // 8

    def body(a_ref, h_ref, dh_ref, da_ref, du_ref):
        def step(s, carry):
            i = nt - 1 - s
            r = pl.multiple_of(i * 8, 8)
            rp = pl.multiple_of(jnp.maximum(i - 1, 0) * 8, 8)
            at, ht, dt = a_ref[pl.ds(r, 8), :], h_ref[pl.ds(r, 8), :], dh_ref[pl.ds(r, 8), :]
            hprev_tile = h_ref[pl.ds(rp, 8), :]
            h_before = jnp.where(i > 0, hprev_tile[7:8, :], 0.0)
            da_rows, du_rows = [None] * 8, [None] * 8
            for j in range(7, -1, -1):
                lam = dt[j:j + 1, :] + carry
                du_rows[j] = lam
                hp = ht[j - 1:j, :] if j > 0 else h_before
                da_rows[j] = lam * hp
                carry = at[j:j + 1, :] * lam
            da_ref[pl.ds(r, 8), :] = jnp.concatenate(da_rows, axis=0)
            du_ref[pl.ds(r, 8), :] = jnp.concatenate(du_rows, axis=0)
            return carry

        lax.fori_loop(0, nt, step, jnp.zeros((1, cb), F32))

    return pl.pallas_call(
        body, name="lru_scan_bwd", grid=(C // cb,), out_shape=[jax.ShapeDtypeStruct((T, C), F32)] * 2,
        in_specs=[pl.BlockSpec((T, cb), lambda j: (0, j))] * 3,
        out_specs=[pl.BlockSpec((T, cb), lambda j: (0, j))] * 2,
        compiler_params=_params(dimension_semantics=("arbitrary",)),
    )(a, hs, dh)


@jax.custom_vjp
def lru_scan(a, u):
    return _scan_fwd_call(a, u)


def _lru_scan_fwd(a, u):
    hs = _scan_fwd_call(a, u)
    return hs, (a, hs)


def _lru_scan_bwd(res, dh):
    a, hs = res
    return tuple(_scan_bwd_call(a, hs, dh))


lru_scan.defvjp(_lru_scan_fwd, _lru_scan_bwd)


def _tri(n, kind):
    r = lax.broadcasted_iota(jnp.int32, (n, n), 0)
    c = lax.broadcasted_iota(jnp.int32, (n, n), 1)
    m = {"gt": r > c, "le": r <= c, "lt": r < c, "ge": r >= c, "eq": r == c}[kind]
    return jnp.where(m, 1.0, 0.0).astype(BF16)


def _sb_fwd_call(q, k, v, shards=()):
    T, HD = q.shape
    H = HD // HEAD_DIM
    tb = _pick(T, (SB_BLOCK, 128))
    scale = HEAD_DIM ** -0.5

    def body(q_ref, k_ref, v_ref, o_ref, tot_ref):
        i = pl.program_id(1)
        qb = q_ref[...].astype(BF16)
        u_gt = _tri(tb, "gt")
        row = lax.broadcasted_iota(jnp.int32, (tb, tb), 0)
        col = lax.broadcasted_iota(jnp.int32, (tb, tb), 1)

        def step(j, n, carry, diagonal):
            acc, run = carry
            off = pl.multiple_of(j * tb, tb)
            kb = k_ref[pl.ds(off, n * tb), :].astype(BF16)
            vb = v_ref[pl.ds(off, n * tb), :].astype(BF16)
            z = lax.dot_general(qb, kb, _DN["nt"], preferred_element_type=F32) * scale
            sp, ls, _ = _softplus_parts(z)
            if diagonal:
                sp = jnp.where(col < row, sp, 0.0)
            parts = [None] * n
            for s in reversed(range(n)):
                xs = sp[:, s * tb:(s + 1) * tb]
                parts[s] = _dot01_raw(xs, u_gt, _DN["nn"], False) + run
                run = run + jnp.sum(xs, axis=1, keepdims=True)
            between = parts[0] if n == 1 else jnp.concatenate(parts, axis=1)
            w = jnp.exp(ls - between)
            if diagonal:
                w = jnp.where(col < row, w, 0.0)
            acc = acc + lax.dot_general(w.astype(BF16), vb, _DN["nn"], preferred_element_type=F32)
            return acc, run

        carry = step(i, 1, (jnp.zeros((tb, HEAD_DIM), F32), jnp.zeros((tb, 1), F32)), True)
        left = i
        for n in SB_WIDE:
            carry = lax.fori_loop(0, lax.div(left, n), lambda t, c, n=n, left=left: step(left - n * (t + 1), n, c, False),
                                  carry)
            left = lax.rem(left, n)
        acc, run = carry
        o_ref[...] = acc
        tot_ref[...] = jnp.broadcast_to(run, (tb, HEAD_DIM))

    blk = pl.BlockSpec((tb, HEAD_DIM), lambda h, i: (i, h))
    full = pl.BlockSpec((T, HEAD_DIM), lambda h, i: (0, h))
    (o, tot), gathered = carried_call(
        body, [GatherJob(s) for s in shards], (H, T // tb), name="sb_attn_fwd",
        out_shape=[jax.ShapeDtypeStruct((T, HD), F32)] * 2, in_specs=[blk, full, full], out_specs=[blk, blk],
        scratch_shapes=[], operands=(q, k, v))
    return o, tot, tuple(gathered)


def _sb_bwd_call(q, k, v, tot, do, jobs=()):
    T, HD = q.shape
    H = HD // HEAD_DIM
    tb = _pick(T, (SB_BLOCK, 128))
    scale = HEAD_DIM ** -0.5

    def body(q_ref, k_ref, v_ref, tot_ref, do_ref, dq_ref, dk_ref, dv_ref):
        i = pl.program_id(1)

        @pl.when(i == 0)
        def _():
            dk_ref[...] = jnp.zeros(dk_ref.shape, F32)
            dv_ref[...] = jnp.zeros(dv_ref.shape, F32)

        qb = q_ref[...].astype(BF16)
        dob = do_ref[...].astype(BF16)
        tot = tot_ref[:, 0:1]
        u_le = _tri(tb, "le")
        u_lt = _tri(tb, "lt")
        row = lax.broadcasted_iota(jnp.int32, (tb, tb), 0)
        col = lax.broadcasted_iota(jnp.int32, (tb, tb), 1)

        def prefix_sums(x, u, start):
            out = []
            for s in range(x.shape[1] // tb):
                xs = x[:, s * tb:(s + 1) * tb]
                out.append(_dot01_raw(xs, u, _DN["nn"], False) + start)
                start = start + jnp.sum(xs, axis=1, keepdims=True)
            return (out[0] if len(out) == 1 else jnp.concatenate(out, axis=1)), start

        def step(j, n, carry, diagonal):
            dq, cs, cd = carry
            off = pl.multiple_of(j * tb, tb)
            kb = k_ref[pl.ds(off, n * tb), :].astype(BF16)
            vb = v_ref[pl.ds(off, n * tb), :].astype(BF16)
            z = lax.dot_general(qb, kb, _DN["nt"], preferred_element_type=F32) * scale
            sp, ls, sig = _softplus_parts(z)
            if diagonal:
                sp = jnp.where(col < row, sp, 0.0)
            prefix, cs = prefix_sums(sp, u_le, cs)
            w = jnp.exp(ls - (tot - prefix))
            if diagonal:
                w = jnp.where(col < row, w, 0.0)
            wb = w.astype(BF16)
            dv_ref[pl.ds(off, n * tb), :] += lax.dot_general(wb, dob, _DN["tn"], preferred_element_type=F32)
            dw = lax.dot_general(dob, vb, _DN["nt"], preferred_element_type=F32)
            dl = dw * w
            before, cd = prefix_sums(dl, u_lt, cd)
            dz = (dl * (1.0 - sig) - before * sig) * scale
            if diagonal:
                dz = jnp.where(col < row, dz, 0.0)
            dzb = dz.astype(BF16)
            dq = dq + lax.dot_general(dzb, kb, _DN["nn"], preferred_element_type=F32)
            dk_ref[pl.ds(off, n * tb), :] += lax.dot_general(dzb, qb, _DN["tn"], preferred_element_type=F32)
            return dq, cs, cd

        z1 = jnp.zeros((tb, 1), F32)
        carry, done = (jnp.zeros((tb, HEAD_DIM), F32), z1, z1), 0
        for n in SB_WIDE:
            trips = lax.div(i - done, n)
            carry = lax.fori_loop(0, trips, lambda t, c, n=n, done=done: step(done + n * t, n, c, False), carry)
            done = done + trips * n
        dq, _, _ = step(i, 1, carry, True)
        dq_ref[...] = dq

    blk = pl.BlockSpec((tb, HEAD_DIM), lambda h, i: (i, h))
    full = pl.BlockSpec((T, HEAD_DIM), lambda h, i: (0, h))
    grads, exchanged = carried_call(
        body, jobs, (H, T // tb), name="sb_attn_bwd", out_shape=[jax.ShapeDtypeStruct((T, HD), F32)] * 3,
        in_specs=[blk, full, full, blk, blk], out_specs=[blk, full, full], scratch_shapes=[],
        operands=(q, k, v, tot, do))
    return (*grads, exchanged) if jobs else tuple(grads)


def _dsw_tile(q, kp, kc, vp, vc, n):
    blk = DSW_BLOCK
    scale = HEAD_DIM ** -0.5
    qi = lax.broadcasted_iota(jnp.int32, (blk, blk), 0)
    kj = lax.broadcasted_iota(jnp.int32, (blk, blk), 1)
    neg = -1e30
    s_p = jnp.where((kj >= qi) & (n > 0), bdot(q, kp, "nt") * scale, neg)
    s_c = jnp.where(kj <= qi, bdot(q, kc, "nt") * scale, neg)
    m = jnp.maximum(jnp.max(s_p, axis=-1, keepdims=True), jnp.max(s_c, axis=-1, keepdims=True))
    p_p, p_c = jnp.exp(s_p - m), jnp.exp(s_c - m)
    den = jnp.sum(p_p, axis=-1, keepdims=True) + jnp.sum(p_c, axis=-1, keepdims=True)
    o = bdot(p_p / den, vp, "nn") + bdot(p_c / den, vc, "nn")
    lse = m + jnp.log(den)
    return o, jnp.broadcast_to(lse, (blk, HEAD_DIM))


DSW_HEADS_PER_STEP_FWD = 6
DSW_HEADS_PER_STEP_BWD = 3


def _dsw_specs(nsub, hps):
    w = hps * HEAD_DIM
    cur = pl.BlockSpec((None, DSW_BLOCK, w), lambda r, h, n: (r, n, h))
    prev = pl.BlockSpec((None, DSW_BLOCK, w), lambda r, h, n: (r, jnp.maximum(n - 1, 0), h))
    whole = pl.BlockSpec((None, nsub, w), lambda r, h, n: (r, 0, h))
    return cur, prev, whole


def _head(ref, h, rows=slice(None)):
    return ref[rows, h * HEAD_DIM:(h + 1) * HEAD_DIM]


def _dsw_fwd_call(q, k, v):
    d, nsub, HD = q.shape
    hps = DSW_HEADS_PER_STEP_FWD
    cur, prev, _ = _dsw_specs(nsub, hps)

    def body(q_ref, kp_ref, kc_ref, vp_ref, vc_ref, o_ref, l_ref):
        n = pl.program_id(2)
        for h in range(hps):
            o, l = _dsw_tile(*[_head(ref, h) for ref in (q_ref, kp_ref, kc_ref, vp_ref, vc_ref)], n)
            o_ref[:, h * HEAD_DIM:(h + 1) * HEAD_DIM] = o
            l_ref[:, h * HEAD_DIM:(h + 1) * HEAD_DIM] = l

    return pl.pallas_call(
        body, name=f"dsw_attn_fwd_d{d}", grid=(d, HD // HEAD_DIM // hps, nsub // DSW_BLOCK),
        out_shape=[jax.ShapeDtypeStruct(q.shape, F32)] * 2,
        in_specs=[cur, prev, cur, prev, cur], out_specs=[cur, cur],
        compiler_params=_params(dimension_semantics=("arbitrary",) * 3),
    )(q, k, k, v, v)


def _dsw_bwd_call(q, k, v, do, dl):
    d, nsub, HD = q.shape
    hps = DSW_HEADS_PER_STEP_BWD
    cur, prev, whole = _dsw_specs(nsub, hps)
    blk = DSW_BLOCK

    def body(q_ref, kp_ref, kc_ref, vp_ref, vc_ref, do_ref, dl_ref, dq_ref, dk_ref, dv_ref):
        n = pl.program_id(2)

        @pl.when(n == 0)
        def _():
            dk_ref[...] = jnp.zeros(dk_ref.shape, F32)
            dv_ref[...] = jnp.zeros(dv_ref.shape, F32)

        c0 = pl.multiple_of(n * blk, blk)
        p0 = pl.multiple_of(jnp.maximum(n - 1, 0) * blk, blk)
        for h in range(hps):
            cols = slice(h * HEAD_DIM, (h + 1) * HEAD_DIM)
            _, vjp = jax.vjp(lambda a, b, c, e, f: _dsw_tile(a, b, c, e, f, n),
                             *[_head(ref, h) for ref in (q_ref, kp_ref, kc_ref, vp_ref, vc_ref)])
            dq, dkp, dkc, dvp, dvc = vjp((_head(do_ref, h), _head(dl_ref, h)))
            dq_ref[:, cols] = dq
            dk_ref[pl.ds(c0, blk), cols] += dkc
            dv_ref[pl.ds(c0, blk), cols] += dvc
            dk_ref[pl.ds(p0, blk), cols] += dkp
            dv_ref[pl.ds(p0, blk), cols] += dvp

    return pl.pallas_call(
        body, name=f"dsw_attn_bwd_d{d}", grid=(d, HD // HEAD_DIM // hps, nsub // blk),
        out_shape=[jax.ShapeDtypeStruct(q.shape, F32)] * 3,
        in_specs=[cur, prev, cur, prev, cur, cur, cur], out_specs=[cur, whole, whole],
        compiler_params=_params(dimension_semantics=("arbitrary",) * 3),
    )(q, k, k, v, v, do, dl)


@jax.custom_vjp
def dsw_attn(q, k, v):
    return tuple(_dsw_fwd_call(q, k, v))


def _dsw_attn_fwd(q, k, v):
    return tuple(_dsw_fwd_call(q, k, v)), (q, k, v)


def _dsw_attn_bwd(res, cts):
    return tuple(_dsw_bwd_call(*res, *cts))


dsw_attn.defvjp(_dsw_attn_fwd, _dsw_attn_bwd)


_LOG2_CHUNK = GDN_CHUNK.bit_length() - 1
_LOG2_HEAD_DIM = HEAD_DIM.bit_length() - 1


def _unit_lower_inverse(a):
    n = a.shape[0]
    r = lax.broadcasted_iota(jnp.int32, (n, n), 0)
    c = lax.broadcasted_iota(jnp.int32, (n, n), 1)
    x = -a
    t = jnp.where(r == c, 1.0, 0.0) + x
    for _ in range(_LOG2_CHUNK - 1):
        x = _mm3(x, x)
        t = t + _mm3(t, x)
    return t


@jax.custom_vjp
def unit_lower_inverse(a):
    return _unit_lower_inverse(a)


def _unit_lower_inverse_fwd(a):
    t = _unit_lower_inverse(a)
    return t, t


def _unit_lower_inverse_bwd(t, ct):
    return (-_mm3(_mm3(t.T, ct), t.T),)


unit_lower_inverse.defvjp(_unit_lower_inverse_fwd, _unit_lower_inverse_bwd)


@jax.custom_vjp
def known_inverse(a, t):
    return t


def _known_inverse_fwd(a, t):
    return t, t


def _known_inverse_bwd(t, ct):
    return _unit_lower_inverse_bwd(t, ct)[0], jnp.zeros_like(t)


known_inverse.defvjp(_known_inverse_fwd, _known_inverse_bwd)


def _gdn_step(state, q, k, v, gb, bb, t_known=None):
    new_state, out, _ = _gdn_step_all(state, q, k, v, gb, bb, t_known)
    return new_state, out


def _gdn_step_all(state, q, k, v, gb, bb, t_known=None):
    C, NH = GDN_CHUNK, GDN_HEADS_PER_STEP
    R = NH * C
    r = lax.broadcasted_iota(jnp.int32, (R, R), 0)
    c = lax.broadcasted_iota(jnp.int32, (R, R), 1)
    same = lax.shift_right_logical(r, _LOG2_CHUNK) == lax.shift_right_logical(c, _LOG2_CHUNK)
    causal, strict = same & (r >= c), same & (r > c)
    gc = dot01(gb, jnp.where(causal, 1.0, 0.0).astype(BF16), True)
    g_sq = jnp.concatenate([gc] * (R // HEAD_DIM), axis=1)
    g_row = dot01(jnp.where(r == c, g_sq, 0.0), jnp.ones((R, R), BF16), True)
    decay = jnp.where(causal, jnp.exp(jnp.where(causal, g_sq - g_row, 0.0)), 0.0)
    kb, vb = k * bb, v * bb
    a_mat = jnp.where(strict, bdot(kb, k, "nt") * decay, 0.0)
    t_mat = unit_lower_inverse(a_mat) if t_known is None else known_inverse(a_mat, t_known)
    uw = bdot(t_mat, jnp.concatenate([vb, kb * jnp.exp(gc)], axis=1), "nn")
    u, w = uw[:, :HEAD_DIM], uw[:, HEAD_DIM:]
    hr = lax.shift_right_logical(lax.broadcasted_iota(jnp.int32, (R, NH * HEAD_DIM), 0), _LOG2_CHUNK)
    hc = lax.shift_right_logical(lax.broadcasted_iota(jnp.int32, (R, NH * HEAD_DIM), 1), _LOG2_HEAD_DIM)

    def widen(m):
        return jnp.where(hr == hc, jnp.concatenate([m] * NH, axis=1), 0.0)

    v_new = u - bdot(widen(w), state, "nn")
    attn = bdot(q, k, "nt") * decay
    out = bdot(widen(q * jnp.exp(gc)), state, "nn") + bdot(attn, v_new, "nn")
    last = [gc[h * C + C - 1:h * C + C, :] for h in range(NH)]
    g_last_rows = jnp.concatenate([jnp.broadcast_to(l, (C, HEAD_DIM)) for l in last], axis=0)
    g_last_state = jnp.concatenate([jnp.broadcast_to(l, (HEAD_DIM, HEAD_DIM)) for l in last], axis=0)
    k_dec = k * jnp.exp(g_last_rows - gc)
    new_state = state * jnp.exp(g_last_state) + bdot(widen(k_dec), v_new, "tn")
    return new_state, out, t_mat


def _gdn_stack(ref, g):
    h0 = g * GDN_HEADS_PER_STEP
    return jnp.concatenate([ref[:, (h0 + h) * HEAD_DIM:(h0 + h + 1) * HEAD_DIM] for h in range(GDN_HEADS_PER_STEP)],
                           axis=0)


def _gdn_unstack(ref, g, val):
    h0 = g * GDN_HEADS_PER_STEP
    for h in range(GDN_HEADS_PER_STEP):
        ref[:, (h0 + h) * HEAD_DIM:(h0 + h + 1) * HEAD_DIM] = val[h * GDN_CHUNK:(h + 1) * GDN_CHUNK]


def _gdn_fwd_call(q, k, v, gb, bb, shards=()):
    T, HD = v.shape
    H = HD // HEAD_DIM
    N = T // GDN_CHUNK
    hb = GDN_HEADS_PER_STEP * GDN_GROUPS_PER_STEP
    W = hb * HEAD_DIM
    SW = GDN_HEADS_PER_STEP * HEAD_DIM

    R = GDN_HEADS_PER_STEP * GDN_CHUNK

    def body(q_ref, k_ref, v_ref, g_ref, b_ref, o_ref, s_ref, t_ref, state):
        n = pl.program_id(1)

        @pl.when(n == 0)
        def _():
            state[...] = jnp.zeros(state.shape, F32)

        s_in = state[...]
        s_ref[...] = s_in
        res = [_gdn_step_all(s_in[g * SW:(g + 1) * SW],
                             *[_gdn_stack(ref, g) for ref in (q_ref, k_ref, v_ref, g_ref, b_ref)])
               for g in range(GDN_GROUPS_PER_STEP)]
        for g, (ns, o, t_mat) in enumerate(res):
            state[g * SW:(g + 1) * SW, :] = ns
            t_ref[g] = t_mat
            _gdn_unstack(o_ref, g, o)

    blk = pl.BlockSpec((GDN_CHUNK, W), lambda h, n: (n, h))
    sblk = pl.BlockSpec((None, W, HEAD_DIM), lambda h, n: (n, h, 0))
    tblk = pl.BlockSpec((None, GDN_GROUPS_PER_STEP, R, R), lambda h, n: (n, h, 0, 0))
    (o, states, inverses), gathered = carried_call(
        body, [GatherJob(s) for s in shards], (H // hb, N), name="gdn_chunk_fwd",
        out_shape=[jax.ShapeDtypeStruct((T, HD), F32), jax.ShapeDtypeStruct((N, H * HEAD_DIM, HEAD_DIM), F32),
                   jax.ShapeDtypeStruct((N, H // GDN_HEADS_PER_STEP, R, R), F32)],
        in_specs=[blk] * 5, out_specs=[blk, sblk, tblk], scratch_shapes=[pltpu.VMEM((W, HEAD_DIM), F32)],
        operands=(q, k, v, gb, bb))
    return o, states, inverses, tuple(gathered)


def _gdn_bwd_call(q, k, v, gb, bb, states, inverses, do):
    T, HD = v.shape
    H = HD // HEAD_DIM
    N = T // GDN_CHUNK
    hb = GDN_HEADS_PER_STEP * GDN_GROUPS_PER_STEP
    W = hb * HEAD_DIM
    SW = GDN_HEADS_PER_STEP * HEAD_DIM

    def body(q_ref, k_ref, v_ref, g_ref, b_ref, s_ref, t_ref, do_ref, dq_ref, dk_ref, dv_ref, dg_ref, db_ref, dstate):
        n = pl.program_id(1)

        @pl.when(n == 0)
        def _():
            dstate[...] = jnp.zeros(dstate.shape, F32)

        res = []
        for g in range(GDN_GROUPS_PER_STEP):
            rows = slice(g * SW, (g + 1) * SW)
            t_known = t_ref[g]
            _, vjp = jax.vjp(lambda *a: _gdn_step(*a, t_known=t_known), s_ref[rows, :],
                             *[_gdn_stack(ref, g) for ref in (q_ref, k_ref, v_ref, g_ref, b_ref)])
            res.append(vjp((dstate[rows, :], _gdn_stack(do_ref, g))))
        for g, (ds, *grads) in enumerate(res):
            dstate[g * SW:(g + 1) * SW, :] = ds
            for ref, grad in zip((dq_ref, dk_ref, dv_ref, dg_ref, db_ref), grads):
                _gdn_unstack(ref, g, grad)

    R = GDN_HEADS_PER_STEP * GDN_CHUNK
    blk = pl.BlockSpec((GDN_CHUNK, W), lambda h, n: (N - 1 - n, h))
    sblk = pl.BlockSpec((None, W, HEAD_DIM), lambda h, n: (N - 1 - n, h, 0))
    tblk = pl.BlockSpec((None, GDN_GROUPS_PER_STEP, R, R), lambda h, n: (N - 1 - n, h, 0, 0))
    return pl.pallas_call(
        body, name="gdn_chunk_bwd", grid=(H // hb, N), out_shape=[jax.ShapeDtypeStruct((T, HD), F32)] * 5,
        in_specs=[blk] * 5 + [sblk, tblk, blk], out_specs=[blk] * 5,
        scratch_shapes=[pltpu.VMEM((W, HEAD_DIM), F32)],
        compiler_params=_params(dimension_semantics=("arbitrary", "arbitrary")),
    )(q, k, v, gb, bb, states, inverses, do)


@jax.custom_vjp
def gdn_core(q, k, v, gb, bb, shards):
    o, _, _, gathered = _gdn_fwd_call(q, k, v, gb, bb, shards)
    return o, gathered


def _gdn_core_fwd(q, k, v, gb, bb, shards):
    o, states, inverses, gathered = _gdn_fwd_call(q, k, v, gb, bb, shards)
    return (o, gathered), (q, k, v, gb, bb, states, inverses, shards)


def _gdn_core_bwd(res, cts):
    *core, shards = res
    return (*_gdn_bwd_call(*core, cts[0]), tuple(jnp.zeros_like(s) for s in shards))


gdn_core.defvjp(_gdn_core_fwd, _gdn_core_bwd)


def _mixer_gdn(h, w, shards=()):
    wt = w["gdn_in_t"]
    nqkv = 2 * GDN_KEY_DIM + GDN_VAL_DIM
    qkv = lin_t(h, wt[:nqkv])
    z = lin_t(h, wt[nqkv:nqkv + GDN_VAL_DIM])
    w_ba = jnp.pad(wt[nqkv + GDN_VAL_DIM:], ((0, LANES - 2 * GDN_V_HEADS), (0, 0)))
    ba = lin_t(h, w_ba)
    qkv = dwconv(qkv, w["gdn_conv_w"])
    q, k, v, gb, bb = rowop(_fn_gdn_pre, "gdn_pre", [qkv, ba], [], [w["gdn_a_log"], w["gdn_dt_bias"]],
                            [GDN_VAL_DIM] * 5)
    o, gathered = gdn_core(q, k, v, gb, bb, tuple(shards))
    (y,) = rowop(_fn_gdn_post, "gdn_post", [o, z], [], [w["gdn_o_norm"]], [GDN_VAL_DIM])
    return lin_n(y, w["gdn_out"]), gathered


def _to_strided(x, cols, d):
    T = x.shape[0]
    return x[:, cols].reshape(T // d, d, -1).transpose(1, 0, 2)


def _from_strided(x):
    d, n, c = x.shape
    return x.transpose(1, 0, 2).reshape(d * n, c)


def _mixer_dsw(h, cosm, sinm, w):
    qkv = lin_t(h, w["dsw_in_t"])
    nhd = DSW_HEADS * HEAD_DIM
    q, k, v = rowop(_fn_dsw_pre, "dsw_pre", [qkv], [cosm, sinm], [w["dsw_q_norm"], w["dsw_k_norm"]], [nhd] * 3)
    outs, lses = [], []
    for gi, (_, d) in enumerate(DSW_GROUPS):
        cols = slice(gi * DSW_HG * HEAD_DIM, (gi + 1) * DSW_HG * HEAD_DIM)
        o_g, l_g = dsw_attn(_to_strided(q, cols, d), _to_strided(k, cols, d), _to_strided(v, cols, d))
        outs.append(_from_strided(o_g))
        lses.append(_from_strided(l_g))
    (o,) = rowop(_fn_dsw_combine, "dsw_combine", [jnp.concatenate(outs, axis=1), jnp.concatenate(lses, axis=1)],
                 [], [], [nhd])
    return lin_n(o, w["dsw_out"])


def _mixer_lru(h, w):
    wt = w["lru_in_t"]
    gate = lin_t(h, wt[:LRU_WIDTH])
    xr = dwconv(lin_t(h, wt[LRU_WIDTH:]), w["lru_conv_w"])
    a, u = rowop(_fn_lru_gates, "lru_gates", [xr], [],
                 [w["lru_conv_b"], w["lru_w_a"], w["lru_b_a"], w["lru_w_x"], w["lru_b_x"], w["lru_lambda"]],
                 [LRU_WIDTH] * 2)
    hs = lru_scan(a, u)
    (y,) = rowop(_fn_lru_out, "lru_out", [hs, gate], [], [], [LRU_WIDTH])
    return lin_n(y, w["lru_out"])


def mixer_segment(i, x, f_prev, cosm, sinm, w, shards):
    x, h = rowop(_fn_add_norm, "add_norm", [x, f_prev], [], [w["mix_norm"]], [D_MODEL] * 2)
    kind, gathered = i % 4, ()
    if kind == 1:
        y, gathered = _mixer_gdn(h, w, shards)
    elif kind == 2:
        y = _mixer_dsw(h, cosm, sinm, w)
    else:
        y = _mixer_lru(h, w)
    x, h = rowop(_fn_add_norm, "add_norm", [x, y], [], [w["ffn_norm"]], [D_MODEL] * 2)
    return (x, h), gathered


def sb_segment_in(x, w):
    (h,) = rowop(_fn_norm, "norm", [x], [], [w["mix_norm"]], [D_MODEL])
    qkv = lin_t(h, w["sb_in_t"])
    return rowop(_fn_sb_pre, "sb_pre", [qkv], [], [w["sb_q_norm"], w["sb_k_norm"]], [D_MODEL] * 3)


def sb_segment_out(x, o, w):
    return rowop(_fn_add_norm, "add_norm", [x, lin_n(o, w["sb_out"])], [], [w["ffn_norm"]], [D_MODEL] * 2)


def swiglu_fwd(gu):
    T, F2 = gu.shape
    F = F2 // 2
    tr = _pick(T, (128, 64, 32, 16))

    def body(gu_ref, o_ref):
        o_ref[...] = (jax.nn.silu(gu_ref[:, :F]) * gu_ref[:, F:]).astype(o_ref.dtype)

    return pl.pallas_call(
        body, name="swiglu_fwd", grid=(T // tr,), out_shape=jax.ShapeDtypeStruct((T, F), BF16),
        in_specs=[_row_spec(tr, F2)], out_specs=_row_spec(tr, F),
        compiler_params=_params(dimension_semantics=("arbitrary",)),
    )(gu)


def swiglu_bwd(gu, dact):
    T, F2 = gu.shape
    F = F2 // 2
    tr = _pick(T, (128, 64, 32, 16))

    def body(gu_ref, d_ref, o_ref):
        g, up, d = gu_ref[:, :F], gu_ref[:, F:], d_ref[...]
        s = jax.nn.sigmoid(g)
        o_ref[:, :F] = (d * up * (s * (1.0 + g * (1.0 - s)))).astype(o_ref.dtype)
        o_ref[:, F:] = (d * (g * s)).astype(o_ref.dtype)

    return pl.pallas_call(
        body, name="swiglu_bwd", grid=(T // tr,), out_shape=jax.ShapeDtypeStruct((T, F2), BF16),
        in_specs=[_row_spec(tr, F2), _row_spec(tr, F)], out_specs=_row_spec(tr, F2),
        compiler_params=_params(dimension_semantics=("arbitrary",)),
    )(gu, dact)


ANY = pl.BlockSpec(memory_space=pl.ANY)


SLAB_BYTES = 4 * 1024 * 1024


def _col_tile(rows, C, itemsize):
    return _tile(C, max(LANES, SLAB_BYTES // (rows * itemsize)))


class GatherJob:
    def __init__(self, shard):
        self.operand = shard
        r, C = shard.shape
        self.out_shape = jax.ShapeDtypeStruct((N_DEV, r, C), shard.dtype)
        self.scratch = [pltpu.SemaphoreType.DMA((7,)), pltpu.SemaphoreType.DMA((7,)), pltpu.SemaphoreType.DMA]

    def _parts(self, x_ref, out_ref, send_sems, recv_sems, local_sem):
        x, y, c = lax.axis_index("x"), lax.axis_index("y"), lax.axis_index("c")
        me, sibling = (x, y, c), (x, y, 1 - c)
        chips = [(1 - x, y), (x, 1 - y), (1 - x, 1 - y)]

        def slot(px, py, pc):
            return out_ref.at[4 * px + 2 * py + pc]

        def copy(k, block, to, src=None):
            return pltpu.make_async_remote_copy(
                src_ref=slot(*block) if src is None else src, dst_ref=slot(*block),
                send_sem=send_sems.at[k], recv_sem=recv_sems.at[k], device_id=to, device_id_type=MESH)

        def mine():
            return pltpu.make_async_copy(x_ref, slot(*me), local_sem)

        def first():
            return [copy(0, me, sibling, src=x_ref)] + [copy(1 + j, me, (*chip, c), src=x_ref)
                                                        for j, chip in enumerate(chips)]

        def passed():
            return [copy(4 + j, (*chip, c), sibling) for j, chip in enumerate(chips)]

        def landed():
            return [copy(1 + j, (*chip, c), me) for j, chip in enumerate(chips)]

        def from_sibling():
            return [copy(0, sibling, me)] + [copy(4 + j, (*chip, 1 - c), me) for j, chip in enumerate(chips)]

        return mine, first, passed, landed, from_sibling

    def start(self, x_ref, out_ref, *sems):
        mine, first, _, _, _ = self._parts(x_ref, out_ref, *sems)
        mine().start()
        for cp in first():
            cp.start()

    def finish(self, x_ref, out_ref, *sems):
        mine, first, passed, landed, from_sibling = self._parts(x_ref, out_ref, *sems)
        onward = passed()
        for arrived, cp in zip(landed(), onward):
            arrived.wait_recv()
            cp.start()
        for cp in from_sibling():
            cp.wait_recv()
        for cp in first() + onward:
            cp.wait_send()
        mine().wait()


class PairJob:
    def __init__(self, g):
        self.operand = g
        _, _, r, C = g.shape
        self.out_shape = jax.ShapeDtypeStruct((N_CHIPS, r, C), g.dtype)
        self.scratch = [pltpu.SemaphoreType.DMA((N_CHIPS,)), pltpu.SemaphoreType.DMA((N_CHIPS,))]

    def _copies(self, g_ref, out_ref, send_sems, recv_sems):
        x, y, c = lax.axis_index("x"), lax.axis_index("y"), lax.axis_index("c")
        return [pltpu.make_async_remote_copy(
            src_ref=g_ref.at[q, 1 - c], dst_ref=out_ref.at[q], send_sem=send_sems.at[q], recv_sem=recv_sems.at[q],
            device_id=(x, y, 1 - c), device_id_type=MESH) for q in range(N_CHIPS)]

    def start(self, *refs):
        for cp in self._copies(*refs):
            cp.start()

    def finish(self, *refs):
        for cp in self._copies(*refs):
            cp.wait()


class ChipsJob:
    def __init__(self, p):
        self.operand = p
        self.out_shape = jax.ShapeDtypeStruct(p.shape, p.dtype)
        self.scratch = [pltpu.SemaphoreType.DMA((3,)), pltpu.SemaphoreType.DMA((3,)), pltpu.SemaphoreType.DMA]

    def _parts(self, p_ref, out_ref, send_sems, recv_sems, local_sem):
        x, y, c = lax.axis_index("x"), lax.axis_index("y"), lax.axis_index("c")
        mychip = 2 * x + y
        chips = [(1 - x, y), (x, 1 - y), (1 - x, 1 - y)]
        def mine():
            return pltpu.make_async_copy(p_ref.at[mychip], out_ref.at[mychip], local_sem)

        def sends():
            return [pltpu.make_async_remote_copy(
                src_ref=p_ref.at[2 * cx + cy], dst_ref=out_ref.at[mychip], send_sem=send_sems.at[j],
                recv_sem=recv_sems.at[j], device_id=(cx, cy, c), device_id_type=MESH)
                for j, (cx, cy) in enumerate(chips)]

        def arrivals():
            return [pltpu.make_async_remote_copy(
                src_ref=p_ref.at[mychip], dst_ref=out_ref.at[2 * cx + cy], send_sem=send_sems.at[j],
                recv_sem=recv_sems.at[j], device_id=(cx, cy, c), device_id_type=MESH)
                for j, (cx, cy) in enumerate(chips)]

        return mine, sends, arrivals

    def start(self, *refs):
        mine, sends, _ = self._parts(*refs)
        mine().start()
        for cp in sends():
            cp.start()

    def finish(self, *refs):
        mine, sends, arrivals = self._parts(*refs)
        for cp in arrivals():
            cp.wait_recv()
        for cp in sends():
            cp.wait_send()
        mine().wait()


def carried_call(body, jobs, grid, *, name, out_shape, in_specs, out_specs, scratch_shapes, operands):
    jobs = list(jobs)
    n_in, n_out, n_scr, nj = len(in_specs), len(out_specs), len(scratch_shapes), len(jobs)
    n_sem = [len(j.scratch) for j in jobs]

    def full_body(*refs):
        core_in = refs[:n_in]
        job_in = refs[n_in:n_in + nj]
        core_out = refs[n_in + nj:n_in + nj + n_out]
        job_out = refs[n_in + nj + n_out:n_in + 2 * nj + n_out]
        rest = refs[n_in + 2 * nj + n_out:]
        core_scr, sems, pos = rest[:n_scr], [], n_scr
        for n in n_sem:
            sems.append(rest[pos:pos + n])
            pos += n
        ids = [pl.program_id(a) for a in range(len(grid))]
        first = functools.reduce(jnp.logical_and, [i == 0 for i in ids])
        last = functools.reduce(jnp.logical_and, [i == g - 1 for i, g in zip(ids, grid)])
        if jobs:
            @pl.when(first)
            def _():
                for j, job in enumerate(jobs):
                    job.start(job_in[j], job_out[j], *sems[j])

        body(*core_in, *core_out, *core_scr)
        if jobs:
            @pl.when(last)
            def _():
                for j, job in enumerate(jobs):
                    job.finish(job_in[j], job_out[j], *sems[j])

    res = pl.pallas_call(
        full_body, name=name, grid=grid,
        out_shape=list(out_shape) + [j.out_shape for j in jobs],
        in_specs=list(in_specs) + [ANY] * nj, out_specs=list(out_specs) + [ANY] * nj,
        scratch_shapes=list(scratch_shapes) + [s for j in jobs for s in j.scratch],
        compiler_params=_params(dimension_semantics=("arbitrary",) * len(grid)),
    )(*operands, *[j.operand for j in jobs])
    return res[:n_out], res[n_out:]


def _exchange(job, name):
    def body(*refs):
        job.start(*refs)
        job.finish(*refs)

    return pl.pallas_call(
        body, name=name, out_shape=job.out_shape, in_specs=[ANY], out_specs=ANY, scratch_shapes=job.scratch,
        compiler_params=pltpu.CompilerParams(has_side_effects=True),
    )(job.operand)


def all_gather(shard, name):
    return _exchange(GatherJob(shard), name)


def exchange_pair(g, name):
    return _exchange(PairJob(g), name)


def exchange_chips(p, name):
    return _exchange(ChipsJob(p), name)


def pair_add(g, got, name):
    _, _, r, C = g.shape
    cb = _col_tile(r, C, g.dtype.itemsize)
    c = lax.axis_index("c")

    def body(c_ref, a_ref, b_ref, o_ref):
        o_ref[...] = (a_ref[...].astype(F32) + b_ref[...].astype(F32)).astype(o_ref.dtype)

    return pl.pallas_call(
        body, name=name, out_shape=jax.ShapeDtypeStruct((N_CHIPS, r, C), g.dtype),
        grid_spec=pltpu.PrefetchScalarGridSpec(
            num_scalar_prefetch=1, grid=(N_CHIPS, C // cb),
            in_specs=[pl.BlockSpec((None, None, r, cb), lambda q, j, cr: (q, cr[0], 0, j)),
                      pl.BlockSpec((None, r, cb), lambda q, j, cr: (q, 0, j))],
            out_specs=pl.BlockSpec((None, r, cb), lambda q, j, cr: (q, 0, j))),
        compiler_params=_params(dimension_semantics=("arbitrary", "arbitrary")),
    )(jnp.reshape(c, (1,)).astype(jnp.int32), g, got)


def sum_slots(parts, name):
    n, r, C = parts.shape
    cb = _col_tile(n * r, C, parts.dtype.itemsize)

    def body(p_ref, o_ref):
        acc = p_ref[0].astype(F32)
        for q in range(1, n):
            acc = acc + p_ref[q].astype(F32)
        o_ref[...] = acc

    return pl.pallas_call(
        body, name=name, out_shape=jax.ShapeDtypeStruct((r, C), F32), grid=(C // cb,),
        in_specs=[pl.BlockSpec((n, r, cb), lambda j: (0, 0, j))], out_specs=pl.BlockSpec((r, cb), lambda j: (0, j)),
        compiler_params=_params(dimension_semantics=("arbitrary",)),
    )(parts)


def reduce_scatter(g, tag):
    _, r, C = g.shape
    g4 = g.reshape(N_CHIPS, 2, r, C)
    got = exchange_pair(g4, f"rs_pair_{tag}")
    pairs = pair_add(g4, got, f"rs_pair_add_{tag}")
    parts = exchange_chips(pairs, f"rs_chips_{tag}")
    return sum_slots(parts, f"rs_sum_{tag}")


def all_reduce(v, tag):
    return sum_slots(all_gather(v, f"ar_gather_{tag}"), f"ar_sum_{tag}")


def adamw(w, g, m, v, name, jobs=()):
    R, C = w.shape
    tr = R
    for cand in (512, 256, 128, 64, 32, 16, 8):
        if R % cand == 0 and cand * C * 4 * 7 * 2 <= 40 * 1024 * 1024:
            tr = cand
            break
    c1 = 1.0 - ADAM_B1 ** ADAM_STEP
    c2 = 1.0 - ADAM_B2 ** ADAM_STEP

    def body(w_ref, g_ref, m_ref, v_ref, d_ref, nm_ref, nv_ref):
        gv = g_ref[...]
        nm = ADAM_B1 * m_ref[...] + (1.0 - ADAM_B1) * gv
        nv = ADAM_B2 * v_ref[...] + (1.0 - ADAM_B2) * (gv * gv)
        d_ref[...] = -ADAM_LR * ((nm / c1) / (jnp.sqrt(nv / c2) + ADAM_EPS) + ADAM_WD * w_ref[...])
        nm_ref[...] = nm
        nv_ref[...] = nv

    spec = pl.BlockSpec((tr, C), lambda i: (i, 0))
    res, exchanged = carried_call(
        body, jobs, (R // tr,), name=name, out_shape=[jax.ShapeDtypeStruct((R, C), F32)] * 3,
        in_specs=[spec] * 4, out_specs=[spec] * 3, scratch_shapes=[], operands=(w, g, m, v))
    return (tuple(res), exchanged) if jobs else tuple(res)


NAMES = ['mix_norm', 'ffn_norm', 'ffn_w_gu', 'ffn_w_down', 'sb_w_in', 'sb_q_norm', 'sb_k_norm', 'sb_w_out',
         'gdn_w_in', 'gdn_conv_w', 'gdn_a_log', 'gdn_dt_bias', 'gdn_o_norm', 'gdn_w_out', 'dsw_w_in', 'dsw_q_norm',
         'dsw_k_norm', 'dsw_w_out', 'lru_w_in', 'lru_conv_w', 'lru_conv_b', 'lru_w_a', 'lru_b_a', 'lru_w_x',
         'lru_b_x', 'lru_lambda', 'lru_w_out']
REPLICATED = ['mix_norm', 'ffn_norm', 'sb_q_norm', 'sb_k_norm', 'gdn_a_log', 'gdn_dt_bias', 'gdn_o_norm',
              'dsw_q_norm', 'dsw_k_norm']
SMALL_SHARDED = ['gdn_conv_w', 'lru_conv_w', 'lru_conv_b', 'lru_b_a', 'lru_b_x', 'lru_lambda']
IN_T = ['sb_w_in', 'gdn_w_in', 'dsw_w_in', 'lru_w_in']
OUT_N = ['sb_w_out', 'gdn_w_out', 'dsw_w_out', 'lru_w_out']


def _pack(arrs, pad_rows_to=8):
    flat = jnp.concatenate([a.reshape(-1) for a in arrs])
    n = flat.shape[0]
    rows = -(-n // LANES)
    rows = -(-rows // pad_rows_to) * pad_rows_to
    return jnp.pad(flat, (0, rows * LANES - n)).reshape(rows, LANES)


def _unpack(buf, shapes):
    flat = buf.reshape(-1)
    out, o = [], 0
    for s in shapes:
        n = math.prod(s)
        out.append(flat[o:o + n].reshape(s))
        o += n
    return out


def kernel(x, positions, mix_norm, ffn_norm, ffn_w_gu, ffn_w_down, sb_w_in, sb_q_norm, sb_k_norm, sb_w_out, gdn_w_in, gdn_conv_w, gdn_a_log, gdn_dt_bias, gdn_o_norm, gdn_w_out, dsw_w_in, dsw_q_norm, dsw_k_norm, dsw_w_out, lru_w_in, lru_conv_w, lru_conv_b, lru_w_a, lru_b_a, lru_w_x, lru_b_x, lru_lambda, lru_w_out, loss_target, m_mix_norm, m_ffn_norm, m_ffn_w_gu, m_ffn_w_down, m_sb_w_in, m_sb_q_norm, m_sb_k_norm, m_sb_w_out, m_gdn_w_in, m_gdn_conv_w, m_gdn_a_log, m_gdn_dt_bias, m_gdn_o_norm, m_gdn_w_out, m_dsw_w_in, m_dsw_q_norm, m_dsw_k_norm, m_dsw_w_out, m_lru_w_in, m_lru_conv_w, m_lru_conv_b, m_lru_w_a, m_lru_b_a, m_lru_w_x, m_lru_b_x, m_lru_lambda, m_lru_w_out, v_mix_norm, v_ffn_norm, v_ffn_w_gu, v_ffn_w_down, v_sb_w_in, v_sb_q_norm, v_sb_k_norm, v_sb_w_out, v_gdn_w_in, v_gdn_conv_w, v_gdn_a_log, v_gdn_dt_bias, v_gdn_o_norm, v_gdn_w_out, v_dsw_w_in, v_dsw_q_norm, v_dsw_k_norm, v_dsw_w_out, v_lru_w_in, v_lru_conv_w, v_lru_conv_b, v_lru_w_a, v_lru_b_a, v_lru_w_x, v_lru_b_x, v_lru_lambda, v_lru_w_out):
    args = locals()
    W = {n: args[n] for n in NAMES}
    M = {n: args["m_" + n] for n in NAMES}
    V = {n: args["v_" + n] for n in NAMES}
    T = x.shape[1]
    x2 = x[0]
    tgt = loss_target[0]

    S = {}
    for n in IN_T:
        S[n] = W[n][0].T.astype(BF16)
    for n in OUT_N:
        S[n] = W[n][0].astype(BF16)
    for i in range(DEPTH):
        S[f"ffn_gu{i}"] = ffn_w_gu[i].T.astype(BF16)
        S[f"ffn_down{i}"] = ffn_w_down[i].astype(BF16)
    S["lru_gates"] = jnp.concatenate(
        [lru_w_a[0].reshape(-1, LRU_BLOCK_DIM), lru_w_x[0].reshape(-1, LRU_BLOCK_DIM)], axis=0).astype(BF16)
    Gt = {}

    def flat(key):
        return Gt[key].reshape(-1, Gt[key].shape[-1])

    Gt["sb_w_in"] = all_gather(S["sb_w_in"], "ag_sb_w_in")
    Gt["sb_w_out"] = all_gather(S["sb_w_out"], "ag_sb_w_out")
    small_shapes = [W[n].shape for n in SMALL_SHARDED]
    sm = all_gather(_pack([W[n] for n in SMALL_SHARDED]), "ag_small")
    sm = [jnp.stack(parts) for parts in zip(*[_unpack(sm[p], small_shapes) for p in range(N_DEV)])]
    smd = dict(zip(SMALL_SHARDED, sm))

    def mixer_weights(i):
        kind = i % 4
        w = {"mix_norm": mix_norm[i:i + 1], "ffn_norm": ffn_norm[i:i + 1]}
        if kind == 0:
            w.update(sb_in_t=flat("sb_w_in"), sb_out=flat("sb_w_out"), sb_q_norm=sb_q_norm, sb_k_norm=sb_k_norm)
        elif kind == 1:
            w.update(gdn_in_t=flat("gdn_w_in"), gdn_out=flat("gdn_w_out"), gdn_a_log=gdn_a_log,
                     gdn_dt_bias=gdn_dt_bias, gdn_o_norm=gdn_o_norm,
                     gdn_conv_w=smd["gdn_conv_w"][:, 0].transpose(1, 0, 2).reshape(4, -1))
        elif kind == 2:
            w.update(dsw_in_t=flat("dsw_w_in"), dsw_out=flat("dsw_w_out"), dsw_q_norm=dsw_q_norm,
                     dsw_k_norm=dsw_k_norm)
        else:
            gg = Gt["lru_gates"].reshape(N_DEV, 2, LRU_BLOCKS, 32, LRU_BLOCK_DIM)
            gg = gg.transpose(1, 2, 0, 3, 4).reshape(2, LRU_BLOCKS, LRU_BLOCK_DIM, LRU_BLOCK_DIM).astype(F32)
            w.update(lru_in_t=flat("lru_w_in"), lru_out=flat("lru_w_out"), lru_w_a=gg[0], lru_w_x=gg[1],
                     lru_conv_w=smd["lru_conv_w"][:, 0].transpose(1, 0, 2).reshape(4, -1),
                     lru_conv_b=smd["lru_conv_b"][:, 0].reshape(1, -1),
                     lru_lambda=smd["lru_lambda"][:, 0].reshape(1, -1),
                     lru_b_a=smd["lru_b_a"][:, 0].transpose(1, 0, 2).reshape(1, -1),
                     lru_b_x=smd["lru_b_x"][:, 0].transpose(1, 0, 2).reshape(1, -1))
        return w

    carried_by_mixer = {0: ["ffn_gu0", "ffn_down0", "gdn_w_in", "gdn_w_out", "ffn_gu1", "ffn_down1"],
                        1: ["ffn_gu2", "ffn_down2", "lru_w_in", "lru_gates", "ffn_gu3", "ffn_down3"]}
    carried_by_ffn = {0: (["dsw_w_in"], ["dsw_w_out", "lru_w_out"])}

    half = ROPE_DIM // 2
    inv_freq = ROPE_THETA ** (-jnp.arange(half, dtype=F32) / half)
    ang = positions[0].astype(F32)[:, None] * inv_freq
    cs, sn = jnp.cos(ang), jnp.sin(ang)
    cosm = jnp.concatenate([cs, cs, jnp.ones((T, HEAD_DIM - ROPE_DIM), F32)], axis=1)
    sinm = jnp.concatenate([-sn, sn, jnp.zeros((T, HEAD_DIM - ROPE_DIM), F32)], axis=1)

    def gather_jobs(keys):
        return [GatherJob(S[k]) for k in keys]

    xs, f, seg_vjps, ffn_res = x2, None, [], []
    for i in range(DEPTH):
        keys = carried_by_mixer.get(i, [])
        shards = tuple(S[k] for k in keys)
        if i == 0:
            w0 = mixer_weights(0)
            qkv0, vjp_in = jax.vjp(sb_segment_in, xs, {k: w0[k] for k in ("mix_norm", "sb_in_t", "sb_q_norm", "sb_k_norm")})
            o0, tot0, gathered = _sb_fwd_call(*qkv0, shards)
            (xs, h), vjp_i = jax.vjp(sb_segment_out, xs, o0, {k: w0[k] for k in ("sb_out", "ffn_norm")})
        else:
            (xs, h), vjp_i, gathered = jax.vjp(
                lambda xx, ff, ww, i=i, shards=shards: mixer_segment(i, xx, ff, cosm, sinm, ww, shards),
                xs, f, mixer_weights(i), has_aux=True)
        Gt.update(zip(keys, gathered))
        seg_vjps.append(vjp_i)
        keys_gu, keys_down = carried_by_ffn.get(i, ([], []))
        h = h.astype(BF16)
        gu, got = mm_carry(h, flat(f"ffn_gu{i}"), "nt", F32, gather_jobs(keys_gu))
        Gt.update(zip(keys_gu, got))
        act = swiglu_fwd(gu)
        f, got = mm_carry(act, flat(f"ffn_down{i}"), "nn", F32, gather_jobs(keys_down))
        Gt.update(zip(keys_down, got))
        ffn_res.append((h, act, gu))
    loss_part, dy = loss_head(xs, f, tgt)
    loss = lax.psum(loss_part, ("x", "y", "c"))

    reduced, gw_small, gw_rep = {}, {}, {}

    def to_partials(i, dw, dwd, dwgu):
        out = {}
        if dwgu is not None:
            out = {f"ffn_gu{i}": dwgu.reshape(N_DEV, -1, D_MODEL), f"ffn_down{i}": dwd.reshape(N_DEV, -1, D_MODEL)}
        for k, g in dw.items():
            if k.endswith("_in_t"):
                out[k.replace("_in_t", "_w_in")] = g.reshape(N_DEV, -1, D_MODEL)
            elif k.endswith("_out"):
                out[k.replace("_out", "_w_out")] = g.reshape(N_DEV, -1, D_MODEL)
            elif k in ("mix_norm", "ffn_norm"):
                gw_rep[(k, i)] = g
            elif k in REPLICATED:
                gw_rep[k] = g
            elif k not in ("lru_w_a", "lru_w_x"):
                gw_small[k] = g
        if "lru_w_a" in dw:
            gg = jnp.stack([dw["lru_w_a"], dw["lru_w_x"]]).reshape(2, LRU_BLOCKS, N_DEV, 32, LRU_BLOCK_DIM)
            out["lru_gates"] = gg.transpose(2, 0, 1, 3, 4).reshape(N_DEV, 2 * LRU_BLOCKS * 32, LRU_BLOCK_DIM)
        return out

    def spread(keys, sizes, capacities):
        room, bins = list(capacities), [[] for _ in capacities]
        for k in sorted(keys, key=lambda k: -sizes[k]):
            b = max(range(len(room)), key=lambda j: room[j])
            bins[b].append(k)
            room[b] -= sizes[k]
        return bins

    ready, dx, df = {}, dy, dy
    for i in reversed(range(DEPTH)):
        h, act, gu = ffn_res[i]
        keys = list(ready)
        g4 = {k: ready[k].reshape(N_CHIPS, 2, *ready[k].shape[1:]) for k in keys}
        dact, got = mm_carry(df, flat(f"ffn_down{i}"), "nt", F32, [PairJob(g4[k]) for k in keys])
        pairs = {k: pair_add(g4[k], g, f"rs_pair_add_{k}") for k, g in zip(keys, got)}
        sizes = {k: math.prod(pairs[k].shape[1:]) * pairs[k].dtype.itemsize for k in keys}
        bins = spread(keys, sizes, [c * D_MODEL * 2 for c in (800, 1300, 1550)])
        dwd, p0 = mm_carry(act, df, "tn", BF16, [ChipsJob(pairs[k]) for k in bins[0]])
        dgu = swiglu_bwd(gu, dact)
        dh, p1 = mm_carry(dgu, flat(f"ffn_gu{i}"), "nn", F32, [ChipsJob(pairs[k]) for k in bins[1]])
        dwgu, p2 = mm_carry(dgu, h, "tn", BF16, [ChipsJob(pairs[k]) for k in bins[2]])
        for k, parts in zip(bins[0] + bins[1] + bins[2], list(p0) + list(p1) + list(p2)):
            reduced[k] = sum_slots(parts, f"rs_sum_{k}")
        if i > 0:
            dx, df, dw = seg_vjps[i]((dx, dh))
            ready = to_partials(i, dw, dwd, dwgu)
    dx_out, do0, dw = seg_vjps[0]((dx, dh))
    ready = to_partials(0, dw, dwd, dwgu)
    keys = list(ready)
    g4 = {k: ready[k].reshape(N_CHIPS, 2, *ready[k].shape[1:]) for k in keys}
    pairs = {k: pair_add(g4[k], exchange_pair(g4[k], f"rs_pair_{k}"), f"rs_pair_add_{k}") for k in keys}
    dq0, dk0, dv0, parts = _sb_bwd_call(*qkv0, tot0, do0, [ChipsJob(pairs[k]) for k in keys])
    for k, p in zip(keys, parts):
        reduced[k] = sum_slots(p, f"rs_sum_{k}")
    dx_in, dw = vjp_in((dq0, dk0, dv0))
    gx = dx_in + dx_out
    g4 = to_partials(0, dw, None, None)["sb_w_in"].reshape(N_CHIPS, 2, -1, D_MODEL)
    sb_in_pairs = pair_add(g4, exchange_pair(g4, "rs_pair_sb_w_in"), "rs_pair_add_sb_w_in")

    G = {}
    for n in IN_T[1:]:
        G[n] = reduced[n].T[None]
    for n in OUT_N:
        G[n] = reduced[n][None]
    G["ffn_w_gu"] = jnp.stack([reduced[f"ffn_gu{i}"].T for i in range(DEPTH)])
    G["ffn_w_down"] = jnp.stack([reduced[f"ffn_down{i}"] for i in range(DEPTH)])
    gg = reduced["lru_gates"].reshape(2, 1, LRU_BLOCKS, 32, LRU_BLOCK_DIM)
    G["lru_w_a"], G["lru_w_x"] = gg[0], gg[1]
    gw = dict(gw_small)
    for n in REPLICATED:
        gw[n] = (jnp.concatenate([gw_rep[(n, i)] for i in range(DEPTH)], axis=0) if n in ("mix_norm", "ffn_norm")
                 else gw_rep[n])
    gs = {
        "gdn_conv_w": gw["gdn_conv_w"].reshape(4, N_DEV, -1).transpose(1, 0, 2)[:, None],
        "lru_conv_w": gw["lru_conv_w"].reshape(4, N_DEV, -1).transpose(1, 0, 2)[:, None],
        "lru_conv_b": gw["lru_conv_b"].reshape(N_DEV, 1, -1),
        "lru_lambda": gw["lru_lambda"].reshape(N_DEV, 1, -1),
        "lru_b_a": gw["lru_b_a"].reshape(LRU_BLOCKS, N_DEV, 32).transpose(1, 0, 2)[:, None],
        "lru_b_x": gw["lru_b_x"].reshape(LRU_BLOCKS, N_DEV, 32).transpose(1, 0, 2)[:, None],
    }
    packed = jnp.stack([_pack([gs[n][p] for n in SMALL_SHARDED]) for p in range(N_DEV)])
    for n, g in zip(SMALL_SHARDED, _unpack(reduce_scatter(packed, "small"), small_shapes)):
        G[n] = g
    rep_shapes = [W[n].shape for n in REPLICATED]
    for n, g in zip(REPLICATED, _unpack(all_reduce(_pack([gw[n] for n in REPLICATED]), "rep"), rep_shapes)):
        G[n] = g

    D, NM, NV = {}, {}, {}
    big = [n for n in NAMES if n not in REPLICATED and n not in SMALL_SHARDED]
    assert IN_T[0] == "sb_w_in" and big.index("ffn_w_gu") < big.index("sb_w_in")
    for n in big:
        shp = W[n].shape
        two = (-1, shp[-1])
        operands = (W[n].reshape(two), G[n].reshape(two), M[n].reshape(two), V[n].reshape(two))
        if n == "ffn_w_gu":
            (d, nm, nv), (parts,) = adamw(*operands, f"adamw_{n}", [ChipsJob(sb_in_pairs)])
            G["sb_w_in"] = sum_slots(parts, "rs_sum_sb_w_in").T[None]
        else:
            d, nm, nv = adamw(*operands, f"adamw_{n}")
        D[n], NM[n], NV[n] = d.reshape(shp), nm.reshape(shp), nv.reshape(shp)
    for group, tag in ((SMALL_SHARDED, "small"), (REPLICATED, "rep")):
        shapes = [W[n].shape for n in group]
        res = adamw(_pack([W[n] for n in group]), _pack([G[n] for n in group]), _pack([M[n] for n in group]),
                    _pack([V[n] for n in group]), f"adamw_{tag}")
        for dst, buf in zip((D, NM, NV), res):
            for n, a in zip(group, _unpack(buf, shapes)):
                dst[n] = a

    return (loss, gx[None], *[G[n] for n in NAMES], *[D[n] for n in NAMES], *[NM[n] for n in NAMES],
            *[NV[n] for n in NAMES])
```

```python
import functools
import math

import jax
import jax.numpy as jnp
from jax import lax
from jax.experimental import pallas as pl
from jax.experimental.pallas import tpu as pltpu

F32 = jnp.float32
BF16 = jnp.bfloat16

D_MODEL = 2048
HEAD_DIM = 128
NORM_EPS = 1e-6
SB_HEADS = 16
SB_BLOCK = 256
SB_WIDE = (4, 2, 1)
GDN_K_HEADS = 16
GDN_V_HEADS = 32
GDN_KEY_DIM = 2048
GDN_VAL_DIM = 4096
GDN_CHUNK = 64
GDN_HEADS_PER_STEP = 4
GDN_GROUPS_PER_STEP = 1
DSW_GROUPS = ((128, 1), (512, 4), (2048, 16))
DSW_HG = 6
DSW_HEADS = 18
DSW_BLOCK = 128
ROPE_DIM = 32
ROPE_THETA = 500000.0
LRU_WIDTH = 2048
LRU_BLOCKS = 8
LRU_BLOCK_DIM = 256
LRU_C = 8.0
FFN_HIDDEN = 5632
DEPTH = 4
ADAM_LR, ADAM_B1, ADAM_B2, ADAM_EPS, ADAM_WD, ADAM_STEP = 0.001, 0.9, 0.999, 1e-08, 0.01, 10
N_DEV = 8
N_CHIPS = 4

V7X_VMEM_LIMIT = 56 * 1024 * 1024
LANES = 128
MESH = pl.DeviceIdType.MESH


def _params(**kw):
    return pltpu.CompilerParams(vmem_limit_bytes=V7X_VMEM_LIMIT, **kw)


def _pick(n, cands):
    for c in cands:
        if n % c == 0:
            return c
    return n


def _tile(n, cap, unit=LANES):
    best = None
    for t in range(unit, min(n, cap) + 1, unit):
        if n % t == 0:
            best = t
    return best or n


_DN = {"nn": (((1,), (0,)), ((), ())), "nt": (((1,), (1,)), ((), ())), "tn": (((0,), (0,)), ((), ()))}


def _raw_dot(a, b, mode):
    return lax.dot_general(a.astype(BF16), b.astype(BF16), _DN[mode], preferred_element_type=F32)


@functools.partial(jax.custom_vjp, nondiff_argnums=(2,))
def bdot(a, b, mode):
    return _raw_dot(a, b, mode)


def _bdot_fwd(a, b, mode):
    return _raw_dot(a, b, mode), (a, b)


def _bdot_bwd(mode, res, ct):
    a, b = res
    if mode == "nn":
        return bdot(ct, b, "nt"), bdot(a, ct, "tn")
    if mode == "nt":
        return bdot(ct, b, "nn"), bdot(ct, a, "tn")
    return bdot(b, ct, "nt"), bdot(a, ct, "nn")


bdot.defvjp(_bdot_fwd, _bdot_bwd)


def _split2(x):
    hi = x.astype(BF16)
    lo = (x - hi.astype(F32)).astype(BF16)
    return hi, lo


def _dot01_raw(x, m, dn, left):
    hi, lo = _split2(x)
    if left:
        return (lax.dot_general(m, hi, dn, preferred_element_type=F32)
                + lax.dot_general(m, lo, dn, preferred_element_type=F32))
    return (lax.dot_general(hi, m, dn, preferred_element_type=F32)
            + lax.dot_general(lo, m, dn, preferred_element_type=F32))


@functools.partial(jax.custom_vjp, nondiff_argnums=(2,))
def dot01(x, m, left):
    return _dot01_raw(x, m, _DN["nn"], left)


def _dot01_fwd(x, m, left):
    return _dot01_raw(x, m, _DN["nn"], left), m


def _dot01_bwd(left, m, ct):
    dx = _dot01_raw(ct, m, _DN["tn"] if left else _DN["nt"], left)
    return dx, jnp.zeros_like(m)


dot01.defvjp(_dot01_fwd, _dot01_bwd)


def _mm3(a, b):
    ah, al = _split2(a)
    bh, bl = _split2(b)
    dn = _DN["nn"]
    return (lax.dot_general(ah, bh, dn, preferred_element_type=F32)
            + lax.dot_general(ah, bl, dn, preferred_element_type=F32)
            + lax.dot_general(al, bh, dn, preferred_element_type=F32))


def _softplus_parts(z):
    e = jnp.exp(-jnp.abs(z))
    l = jnp.log(1.0 + e)
    sp = jnp.maximum(z, 0.0) + l
    ls = jnp.minimum(z, 0.0) - l
    inv = 1.0 / (1.0 + e)
    sig = jnp.where(z >= 0.0, inv, e * inv)
    return sp, ls, sig


def _rms(x, g):
    return x * lax.rsqrt(jnp.mean(x * x, axis=-1, keepdims=True) + NORM_EPS) * g


def _swap16(x):
    lane = lax.broadcasted_iota(jnp.int32, x.shape, 1)
    return jnp.where(lane < 16, pltpu.roll(x, 112, axis=1), jnp.where(lane < 32, pltpu.roll(x, 16, axis=1), 0.0))


@jax.custom_vjp
def rope(x, cosm, sinm):
    return x * cosm + _swap16(x) * sinm


def _rope_fwd(x, cosm, sinm):
    return rope(x, cosm, sinm), (cosm, sinm)


def _rope_bwd(res, ct):
    cosm, sinm = res
    return ct * cosm + _swap16(ct * sinm), jnp.zeros_like(cosm), jnp.zeros_like(sinm)


rope.defvjp(_rope_fwd, _rope_bwd)


def mm(a, b, mode, out_dtype=F32):
    return mm_carry(a, b, mode, out_dtype, ())[0]


def mm_carry(a, b, mode, out_dtype=F32, jobs=()):
    if mode == "nt":
        (M, K), N = a.shape, b.shape[0]
    elif mode == "nn":
        (M, K), N = a.shape, b.shape[1]
    else:
        (K, M), N = a.shape, b.shape[1]
    if mode == "nt":
        tm, tn, tk = _tile(M, 1024), _tile(N, 1024 if a.dtype == BF16 else 768), _tile(K, 2048)
    elif mode == "nn":
        tm, tn, tk = _tile(M, 1024), _tile(N, 2048), _tile(K, 1408)
    else:
        tm, tn = _tile(M, 1408), _tile(N, 2048)
        tk = _tile(K, 1024 if a.dtype == BF16 and b.dtype == BF16 else 512)
    nk = K // tk
    if mode == "nt":
        a_spec = pl.BlockSpec((tm, tk), lambda i, j, k: (i, k))
        b_spec = pl.BlockSpec((tn, tk), lambda i, j, k: (j, k))
    elif mode == "nn":
        a_spec = pl.BlockSpec((tm, tk), lambda i, j, k: (i, k))
        b_spec = pl.BlockSpec((tk, tn), lambda i, j, k: (k, j))
    else:
        a_spec = pl.BlockSpec((tk, tm), lambda i, j, k: (k, i))
        b_spec = pl.BlockSpec((tk, tn), lambda i, j, k: (k, j))

    def body(a_ref, b_ref, o_ref, *scr):
        p = _raw_dot(a_ref[...], b_ref[...], mode)
        if nk == 1:
            o_ref[...] = p.astype(o_ref.dtype)
        else:
            acc = scr[0]
            k = pl.program_id(2)

            @pl.when(k == 0)
            def _():
                acc[...] = p

            @pl.when(k > 0)
            def _():
                acc[...] += p

            @pl.when(k == nk - 1)
            def _():
                o_ref[...] = acc[...].astype(o_ref.dtype)

    grid = (M // tm, N // tn, nk)
    res = carried_call(
        body, jobs, grid, name=f"mm_{mode}_{M}x{N}x{K}" + ("_c" if jobs else ""),
        out_shape=[jax.ShapeDtypeStruct((M, N), out_dtype)], in_specs=[a_spec, b_spec],
        out_specs=[pl.BlockSpec((tm, tn), lambda i, j, k: (i, j))],
        scratch_shapes=[] if nk == 1 else [pltpu.VMEM((tm, tn), F32)], operands=(a, b))
    return res[0][0], res[1]


@jax.custom_vjp
def lin_t(x, wt):
    return mm(x, wt, "nt")


def _lin_t_fwd(x, wt):
    return mm(x, wt, "nt"), (x, wt)


def _lin_t_bwd(res, dy):
    x, wt = res
    return mm(dy, wt, "nn"), mm(dy, x, "tn", out_dtype=wt.dtype)


lin_t.defvjp(_lin_t_fwd, _lin_t_bwd)


@jax.custom_vjp
def lin_n(x, w):
    return mm(x, w, "nn")


def _lin_n_fwd(x, w):
    return mm(x, w, "nn"), (x, w)


def _lin_n_bwd(res, dy):
    x, w = res
    return mm(dy, w, "nt"), mm(x, dy, "tn", out_dtype=w.dtype)


lin_n.defvjp(_lin_n_fwd, _lin_n_bwd)


def _full_spec(shape):
    nd = len(shape)
    return pl.BlockSpec(tuple(shape), lambda i: (0,) * nd)


def _row_spec(tr, c):
    return pl.BlockSpec((tr, c), lambda i: (i, 0))


def _rowop_tr(total_cols, T):
    budget = 20 * 1024 * 1024
    for tr in (512, 256, 128, 64, 32, 16, 8):
        if T % tr == 0 and total_cols * tr * 4 * 2 <= budget:
            return tr
    return 8


def rowop(fn, name, rows, consts, params, out_cols):
    nr, nc, npar, nout = len(rows), len(consts), len(params), len(out_cols)
    T = rows[0].shape[0]
    in_cols = [r.shape[1] for r in rows] + [c.shape[1] for c in consts]
    tr_f = _rowop_tr(sum(in_cols) + sum(out_cols), T)
    tr_b = _rowop_tr(sum(in_cols) + sum(out_cols) + sum(r.shape[1] for r in rows), T)

    def fwd_call(rows, consts, params):
        def body(*refs):
            ins = [r[...] for r in refs[:nr + nc + npar]]
            outs = fn(*ins)
            for o_ref, o in zip(refs[nr + nc + npar:], outs):
                o_ref[...] = o.astype(F32)

        return pl.pallas_call(
            body, name=name + "_fwd", grid=(T // tr_f,),
            out_shape=[jax.ShapeDtypeStruct((T, c), F32) for c in out_cols],
            in_specs=[_row_spec(tr_f, c) for c in in_cols] + [_full_spec(p.shape) for p in params],
            out_specs=[_row_spec(tr_f, c) for c in out_cols],
            compiler_params=_params(dimension_semantics=("arbitrary",)),
        )(*rows, *consts, *params)

    def bwd_call(rows, consts, params, douts):
        def body(*refs):
            i = pl.program_id(0)
            rv = [r[...] for r in refs[:nr]]
            cv = [r[...] for r in refs[nr:nr + nc]]
            pv = [r[...] for r in refs[nr + nc:nr + nc + npar]]
            dv = [r[...] for r in refs[nr + nc + npar:nr + nc + npar + nout]]
            orefs = refs[nr + nc + npar + nout:]
            _, vjp = jax.vjp(lambda rr, pp: tuple(fn(*rr, *cv, *pp)), rv, pv)
            drows, dpars = vjp(tuple(dv))
            for o_ref, g in zip(orefs[:nr], drows):
                o_ref[...] = g.astype(F32)

            @pl.when(i == 0)
            def _():
                for o_ref in orefs[nr:]:
                    o_ref[...] = jnp.zeros(o_ref.shape, F32)

            for o_ref, g in zip(orefs[nr:], dpars):
                o_ref[...] += g.astype(F32)

        res = pl.pallas_call(
            body, name=name + "_bwd", grid=(T // tr_b,),
            out_shape=[jax.ShapeDtypeStruct(r.shape, F32) for r in rows]
            + [jax.ShapeDtypeStruct(p.shape, F32) for p in params],
            in_specs=[_row_spec(tr_b, c) for c in in_cols] + [_full_spec(p.shape) for p in params]
            + [_row_spec(tr_b, c) for c in out_cols],
            out_specs=[_row_spec(tr_b, r.shape[1]) for r in rows] + [_full_spec(p.shape) for p in params],
            compiler_params=_params(dimension_semantics=("arbitrary",)),
        )(*rows, *consts, *params, *douts)
        return tuple(res[:nr]), tuple(res[nr:])

    @jax.custom_vjp
    def op(rows, consts, params):
        return tuple(fwd_call(rows, consts, params))

    def op_fwd(rows, consts, params):
        return tuple(fwd_call(rows, consts, params)), (rows, consts, params)

    def op_bwd(res, douts):
        rows, consts, params = res
        drows, dpars = bwd_call(rows, consts, params, douts)
        return drows, tuple(jnp.zeros_like(c) for c in consts), dpars

    op.defvjp(op_fwd, op_bwd)
    return op(tuple(rows), tuple(consts), tuple(params))


def _fn_norm(x, g):
    return (_rms(x, g),)


def _fn_add_norm(x, y, g):
    s = x + y
    return s, _rms(s, g)


def _heads(x, n, width=HEAD_DIM):
    return [x[:, h * width:(h + 1) * width] for h in range(n)]


def _fn_sb_pre(qkv, qn, kn):
    hs = _heads(qkv, 3 * SB_HEADS)
    q = jnp.concatenate([_rms(h, qn) for h in hs[:SB_HEADS]], axis=1)
    k = jnp.concatenate([_rms(h, kn) for h in hs[SB_HEADS:2 * SB_HEADS]], axis=1)
    v = jnp.concatenate(hs[2 * SB_HEADS:], axis=1)
    return q, k, v


def _fn_dsw_pre(qkv, cosm, sinm, qn, kn):
    hs = _heads(qkv, 3 * DSW_HEADS)
    q = jnp.concatenate([rope(_rms(h, qn), cosm, sinm) for h in hs[:DSW_HEADS]], axis=1)
    k = jnp.concatenate([rope(_rms(h, kn), cosm, sinm) for h in hs[DSW_HEADS:2 * DSW_HEADS]], axis=1)
    v = jnp.concatenate(hs[2 * DSW_HEADS:], axis=1)
    return q, k, v


def _fn_dsw_combine(o, lse):
    os_, ls_ = _heads(o, DSW_HEADS), _heads(lse, DSW_HEADS)
    out = [None] * DSW_HEADS
    for hg in range(DSW_HG):
        l3 = [ls_[g * DSW_HG + hg] for g in range(3)]
        m = jnp.maximum(jnp.maximum(l3[0], l3[1]), l3[2])
        e3 = [jnp.exp(l - m) for l in l3]
        den = e3[0] + e3[1] + e3[2]
        for g in range(3):
            out[g * DSW_HG + hg] = os_[g * DSW_HG + hg] * (e3[g] / den)
    return (jnp.concatenate(out, axis=1),)


def _l2(x):
    return x * lax.rsqrt(jnp.sum(x * x, axis=-1, keepdims=True) + NORM_EPS)


def _fn_gdn_pre(qkv, ba, a_log, dt_bias):
    x = jax.nn.silu(qkv)
    hs = _heads(x, 2 * GDN_K_HEADS + GDN_V_HEADS)
    rep = GDN_V_HEADS // GDN_K_HEADS
    qh = [_l2(h) * HEAD_DIM ** -0.5 for h in hs[:GDN_K_HEADS]]
    kh = [_l2(h) for h in hs[GDN_K_HEADS:2 * GDN_K_HEADS]]
    q = jnp.concatenate([qh[h // rep] for h in range(GDN_V_HEADS)], axis=1)
    k = jnp.concatenate([kh[h // rep] for h in range(GDN_V_HEADS)], axis=1)
    v = jnp.concatenate(hs[2 * GDN_K_HEADS:], axis=1)
    b = ba[:, :GDN_V_HEADS]
    a = ba[:, GDN_V_HEADS:2 * GDN_V_HEADS]
    beta = jax.nn.sigmoid(b)
    g = -jnp.exp(a_log) * jax.nn.softplus(a + dt_bias)
    rows = b.shape[0]
    beta_b = jnp.concatenate([jnp.broadcast_to(beta[:, h:h + 1], (rows, HEAD_DIM)) for h in range(GDN_V_HEADS)], axis=1)
    g_b = jnp.concatenate([jnp.broadcast_to(g[:, h:h + 1], (rows, HEAD_DIM)) for h in range(GDN_V_HEADS)], axis=1)
    return q, k, v, g_b, beta_b


def _fn_gdn_post(o, z, o_norm):
    os_, zs = _heads(o, GDN_V_HEADS), _heads(z, GDN_V_HEADS)
    return (jnp.concatenate([_rms(oh, o_norm) * jax.nn.silu(zh) for oh, zh in zip(os_, zs)], axis=1),)


def _expm1(x):
    return jnp.tanh(0.5 * x) * (jnp.exp(x) + 1.0)


def _fn_lru_gates(xc, conv_b, w_a, b_a, w_x, b_x, lam):
    xr = xc + conv_b
    xs = _heads(xr, LRU_BLOCKS, LRU_BLOCK_DIM)
    r = jnp.concatenate([bdot(xs[n], w_a[n], "nn") for n in range(LRU_BLOCKS)], axis=1) + b_a
    i = jnp.concatenate([bdot(xs[n], w_x[n], "nn") for n in range(LRU_BLOCKS)], axis=1) + b_x
    r = jax.nn.sigmoid(r)
    i = jax.nn.sigmoid(i)
    log_a = -LRU_C * r * jax.nn.softplus(-lam)
    a = jnp.exp(log_a)
    u = jnp.sqrt(-_expm1(2.0 * log_a)) * (i * xr)
    return a, u


def _fn_lru_out(hs, gate):
    c = math.sqrt(2.0 / math.pi)
    gl = 0.5 * gate * (1.0 + jnp.tanh(c * (gate + 0.044715 * (gate * gate * gate))))
    return (hs * gl,)


def loss_head(x, f, target):
    T, D = x.shape
    tr = _pick(T, (256, 128, 64, 32, 16, 8))

    def body(x_ref, f_ref, t_ref, l_ref, dy_ref):
        i = pl.program_id(0)
        err = (x_ref[...] + f_ref[...]) - t_ref[...]
        dy_ref[...] = err * (1.0 / D)
        part = 0.5 * jnp.sum(jnp.mean(err * err, axis=-1, keepdims=True), axis=0, keepdims=True)

        @pl.when(i == 0)
        def _():
            l_ref[...] = jnp.zeros(l_ref.shape, F32)

        l_ref[...] += jnp.broadcast_to(part, l_ref.shape)

    l, dy = pl.pallas_call(
        body, name="loss_head", grid=(T // tr,),
        out_shape=[jax.ShapeDtypeStruct((8, LANES), F32), jax.ShapeDtypeStruct((T, D), F32)],
        in_specs=[_row_spec(tr, D)] * 3, out_specs=[_full_spec((8, LANES)), _row_spec(tr, D)],
        compiler_params=_params(dimension_semantics=("arbitrary",)),
    )(x, f, target)
    return l[0, 0], dy


def _as_bf16(x):
    return x.astype(BF16).astype(F32)


def _shift_rows(x, s):
    if s == 0:
        return x
    n = x.shape[0]
    row = lax.broadcasted_iota(jnp.int32, x.shape, 0)
    rolled = pltpu.roll(x, s % n, axis=0)
    keep = (row >= s) if s > 0 else (row < n + s)
    return jnp.where(keep, rolled, 0.0)


def _conv_fwd_call(x, w):
    T, C = x.shape
    K = w.shape[0]
    cb = _pick(C, (256, 128))

    def body(x_ref, w_ref, y_ref):
        xv, wv = _as_bf16(x_ref[...]), _as_bf16(w_ref[...])
        acc = xv * wv[K - 1:K, :]
        for k in range(K - 1):
            acc = acc + _shift_rows(xv, K - 1 - k) * wv[k:k + 1, :]
        y_ref[...] = acc

    return pl.pallas_call(
        body, name=f"conv_fwd_{C}", grid=(C // cb,), out_shape=jax.ShapeDtypeStruct((T, C), F32),
        in_specs=[pl.BlockSpec((T, cb), lambda j: (0, j)), pl.BlockSpec((K, cb), lambda j: (0, j))],
        out_specs=pl.BlockSpec((T, cb), lambda j: (0, j)),
        compiler_params=_params(dimension_semantics=("arbitrary",)),
    )(x, w)


def _conv_bwd_call(x, w, dy):
    T, C = x.shape
    K = w.shape[0]
    cb = _pick(C, (256, 128))

    def body(x_ref, w_ref, dy_ref, dx_ref, dw_ref):
        xv, dv, wv = _as_bf16(x_ref[...]), _as_bf16(dy_ref[...]), _as_bf16(w_ref[...])
        acc = dv * wv[K - 1:K, :]
        rows = [None] * K
        rows[K - 1] = jnp.sum(dv * xv, axis=0, keepdims=True)
        for k in range(K - 1):
            s = K - 1 - k
            acc = acc + _shift_rows(dv, -s) * wv[k:k + 1, :]
            rows[k] = jnp.sum(dv * _shift_rows(xv, s), axis=0, keepdims=True)
        dx_ref[...] = acc
        dw_ref[...] = jnp.concatenate(rows + [jnp.zeros((8 - K, cb), F32)], axis=0)

    dx, dw = pl.pallas_call(
        body, name=f"conv_bwd_{C}", grid=(C // cb,),
        out_shape=[jax.ShapeDtypeStruct((T, C), F32), jax.ShapeDtypeStruct((8, C), F32)],
        in_specs=[pl.BlockSpec((T, cb), lambda j: (0, j)), pl.BlockSpec((K, cb), lambda j: (0, j)),
                  pl.BlockSpec((T, cb), lambda j: (0, j))],
        out_specs=[pl.BlockSpec((T, cb), lambda j: (0, j)), pl.BlockSpec((8, cb), lambda j: (0, j))],
        compiler_params=_params(dimension_semantics=("arbitrary",)),
    )(x, w, dy)
    return dx, dw[:K]


@jax.custom_vjp
def dwconv(x, w):
    return _conv_fwd_call(x, w)


def _dwconv_fwd(x, w):
    return _conv_fwd_call(x, w), (x, w)


def _dwconv_bwd(res, dy):
    return _conv_bwd_call(*res, dy)


dwconv.defvjp(_dwconv_fwd, _dwconv_bwd)


def _scan_fwd_call(a, u):
    T, C = a.shape
    cb = _pick(C, (256, 128))

    def body(a_ref, u_ref, h_ref):
        def step(i, h):
            r = pl.multiple_of(i * 8, 8)
            at, ut = a_ref[pl.ds(r, 8), :], u_ref[pl.ds(r, 8), :]
            rows = []
            for j in range(8):
                h = at[j:j + 1, :] * h + ut[j:j + 1, :]
                rows.append(h)
            h_ref[pl.ds(r, 8), :] = jnp.concatenate(rows, axis=0)
            return h

        lax.fori_loop(0, T // 8, step, jnp.zeros((1, cb), F32))

    return pl.pallas_call(
        body, name="lru_scan_fwd", grid=(C // cb,), out_shape=jax.ShapeDtypeStruct((T, C), F32),
        in_specs=[pl.BlockSpec((T, cb), lambda j: (0, j))] * 2, out_specs=pl.BlockSpec((T, cb), lambda j: (0, j)),
        compiler_params=_params(dimension_semantics=("arbitrary",)),
    )(a, u)


def _scan_bwd_call(a, hs, dh):
    T, C = a.shape
    cb = _pick(C, (256, 128))
    nt = T // 8

    def body(a_ref, h_ref, dh_ref, da_ref, du_ref):
        def step(s, carry):
            i = nt - 1 - s
            r = pl.multiple_of(i * 8, 8)
            rp = pl.multiple_of(jnp.maximum(i - 1, 0) * 8, 8)
            at, ht, dt = a_ref[pl.ds(r, 8), :], h_ref[pl.ds(r, 8), :], dh_ref[pl.ds(r, 8), :]
            hprev_tile = h_ref[pl.ds(rp, 8), :]
            h_before = jnp.where(i > 0, hprev_tile[7:8, :], 0.0)
            da_rows, du_rows = [None] * 8, [None] * 8
            for j in range(7, -1, -1):
                lam = dt[j:j + 1, :] + carry
                du_rows[j] = lam
                hp = ht[j - 1:j, :] if j > 0 else h_before
                da_rows[j] = lam * hp
                carry = at[j:j + 1, :] * lam
            da_ref[pl.ds(r, 8), :] = jnp.concatenate(da_rows, axis=0)
            du_ref[pl.ds(r, 8), :] = jnp.concatenate(du_rows, axis=0)
            return carry

        lax.fori_loop(0, nt, step, jnp.zeros((1, cb), F32))

    return pl.pallas_call(
        body, name="lru_scan_bwd", grid=(C // cb,), out_shape=[jax.ShapeDtypeStruct((T, C), F32)] * 2,
        in_specs=[pl.BlockSpec((T, cb), lambda j: (0, j))] * 3,
        out_specs=[pl.BlockSpec((T, cb), lambda j: (0, j))] * 2,
        compiler_params=_params(dimension_semantics=("arbitrary",)),
    )(a, hs, dh)


@jax.custom_vjp
def lru_scan(a, u):
    return _scan_fwd_call(a, u)


def _lru_scan_fwd(a, u):
    hs = _scan_fwd_call(a, u)
    return hs, (a, hs)


def _lru_scan_bwd(res, dh):
    a, hs = res
    return tuple(_scan_bwd_call(a, hs, dh))


lru_scan.defvjp(_lru_scan_fwd, _lru_scan_bwd)


def _tri(n, kind):
    r = lax.broadcasted_iota(jnp.int32, (n, n), 0)
    c = lax.broadcasted_iota(jnp.int32, (n, n), 1)
    m = {"gt": r > c, "le": r <= c, "lt": r < c, "ge": r >= c, "eq": r == c}[kind]
    return jnp.where(m, 1.0, 0.0).astype(BF16)


def _sb_fwd_call(q, k, v, shards=()):
    T, HD = q.shape
    H = HD // HEAD_DIM
    tb = _pick(T, (SB_BLOCK, 128))
    scale = HEAD_DIM ** -0.5

    def body(q_ref, k_ref, v_ref, o_ref, tot_ref):
        i = pl.program_id(1)
        qb = q_ref[...].astype(BF16)
        u_gt = _tri(tb, "gt")
        row = lax.broadcasted_iota(jnp.int32, (tb, tb), 0)
        col = lax.broadcasted_iota(jnp.int32, (tb, tb), 1)

        def step(j, n, carry, diagonal):
            acc, run = carry
            off = pl.multiple_of(j * tb, tb)
            kb = k_ref[pl.ds(off, n * tb), :].astype(BF16)
            vb = v_ref[pl.ds(off, n * tb), :].astype(BF16)
            z = lax.dot_general(qb, kb, _DN["nt"], preferred_element_type=F32) * scale
            sp, ls, _ = _softplus_parts(z)
            if diagonal:
                sp = jnp.where(col < row, sp, 0.0)
            parts = [None] * n
            for s in reversed(range(n)):
                xs = sp[:, s * tb:(s + 1) * tb]
                parts[s] = _dot01_raw(xs, u_gt, _DN["nn"], False) + run
                run = run + jnp.sum(xs, axis=1, keepdims=True)
            between = parts[0] if n == 1 else jnp.concatenate(parts, axis=1)
            w = jnp.exp(ls - between)
            if diagonal:
                w = jnp.where(col < row, w, 0.0)
            acc = acc + lax.dot_general(w.astype(BF16), vb, _DN["nn"], preferred_element_type=F32)
            return acc, run

        carry = step(i, 1, (jnp.zeros((tb, HEAD_DIM), F32), jnp.zeros((tb, 1), F32)), True)
        left = i
        for n in SB_WIDE:
            carry = lax.fori_loop(0, lax.div(left, n), lambda t, c, n=n, left=left: step(left - n * (t + 1), n, c, False),
                                  carry)
            left = lax.rem(left, n)
        acc, run = carry
        o_ref[...] = acc
        tot_ref[...] = jnp.broadcast_to(run, (tb, HEAD_DIM))

    blk = pl.BlockSpec((tb, HEAD_DIM), lambda h, i: (i, h))
    full = pl.BlockSpec((T, HEAD_DIM), lambda h, i: (0, h))
    (o, tot), gathered = carried_call(
        body, [GatherJob(s) for s in shards], (H, T // tb), name="sb_attn_fwd",
        out_shape=[jax.ShapeDtypeStruct((T, HD), F32)] * 2, in_specs=[blk, full, full], out_specs=[blk, blk],
        scratch_shapes=[], operands=(q, k, v))
    return o, tot, tuple(gathered)


def _sb_bwd_call(q, k, v, tot, do, jobs=()):
    T, HD = q.shape
    H = HD // HEAD_DIM
    tb = _pick(T, (SB_BLOCK, 128))
    scale = HEAD_DIM ** -0.5

    def body(q_ref, k_ref, v_ref, tot_ref, do_ref, dq_ref, dk_ref, dv_ref):
        i = pl.program_id(1)

        @pl.when(i == 0)
        def _():
            dk_ref[...] = jnp.zeros(dk_ref.shape, F32)
            dv_ref[...] = jnp.zeros(dv_ref.shape, F32)

        qb = q_ref[...].astype(BF16)
        dob = do_ref[...].astype(BF16)
        tot = tot_ref[:, 0:1]
        u_le = _tri(tb, "le")
        u_lt = _tri(tb, "lt")
        row = lax.broadcasted_iota(jnp.int32, (tb, tb), 0)
        col = lax.broadcasted_iota(jnp.int32, (tb, tb), 1)

        def prefix_sums(x, u, start):
            out = []
            for s in range(x.shape[1] // tb):
                xs = x[:, s * tb:(s + 1) * tb]
                out.append(_dot01_raw(xs, u, _DN["nn"], False) + start)
                start = start + jnp.sum(xs, axis=1, keepdims=True)
            return (out[0] if len(out) == 1 else jnp.concatenate(out, axis=1)), start

        def step(j, n, carry, diagonal):
            dq, cs, cd = carry
            off = pl.multiple_of(j * tb, tb)
            kb = k_ref[pl.ds(off, n * tb), :].astype(BF16)
            vb = v_ref[pl.ds(off, n * tb), :].astype(BF16)
            z = lax.dot_general(qb, kb, _DN["nt"], preferred_element_type=F32) * scale
            sp, ls, sig = _softplus_parts(z)
            if diagonal:
                sp = jnp.where(col < row, sp, 0.0)
            prefix, cs = prefix_sums(sp, u_le, cs)
            w = jnp.exp(ls - (tot - prefix))
            if diagonal:
                w = jnp.where(col < row, w, 0.0)
            wb = w.astype(BF16)
            dv_ref[pl.ds(off, n * tb), :] += lax.dot_general(wb, dob, _DN["tn"], preferred_element_type=F32)
            dw = lax.dot_general(dob, vb, _DN["nt"], preferred_element_type=F32)
            dl = dw * w
            before, cd = prefix_sums(dl, u_lt, cd)
            dz = (dl * (1.0 - sig) - before * sig) * scale
            if diagonal:
                dz = jnp.where(col < row, dz, 0.0)
            dzb = dz.astype(BF16)
            dq = dq + lax.dot_general(dzb, kb, _DN["nn"], preferred_element_type=F32)
            dk_ref[pl.ds(off, n * tb), :] += lax.dot_general(dzb, qb, _DN["tn"], preferred_element_type=F32)
            return dq, cs, cd

        z1 = jnp.zeros((tb, 1), F32)
        carry, done = (jnp.zeros((tb, HEAD_DIM), F32), z1, z1), 0
        for n in SB_WIDE:
            trips = lax.div(i - done, n)
            carry = lax.fori_loop(0, trips, lambda t, c, n=n, done=done: step(done + n * t, n, c, False), carry)
            done = done + trips * n
        dq, _, _ = step(i, 1, carry, True)
        dq_ref[...] = dq

    blk = pl.BlockSpec((tb, HEAD_DIM), lambda h, i: (i, h))
    full = pl.BlockSpec((T, HEAD_DIM), lambda h, i: (0, h))
    grads, exchanged = carried_call(
        body, jobs, (H, T // tb), name="sb_attn_bwd", out_shape=[jax.ShapeDtypeStruct((T, HD), F32)] * 3,
        in_specs=[blk, full, full, blk, blk], out_specs=[blk, full, full], scratch_shapes=[],
        operands=(q, k, v, tot, do))
    return (*grads, exchanged) if jobs else tuple(grads)


def _dsw_tile(q, kp, kc, vp, vc, n):
    blk = DSW_BLOCK
    scale = HEAD_DIM ** -0.5
    qi = lax.broadcasted_iota(jnp.int32, (blk, blk), 0)
    kj = lax.broadcasted_iota(jnp.int32, (blk, blk), 1)
    neg = -1e30
    s_p = jnp.where((kj >= qi) & (n > 0), bdot(q, kp, "nt") * scale, neg)
    s_c = jnp.where(kj <= qi, bdot(q, kc, "nt") * scale, neg)
    m = jnp.maximum(jnp.max(s_p, axis=-1, keepdims=True), jnp.max(s_c, axis=-1, keepdims=True))
    p_p, p_c = jnp.exp(s_p - m), jnp.exp(s_c - m)
    den = jnp.sum(p_p, axis=-1, keepdims=True) + jnp.sum(p_c, axis=-1, keepdims=True)
    o = bdot(p_p / den, vp, "nn") + bdot(p_c / den, vc, "nn")
    lse = m + jnp.log(den)
    return o, jnp.broadcast_to(lse, (blk, HEAD_DIM))


DSW_HEADS_PER_STEP_FWD = 6
DSW_HEADS_PER_STEP_BWD = 3


def _dsw_specs(nsub, hps):
    w = hps * HEAD_DIM
    cur = pl.BlockSpec((None, DSW_BLOCK, w), lambda r, h, n: (r, n, h))
    prev = pl.BlockSpec((None, DSW_BLOCK, w), lambda r, h, n: (r, jnp.maximum(n - 1, 0), h))
    whole = pl.BlockSpec((None, nsub, w), lambda r, h, n: (r, 0, h))
    return cur, prev, whole


def _head(ref, h, rows=slice(None)):
    return ref[rows, h * HEAD_DIM:(h + 1) * HEAD_DIM]


def _dsw_fwd_call(q, k, v):
    d, nsub, HD = q.shape
    hps = DSW_HEADS_PER_STEP_FWD
    cur, prev, _ = _dsw_specs(nsub, hps)

    def body(q_ref, kp_ref, kc_ref, vp_ref, vc_ref, o_ref, l_ref):
        n = pl.program_id(2)
        for h in range(hps):
            o, l = _dsw_tile(*[_head(ref, h) for ref in (q_ref, kp_ref, kc_ref, vp_ref, vc_ref)], n)
            o_ref[:, h * HEAD_DIM:(h + 1) * HEAD_DIM] = o
            l_ref[:, h * HEAD_DIM:(h + 1) * HEAD_DIM] = l

    return pl.pallas_call(
        body, name=f"dsw_attn_fwd_d{d}", grid=(d, HD // HEAD_DIM // hps, nsub // DSW_BLOCK),
        out_shape=[jax.ShapeDtypeStruct(q.shape, F32)] * 2,
        in_specs=[cur, prev, cur, prev, cur], out_specs=[cur, cur],
        compiler_params=_params(dimension_semantics=("arbitrary",) * 3),
    )(q, k, k, v, v)


def _dsw_bwd_call(q, k, v, do, dl):
    d, nsub, HD = q.shape
    hps = DSW_HEADS_PER_STEP_BWD
    cur, prev, whole = _dsw_specs(nsub, hps)
    blk = DSW_BLOCK

    def body(q_ref, kp_ref, kc_ref, vp_ref, vc_ref, do_ref, dl_ref, dq_ref, dk_ref, dv_ref):
        n = pl.program_id(2)

        @pl.when(n == 0)
        def _():
            dk_ref[...] = jnp.zeros(dk_ref.shape, F32)
            dv_ref[...] = jnp.zeros(dv_ref.shape, F32)

        c0 = pl.multiple_of(n * blk, blk)
        p0 = pl.multiple_of(jnp.maximum(n - 1, 0) * blk, blk)
        for h in range(hps):
            cols = slice(h * HEAD_DIM, (h + 1) * HEAD_DIM)
            _, vjp = jax.vjp(lambda a, b, c, e, f: _dsw_tile(a, b, c, e, f, n),
                             *[_head(ref, h) for ref in (q_ref, kp_ref, kc_ref, vp_ref, vc_ref)])
            dq, dkp, dkc, dvp, dvc = vjp((_head(do_ref, h), _head(dl_ref, h)))
            dq_ref[:, cols] = dq
            dk_ref[pl.ds(c0, blk), cols] += dkc
            dv_ref[pl.ds(c0, blk), cols] += dvc
            dk_ref[pl.ds(p0, blk), cols] += dkp
            dv_ref[pl.ds(p0, blk), cols] += dvp

    return pl.pallas_call(
        body, name=f"dsw_attn_bwd_d{d}", grid=(d, HD // HEAD_DIM // hps, nsub // blk),
        out_shape=[jax.ShapeDtypeStruct(q.shape, F32)] * 3,
        in_specs=[cur, prev, cur, prev, cur, cur, cur], out_specs=[cur, whole, whole],
        compiler_params=_params(dimension_semantics=("arbitrary",) * 3),
    )(q, k, k, v, v, do, dl)


@jax.custom_vjp
def dsw_attn(q, k, v):
    return tuple(_dsw_fwd_call(q, k, v))


def _dsw_attn_fwd(q, k, v):
    return tuple(_dsw_fwd_call(q, k, v)), (q, k, v)


def _dsw_attn_bwd(res, cts):
    return tuple(_dsw_bwd_call(*res, *cts))


dsw_attn.defvjp(_dsw_attn_fwd, _dsw_attn_bwd)


_LOG2_CHUNK = GDN_CHUNK.bit_length() - 1
_LOG2_HEAD_DIM = HEAD_DIM.bit_length() - 1


def _unit_lower_inverse(a):
    n = a.shape[0]
    r = lax.broadcasted_iota(jnp.int32, (n, n), 0)
    c = lax.broadcasted_iota(jnp.int32, (n, n), 1)
    x = -a
    t = jnp.where(r == c, 1.0, 0.0) + x
    for _ in range(_LOG2_CHUNK - 1):
        x = _mm3(x, x)
        t = t + _mm3(t, x)
    return t


@jax.custom_vjp
def unit_lower_inverse(a):
    return _unit_lower_inverse(a)


def _unit_lower_inverse_fwd(a):
    t = _unit_lower_inverse(a)
    return t, t


def _unit_lower_inverse_bwd(t, ct):
    return (-_mm3(_mm3(t.T, ct), t.T),)


unit_lower_inverse.defvjp(_unit_lower_inverse_fwd, _unit_lower_inverse_bwd)


@jax.custom_vjp
def known_inverse(a, t):
    return t


def _known_inverse_fwd(a, t):
    return t, t


def _known_inverse_bwd(t, ct):
    return _unit_lower_inverse_bwd(t, ct)[0], jnp.zeros_like(t)


known_inverse.defvjp(_known_inverse_fwd, _known_inverse_bwd)


def _gdn_step(state, q, k, v, gb, bb, t_known=None):
    new_state, out, _ = _gdn_step_all(state, q, k, v, gb, bb, t_known)
    return new_state, out


def _gdn_step_all(state, q, k, v, gb, bb, t_known=None):
    C, NH = GDN_CHUNK, GDN_HEADS_PER_STEP
    R = NH * C
    r = lax.broadcasted_iota(jnp.int32, (R, R), 0)
    c = lax.broadcasted_iota(jnp.int32, (R, R), 1)
    same = lax.shift_right_logical(r, _LOG2_CHUNK) == lax.shift_right_logical(c, _LOG2_CHUNK)
    causal, strict = same & (r >= c), same & (r > c)
    gc = dot01(gb, jnp.where(causal, 1.0, 0.0).astype(BF16), True)
    g_sq = jnp.concatenate([gc] * (R // HEAD_DIM), axis=1)
    g_row = dot01(jnp.where(r == c, g_sq, 0.0), jnp.ones((R, R), BF16), True)
    decay = jnp.where(causal, jnp.exp(jnp.where(causal, g_sq - g_row, 0.0)), 0.0)
    kb, vb = k * bb, v * bb
    a_mat = jnp.where(strict, bdot(kb, k, "nt") * decay, 0.0)
    t_mat = unit_lower_inverse(a_mat) if t_known is None else known_inverse(a_mat, t_known)
    uw = bdot(t_mat, jnp.concatenate([vb, kb * jnp.exp(gc)], axis=1), "nn")
    u, w = uw[:, :HEAD_DIM], uw[:, HEAD_DIM:]
    hr = lax.shift_right_logical(lax.broadcasted_iota(jnp.int32, (R, NH * HEAD_DIM), 0), _LOG2_CHUNK)
    hc = lax.shift_right_logical(lax.broadcasted_iota(jnp.int32, (R, NH * HEAD_DIM), 1), _LOG2_HEAD_DIM)

    def widen(m):
        return jnp.where(hr == hc, jnp.concatenate([m] * NH, axis=1), 0.0)

    v_new = u - bdot(widen(w), state, "nn")
    attn = bdot(q, k, "nt") * decay
    out = bdot(widen(q * jnp.exp(gc)), state, "nn") + bdot(attn, v_new, "nn")
    last = [gc[h * C + C - 1:h * C + C, :] for h in range(NH)]
    g_last_rows = jnp.concatenate([jnp.broadcast_to(l, (C, HEAD_DIM)) for l in last], axis=0)
    g_last_state = jnp.concatenate([jnp.broadcast_to(l, (HEAD_DIM, HEAD_DIM)) for l in last], axis=0)
    k_dec = k * jnp.exp(g_last_rows - gc)
    new_state = state * jnp.exp(g_last_state) + bdot(widen(k_dec), v_new, "tn")
    return new_state, out, t_mat


def _gdn_stack(ref, g):
    h0 = g * GDN_HEADS_PER_STEP
    return jnp.concatenate([ref[:, (h0 + h) * HEAD_DIM:(h0 + h + 1) * HEAD_DIM] for h in range(GDN_HEADS_PER_STEP)],
                           axis=0)


def _gdn_unstack(ref, g, val):
    h0 = g * GDN_HEADS_PER_STEP
    for h in range(GDN_HEADS_PER_STEP):
        ref[:, (h0 + h) * HEAD_DIM:(h0 + h + 1) * HEAD_DIM] = val[h * GDN_CHUNK:(h + 1) * GDN_CHUNK]


def _gdn_fwd_call(q, k, v, gb, bb, shards=()):
    T, HD = v.shape
    H = HD // HEAD_DIM
    N = T // GDN_CHUNK
    hb = GDN_HEADS_PER_STEP * GDN_GROUPS_PER_STEP
    W = hb * HEAD_DIM
    SW = GDN_HEADS_PER_STEP * HEAD_DIM

    R = GDN_HEADS_PER_STEP * GDN_CHUNK

    def body(q_ref, k_ref, v_ref, g_ref, b_ref, o_ref, s_ref, t_ref, state):
        n = pl.program_id(1)

        @pl.when(n == 0)
        def _():
            state[...] = jnp.zeros(state.shape, F32)

        s_in = state[...]
        s_ref[...] = s_in
        res = [_gdn_step_all(s_in[g * SW:(g + 1) * SW],
                             *[_gdn_stack(ref, g) for ref in (q_ref, k_ref, v_ref, g_ref, b_ref)])
               for g in range(GDN_GROUPS_PER_STEP)]
        for g, (ns, o, t_mat) in enumerate(res):
            state[g * SW:(g + 1) * SW, :] = ns
            t_ref[g] = t_mat
            _gdn_unstack(o_ref, g, o)

    blk = pl.BlockSpec((GDN_CHUNK, W), lambda h, n: (n, h))
    sblk = pl.BlockSpec((None, W, HEAD_DIM), lambda h, n: (n, h, 0))
    tblk = pl.BlockSpec((None, GDN_GROUPS_PER_STEP, R, R), lambda h, n: (n, h, 0, 0))
    (o, states, inverses), gathered = carried_call(
        body, [GatherJob(s) for s in shards], (H // hb, N), name="gdn_chunk_fwd",
        out_shape=[jax.ShapeDtypeStruct((T, HD), F32), jax.ShapeDtypeStruct((N, H * HEAD_DIM, HEAD_DIM), F32),
                   jax.ShapeDtypeStruct((N, H // GDN_HEADS_PER_STEP, R, R), F32)],
        in_specs=[blk] * 5, out_specs=[blk, sblk, tblk], scratch_shapes=[pltpu.VMEM((W, HEAD_DIM), F32)],
        operands=(q, k, v, gb, bb))
    return o, states, inverses, tuple(gathered)


def _gdn_bwd_call(q, k, v, gb, bb, states, inverses, do):
    T, HD = v.shape
    H = HD // HEAD_DIM
    N = T // GDN_CHUNK
    hb = GDN_HEADS_PER_STEP * GDN_GROUPS_PER_STEP
    W = hb * HEAD_DIM
    SW = GDN_HEADS_PER_STEP * HEAD_DIM

    def body(q_ref, k_ref, v_ref, g_ref, b_ref, s_ref, t_ref, do_ref, dq_ref, dk_ref, dv_ref, dg_ref, db_ref, dstate):
        n = pl.program_id(1)

        @pl.when(n == 0)
        def _():
            dstate[...] = jnp.zeros(dstate.shape, F32)

        res = []
        for g in range(GDN_GROUPS_PER_STEP):
            rows = slice(g * SW, (g + 1) * SW)
            t_known = t_ref[g]
            _, vjp = jax.vjp(lambda *a: _gdn_step(*a, t_known=t_known), s_ref[rows, :],
                             *[_gdn_stack(ref, g) for ref in (q_ref, k_ref, v_ref, g_ref, b_ref)])
            res.append(vjp((dstate[rows, :], _gdn_stack(do_ref, g))))
        for g, (ds, *grads) in enumerate(res):
            dstate[g * SW:(g + 1) * SW, :] = ds
            for ref, grad in zip((dq_ref, dk_ref, dv_ref, dg_ref, db_ref), grads):
                _gdn_unstack(ref, g, grad)

    R = GDN_HEADS_PER_STEP * GDN_CHUNK
    blk = pl.BlockSpec((GDN_CHUNK, W), lambda h, n: (N - 1 - n, h))
    sblk = pl.BlockSpec((None, W, HEAD_DIM), lambda h, n: (N - 1 - n, h, 0))
    tblk = pl.BlockSpec((None, GDN_GROUPS_PER_STEP, R, R), lambda h, n: (N - 1 - n, h, 0, 0))
    return pl.pallas_call(
        body, name="gdn_chunk_bwd", grid=(H // hb, N), out_shape=[jax.ShapeDtypeStruct((T, HD), F32)] * 5,
        in_specs=[blk] * 5 + [sblk, tblk, blk], out_specs=[blk] * 5,
        scratch_shapes=[pltpu.VMEM((W, HEAD_DIM), F32)],
        compiler_params=_params(dimension_semantics=("arbitrary", "arbitrary")),
    )(q, k, v, gb, bb, states, inverses, do)


@jax.custom_vjp
def gdn_core(q, k, v, gb, bb, shards):
    o, _, _, gathered = _gdn_fwd_call(q, k, v, gb, bb, shards)
    return o, gathered


def _gdn_core_fwd(q, k, v, gb, bb, shards):
    o, states, inverses, gathered = _gdn_fwd_call(q, k, v, gb, bb, shards)
    return (o, gathered), (q, k, v, gb, bb, states, inverses, shards)


def _gdn_core_bwd(res, cts):
    *core, shards = res
    return (*_gdn_bwd_call(*core, cts[0]), tuple(jnp.zeros_like(s) for s in shards))


gdn_core.defvjp(_gdn_core_fwd, _gdn_core_bwd)


def _mixer_gdn(h, w, shards=()):
    wt = w["gdn_in_t"]
    nqkv = 2 * GDN_KEY_DIM + GDN_VAL_DIM
    qkv = lin_t(h, wt[:nqkv])
    z = lin_t(h, wt[nqkv:nqkv + GDN_VAL_DIM])
    w_ba = jnp.pad(wt[nqkv + GDN_VAL_DIM:], ((0, LANES - 2 * GDN_V_HEADS), (0, 0)))
    ba = lin_t(h, w_ba)
    qkv = dwconv(qkv, w["gdn_conv_w"])
    q, k, v, gb, bb = rowop(_fn_gdn_pre, "gdn_pre", [qkv, ba], [], [w["gdn_a_log"], w["gdn_dt_bias"]],
                            [GDN_VAL_DIM] * 5)
    o, gathered = gdn_core(q, k, v, gb, bb, tuple(shards))
    (y,) = rowop(_fn_gdn_post, "gdn_post", [o, z], [], [w["gdn_o_norm"]], [GDN_VAL_DIM])
    return lin_n(y, w["gdn_out"]), gathered


def _to_strided(x, cols, d):
    T = x.shape[0]
    return x[:, cols].reshape(T // d, d, -1).transpose(1, 0, 2)


def _from_strided(x):
    d, n, c = x.shape
    return x.transpose(1, 0, 2).reshape(d * n, c)


def _mixer_dsw(h, cosm, sinm, w):
    qkv = lin_t(h, w["dsw_in_t"])
    nhd = DSW_HEADS * HEAD_DIM
    q, k, v = rowop(_fn_dsw_pre, "dsw_pre", [qkv], [cosm, sinm], [w["dsw_q_norm"], w["dsw_k_norm"]], [nhd] * 3)
    outs, lses = [], []
    for gi, (_, d) in enumerate(DSW_GROUPS):
        cols = slice(gi * DSW_HG * HEAD_DIM, (gi + 1) * DSW_HG * HEAD_DIM)
        o_g, l_g = dsw_attn(_to_strided(q, cols, d), _to_strided(k, cols, d), _to_strided(v, cols, d))
        outs.append(_from_strided(o_g))
        lses.append(_from_strided(l_g))
    (o,) = rowop(_fn_dsw_combine, "dsw_combine", [jnp.concatenate(outs, axis=1), jnp.concatenate(lses, axis=1)],
                 [], [], [nhd])
    return lin_n(o, w["dsw_out"])


def _mixer_lru(h, w):
    wt = w["lru_in_t"]
    gate = lin_t(h, wt[:LRU_WIDTH])
    xr = dwconv(lin_t(h, wt[LRU_WIDTH:]), w["lru_conv_w"])
    a, u = rowop(_fn_lru_gates, "lru_gates", [xr], [],
                 [w["lru_conv_b"], w["lru_w_a"], w["lru_b_a"], w["lru_w_x"], w["lru_b_x"], w["lru_lambda"]],
                 [LRU_WIDTH] * 2)
    hs = lru_scan(a, u)
    (y,) = rowop(_fn_lru_out, "lru_out", [hs, gate], [], [], [LRU_WIDTH])
    return lin_n(y, w["lru_out"])


def mixer_segment(i, x, f_prev, cosm, sinm, w, shards):
    x, h = rowop(_fn_add_norm, "add_norm", [x, f_prev], [], [w["mix_norm"]], [D_MODEL] * 2)
    kind, gathered = i % 4, ()
    if kind == 1:
        y, gathered = _mixer_gdn(h, w, shards)
    elif kind == 2:
        y = _mixer_dsw(h, cosm, sinm, w)
    else:
        y = _mixer_lru(h, w)
    x, h = rowop(_fn_add_norm, "add_norm", [x, y], [], [w["ffn_norm"]], [D_MODEL] * 2)
    return (x, h), gathered


def sb_segment_in(x, w):
    (h,) = rowop(_fn_norm, "norm", [x], [], [w["mix_norm"]], [D_MODEL])
    qkv = lin_t(h, w["sb_in_t"])
    return rowop(_fn_sb_pre, "sb_pre", [qkv], [], [w["sb_q_norm"], w["sb_k_norm"]], [D_MODEL] * 3)


def sb_segment_out(x, o, w):
    return rowop(_fn_add_norm, "add_norm", [x, lin_n(o, w["sb_out"])], [], [w["ffn_norm"]], [D_MODEL] * 2)


def swiglu_fwd(gu):
    T, F2 = gu.shape
    F = F2 // 2
    tr = _pick(T, (128, 64, 32, 16))

    def body(gu_ref, o_ref):
        o_ref[...] = (jax.nn.silu(gu_ref[:, :F]) * gu_ref[:, F:]).astype(o_ref.dtype)

    return pl.pallas_call(
        body, name="swiglu_fwd", grid=(T // tr,), out_shape=jax.ShapeDtypeStruct((T, F), BF16),
        in_specs=[_row_spec(tr, F2)], out_specs=_row_spec(tr, F),
        compiler_params=_params(dimension_semantics=("arbitrary",)),
    )(gu)


def swiglu_bwd(gu, dact):
    T, F2 = gu.shape
    F = F2 // 2
    tr = _pick(T, (128, 64, 32, 16))

    def body(gu_ref, d_ref, o_ref):
        g, up, d = gu_ref[:, :F], gu_ref[:, F:], d_ref[...]
        s = jax.nn.sigmoid(g)
        o_ref[:, :F] = (d * up * (s * (1.0 + g * (1.0 - s)))).astype(o_ref.dtype)
        o_ref[:, F:] = (d * (g * s)).astype(o_ref.dtype)

    return pl.pallas_call(
        body, name="swiglu_bwd", grid=(T // tr,), out_shape=jax.ShapeDtypeStruct((T, F2), BF16),
        in_specs=[_row_spec(tr, F2), _row_spec(tr, F)], out_specs=_row_spec(tr, F2),
        compiler_params=_params(dimension_semantics=("arbitrary",)),
    )(gu, dact)


ANY = pl.BlockSpec(memory_space=pl.ANY)


SLAB_BYTES = 4 * 1024 * 1024


def _col_tile(rows, C, itemsize):
    return _tile(C, max(LANES, SLAB_BYTES // (rows * itemsize)))


class GatherJob:
    def __init__(self, shard):
        self.operand = shard
        r, C = shard.shape
        self.out_shape = jax.ShapeDtypeStruct((N_DEV, r, C), shard.dtype)
        self.scratch = [pltpu.SemaphoreType.DMA((7,)), pltpu.SemaphoreType.DMA((7,)), pltpu.SemaphoreType.DMA]

    def _parts(self, x_ref, out_ref, send_sems, recv_sems, local_sem):
        x, y, c = lax.axis_index("x"), lax.axis_index("y"), lax.axis_index("c")
        me, sibling = (x, y, c), (x, y, 1 - c)
        chips = [(1 - x, y), (x, 1 - y), (1 - x, 1 - y)]

        def slot(px, py, pc):
            return out_ref.at[4 * px + 2 * py + pc]

        def copy(k, block, to, src=None):
            return pltpu.make_async_remote_copy(
                src_ref=slot(*block) if src is None else src, dst_ref=slot(*block),
                send_sem=send_sems.at[k], recv_sem=recv_sems.at[k], device_id=to, device_id_type=MESH)

        def mine():
            return pltpu.make_async_copy(x_ref, slot(*me), local_sem)

        def first():
            return [copy(0, me, sibling, src=x_ref)] + [copy(1 + j, me, (*chip, c), src=x_ref)
                                                        for j, chip in enumerate(chips)]

        def passed():
            return [copy(4 + j, (*chip, c), sibling) for j, chip in enumerate(chips)]

        def landed():
            return [copy(1 + j, (*chip, c), me) for j, chip in enumerate(chips)]

        def from_sibling():
            return [copy(0, sibling, me)] + [copy(4 + j, (*chip, 1 - c), me) for j, chip in enumerate(chips)]

        return mine, first, passed, landed, from_sibling

    def start(self, x_ref, out_ref, *sems):
        mine, first, _, _, _ = self._parts(x_ref, out_ref, *sems)
        mine().start()
        for cp in first():
            cp.start()

    def finish(self, x_ref, out_ref, *sems):
        mine, first, passed, landed, from_sibling = self._parts(x_ref, out_ref, *sems)
        onward = passed()
        for arrived, cp in zip(landed(), onward):
            arrived.wait_recv()
            cp.start()
        for cp in from_sibling():
            cp.wait_recv()
        for cp in first() + onward:
            cp.wait_send()
        mine().wait()


class PairJob:
    def __init__(self, g):
        self.operand = g
        _, _, r, C = g.shape
        self.out_shape = jax.ShapeDtypeStruct((N_CHIPS, r, C), g.dtype)
        self.scratch = [pltpu.SemaphoreType.DMA((N_CHIPS,)), pltpu.SemaphoreType.DMA((N_CHIPS,))]

    def _copies(self, g_ref, out_ref, send_sems, recv_sems):
        x, y, c = lax.axis_index("x"), lax.axis_index("y"), lax.axis_index("c")
        return [pltpu.make_async_remote_copy(
            src_ref=g_ref.at[q, 1 - c], dst_ref=out_ref.at[q], send_sem=send_sems.at[q], recv_sem=recv_sems.at[q],
            device_id=(x, y, 1 - c), device_id_type=MESH) for q in range(N_CHIPS)]

    def start(self, *refs):
        for cp in self._copies(*refs):
            cp.start()

    def finish(self, *refs):
        for cp in self._copies(*refs):
            cp.wait()


class ChipsJob:
    def __init__(self, p):
        self.operand = p
        self.out_shape = jax.ShapeDtypeStruct(p.shape, p.dtype)
        self.scratch = [pltpu.SemaphoreType.DMA((3,)), pltpu.SemaphoreType.DMA((3,)), pltpu.SemaphoreType.DMA]

    def _parts(self, p_ref, out_ref, send_sems, recv_sems, local_sem):
        x, y, c = lax.axis_index("x"), lax.axis_index("y"), lax.axis_index("c")
        mychip = 2 * x + y
        chips = [(1 - x, y), (x, 1 - y), (1 - x, 1 - y)]
        def mine():
            return pltpu.make_async_copy(p_ref.at[mychip], out_ref.at[mychip], local_sem)

        def sends():
            return [pltpu.make_async_remote_copy(
                src_ref=p_ref.at[2 * cx + cy], dst_ref=out_ref.at[mychip], send_sem=send_sems.at[j],
                recv_sem=recv_sems.at[j], device_id=(cx, cy, c), device_id_type=MESH)
                for j, (cx, cy) in enumerate(chips)]

        def arrivals():
            return [pltpu.make_async_remote_copy(
                src_ref=p_ref.at[mychip], dst_ref=out_ref.at[2 * cx + cy], send_sem=send_sems.at[j],
                recv_sem=recv_sems.at[j], device_id=(cx, cy, c), device_id_type=MESH)
                for j, (cx, cy) in enumerate(chips)]

        return mine, sends, arrivals

    def start(self, *refs):
        mine, sends, _ = self._parts(*refs)
        mine().start()
        for cp in sends():
            cp.start()

    def finish(self, *refs):
        mine, sends, arrivals = self._parts(*refs)
        for cp in arrivals():
            cp.wait_recv()
        for cp in sends():
            cp.wait_send()
        mine().wait()


def carried_call(body, jobs, grid, *, name, out_shape, in_specs, out_specs, scratch_shapes, operands):
    jobs = list(jobs)
    n_in, n_out, n_scr, nj = len(in_specs), len(out_specs), len(scratch_shapes), len(jobs)
    n_sem = [len(j.scratch) for j in jobs]

    def full_body(*refs):
        core_in = refs[:n_in]
        job_in = refs[n_in:n_in + nj]
        core_out = refs[n_in + nj:n_in + nj + n_out]
        job_out = refs[n_in + nj + n_out:n_in + 2 * nj + n_out]
        rest = refs[n_in + 2 * nj + n_out:]
        core_scr, sems, pos = rest[:n_scr], [], n_scr
        for n in n_sem:
            sems.append(rest[pos:pos + n])
            pos += n
        ids = [pl.program_id(a) for a in range(len(grid))]
        first = functools.reduce(jnp.logical_and, [i == 0 for i in ids])
        last = functools.reduce(jnp.logical_and, [i == g - 1 for i, g in zip(ids, grid)])
        if jobs:
            @pl.when(first)
            def _():
                for j, job in enumerate(jobs):
                    job.start(job_in[j], job_out[j], *sems[j])

        body(*core_in, *core_out, *core_scr)
        if jobs:
            @pl.when(last)
            def _():
                for j, job in enumerate(jobs):
                    job.finish(job_in[j], job_out[j], *sems[j])

    res = pl.pallas_call(
        full_body, name=name, grid=grid,
        out_shape=list(out_shape) + [j.out_shape for j in jobs],
        in_specs=list(in_specs) + [ANY] * nj, out_specs=list(out_specs) + [ANY] * nj,
        scratch_shapes=list(scratch_shapes) + [s for j in jobs for s in j.scratch],
        compiler_params=_params(dimension_semantics=("arbitrary",) * len(grid)),
    )(*operands, *[j.operand for j in jobs])
    return res[:n_out], res[n_out:]


def _exchange(job, name):
    def body(*refs):
        job.start(*refs)
        job.finish(*refs)

    return pl.pallas_call(
        body, name=name, out_shape=job.out_shape, in_specs=[ANY], out_specs=ANY, scratch_shapes=job.scratch,
        compiler_params=pltpu.CompilerParams(has_side_effects=True),
    )(job.operand)


def all_gather(shard, name):
    return _exchange(GatherJob(shard), name)


def exchange_pair(g, name):
    return _exchange(PairJob(g), name)


def exchange_chips(p, name):
    return _exchange(ChipsJob(p), name)


def pair_add(g, got, name):
    _, _, r, C = g.shape
    cb = _col_tile(r, C, g.dtype.itemsize)
    c = lax.axis_index("c")

    def body(c_ref, a_ref, b_ref, o_ref):
        o_ref[...] = (a_ref[...].astype(F32) + b_ref[...].astype(F32)).astype(o_ref.dtype)

    return pl.pallas_call(
        body, name=name, out_shape=jax.ShapeDtypeStruct((N_CHIPS, r, C), g.dtype),
        grid_spec=pltpu.PrefetchScalarGridSpec(
            num_scalar_prefetch=1, grid=(N_CHIPS, C // cb),
            in_specs=[pl.BlockSpec((None, None, r, cb), lambda q, j, cr: (q, cr[0], 0, j)),
                      pl.BlockSpec((None, r, cb), lambda q, j, cr: (q, 0, j))],
            out_specs=pl.BlockSpec((None, r, cb), lambda q, j, cr: (q, 0, j))),
        compiler_params=_params(dimension_semantics=("arbitrary", "arbitrary")),
    )(jnp.reshape(c, (1,)).astype(jnp.int32), g, got)


def sum_slots(parts, name):
    n, r, C = parts.shape
    cb = _col_tile(n * r, C, parts.dtype.itemsize)

    def body(p_ref, o_ref):
        acc = p_ref[0].astype(F32)
        for q in range(1, n):
            acc = acc + p_ref[q].astype(F32)
        o_ref[...] = acc

    return pl.pallas_call(
        body, name=name, out_shape=jax.ShapeDtypeStruct((r, C), F32), grid=(C // cb,),
        in_specs=[pl.BlockSpec((n, r, cb), lambda j: (0, 0, j))], out_specs=pl.BlockSpec((r, cb), lambda j: (0, j)),
        compiler_params=_params(dimension_semantics=("arbitrary",)),
    )(parts)


def reduce_scatter(g, tag):
    _, r, C = g.shape
    g4 = g.reshape(N_CHIPS, 2, r, C)
    got = exchange_pair(g4, f"rs_pair_{tag}")
    pairs = pair_add(g4, got, f"rs_pair_add_{tag}")
    parts = exchange_chips(pairs, f"rs_chips_{tag}")
    return sum_slots(parts, f"rs_sum_{tag}")


def all_reduce(v, tag):
    return sum_slots(all_gather(v, f"ar_gather_{tag}"), f"ar_sum_{tag}")


def adamw(w, g, m, v, name, jobs=()):
    R, C = w.shape
    tr = R
    for cand in (512, 256, 128, 64, 32, 16, 8):
        if R % cand == 0 and cand * C * 4 * 7 * 2 <= 40 * 1024 * 1024:
            tr = cand
            break
    c1 = 1.0 - ADAM_B1 ** ADAM_STEP
    c2 = 1.0 - ADAM_B2 ** ADAM_STEP

    def body(w_ref, g_ref, m_ref, v_ref, d_ref, nm_ref, nv_ref):
        gv = g_ref[...]
        nm = ADAM_B1 * m_ref[...] + (1.0 - ADAM_B1) * gv
        nv = ADAM_B2 * v_ref[...] + (1.0 - ADAM_B2) * (gv * gv)
        d_ref[...] = -ADAM_LR * ((nm / c1) / (jnp.sqrt(nv / c2) + ADAM_EPS) + ADAM_WD * w_ref[...])
        nm_ref[...] = nm
        nv_ref[...] = nv

    spec = pl.BlockSpec((tr, C), lambda i: (i, 0))
    res, exchanged = carried_call(
        body, jobs, (R // tr,), name=name, out_shape=[jax.ShapeDtypeStruct((R, C), F32)] * 3,
        in_specs=[spec] * 4, out_specs=[spec] * 3, scratch_shapes=[], operands=(w, g, m, v))
    return (tuple(res), exchanged) if jobs else tuple(res)


NAMES = ['mix_norm', 'ffn_norm', 'ffn_w_gu', 'ffn_w_down', 'sb_w_in', 'sb_q_norm', 'sb_k_norm', 'sb_w_out',
         'gdn_w_in', 'gdn_conv_w', 'gdn_a_log', 'gdn_dt_bias', 'gdn_o_norm', 'gdn_w_out', 'dsw_w_in', 'dsw_q_norm',
         'dsw_k_norm', 'dsw_w_out', 'lru_w_in', 'lru_conv_w', 'lru_conv_b', 'lru_w_a', 'lru_b_a', 'lru_w_x',
         'lru_b_x', 'lru_lambda', 'lru_w_out']
REPLICATED = ['mix_norm', 'ffn_norm', 'sb_q_norm', 'sb_k_norm', 'gdn_a_log', 'gdn_dt_bias', 'gdn_o_norm',
              'dsw_q_norm', 'dsw_k_norm']
SMALL_SHARDED = ['gdn_conv_w', 'lru_conv_w', 'lru_conv_b', 'lru_b_a', 'lru_b_x', 'lru_lambda']
IN_T = ['sb_w_in', 'gdn_w_in', 'dsw_w_in', 'lru_w_in']
OUT_N = ['sb_w_out', 'gdn_w_out', 'dsw_w_out', 'lru_w_out']


def _pack(arrs, pad_rows_to=8):
    flat = jnp.concatenate([a.reshape(-1) for a in arrs])
    n = flat.shape[0]
    rows = -(-n // LANES)
    rows = -(-rows // pad_rows_to) * pad_rows_to
    return jnp.pad(flat, (0, rows * LANES - n)).reshape(rows, LANES)


def _unpack(buf, shapes):
    flat = buf.reshape(-1)
    out, o = [], 0
    for s in shapes:
        n = math.prod(s)
        out.append(flat[o:o + n].reshape(s))
        o += n
    return out


def kernel(x, positions, mix_norm, ffn_norm, ffn_w_gu, ffn_w_down, sb_w_in, sb_q_norm, sb_k_norm, sb_w_out, gdn_w_in, gdn_conv_w, gdn_a_log, gdn_dt_bias, gdn_o_norm, gdn_w_out, dsw_w_in, dsw_q_norm, dsw_k_norm, dsw_w_out, lru_w_in, lru_conv_w, lru_conv_b, lru_w_a, lru_b_a, lru_w_x, lru_b_x, lru_lambda, lru_w_out, loss_target, m_mix_norm, m_ffn_norm, m_ffn_w_gu, m_ffn_w_down, m_sb_w_in, m_sb_q_norm, m_sb_k_norm, m_sb_w_out, m_gdn_w_in, m_gdn_conv_w, m_gdn_a_log, m_gdn_dt_bias, m_gdn_o_norm, m_gdn_w_out, m_dsw_w_in, m_dsw_q_norm, m_dsw_k_norm, m_dsw_w_out, m_lru_w_in, m_lru_conv_w, m_lru_conv_b, m_lru_w_a, m_lru_b_a, m_lru_w_x, m_lru_b_x, m_lru_lambda, m_lru_w_out, v_mix_norm, v_ffn_norm, v_ffn_w_gu, v_ffn_w_down, v_sb_w_in, v_sb_q_norm, v_sb_k_norm, v_sb_w_out, v_gdn_w_in, v_gdn_conv_w, v_gdn_a_log, v_gdn_dt_bias, v_gdn_o_norm, v_gdn_w_out, v_dsw_w_in, v_dsw_q_norm, v_dsw_k_norm, v_dsw_w_out, v_lru_w_in, v_lru_conv_w, v_lru_conv_b, v_lru_w_a, v_lru_b_a, v_lru_w_x, v_lru_b_x, v_lru_lambda, v_lru_w_out):
    args = locals()
    W = {n: args[n] for n in NAMES}
    M = {n: args["m_" + n] for n in NAMES}
    V = {n: args["v_" + n] for n in NAMES}
    T = x.shape[1]
    x2 = x[0]
    tgt = loss_target[0]

    S = {}
    for n in IN_T:
        S[n] = W[n][0].T.astype(BF16)
    for n in OUT_N:
        S[n] = W[n][0].astype(BF16)
    for i in range(DEPTH):
        S[f"ffn_gu{i}"] = ffn_w_gu[i].T.astype(BF16)
        S[f"ffn_down{i}"] = ffn_w_down[i].astype(BF16)
    S["lru_gates"] = jnp.concatenate(
        [lru_w_a[0].reshape(-1, LRU_BLOCK_DIM), lru_w_x[0].reshape(-1, LRU_BLOCK_DIM)], axis=0).astype(BF16)
    Gt = {}

    def flat(key):
        return Gt[key].reshape(-1, Gt[key].shape[-1])

    Gt["sb_w_in"] = all_gather(S["sb_w_in"], "ag_sb_w_in")
    Gt["sb_w_out"] = all_gather(S["sb_w_out"], "ag_sb_w_out")
    small_shapes = [W[n].shape for n in SMALL_SHARDED]
    sm = all_gather(_pack([W[n] for n in SMALL_SHARDED]), "ag_small")
    sm = [jnp.stack(parts) for parts in zip(*[_unpack(sm[p], small_shapes) for p in range(N_DEV)])]
    smd = dict(zip(SMALL_SHARDED, sm))

    def mixer_weights(i):
        kind = i % 4
        w = {"mix_norm": mix_norm[i:i + 1], "ffn_norm": ffn_norm[i:i + 1]}
        if kind == 0:
            w.update(sb_in_t=flat("sb_w_in"), sb_out=flat("sb_w_out"), sb_q_norm=sb_q_norm, sb_k_norm=sb_k_norm)
        elif kind == 1:
            w.update(gdn_in_t=flat("gdn_w_in"), gdn_out=flat("gdn_w_out"), gdn_a_log=gdn_a_log,
                     gdn_dt_bias=gdn_dt_bias, gdn_o_norm=gdn_o_norm,
                     gdn_conv_w=smd["gdn_conv_w"][:, 0].transpose(1, 0, 2).reshape(4, -1))
        elif kind == 2:
            w.update(dsw_in_t=flat("dsw_w_in"), dsw_out=flat("dsw_w_out"), dsw_q_norm=dsw_q_norm,
                     dsw_k_norm=dsw_k_norm)
        else:
            gg = Gt["lru_gates"].reshape(N_DEV, 2, LRU_BLOCKS, 32, LRU_BLOCK_DIM)
            gg = gg.transpose(1, 2, 0, 3, 4).reshape(2, LRU_BLOCKS, LRU_BLOCK_DIM, LRU_BLOCK_DIM).astype(F32)
            w.update(lru_in_t=flat("lru_w_in"), lru_out=flat("lru_w_out"), lru_w_a=gg[0], lru_w_x=gg[1],
                     lru_conv_w=smd["lru_conv_w"][:, 0].transpose(1, 0, 2).reshape(4, -1),
                     lru_conv_b=smd["lru_conv_b"][:, 0].reshape(1, -1),
                     lru_lambda=smd["lru_lambda"][:, 0].reshape(1, -1),
                     lru_b_a=smd["lru_b_a"][:, 0].transpose(1, 0, 2).reshape(1, -1),
                     lru_b_x=smd["lru_b_x"][:, 0].transpose(1, 0, 2).reshape(1, -1))
        return w

    carried_by_mixer = {0: ["ffn_gu0", "ffn_down0", "gdn_w_in", "gdn_w_out", "ffn_gu1", "ffn_down1"],
                        1: ["ffn_gu2", "ffn_down2", "lru_w_in", "lru_gates", "ffn_gu3", "ffn_down3"]}
    carried_by_ffn = {0: (["dsw_w_in"], ["dsw_w_out", "lru_w_out"])}

    half = ROPE_DIM // 2
    inv_freq = ROPE_THETA ** (-jnp.arange(half, dtype=F32) / half)
    ang = positions[0].astype(F32)[:, None] * inv_freq
    cs, sn = jnp.cos(ang), jnp.sin(ang)
    cosm = jnp.concatenate([cs, cs, jnp.ones((T, HEAD_DIM - ROPE_DIM), F32)], axis=1)
    sinm = jnp.concatenate([-sn, sn, jnp.zeros((T, HEAD_DIM - ROPE_DIM), F32)], axis=1)

    def gather_jobs(keys):
        return [GatherJob(S[k]) for k in keys]

    xs, f, seg_vjps, ffn_res = x2, None, [], []
    for i in range(DEPTH):
        keys = carried_by_mixer.get(i, [])
        shards = tuple(S[k] for k in keys)
        if i == 0:
            w0 = mixer_weights(0)
            qkv0, vjp_in = jax.vjp(sb_segment_in, xs, {k: w0[k] for k in ("mix_norm", "sb_in_t", "sb_q_norm", "sb_k_norm")})
            o0, tot0, gathered = _sb_fwd_call(*qkv0, shards)
            (xs, h), vjp_i = jax.vjp(sb_segment_out, xs, o0, {k: w0[k] for k in ("sb_out", "ffn_norm")})
        else:
            (xs, h), vjp_i, gathered = jax.vjp(
                lambda xx, ff, ww, i=i, shards=shards: mixer_segment(i, xx, ff, cosm, sinm, ww, shards),
                xs, f, mixer_weights(i), has_aux=True)
        Gt.update(zip(keys, gathered))
        seg_vjps.append(vjp_i)
        keys_gu, keys_down = carried_by_ffn.get(i, ([], []))
        h = h.astype(BF16)
        gu, got = mm_carry(h, flat(f"ffn_gu{i}"), "nt", F32, gather_jobs(keys_gu))
        Gt.update(zip(keys_gu, got))
        act = swiglu_fwd(gu)
        f, got = mm_carry(act, flat(f"ffn_down{i}"), "nn", F32, gather_jobs(keys_down))
        Gt.update(zip(keys_down, got))
        ffn_res.append((h, act, gu))
    loss_part, dy = loss_head(xs, f, tgt)
    loss = lax.psum(loss_part, ("x", "y", "c"))

    reduced, gw_small, gw_rep = {}, {}, {}

    def to_partials(i, dw, dwd, dwgu):
        out = {}
        if dwgu is not None:
            out = {f"ffn_gu{i}": dwgu.reshape(N_DEV, -1, D_MODEL), f"ffn_down{i}": dwd.reshape(N_DEV, -1, D_MODEL)}
        for k, g in dw.items():
            if k.endswith("_in_t"):
                out[k.replace("_in_t", "_w_in")] = g.reshape(N_DEV, -1, D_MODEL)
            elif k.endswith("_out"):
                out[k.replace("_out", "_w_out")] = g.reshape(N_DEV, -1, D_MODEL)
            elif k in ("mix_norm", "ffn_norm"):
                gw_rep[(k, i)] = g
            elif k in REPLICATED:
                gw_rep[k] = g
            elif k not in ("lru_w_a", "lru_w_x"):
                gw_small[k] = g
        if "lru_w_a" in dw:
            gg = jnp.stack([dw["lru_w_a"], dw["lru_w_x"]]).reshape(2, LRU_BLOCKS, N_DEV, 32, LRU_BLOCK_DIM)
            out["lru_gates"] = gg.transpose(2, 0, 1, 3, 4).reshape(N_DEV, 2 * LRU_BLOCKS * 32, LRU_BLOCK_DIM)
        return out

    def spread(keys, sizes, capacities):
        room, bins = list(capacities), [[] for _ in capacities]
        for k in sorted(keys, key=lambda k: -sizes[k]):
            b = max(range(len(room)), key=lambda j: room[j])
            bins[b].append(k)
            room[b] -= sizes[k]
        return bins

    ready, dx, df = {}, dy, dy
    for i in reversed(range(DEPTH)):
        h, act, gu = ffn_res[i]
        keys = list(ready)
        g4 = {k: ready[k].reshape(N_CHIPS, 2, *ready[k].shape[1:]) for k in keys}
        dact, got = mm_carry(df, flat(f"ffn_down{i}"), "nt", F32, [PairJob(g4[k]) for k in keys])
        pairs = {k: pair_add(g4[k], g, f"rs_pair_add_{k}") for k, g in zip(keys, got)}
        sizes = {k: math.prod(pairs[k].shape[1:]) * pairs[k].dtype.itemsize for k in keys}
        bins = spread(keys, sizes, [c * D_MODEL * 2 for c in (800, 1300, 1550)])
        dwd, p0 = mm_carry(act, df, "tn", BF16, [ChipsJob(pairs[k]) for k in bins[0]])
        dgu = swiglu_bwd(gu, dact)
        dh, p1 = mm_carry(dgu, flat(f"ffn_gu{i}"), "nn", F32, [ChipsJob(pairs[k]) for k in bins[1]])
        dwgu, p2 = mm_carry(dgu, h, "tn", BF16, [ChipsJob(pairs[k]) for k in bins[2]])
        for k, parts in zip(bins[0] + bins[1] + bins[2], list(p0) + list(p1) + list(p2)):
            reduced[k] = sum_slots(parts, f"rs_sum_{k}")
        if i > 0:
            dx, df, dw = seg_vjps[i]((dx, dh))
            ready = to_partials(i, dw, dwd, dwgu)
    dx_out, do0, dw = seg_vjps[0]((dx, dh))
    ready = to_partials(0, dw, dwd, dwgu)
    keys = list(ready)
    g4 = {k: ready[k].reshape(N_CHIPS, 2, *ready[k].shape[1:]) for k in keys}
    pairs = {k: pair_add(g4[k], exchange_pair(g4[k], f"rs_pair_{k}"), f"rs_pair_add_{k}") for k in keys}
    dq0, dk0, dv0, parts = _sb_bwd_call(*qkv0, tot0, do0, [ChipsJob(pairs[k]) for k in keys])
    for k, p in zip(keys, parts):
        reduced[k] = sum_slots(p, f"rs_sum_{k}")
    dx_in, dw = vjp_in((dq0, dk0, dv0))
    gx = dx_in + dx_out
    g4 = to_partials(0, dw, None, None)["sb_w_in"].reshape(N_CHIPS, 2, -1, D_MODEL)
    sb_in_pairs = pair_add(g4, exchange_pair(g4, "rs_pair_sb_w_in"), "rs_pair_add_sb_w_in")

    G = {}
    for n in IN_T[1:]:
        G[n] = reduced[n].T[None]
    for n in OUT_N:
        G[n] = reduced[n][None]
    G["ffn_w_gu"] = jnp.stack([reduced[f"ffn_gu{i}"].T for i in range(DEPTH)])
    G["ffn_w_down"] = jnp.stack([reduced[f"ffn_down{i}"] for i in range(DEPTH)])
    gg = reduced["lru_gates"].reshape(2, 1, LRU_BLOCKS, 32, LRU_BLOCK_DIM)
    G["lru_w_a"], G["lru_w_x"] = gg[0], gg[1]
    gw = dict(gw_small)
    for n in REPLICATED:
        gw[n] = (jnp.concatenate([gw_rep[(n, i)] for i in range(DEPTH)], axis=0) if n in ("mix_norm", "ffn_norm")
                 else gw_rep[n])
    gs = {
        "gdn_conv_w": gw["gdn_conv_w"].reshape(4, N_DEV, -1).transpose(1, 0, 2)[:, None],
        "lru_conv_w": gw["lru_conv_w"].reshape(4, N_DEV, -1).transpose(1, 0, 2)[:, None],
        "lru_conv_b": gw["lru_conv_b"].reshape(N_DEV, 1, -1),
        "lru_lambda": gw["lru_lambda"].reshape(N_DEV, 1, -1),
        "lru_b_a": gw["lru_b_a"].reshape(LRU_BLOCKS, N_DEV, 32).transpose(1, 0, 2)[:, None],
        "lru_b_x": gw["lru_b_x"].reshape(LRU_BLOCKS, N_DEV, 32).transpose(1, 0, 2)[:, None],
    }
    packed = jnp.stack([_pack([gs[n][p] for n in SMALL_SHARDED]) for p in range(N_DEV)])
    for n, g in zip(SMALL_SHARDED, _unpack(reduce_scatter(packed, "small"), small_shapes)):
        G[n] = g
    rep_shapes = [W[n].shape for n in REPLICATED]
    for n, g in zip(REPLICATED, _unpack(all_reduce(_pack([gw[n] for n in REPLICATED]), "rep"), rep_shapes)):
        G[n] = g

    D, NM, NV = {}, {}, {}
    big = [n for n in NAMES if n not in REPLICATED and n not in SMALL_SHARDED]
    assert IN_T[0] == "sb_w_in" and big.index("ffn_w_gu") < big.index("sb_w_in")
    for n in big:
        shp = W[n].shape
        two = (-1, shp[-1])
        operands = (W[n].reshape(two), G[n].reshape(two), M[n].reshape(two), V[n].reshape(two))
        if n == "ffn_w_gu":
            (d, nm, nv), (parts,) = adamw(*operands, f"adamw_{n}", [ChipsJob(sb_in_pairs)])
            G["sb_w_in"] = sum_slots(parts, "rs_sum_sb_w_in").T[None]
        else:
            d, nm, nv = adamw(*operands, f"adamw_{n}")
        D[n], NM[n], NV[n] = d.reshape(shp), nm.reshape(shp), nv.reshape(shp)
    for group, tag in ((SMALL_SHARDED, "small"), (REPLICATED, "rep")):
        shapes = [W[n].shape for n in group]
        res = adamw(_pack([W[n] for n in group]), _pack([G[n] for n in group]), _pack([M[n] for n in group]),
                    _pack([V[n] for n in group]), f"adamw_{tag}")
        for dst, buf in zip((D, NM, NV), res):
            for n, a in zip(group, _unpack(buf, shapes)):
                dst[n] = a

    return (loss, gx[None], *[G[n] for n in NAMES], *[D[n] for n in NAMES], *[NM[n] for n in NAMES],
            *[NV[n] for n in NAMES])
```
